```python
import math
import jax, jax.numpy as jnp
from jax import lax
import numpy as np

D_MODEL = 2048
BATCH = 2
SEQ = 4096
DEPTH = 1

M_WIDTH = D_MODEL
M_HEADS = 8
M_HEAD_DIM = M_WIDTH // M_HEADS
M_QKV_BLOCK = 4
M_CHUNK = 64
CONV_WIDTH = 4
R_BLOCK = 256
R_WIDTH = (4 * D_MODEL // 3) // R_BLOCK * R_BLOCK
R_HEADS = R_WIDTH // R_BLOCK
LRU_C = 8.0
N_BRANCH = 2
IN_COLS = 2 * M_WIDTH + 2 * R_WIDTH + N_BRANCH * D_MODEL
N_EXPERTS = 64
TOP_K = 8
N_GROUPS = 8
TOPK_GROUPS = 4
D_EXPERT = D_MODEL // 4
ROUTED_SCALE = 2.5
MOE_BLOCK = 128
ALPHA = (2.0 * DEPTH) ** 0.25
BETA = (8.0 * DEPTH) ** -0.25
LN_EPS = 1e-5

kernel_name = "hybrid_mlstm_rglru_moe_block"

F32 = jnp.float32


def layer_norm(x, g, b):
    xf = x.astype(F32)
    mu = jnp.mean(xf, axis=-1, keepdims=True)
    var = jnp.mean(jnp.square(xf - mu), axis=-1, keepdims=True)
    return ((xf - mu) * lax.rsqrt(var + LN_EPS) * g + b).astype(x.dtype)


def causal_depthwise_conv(x, w, b):
    width = w.shape[0]
    S = x.shape[1]
    xp = jnp.pad(x, ((0, 0), (width - 1, 0), (0, 0)))
    y = b + xp[:, 0:S] * w[0]
    for j in range(1, width):
        y = y + xp[:, j:j + S] * w[j]
    return y


def block_diag(x, w):
    nb, bi, bo = w.shape
    xs = x.reshape(x.shape[:-1] + (nb, bi))
    return jnp.einsum('...ni,nio->...no', xs, w).reshape(x.shape[:-1] + (nb * bo,))


def mlstm_chunkwise(q, k, v, ig, lf):
    B, H, S, dk = q.shape
    dv = v.shape[-1]
    L = M_CHUNK
    NC = S // L

    def chunks(t):
        return jnp.moveaxis(t.reshape((B, H, NC, L) + t.shape[3:]), 2, 0)

    causal = jnp.tril(jnp.ones((L, L), dtype=bool))

    def step(carry, inp):
        C, n, m = carry
        q_, k_, v_, i_, f_ = inp
        b = jnp.cumsum(f_, axis=-1)
        dmat = b[..., :, None] - b[..., None, :] + i_[..., None, :]
        dmat = jnp.where(causal, dmat, -jnp.inf)
        a = b + m[..., None]
        m_row = jnp.maximum(a, jnp.max(dmat, axis=-1))
        s = jnp.einsum('bhld,bhsd->bhls', q_, k_) * jnp.exp(dmat - m_row[..., None])
        inter = jnp.exp(a - m_row)
        num = inter[..., None] * jnp.einsum('bhld,bhde->bhle', q_, C) + jnp.einsum('bhls,bhse->bhle', s, v_)
        den = inter * jnp.einsum('bhld,bhd->bhl', q_, n) + jnp.sum(s, axis=-1)
        h = num / jnp.maximum(jnp.abs(den), jnp.exp(-m_row))[..., None]
        b_last = b[..., -1]
        w_log = b_last[..., None] - b + i_
        m_new = jnp.maximum(b_last + m, jnp.max(w_log, axis=-1))
        decay = jnp.exp(b_last + m - m_new)
        w_s = jnp.exp(w_log - m_new[..., None])
        C_new = decay[..., None, None] * C + jnp.einsum('bhs,bhsd,bhse->bhde', w_s, k_, v_)
        n_new = decay[..., None] * n + jnp.einsum('bhs,bhsd->bhd', w_s, k_)
        return (C_new, n_new, m_new), h

    init = (jnp.zeros((B, H, dk, dv), F32), jnp.zeros((B, H, dk), F32), jnp.zeros((B, H), F32))
    _, hs = lax.scan(step, init, (chunks(q), chunks(k), chunks(v), chunks(ig), chunks(lf)))
    return jnp.moveaxis(hs, 0, 2).reshape(B, H, S, dv)


def mlstm_branch(xm, z, conv_w, conv_b, wq, wk, wv, w_if, b_if, norm_w, skip):
    B, S, _ = xm.shape
    xc = jax.nn.silu(causal_depthwise_conv(xm, conv_w, conv_b))
    q = block_diag(xc, wq)
    k = block_diag(xc, wk)
    v = block_diag(xm, wv)
    gates = (jnp.concatenate([q, k, v], axis=-1) @ w_if + b_if).astype(F32)
    ig = jnp.moveaxis(gates[..., :M_HEADS], -1, 1)
    lf = jax.nn.log_sigmoid(jnp.moveaxis(gates[..., M_HEADS:], -1, 1))

    def heads(t):
        return t.reshape(B, S, M_HEADS, M_HEAD_DIM).transpose(0, 2, 1, 3).astype(F32)

    h = mlstm_chunkwise(heads(q), heads(k) * (M_HEAD_DIM ** -0.5), heads(v), ig, lf)
    mu = jnp.mean(h, axis=-1, keepdims=True)
    var = jnp.mean(jnp.square(h - mu), axis=-1, keepdims=True)
    h = (h - mu) * lax.rsqrt(var + LN_EPS)
    h = h.transpose(0, 2, 1, 3).reshape(B, S, M_WIDTH) * norm_w
    h = h.astype(xm.dtype)
    return (h + skip * xc) * jax.nn.silu(z)


def _lin_combine(left, right):
    a1, b1 = left
    a2, b2 = right
    return a1 * a2, a2 * b1 + b2


def rglru_branch(xr, xg, conv_w, conv_b, wa, ba, wx, bx, lam):
    xc = causal_depthwise_conv(xr, conv_w, conv_b)
    r = jax.nn.sigmoid(block_diag(xc, wa) + ba).astype(F32)
    i = jax.nn.sigmoid(block_diag(xc, wx) + bx).astype(F32)
    log_a = -LRU_C * jax.nn.softplus(-lam.astype(F32)) * r
    a = jnp.exp(log_a)
    b_in = jnp.sqrt(-jnp.expm1(2.0 * log_a)) * (i * xc.astype(F32))
    _, h = lax.associative_scan(_lin_combine, (a, b_in), axis=1)
    return h.astype(xr.dtype) * jax.nn.gelu(xg)


def moe_ffn(x, router_w, router_bias, w_gate, w_up, w_down, s_gate, s_up, s_down):
    B, S, D = x.shape
    N = B * S
    E = N_EXPERTS
    xt = x.reshape(N, D)
    scores = jax.nn.sigmoid((xt @ router_w).astype(F32))
    biased = scores + router_bias.astype(F32)
    grp = biased.reshape(N, N_GROUPS, E // N_GROUPS)
    grp_score = jnp.sum(lax.top_k(grp, 2)[0], axis=-1)
    _, top_grp = lax.top_k(grp_score, TOPK_GROUPS)
    grp_mask = jnp.any(top_grp[..., None] == jnp.arange(N_GROUPS), axis=-2)
    exp_mask = jnp.repeat(grp_mask, E // N_GROUPS, axis=-1)
    _, top_e = lax.top_k(jnp.where(exp_mask, biased, -jnp.inf), TOP_K)
    w = jnp.take_along_axis(scores, top_e, axis=-1)
    w = w / jnp.sum(w, axis=-1, keepdims=True) * ROUTED_SCALE

    A = N * TOP_K
    e_flat = top_e.reshape(A)
    tok_flat = jnp.repeat(jnp.arange(N, dtype=jnp.int32), TOP_K)
    w_flat = w.reshape(A)
    order = jnp.argsort(e_flat)
    e_sorted = e_flat[order]
    tok_sorted = tok_flat[order]
    w_sorted = w_flat[order]
    counts = jnp.bincount(e_flat, length=E)
    starts = jnp.cumsum(counts) - counts
    padded = (counts + MOE_BLOCK - 1) // MOE_BLOCK * MOE_BLOCK
    pad_ends = jnp.cumsum(padded)
    pad_starts = pad_ends - padded
    dest = pad_starts[e_sorted] + jnp.arange(A) - starts[e_sorted]
    n_blocks = -(-(A + E * (MOE_BLOCK - 1)) // MOE_BLOCK)
    P = n_blocks * MOE_BLOCK
    row_tok = jnp.full((P,), N, dtype=jnp.int32).at[dest].set(tok_sorted)
    row_w = jnp.zeros((P,), F32).at[dest].set(w_sorted)
    block_e = jnp.minimum(jnp.searchsorted(pad_ends, jnp.arange(n_blocks) * MOE_BLOCK, side='right'), E - 1)
    x_pad = jnp.concatenate([xt, jnp.zeros((1, D), xt.dtype)], axis=0)

    def run_block(args):
        toks, e = args
        xb = x_pad[toks]
        hb = jax.nn.silu(xb @ w_gate[e]) * (xb @ w_up[e])
        return hb @ w_down[e]

    yb = lax.map(run_block, (row_tok.reshape(n_blocks, MOE_BLOCK), block_e))
    routed = jnp.zeros((N + 1, D), F32).at[row_tok].add(yb.reshape(P, D).astype(F32) * row_w[:, None])[:N]
    shared = (jax.nn.silu(xt @ s_gate) * (xt @ s_up)) @ s_down
    return (routed.astype(x.dtype) + shared).reshape(B, S, D)


def hybrid_layer(x, w_in, b_gate, m_conv_w, m_conv_b, m_wq, m_wk, m_wv, m_w_if, m_b_if, m_norm_w, m_skip,
                 r_conv_w, r_conv_b, r_wa, r_ba, r_wx, r_bx, r_lambda, w_pm, w_pr, w_o, ln1_g, ln1_b,
                 router_w, router_bias, e_w_gate, e_w_up, e_w_down, s_w_gate, s_w_up, s_w_down, ln2_g, ln2_b):
    B, S, D = x.shape
    proj = x @ w_in
    o1 = M_WIDTH
    o2 = o1 + M_WIDTH
    o3 = o2 + R_WIDTH
    o4 = o3 + R_WIDTH
    xm, z, xr, xg, gpre = proj[..., :o1], proj[..., o1:o2], proj[..., o2:o3], proj[..., o3:o4], proj[..., o4:]
    y_m = mlstm_branch(xm, z, m_conv_w, m_conv_b, m_wq, m_wk, m_wv, m_w_if, m_b_if, m_norm_w, m_skip)
    y_r = rglru_branch(xr, xg, r_conv_w, r_conv_b, r_wa, r_ba, r_wx, r_bx, r_lambda)
    g = jax.nn.sigmoid(gpre.reshape(B, S, N_BRANCH, D) + b_gate)
    merged = g[..., 0, :] * (y_m @ w_pm) + g[..., 1, :] * (y_r @ w_pr)
    x = layer_norm(ALPHA * x + merged @ w_o, ln1_g, ln1_b)
    y = moe_ffn(x, router_w, router_bias, e_w_gate, e_w_up, e_w_down, s_w_gate, s_w_up, s_w_down)
    return layer_norm(ALPHA * x + y, ln2_g, ln2_b)


def setup_inputs(seed: int = 0) -> dict:
    key = jax.random.key(seed)
    ks = iter(jax.random.split(key, 48))
    L = DEPTH

    def nrm(shape, scale):
        return jax.random.normal(next(ks), shape, F32) * scale

    x = nrm((BATCH, SEQ, D_MODEL), 1.0)
    w_in = nrm((L, D_MODEL, IN_COLS), D_MODEL ** -0.5)
    b_gate = nrm((L, N_BRANCH, D_MODEL), 0.02)
    m_conv_w = nrm((L, CONV_WIDTH, M_WIDTH), CONV_WIDTH ** -0.5)
    m_conv_b = nrm((L, M_WIDTH), 0.02)
    nbq = M_WIDTH // M_QKV_BLOCK
    m_wq = nrm((L, nbq, M_QKV_BLOCK, M_QKV_BLOCK), M_QKV_BLOCK ** -0.5)
    m_wk = nrm((L, nbq, M_QKV_BLOCK, M_QKV_BLOCK), M_QKV_BLOCK ** -0.5)
    m_wv = nrm((L, nbq, M_QKV_BLOCK, M_QKV_BLOCK), M_QKV_BLOCK ** -0.5)
    m_w_if = nrm((L, 3 * M_WIDTH, 2 * M_HEADS), (3 * M_WIDTH) ** -0.5)
    i_bias = nrm((L, M_HEADS), 0.1)
    f_bias = jnp.broadcast_to(jnp.linspace(3.0, 6.0, M_HEADS, dtype=F32), (L, M_HEADS)) + nrm((L, M_HEADS), 0.05)
    m_b_if = jnp.concatenate([i_bias, f_bias], axis=-1)
    m_norm_w = 1.0 + nrm((L, M_WIDTH), 0.02)
    m_skip = 1.0 + nrm((L, M_WIDTH), 0.02)
    r_conv_w = nrm((L, CONV_WIDTH, R_WIDTH), CONV_WIDTH ** -0.5)
    r_conv_b = nrm((L, R_WIDTH), 0.02)
    r_wa = nrm((L, R_HEADS, R_BLOCK, R_BLOCK), R_BLOCK ** -0.5)
    r_ba = nrm((L, R_WIDTH), 0.02)
    r_wx = nrm((L, R_HEADS, R_BLOCK, R_BLOCK), R_BLOCK ** -0.5)
    r_bx = nrm((L, R_WIDTH), 0.02)
    u = jax.random.uniform(next(ks), (L, R_WIDTH), F32, minval=0.9, maxval=0.999)
    a0 = u ** (1.0 / LRU_C)
    r_lambda = jnp.log(a0) - jnp.log1p(-a0)
    w_pm = nrm((L, M_WIDTH, D_MODEL), M_WIDTH ** -0.5 * BETA)
    w_pr = nrm((L, R_WIDTH, D_MODEL), R_WIDTH ** -0.5 * BETA)
    w_o = nrm((L, D_MODEL, D_MODEL), D_MODEL ** -0.5 * BETA)
    ln1_g = 1.0 + nrm((L, D_MODEL), 0.02)
    ln1_b = nrm((L, D_MODEL), 0.02)
    router_w = nrm((L, D_MODEL, N_EXPERTS), D_MODEL ** -0.5)
    router_bias = nrm((L, N_EXPERTS), 0.01)
    e_w_gate = nrm((L, N_EXPERTS, D_MODEL, D_EXPERT), D_MODEL ** -0.5)
    e_w_up = nrm((L, N_EXPERTS, D_MODEL, D_EXPERT), D_MODEL ** -0.5)
    e_w_down = nrm((L, N_EXPERTS, D_EXPERT, D_MODEL), D_EXPERT ** -0.5 * BETA)
    s_w_gate = nrm((L, D_MODEL, D_EXPERT), D_MODEL ** -0.5)
    s_w_up = nrm((L, D_MODEL, D_EXPERT), D_MODEL ** -0.5)
    s_w_down = nrm((L, D_EXPERT, D_MODEL), D_EXPERT ** -0.5 * BETA)
    ln2_g = 1.0 + nrm((L, D_MODEL), 0.02)
    ln2_b = nrm((L, D_MODEL), 0.02)
    return {"x": x, "w_in": w_in, "b_gate": b_gate, "m_conv_w": m_conv_w, "m_conv_b": m_conv_b,
            "m_wq": m_wq, "m_wk": m_wk, "m_wv": m_wv, "m_w_if": m_w_if, "m_b_if": m_b_if,
            "m_norm_w": m_norm_w, "m_skip": m_skip, "r_conv_w": r_conv_w, "r_conv_b": r_conv_b,
            "r_wa": r_wa, "r_ba": r_ba, "r_wx": r_wx, "r_bx": r_bx, "r_lambda": r_lambda,
            "w_pm": w_pm, "w_pr": w_pr, "w_o": w_o, "ln1_g": ln1_g, "ln1_b": ln1_b,
            "router_w": router_w, "router_bias": router_bias, "e_w_gate": e_w_gate, "e_w_up": e_w_up,
            "e_w_down": e_w_down, "s_w_gate": s_w_gate, "s_w_up": s_w_up, "s_w_down": s_w_down,
            "ln2_g": ln2_g, "ln2_b": ln2_b}


def reference(x, w_in, b_gate, m_conv_w, m_conv_b, m_wq, m_wk, m_wv, m_w_if, m_b_if, m_norm_w, m_skip,
              r_conv_w, r_conv_b, r_wa, r_ba, r_wx, r_bx, r_lambda, w_pm, w_pr, w_o, ln1_g, ln1_b,
              router_w, router_bias, e_w_gate, e_w_up, e_w_down, s_w_gate, s_w_up, s_w_down, ln2_g, ln2_b):
    for l in range(DEPTH):
        x = hybrid_layer(x, w_in[l], b_gate[l], m_conv_w[l], m_conv_b[l], m_wq[l], m_wk[l], m_wv[l],
                         m_w_if[l], m_b_if[l], m_norm_w[l], m_skip[l], r_conv_w[l], r_conv_b[l],
                         r_wa[l], r_ba[l], r_wx[l], r_bx[l], r_lambda[l], w_pm[l], w_pr[l], w_o[l],
                         ln1_g[l], ln1_b[l], router_w[l], router_bias[l], e_w_gate[l], e_w_up[l],
                         e_w_down[l], s_w_gate[l], s_w_up[l], s_w_down[l], ln2_g[l], ln2_b[l])
    return x
```

```python
import functools
import math

import jax
import jax.numpy as jnp
from jax import lax
from jax.experimental import pallas as pl
from jax.experimental.pallas import tpu as pltpu

F32 = jnp.float32
BF16 = jnp.bfloat16
I32 = jnp.int32

D_MODEL = 2048
M_WIDTH = 2048
M_HEADS = 8
M_HEAD_DIM = 256
M_QKV_BLOCK = 4
CONV_WIDTH = 4
R_WIDTH = 2560
R_BLOCK = 256
LRU_C = 8.0
N_EXPERTS = 64
TOP_K = 8
N_GROUPS = 8
TOPK_GROUPS = 4
D_EXPERT = 512
ROUTED_SCALE = 2.5
DEPTH = 1
ALPHA = (2.0 * DEPTH) ** 0.25
LN_EPS = 1e-5

V7X_VMEM_LIMIT = 56 * 1024 * 1024
HALO = 8
MLSTM_CHUNK = 256
EXPERT_TILE = 256
NEG_INF = float("-inf")


def _cparams(sem, vmem=V7X_VMEM_LIMIT):
    return pltpu.CompilerParams(dimension_semantics=sem, vmem_limit_bytes=vmem)


def _mm_kernel(a_ref, b_ref, o_ref):
    o_ref[...] = jnp.dot(a_ref[...], b_ref[...], preferred_element_type=F32).astype(o_ref.dtype)


def _matmul(a, b, bm, bn, out_dtype):
    m, k = a.shape
    n = b.shape[1]
    return pl.pallas_call(
        _mm_kernel,
        grid=(m // bm, n // bn),
        in_specs=[pl.BlockSpec((bm, k), lambda i, j: (i, 0)),
                  pl.BlockSpec((k, bn), lambda i, j: (0, j))],
        out_specs=pl.BlockSpec((bm, bn), lambda i, j: (i, j)),
        out_shape=jax.ShapeDtypeStruct((m, n), out_dtype),
        compiler_params=_cparams(("parallel", "parallel")),
        name="in_proj",
    )(a, b)


def _log_sigmoid(x):
    return jnp.minimum(x, 0.0) - jnp.log1p(jnp.exp(-jnp.abs(x)))


def _causal_conv(xp_ref, halo, x, cw_ref, cb_ref, tm):
    xp_ref[0:HALO, :] = halo
    xp_ref[HALO:HALO + tm, :] = x
    base = HALO - (CONV_WIDTH - 1)
    y = cb_ref[...] + xp_ref[base:base + tm, :] * cw_ref[0:1, :]
    for j in range(1, CONV_WIDTH):
        y = y + xp_ref[base + j:base + j + tm, :] * cw_ref[j:j + 1, :]
    return y


def _mprep_kernel(xm_ref, halo_ref, cw_ref, cb_ref, wq_ref, wk_ref, wv_ref, wif_ref, wift_ref,
                  bif_ref, bift_ref, xc_ref, q_ref, k_ref, v_ref, g_ref, gt_ref, xp_ref,
                  *, tm, tiles_per_seq):
    i = pl.program_id(0)
    first = (i % tiles_per_seq) == 0
    halo = jnp.where(first, 0.0, halo_ref[...])
    xm = xm_ref[...]
    y = _causal_conv(xp_ref, halo, xm, cw_ref, cb_ref, tm)
    xc = y * jax.nn.sigmoid(y)
    xc_ref[...] = xc
    xcb = xc.astype(BF16)
    xmb = xm.astype(BF16)
    nblk = M_WIDTH // M_HEAD_DIM
    for g in range(nblk):
        sl = slice(g * M_HEAD_DIM, (g + 1) * M_HEAD_DIM)
        q_ref[:, sl] = jnp.dot(xcb[:, sl], wq_ref[g], preferred_element_type=F32).astype(BF16)
        k_ref[:, sl] = jnp.dot(xcb[:, sl], wk_ref[g], preferred_element_type=F32).astype(BF16)
        v_ref[:, sl] = jnp.dot(xmb[:, sl], wv_ref[g], preferred_element_type=F32).astype(BF16)
    qb, kb, vb = q_ref[...], k_ref[...], v_ref[...]
    w = M_WIDTH
    g = (jnp.dot(qb, wif_ref[0:w, :], preferred_element_type=F32)
         + jnp.dot(kb, wif_ref[w:2 * w, :], preferred_element_type=F32)
         + jnp.dot(vb, wif_ref[2 * w:3 * w, :], preferred_element_type=F32) + bif_ref[...])
    nt = (((1,), (1,)), ((), ()))
    gt = (lax.dot_general(wift_ref[:, 0:w], qb, nt, preferred_element_type=F32)
          + lax.dot_general(wift_ref[:, w:2 * w], kb, nt, preferred_element_type=F32)
          + lax.dot_general(wift_ref[:, 2 * w:3 * w], vb, nt, preferred_element_type=F32) + bift_ref[...])
    col = lax.broadcasted_iota(I32, g.shape, 1)
    g_ref[...] = jnp.where(col >= M_HEADS, _log_sigmoid(g), g)
    row = lax.broadcasted_iota(I32, gt.shape, 0)
    gt_ref[...] = jnp.where(row >= M_HEADS, _log_sigmoid(gt), gt)


def _block_diag_dense(w, group):
    nb, bi, bo = w.shape
    per = group // bi
    w4 = w.reshape(nb // per, per, bi, bo)
    eye = jnp.eye(per, dtype=w.dtype)
    return jnp.einsum("gaio,ab->gaibo", w4, eye).reshape(nb // per, group, group)


def _mlstm_prep(proj, seq, conv_w, conv_b, wq, wk, wv, w_if, b_if, tm):
    n = proj.shape[0]
    c = M_WIDTH
    nblk = c // M_HEAD_DIM
    tiles_per_seq = seq // tm
    wqd = _block_diag_dense(wq, M_HEAD_DIM).astype(BF16)
    wkd = _block_diag_dense(wk, M_HEAD_DIM).astype(BF16)
    wvd = _block_diag_dense(wv, M_HEAD_DIM).astype(BF16)
    wif = w_if.astype(BF16)
    wift = w_if.T.astype(BF16)
    ng = 2 * M_HEADS
    hb = tm // HALO
    full = lambda shape: pl.BlockSpec(shape, lambda i: (0,) * len(shape))
    return pl.pallas_call(
        functools.partial(_mprep_kernel, tm=tm, tiles_per_seq=tiles_per_seq),
        grid=(n // tm,),
        in_specs=[pl.BlockSpec((tm, c), lambda i: (i, 0)),
                  pl.BlockSpec((HALO, c), lambda i: (jnp.maximum(i * hb - 1, 0), 0)),
                  full((CONV_WIDTH, c)), full((1, c)),
                  full((nblk, M_HEAD_DIM, M_HEAD_DIM)), full((nblk, M_HEAD_DIM, M_HEAD_DIM)),
                  full((nblk, M_HEAD_DIM, M_HEAD_DIM)),
                  full((3 * c, ng)), full((ng, 3 * c)), full((1, ng)), full((ng, 1))],
        out_specs=[pl.BlockSpec((tm, c), lambda i: (i, 0)),
                   pl.BlockSpec((tm, c), lambda i: (i, 0)),
                   pl.BlockSpec((tm, c), lambda i: (i, 0)),
                   pl.BlockSpec((tm, c), lambda i: (i, 0)),
                   pl.BlockSpec((tm, ng), lambda i: (i, 0)),
                   pl.BlockSpec((ng, tm), lambda i: (0, i))],
        out_shape=[jax.ShapeDtypeStruct((n, c), F32),
                   jax.ShapeDtypeStruct((n, c), BF16),
                   jax.ShapeDtypeStruct((n, c), BF16),
                   jax.ShapeDtypeStruct((n, c), BF16),
                   jax.ShapeDtypeStruct((n, ng), F32),
                   jax.ShapeDtypeStruct((ng, n), F32)],
        scratch_shapes=[pltpu.VMEM((HALO + tm, c), F32)],
        compiler_params=_cparams(("parallel",)),
        name="mlstm_prep",
    )(proj, proj, conv_w, conv_b.reshape(1, c), wqd, wkd, wvd, wif, wift,
      b_if.reshape(1, ng), b_if.reshape(ng, 1))


def _mlstm_kernel(q_ref, k_ref, v_ref, g_ref, gt_ref, xc_ref, z_ref, nw_ref, sk_ref, o_ref,
                  c_ref, n_ref, m_ref, *, chunk):
    L = chunk
    hd = M_HEAD_DIM

    @pl.when(pl.program_id(1) == 0)
    def _():
        c_ref[...] = jnp.zeros_like(c_ref)
        n_ref[...] = jnp.zeros_like(n_ref)
        m_ref[...] = jnp.zeros_like(m_ref)

    rows = lax.broadcasted_iota(I32, (L, L), 0)
    cols = lax.broadcasted_iota(I32, (L, L), 1)
    causal = cols <= rows
    tril = jnp.where(causal, 1.0, 0.0).astype(F32)
    triu = jnp.where(rows <= cols, 1.0, 0.0).astype(F32)
    g = g_ref[...]
    gt = gt_ref[...]
    hi = lax.Precision.HIGHEST
    bcol_all = jnp.dot(tril, g, precision=hi, preferred_element_type=F32)
    brow_all = jnp.dot(gt, triu, precision=hi, preferred_element_type=F32)
    k_scale = hd ** -0.5
    nt = (((1,), (1,)), ((), ()))
    tn = (((0,), (0,)), ((), ()))

    for h in range(M_HEADS):
        sl = slice(h * hd, (h + 1) * hd)
        qh = q_ref[:, sl]
        kh = k_ref[:, sl]
        vh = v_ref[:, sl]
        i_col = g[:, h:h + 1]
        b_col = bcol_all[:, M_HEADS + h:M_HEADS + h + 1]
        i_row = gt[h:h + 1, :]
        b_row = brow_all[M_HEADS + h:M_HEADS + h + 1, :]
        m_prev = m_ref[h, 0:1, 0:1]
        c_prev = c_ref[h]
        n_prev = n_ref[h]

        dmat = jnp.where(causal, b_col - b_row + i_row, NEG_INF)
        a_col = b_col + m_prev
        m_row = jnp.maximum(a_col, jnp.max(dmat, axis=1, keepdims=True))
        s = lax.dot_general(qh, kh, nt, preferred_element_type=F32) * k_scale
        s = s * jnp.exp(dmat - m_row)
        inter = jnp.exp(a_col - m_row)
        num = inter * jnp.dot(qh, c_prev.astype(BF16), preferred_element_type=F32) \
            + jnp.dot(s.astype(BF16), vh, preferred_element_type=F32)
        qn = jnp.sum(qh.astype(F32) * n_prev, axis=1, keepdims=True)
        den = inter * qn + jnp.sum(s, axis=1, keepdims=True)
        hval = num * (1.0 / jnp.maximum(jnp.abs(den), jnp.exp(-m_row)))

        mu = jnp.mean(hval, axis=1, keepdims=True)
        cen = hval - mu
        var = jnp.mean(cen * cen, axis=1, keepdims=True)
        hn = cen * lax.rsqrt(var + LN_EPS) * nw_ref[:, sl]
        zz = z_ref[:, sl]
        o_ref[:, sl] = ((hn + sk_ref[:, sl] * xc_ref[:, sl]) * (zz * jax.nn.sigmoid(zz))).astype(o_ref.dtype)

        b_last = b_col[L - 1:L, :]
        w_log = b_last - b_col + i_col
        m_new = jnp.maximum(b_last + m_prev, jnp.max(w_log, axis=0, keepdims=True))
        decay = jnp.exp(b_last + m_prev - m_new)
        kw = kh.astype(F32) * (jnp.exp(w_log - m_new) * k_scale)
        c_ref[h] = decay * c_prev + lax.dot_general(kw.astype(BF16), vh, tn, preferred_element_type=F32)
        n_ref[h] = decay * n_prev + jnp.sum(kw, axis=0, keepdims=True)
        m_ref[h] = jnp.broadcast_to(m_new, m_ref.shape[1:])


def _mlstm(q, k, v, g, gt, xc, proj, norm_w, skip, batch, seq, chunk):
    n, c = q.shape
    nc = seq // chunk
    ng = 2 * M_HEADS
    zcol = M_WIDTH // c
    row = lambda b, j: (b * nc + j, 0)
    return pl.pallas_call(
        functools.partial(_mlstm_kernel, chunk=chunk),
        grid=(batch, nc),
        in_specs=[pl.BlockSpec((chunk, c), row), pl.BlockSpec((chunk, c), row), pl.BlockSpec((chunk, c), row),
                  pl.BlockSpec((chunk, ng), row),
                  pl.BlockSpec((ng, chunk), lambda b, j: (0, b * nc + j)),
                  pl.BlockSpec((chunk, c), row),
                  pl.BlockSpec((chunk, c), lambda b, j: (b * nc + j, zcol)),
                  pl.BlockSpec((1, c), lambda b, j: (0, 0)),
                  pl.BlockSpec((1, c), lambda b, j: (0, 0))],
        out_specs=pl.BlockSpec((chunk, c), row),
        out_shape=jax.ShapeDtypeStruct((n, c), BF16),
        scratch_shapes=[pltpu.VMEM((M_HEADS, M_HEAD_DIM, M_HEAD_DIM), F32),
                        pltpu.VMEM((M_HEADS, 1, M_HEAD_DIM), F32),
                        pltpu.VMEM((M_HEADS, 8, 128), F32)],
        compiler_params=_cparams(("parallel", "arbitrary")),
        name="mlstm_chunk",
    )(q, k, v, g, gt, xc, proj, norm_w.reshape(1, c), skip.reshape(1, c))


def _gelu_tanh(x):
    return 0.5 * x * (1.0 + jnp.tanh(math.sqrt(2.0 / math.pi) * (x + 0.044715 * (x * x * x))))


def _rglru_kernel(xr_ref, halo_ref, xg_ref, cw_ref, cb_ref, wa_ref, ba_ref, wx_ref, bx_ref, lam_ref,
                  o_ref, xp_ref, a_ref, b_ref, h_ref, *, tm, cw):
    t = pl.program_id(2)

    @pl.when(t == 0)
    def _():
        h_ref[...] = jnp.zeros_like(h_ref)

    halo = jnp.where(t == 0, 0.0, halo_ref[...])
    xc = _causal_conv(xp_ref, halo, xr_ref[...], cw_ref, cb_ref, tm)
    xcb = xc.astype(BF16)
    nblk = cw // R_BLOCK
    ra = []
    rx = []
    for g in range(nblk):
        sl = slice(g * R_BLOCK, (g + 1) * R_BLOCK)
        ra.append(jnp.dot(xcb[:, sl], wa_ref[g], preferred_element_type=F32))
        rx.append(jnp.dot(xcb[:, sl], wx_ref[g], preferred_element_type=F32))
    r = jax.nn.sigmoid(jnp.concatenate(ra, axis=1) + ba_ref[...])
    ig = jax.nn.sigmoid(jnp.concatenate(rx, axis=1) + bx_ref[...])
    nl = -lam_ref[...]
    softplus = jnp.maximum(nl, 0.0) + jnp.log1p(jnp.exp(-jnp.abs(nl)))
    log_a = (-LRU_C * softplus) * r
    a = jnp.exp(log_a)
    b = jnp.sqrt(1.0 - jnp.exp(2.0 * log_a)) * (ig * xc)

    sub = lax.broadcasted_iota(I32, (tm, cw), 0) % 8
    for d in (1, 2, 4):
        keep = sub >= d
        a_sh = pltpu.roll(a, d, axis=0)
        b_sh = pltpu.roll(b, d, axis=0)
        b = jnp.where(keep, a * b_sh + b, b)
        a = jnp.where(keep, a * a_sh, a)
    a_ref[...] = a
    b_ref[...] = b

    def body(g, h):
        r0 = pl.multiple_of(g * 8, 8)
        hh = b_ref[pl.ds(r0, 8), :] + a_ref[pl.ds(r0, 8), :] * h
        b_ref[pl.ds(r0, 8), :] = hh
        return hh[7:8, :]

    h_ref[...] = lax.fori_loop(0, tm // 8, body, h_ref[...], unroll=8)
    o_ref[...] = (b_ref[...] * _gelu_tanh(xg_ref[...])).astype(o_ref.dtype)


def _rglru(proj, batch, seq, xr_off, xg_off, conv_w, conv_b, wa, ba, wx, bx, lam, tm, cw):
    n = proj.shape[0]
    ncol = R_WIDTH // cw
    nt = seq // tm
    per = cw // R_BLOCK
    xr_cb = xr_off // cw
    xg_cb = xg_off // cw
    hb = tm // HALO
    colv = lambda shape: pl.BlockSpec(shape, lambda b, j, t: (0, j))
    return pl.pallas_call(
        functools.partial(_rglru_kernel, tm=tm, cw=cw),
        grid=(batch, ncol, nt),
        in_specs=[pl.BlockSpec((tm, cw), lambda b, j, t: (b * nt + t, xr_cb + j)),
                  pl.BlockSpec((HALO, cw), lambda b, j, t: (jnp.maximum((b * nt + t) * hb - 1, 0), xr_cb + j)),
                  pl.BlockSpec((tm, cw), lambda b, j, t: (b * nt + t, xg_cb + j)),
                  colv((CONV_WIDTH, cw)), colv((1, cw)),
                  pl.BlockSpec((per, R_BLOCK, R_BLOCK), lambda b, j, t: (j, 0, 0)), colv((1, cw)),
                  pl.BlockSpec((per, R_BLOCK, R_BLOCK), lambda b, j, t: (j, 0, 0)), colv((1, cw)),
                  colv((1, cw))],
        out_specs=pl.BlockSpec((tm, cw), lambda b, j, t: (b * nt + t, j)),
        out_shape=jax.ShapeDtypeStruct((n, R_WIDTH), BF16),
        scratch_shapes=[pltpu.VMEM((HALO + tm, cw), F32), pltpu.VMEM((tm, cw), F32),
                        pltpu.VMEM((tm, cw), F32), pltpu.VMEM((1, cw), F32)],
        compiler_params=_cparams(("parallel", "parallel", "arbitrary")),
        name="rglru",
    )(proj, proj, proj, conv_w, conv_b.reshape(1, R_WIDTH), wa.astype(BF16), ba.reshape(1, R_WIDTH),
      wx.astype(BF16), bx.reshape(1, R_WIDTH), lam.reshape(1, R_WIDTH))


def _merge_kernel(ym_ref, yr_ref, wpm_ref, wpr_ref, g0_ref, g1_ref, bg_ref, o_ref):
    g0 = jax.nn.sigmoid(g0_ref[...] + bg_ref[0:1, :])
    g1 = jax.nn.sigmoid(g1_ref[...] + bg_ref[1:2, :])
    pm = jnp.dot(ym_ref[...], wpm_ref[...], preferred_element_type=F32)
    pr = jnp.dot(yr_ref[...], wpr_ref[...], preferred_element_type=F32)
    o_ref[...] = (g0 * pm + g1 * pr).astype(o_ref.dtype)


def _merge(ym, yr, w_pm, w_pr, proj, gate_off, b_gate, tm, bn):
    n = ym.shape[0]
    d = w_pm.shape[1]
    g0_cb = gate_off // bn
    g1_cb = (gate_off + d) // bn
    return pl.pallas_call(
        _merge_kernel,
        grid=(d // bn, n // tm),
        in_specs=[pl.BlockSpec((tm, ym.shape[1]), lambda j, i: (i, 0)),
                  pl.BlockSpec((tm, yr.shape[1]), lambda j, i: (i, 0)),
                  pl.BlockSpec((w_pm.shape[0], bn), lambda j, i: (0, j)),
                  pl.BlockSpec((w_pr.shape[0], bn), lambda j, i: (0, j)),
                  pl.BlockSpec((tm, bn), lambda j, i: (i, g0_cb + j)),
                  pl.BlockSpec((tm, bn), lambda j, i: (i, g1_cb + j)),
                  pl.BlockSpec((2, bn), lambda j, i: (0, j))],
        out_specs=pl.BlockSpec((tm, bn), lambda j, i: (i, j)),
        out_shape=jax.ShapeDtypeStruct((n, d), BF16),
        compiler_params=_cparams(("parallel", "parallel")),
        name="merge",
    )(ym, yr, w_pm.astype(BF16), w_pr.astype(BF16), proj, proj, b_gate)


def _layer_norm(y, g, b):
    mu = jnp.mean(y, axis=1, keepdims=True)
    cen = y - mu
    var = jnp.mean(cen * cen, axis=1, keepdims=True)
    return cen * lax.rsqrt(var + LN_EPS) * g + b


def _oproj_kernel(mg_ref, wo_ref, x_ref, g_ref, b_ref, rwt_ref, x1_ref, lt_ref):
    y = ALPHA * x_ref[...] + jnp.dot(mg_ref[...], wo_ref[...], preferred_element_type=F32)
    x1 = _layer_norm(y, g_ref[...], b_ref[...])
    x1_ref[...] = x1
    lt_ref[...] = lax.dot_general(rwt_ref[...], x1, (((1,), (1,)), ((), ())),
                                  precision=lax.Precision.HIGHEST, preferred_element_type=F32)


def _oproj(merged, w_o, x, ln_g, ln_b, router_w, tm):
    n, d = x.shape
    e = router_w.shape[1]
    full = lambda shape: pl.BlockSpec(shape, lambda i: (0,) * len(shape))
    return pl.pallas_call(
        _oproj_kernel,
        grid=(n // tm,),
        in_specs=[pl.BlockSpec((tm, d), lambda i: (i, 0)), full((d, d)),
                  pl.BlockSpec((tm, d), lambda i: (i, 0)), full((1, d)), full((1, d)), full((e, d))],
        out_specs=[pl.BlockSpec((tm, d), lambda i: (i, 0)), pl.BlockSpec((e, tm), lambda i: (0, i))],
        out_shape=[jax.ShapeDtypeStruct((n, d), F32), jax.ShapeDtypeStruct((e, n), F32)],
        compiler_params=_cparams(("parallel",)),
        name="out_proj_ln",
    )(merged, w_o.astype(BF16), x, ln_g.reshape(1, d), ln_b.reshape(1, d), router_w.T)


def _first_max(v, idx, sentinel):
    m = jnp.max(v, axis=0, keepdims=True)
    am = jnp.min(jnp.where(v == m, idx, sentinel), axis=0, keepdims=True)
    return m, am


def _route_kernel(lt_ref, bias_ref, tri_ref, te_ref, w_ref, rk_ref, cnt_ref, carry_ref, *, tm):
    @pl.when(pl.program_id(0) == 0)
    def _():
        carry_ref[...] = jnp.zeros_like(carry_ref)

    e = N_EXPERTS
    gs = e // N_GROUPS
    scores = jax.nn.sigmoid(lt_ref[...])
    biased = scores + bias_ref[...]
    sub = lax.broadcasted_iota(I32, (gs, tm), 0)
    grp_rows = []
    for g in range(N_GROUPS):
        slab = biased[g * gs:(g + 1) * gs, :]
        m1, a1 = _first_max(slab, sub, gs)
        m2 = jnp.max(jnp.where(sub == a1, NEG_INF, slab), axis=0, keepdims=True)
        grp_rows.append(m1 + m2)
    grp = jnp.concatenate(grp_rows, axis=0)
    gidx = lax.broadcasted_iota(I32, (N_GROUPS, tm), 0)
    gsel = jnp.zeros((N_GROUPS, tm), F32)
    for _ in range(TOPK_GROUPS):
        _, am = _first_max(grp, gidx, N_GROUPS)
        hit = gidx == am
        gsel = jnp.where(hit, 1.0, gsel)
        grp = jnp.where(hit, NEG_INF, grp)
    masked = jnp.concatenate(
        [jnp.where(gsel[g:g + 1, :] > 0.0, biased[g * gs:(g + 1) * gs, :], NEG_INF) for g in range(N_GROUPS)],
        axis=0)
    eidx = lax.broadcasted_iota(I32, (e, tm), 0)
    member = jnp.zeros((e, tm), F32)
    tops = []
    ws = []
    for _ in range(TOP_K):
        _, am = _first_max(masked, eidx, e)
        hit = eidx == am
        tops.append(am)
        ws.append(jnp.sum(jnp.where(hit, scores, 0.0), axis=0, keepdims=True))
        member = jnp.where(hit, 1.0, member)
        masked = jnp.where(hit, NEG_INF, masked)
    wsum = ws[0]
    for k in range(1, TOP_K):
        wsum = wsum + ws[k]
    te_ref[...] = jnp.concatenate(tops, axis=0)
    w_ref[...] = jnp.concatenate(ws, axis=0) / wsum * ROUTED_SCALE

    cum = jnp.dot(member.astype(BF16), tri_ref[...], preferred_element_type=F32)
    carry = carry_ref[:, 0:1]
    rank = carry + cum - member
    rks = []
    for k in range(TOP_K):
        rks.append(jnp.sum(jnp.where(eidx == tops[k], rank, 0.0), axis=0, keepdims=True))
    rk_ref[...] = jnp.concatenate(rks, axis=0).astype(I32)
    new_carry = carry + cum[:, tm - 1:tm]
    carry_ref[...] = jnp.broadcast_to(new_carry, carry_ref.shape)
    cnt_ref[...] = jnp.broadcast_to(new_carry, cnt_ref.shape).astype(I32)


def _route(logits_t, router_bias, tm):
    e, n = logits_t.shape
    tri = jnp.triu(jnp.ones((tm, tm), F32)).astype(BF16)
    return pl.pallas_call(
        functools.partial(_route_kernel, tm=tm),
        grid=(n // tm,),
        in_specs=[pl.BlockSpec((e, tm), lambda i: (0, i)),
                  pl.BlockSpec((e, 1), lambda i: (0, 0)),
                  pl.BlockSpec((tm, tm), lambda i: (0, 0))],
        out_specs=[pl.BlockSpec((TOP_K, tm), lambda i: (0, i)),
                   pl.BlockSpec((TOP_K, tm), lambda i: (0, i)),
                   pl.BlockSpec((TOP_K, tm), lambda i: (0, i)),
                   pl.BlockSpec((e, 128), lambda i: (0, 0))],
        out_shape=[jax.ShapeDtypeStruct((TOP_K, n), I32), jax.ShapeDtypeStruct((TOP_K, n), F32),
                   jax.ShapeDtypeStruct((TOP_K, n), I32), jax.ShapeDtypeStruct((e, 128), I32)],
        scratch_shapes=[pltpu.VMEM((e, 128), F32)],
        compiler_params=_cparams(("arbitrary",)),
        name="route",
    )(logits_t, router_bias.reshape(e, 1), tri)


def _dest_kernel(te_ref, rk_ref, ps_ref, d_ref):
    te = te_ref[...]
    e = N_EXPERTS
    tm = te.shape[1]
    eidx = lax.broadcasted_iota(I32, (e, tm), 0)
    ps = ps_ref[...]
    rows = []
    for k in range(TOP_K):
        rows.append(jnp.sum(jnp.where(eidx == te[k:k + 1, :], ps, 0), axis=0, keepdims=True))
    d_ref[...] = jnp.concatenate(rows, axis=0) + rk_ref[...]


def _dest(top_e, rank, pad_starts, tm):
    n = top_e.shape[1]
    return pl.pallas_call(
        _dest_kernel,
        grid=(n // tm,),
        in_specs=[pl.BlockSpec((TOP_K, tm), lambda i: (0, i)),
                  pl.BlockSpec((TOP_K, tm), lambda i: (0, i)),
                  pl.BlockSpec((N_EXPERTS, 1), lambda i: (0, 0))],
        out_specs=pl.BlockSpec((TOP_K, tm), lambda i: (0, i)),
        out_shape=jax.ShapeDtypeStruct((TOP_K, n), I32),
        compiler_params=_cparams(("parallel",)),
        name="dest",
    )(top_e, rank, pad_starts.reshape(N_EXPERTS, 1))


def _dispatch_kernel(dest_ref, x_ref, xs_in_ref, xs_ref, sem, *, tm):
    del xs_in_ref

    def row_copy(t, k):
        return pltpu.make_async_copy(x_ref.at[pl.ds(t, 1)], xs_ref.at[pl.ds(dest_ref[k, t], 1)], sem)

    def issue(t, c):
        for k in range(TOP_K):
            row_copy(t, k).start()
        return c

    lax.fori_loop(0, tm, issue, 0)

    def drain(t, c):
        for k in range(TOP_K):
            row_copy(t, k).wait()
        return c

    lax.fori_loop(0, tm, drain, 0)


def _dispatch(x1, dest, p_rows, tm):
    n, d = x1.shape
    xs0 = jnp.zeros((p_rows, d), F32)
    return pl.pallas_call(
        functools.partial(_dispatch_kernel, tm=tm),
        grid=(n // tm,),
        in_specs=[pl.BlockSpec((TOP_K, tm), lambda i: (0, i), memory_space=pltpu.SMEM),
                  pl.BlockSpec((tm, d), lambda i: (i, 0)),
                  pl.BlockSpec(memory_space=pl.ANY)],
        out_specs=pl.BlockSpec(memory_space=pl.ANY),
        out_shape=jax.ShapeDtypeStruct((p_rows, d), F32),
        scratch_shapes=[pltpu.SemaphoreType.DMA(())],
        input_output_aliases={2: 0},
        compiler_params=_cparams(("arbitrary",)),
        name="dispatch",
    )(dest, x1, xs0)


def _expert_kernel(te_ref, blk_ref, nu_ref, xs_ref, wg_ref, wu_ref, wd_ref, ys_ref, wgb_ref, wub_ref, wdb_ref):
    i = pl.program_id(0)
    used = i < nu_ref[0]
    changed = jnp.logical_or(i == 0, te_ref[i] != te_ref[jnp.maximum(i - 1, 0)])

    @pl.when(jnp.logical_and(used, changed))
    def _():
        wgb_ref[...] = wg_ref[0].astype(BF16)
        wub_ref[...] = wu_ref[0].astype(BF16)
        wdb_ref[...] = wd_ref[0].astype(BF16)

    @pl.when(used)
    def _():
        xb = xs_ref[...].astype(BF16)
        hg = jnp.dot(xb, wgb_ref[...], preferred_element_type=F32)
        hu = jnp.dot(xb, wub_ref[...], preferred_element_type=F32)
        hh = (hg * jax.nn.sigmoid(hg)) * hu
        ys_ref[...] = jnp.dot(hh.astype(BF16), wdb_ref[...], preferred_element_type=F32)

    @pl.when(jnp.logical_not(used))
    def _():
        ys_ref[...] = jnp.zeros_like(ys_ref)


def _experts(xs, tile_expert, tile_block, n_used, w_gate, w_up, w_down):
    p_rows, d = xs.shape
    n_tiles = p_rows // EXPERT_TILE
    de = w_gate.shape[2]
    grid_spec = pltpu.PrefetchScalarGridSpec(
        num_scalar_prefetch=3,
        grid=(n_tiles,),
        in_specs=[pl.BlockSpec((EXPERT_TILE, d), lambda i, te, blk, nu: (blk[i], 0)),
                  pl.BlockSpec((1, d, de), lambda i, te, blk, nu: (te[i], 0, 0)),
                  pl.BlockSpec((1, d, de), lambda i, te, blk, nu: (te[i], 0, 0)),
                  pl.BlockSpec((1, de, d), lambda i, te, blk, nu: (te[i], 0, 0))],
        out_specs=pl.BlockSpec((EXPERT_TILE, d), lambda i, te, blk, nu: (i, 0)),
        scratch_shapes=[pltpu.VMEM((d, de), BF16), pltpu.VMEM((d, de), BF16), pltpu.VMEM((de, d), BF16)],
    )
    return pl.pallas_call(
        _expert_kernel,
        grid_spec=grid_spec,
        out_shape=jax.ShapeDtypeStruct((p_rows, d), F32),
        compiler_params=_cparams(("arbitrary",)),
        name="experts",
    )(tile_expert, tile_block, n_used, xs, w_gate, w_up, w_down)


def _combine_kernel(dest_ref, x1_ref, w_ref, ys_ref, sg_ref, su_ref, sd_ref, g_ref, b_ref, o_ref,
                    buf_ref, sem, *, tm):
    def row_copy(t, k):
        return pltpu.make_async_copy(ys_ref.at[pl.ds(dest_ref[k, t], 1)], buf_ref.at[k, pl.ds(t, 1)], sem)

    def issue(t, c):
        for k in range(TOP_K):
            row_copy(t, k).start()
        return c

    lax.fori_loop(0, tm, issue, 0)

    x1 = x1_ref[...]
    xb = x1.astype(BF16)
    hg = jnp.dot(xb, sg_ref[...], preferred_element_type=F32)
    hu = jnp.dot(xb, su_ref[...], preferred_element_type=F32)
    hh = (hg * jax.nn.sigmoid(hg)) * hu
    shared = jnp.dot(hh.astype(BF16), sd_ref[...], preferred_element_type=F32)

    def drain(t, c):
        for k in range(TOP_K):
            row_copy(t, k).wait()
        return c

    lax.fori_loop(0, tm, drain, 0)

    w = w_ref[...]
    routed = buf_ref[0] * w[:, 0:1]
    for k in range(1, TOP_K):
        routed = routed + buf_ref[k] * w[:, k:k + 1]
    o_ref[...] = _layer_norm(ALPHA * x1 + (routed + shared), g_ref[...], b_ref[...])


def _combine(x1, dest, w_tok, ys, s_gate, s_up, s_down, ln_g, ln_b, tm):
    n, d = x1.shape
    de = s_gate.shape[1]
    full = lambda shape: pl.BlockSpec(shape, lambda i: (0,) * len(shape))
    return pl.pallas_call(
        functools.partial(_combine_kernel, tm=tm),
        grid=(n // tm,),
        in_specs=[pl.BlockSpec((TOP_K, tm), lambda i: (0, i), memory_space=pltpu.SMEM),
                  pl.BlockSpec((tm, d), lambda i: (i, 0)),
                  pl.BlockSpec((tm, TOP_K), lambda i: (i, 0)),
                  pl.BlockSpec(memory_space=pl.ANY),
                  full((d, de)), full((d, de)), full((de, d)), full((1, d)), full((1, d))],
        out_specs=pl.BlockSpec((tm, d), lambda i: (i, 0)),
        out_shape=jax.ShapeDtypeStruct((n, d), F32),
        scratch_shapes=[pltpu.VMEM((TOP_K, tm, d), F32), pltpu.SemaphoreType.DMA(())],
        compiler_params=_cparams(("arbitrary",)),
        name="combine",
    )(dest, x1, w_tok, ys, s_gate.astype(BF16), s_up.astype(BF16), s_down.astype(BF16),
      ln_g.reshape(1, d), ln_b.reshape(1, d))


def _moe(x1, logits_t, router_bias, e_w_gate, e_w_up, e_w_down, s_w_gate, s_w_up, s_w_down, ln_g, ln_b):
    n, d = x1.shape
    e = N_EXPERTS
    top_e, w_t, rank, counts = _route(logits_t, router_bias, tm=512)
    counts = counts[:, 0]
    padded = (counts + EXPERT_TILE - 1) // EXPERT_TILE * EXPERT_TILE
    pad_ends = jnp.cumsum(padded)
    pad_starts = pad_ends - padded
    n_tiles = -(-(n * TOP_K + e * (EXPERT_TILE - 1)) // EXPERT_TILE)
    n_used = (pad_ends[-1] // EXPERT_TILE).astype(I32)
    tile_ids = jnp.arange(n_tiles, dtype=I32)
    tile_block = jnp.minimum(tile_ids, n_used - 1)
    tile_expert = jnp.minimum(
        jnp.searchsorted(pad_ends, tile_block * EXPERT_TILE, side="right"), e - 1).astype(I32)
    dest = _dest(top_e, rank, pad_starts.astype(I32), tm=1024)
    xs = _dispatch(x1, dest, n_tiles * EXPERT_TILE, tm=128)
    ys = _experts(xs, tile_expert, tile_block, n_used.reshape(1), e_w_gate, e_w_up, e_w_down)
    return _combine(x1, dest, w_t.T, ys, s_w_gate, s_w_up, s_w_down, ln_g, ln_b, tm=128)


def _layer(x, w_in, b_gate, m_conv_w, m_conv_b, m_wq, m_wk, m_wv, m_w_if, m_b_if, m_norm_w, m_skip,
           r_conv_w, r_conv_b, r_wa, r_ba, r_wx, r_bx, r_lambda, w_pm, w_pr, w_o, ln1_g, ln1_b,
           router_w, router_bias, e_w_gate, e_w_up, e_w_down, s_w_gate, s_w_up, s_w_down, ln2_g, ln2_b):
    batch, seq, d = x.shape
    n = batch * seq
    xt = x.reshape(n, d)
    proj = _matmul(xt.astype(BF16), w_in.astype(BF16), 1024, 1024, F32)
    o_z = M_WIDTH
    o_xr = 2 * M_WIDTH
    o_xg = o_xr + R_WIDTH
    o_gate = o_xg + R_WIDTH
    del o_z
    xc, q, k, v, g, gt = _mlstm_prep(proj, seq, m_conv_w, m_conv_b, m_wq, m_wk, m_wv, m_w_if, m_b_if, tm=256)
    y_m = _mlstm(q, k, v, g, gt, xc, proj, m_norm_w, m_skip, batch, seq, MLSTM_CHUNK)
    y_r = _rglru(proj, batch, seq, o_xr, o_xg, r_conv_w, r_conv_b, r_wa, r_ba, r_wx, r_bx, r_lambda,
                 tm=512, cw=512)
    merged = _merge(y_m, y_r, w_pm, w_pr, proj, o_gate, b_gate, tm=512, bn=1024)
    x1, logits_t = _oproj(merged, w_o, xt, ln1_g, ln1_b, router_w, tm=256)
    out = _moe(x1, logits_t, router_bias, e_w_gate, e_w_up, e_w_down, s_w_gate, s_w_up, s_w_down, ln2_g, ln2_b)
    return out.reshape(batch, seq, d)


def kernel(x, w_in, b_gate, m_conv_w, m_conv_b, m_wq, m_wk, m_wv, m_w_if, m_b_if, m_norm_w, m_skip, r_conv_w, r_conv_b, r_wa, r_ba, r_wx, r_bx, r_lambda, w_pm, w_pr, w_o, ln1_g, ln1_b, router_w, router_bias, e_w_gate, e_w_up, e_w_down, s_w_gate, s_w_up, s_w_down, ln2_g, ln2_b):
    for l in range(DEPTH):
        x = _layer(x, w_in[l], b_gate[l], m_conv_w[l], m_conv_b[l], m_wq[l], m_wk[l], m_wv[l],
                   m_w_if[l], m_b_if[l], m_norm_w[l], m_skip[l], r_conv_w[l], r_conv_b[l],
                   r_wa[l], r_ba[l], r_wx[l], r_bx[l], r_lambda[l], w_pm[l], w_pr[l], w_o[l],
                   ln1_g[l], ln1_b[l], router_w[l], router_bias[l], e_w_gate[l], e_w_up[l],
                   e_w_down[l], s_w_gate[l], s_w_up[l], s_w_down[l], ln2_g[l], ln2_b[l])
    return x
```

```python
import functools
import math

import jax
import jax.numpy as jnp
from jax import lax
from jax.experimental import pallas as pl
from jax.experimental.pallas import tpu as pltpu

F32 = jnp.float32
BF16 = jnp.bfloat16
I32 = jnp.int32
U32 = jnp.uint32

D_MODEL = 2048
M_WIDTH = 2048
M_HEADS = 8
M_HEAD_DIM = 256
M_QKV_BLOCK = 4
CONV_WIDTH = 4
R_WIDTH = 2560
R_BLOCK = 256
LRU_C = 8.0
N_EXPERTS = 64
TOP_K = 8
N_GROUPS = 8
TOPK_GROUPS = 4
D_EXPERT = 512
ROUTED_SCALE = 2.5
DEPTH = 1
ALPHA = (2.0 * DEPTH) ** 0.25
LN_EPS = 1e-5

V7X_VMEM_LIMIT = 56 * 1024 * 1024
HALO = 8
MLSTM_CHUNK = 256
EXPERT_TILE = 256
NEG_INF = float("-inf")


def _cparams(sem, vmem=V7X_VMEM_LIMIT):
    return pltpu.CompilerParams(dimension_semantics=sem, vmem_limit_bytes=vmem)


def _mm_kernel(a_ref, b_ref, o_ref):
    o_ref[...] = jnp.dot(a_ref[...], b_ref[...], preferred_element_type=F32).astype(o_ref.dtype)


def _matmul(a, b, bm, bn, out_dtype):
    m, k = a.shape
    n = b.shape[1]
    return pl.pallas_call(
        _mm_kernel,
        grid=(m // bm, n // bn),
        in_specs=[pl.BlockSpec((bm, k), lambda i, j: (i, 0)),
                  pl.BlockSpec((k, bn), lambda i, j: (0, j))],
        out_specs=pl.BlockSpec((bm, bn), lambda i, j: (i, j)),
        out_shape=jax.ShapeDtypeStruct((m, n), out_dtype),
        compiler_params=_cparams(("parallel", "parallel")),
        name="in_proj",
    )(a, b)


def _log_sigmoid(x):
    return jnp.minimum(x, 0.0) - jnp.log1p(jnp.exp(-jnp.abs(x)))


def _causal_conv(xp_ref, halo, x, cw_ref, cb_ref, tm):
    xp_ref[0:HALO, :] = halo
    xp_ref[HALO:HALO + tm, :] = x
    base = HALO - (CONV_WIDTH - 1)
    y = cb_ref[...] + xp_ref[base:base + tm, :] * cw_ref[0:1, :]
    for j in range(1, CONV_WIDTH):
        y = y + xp_ref[base + j:base + j + tm, :] * cw_ref[j:j + 1, :]
    return y


def _mprep_kernel(xm_ref, halo_ref, cw_ref, cb_ref, wq_ref, wk_ref, wv_ref, wif_ref, wift_ref,
                  bif_ref, bift_ref, xc_ref, q_ref, k_ref, v_ref, g_ref, gt_ref, xp_ref,
                  *, tm, tiles_per_seq):
    i = pl.program_id(0)
    first = (i % tiles_per_seq) == 0
    halo = jnp.where(first, 0.0, halo_ref[...])
    xm = xm_ref[...]
    y = _causal_conv(xp_ref, halo, xm, cw_ref, cb_ref, tm)
    xc = y * jax.nn.sigmoid(y)
    xc_ref[...] = xc
    xcb = xc.astype(BF16)
    xmb = xm.astype(BF16)
    nblk = M_WIDTH // M_HEAD_DIM
    for g in range(nblk):
        sl = slice(g * M_HEAD_DIM, (g + 1) * M_HEAD_DIM)
        q_ref[:, sl] = jnp.dot(xcb[:, sl], wq_ref[g], preferred_element_type=F32).astype(BF16)
        k_ref[:, sl] = jnp.dot(xcb[:, sl], wk_ref[g], preferred_element_type=F32).astype(BF16)
        v_ref[:, sl] = jnp.dot(xmb[:, sl], wv_ref[g], preferred_element_type=F32).astype(BF16)
    qb, kb, vb = q_ref[...], k_ref[...], v_ref[...]
    w = M_WIDTH
    g = (jnp.dot(qb, wif_ref[0:w, :], preferred_element_type=F32)
         + jnp.dot(kb, wif_ref[w:2 * w, :], preferred_element_type=F32)
         + jnp.dot(vb, wif_ref[2 * w:3 * w, :], preferred_element_type=F32) + bif_ref[...])
    nt = (((1,), (1,)), ((), ()))
    gt = (lax.dot_general(wift_ref[:, 0:w], qb, nt, preferred_element_type=F32)
          + lax.dot_general(wift_ref[:, w:2 * w], kb, nt, preferred_element_type=F32)
          + lax.dot_general(wift_ref[:, 2 * w:3 * w], vb, nt, preferred_element_type=F32) + bift_ref[...])
    col = lax.broadcasted_iota(I32, g.shape, 1)
    g_ref[...] = jnp.where(col >= M_HEADS, _log_sigmoid(g), g)
    row = lax.broadcasted_iota(I32, gt.shape, 0)
    gt_ref[...] = jnp.where(row >= M_HEADS, _log_sigmoid(gt), gt)


def _block_diag_dense(w, group):
    nb, bi, bo = w.shape
    per = group // bi
    w4 = w.reshape(nb // per, per, bi, bo)
    eye = jnp.eye(per, dtype=w.dtype)
    return jnp.einsum("gaio,ab->gaibo", w4, eye).reshape(nb // per, group, group)


def _mlstm_prep(proj, seq, conv_w, conv_b, wq, wk, wv, w_if, b_if, tm):
    n = proj.shape[0]
    c = M_WIDTH
    nblk = c // M_HEAD_DIM
    tiles_per_seq = seq // tm
    wqd = _block_diag_dense(wq, M_HEAD_DIM).astype(BF16)
    wkd = _block_diag_dense(wk, M_HEAD_DIM).astype(BF16)
    wvd = _block_diag_dense(wv, M_HEAD_DIM).astype(BF16)
    wif = w_if.astype(BF16)
    wift = w_if.T.astype(BF16)
    ng = 2 * M_HEADS
    hb = tm // HALO
    full = lambda shape: pl.BlockSpec(shape, lambda i: (0,) * len(shape))
    return pl.pallas_call(
        functools.partial(_mprep_kernel, tm=tm, tiles_per_seq=tiles_per_seq),
        grid=(n // tm,),
        in_specs=[pl.BlockSpec((tm, c), lambda i: (i, 0)),
                  pl.BlockSpec((HALO, c), lambda i: (jnp.maximum(i * hb - 1, 0), 0)),
                  full((CONV_WIDTH, c)), full((1, c)),
                  full((nblk, M_HEAD_DIM, M_HEAD_DIM)), full((nblk, M_HEAD_DIM, M_HEAD_DIM)),
                  full((nblk, M_HEAD_DIM, M_HEAD_DIM)),
                  full((3 * c, ng)), full((ng, 3 * c)), full((1, ng)), full((ng, 1))],
        out_specs=[pl.BlockSpec((tm, c), lambda i: (i, 0)),
                   pl.BlockSpec((tm, c), lambda i: (i, 0)),
                   pl.BlockSpec((tm, c), lambda i: (i, 0)),
                   pl.BlockSpec((tm, c), lambda i: (i, 0)),
                   pl.BlockSpec((tm, ng), lambda i: (i, 0)),
                   pl.BlockSpec((ng, tm), lambda i: (0, i))],
        out_shape=[jax.ShapeDtypeStruct((n, c), F32),
                   jax.ShapeDtypeStruct((n, c), BF16),
                   jax.ShapeDtypeStruct((n, c), BF16),
                   jax.ShapeDtypeStruct((n, c), BF16),
                   jax.ShapeDtypeStruct((n, ng), F32),
                   jax.ShapeDtypeStruct((ng, n), F32)],
        scratch_shapes=[pltpu.VMEM((HALO + tm, c), F32)],
        compiler_params=_cparams(("parallel",)),
        name="mlstm_prep",
    )(proj, proj, conv_w, conv_b.reshape(1, c), wqd, wkd, wvd, wif, wift,
      b_if.reshape(1, ng), b_if.reshape(ng, 1))


def _mlstm_kernel(q_ref, k_ref, v_ref, g_ref, gt_ref, xc_ref, z_ref, nw_ref, sk_ref, o_ref,
                  c_ref, n_ref, m_ref, *, chunk):
    L = chunk
    hd = M_HEAD_DIM

    @pl.when(pl.program_id(1) == 0)
    def _():
        c_ref[...] = jnp.zeros_like(c_ref)
        n_ref[...] = jnp.zeros_like(n_ref)
        m_ref[...] = jnp.zeros_like(m_ref)

    rows = lax.broadcasted_iota(I32, (L, L), 0)
    cols = lax.broadcasted_iota(I32, (L, L), 1)
    causal = cols <= rows
    tril = jnp.where(causal, 1.0, 0.0).astype(F32)
    triu = jnp.where(rows <= cols, 1.0, 0.0).astype(F32)
    g = g_ref[...]
    gt = gt_ref[...]
    hi = lax.Precision.HIGHEST
    bcol_all = jnp.dot(tril, g, precision=hi, preferred_element_type=F32)
    brow_all = jnp.dot(gt, triu, precision=hi, preferred_element_type=F32)
    k_scale = hd ** -0.5
    nt = (((1,), (1,)), ((), ()))
    tn = (((0,), (0,)), ((), ()))

    for h in range(M_HEADS):
        sl = slice(h * hd, (h + 1) * hd)
        qh = q_ref[:, sl]
        kh = k_ref[:, sl]
        vh = v_ref[:, sl]
        i_col = g[:, h:h + 1]
        b_col = bcol_all[:, M_HEADS + h:M_HEADS + h + 1]
        i_row = gt[h:h + 1, :]
        b_row = brow_all[M_HEADS + h:M_HEADS + h + 1, :]
        m_prev = m_ref[h, 0:1, 0:1]
        c_prev = c_ref[h]
        n_prev = n_ref[h]

        dmat = jnp.where(causal, b_col - b_row + i_row, NEG_INF)
        a_col = b_col + m_prev
        m_row = jnp.maximum(a_col, jnp.max(dmat, axis=1, keepdims=True))
        s = lax.dot_general(qh, kh, nt, preferred_element_type=F32) * k_scale
        s = s * jnp.exp(dmat - m_row)
        inter = jnp.exp(a_col - m_row)
        num = inter * jnp.dot(qh, c_prev.astype(BF16), preferred_element_type=F32) \
            + jnp.dot(s.astype(BF16), vh, preferred_element_type=F32)
        qn = jnp.sum(qh.astype(F32) * n_prev, axis=1, keepdims=True)
        den = inter * qn + jnp.sum(s, axis=1, keepdims=True)
        hval = num * (1.0 / jnp.maximum(jnp.abs(den), jnp.exp(-m_row)))

        mu = jnp.mean(hval, axis=1, keepdims=True)
        cen = hval - mu
        var = jnp.mean(cen * cen, axis=1, keepdims=True)
        hn = cen * lax.rsqrt(var + LN_EPS) * nw_ref[:, sl]
        zz = z_ref[:, sl]
        o_ref[:, sl] = ((hn + sk_ref[:, sl] * xc_ref[:, sl]) * (zz * jax.nn.sigmoid(zz))).astype(o_ref.dtype)

        b_last = b_col[L - 1:L, :]
        w_log = b_last - b_col + i_col
        m_new = jnp.maximum(b_last + m_prev, jnp.max(w_log, axis=0, keepdims=True))
        decay = jnp.exp(b_last + m_prev - m_new)
        kw = kh.astype(F32) * (jnp.exp(w_log - m_new) * k_scale)
        c_ref[h] = decay * c_prev + lax.dot_general(kw.astype(BF16), vh, tn, preferred_element_type=F32)
        n_ref[h] = decay * n_prev + jnp.sum(kw, axis=0, keepdims=True)
        m_ref[h] = jnp.broadcast_to(m_new, m_ref.shape[1:])


def _mlstm(q, k, v, g, gt, xc, proj, norm_w, skip, batch, seq, chunk):
    n, c = q.shape
    nc = seq // chunk
    ng = 2 * M_HEADS
    zcol = M_WIDTH // c
    row = lambda b, j: (b * nc + j, 0)
    return pl.pallas_call(
        functools.partial(_mlstm_kernel, chunk=chunk),
        grid=(batch, nc),
        in_specs=[pl.BlockSpec((chunk, c), row), pl.BlockSpec((chunk, c), row), pl.BlockSpec((chunk, c), row),
                  pl.BlockSpec((chunk, ng), row),
                  pl.BlockSpec((ng, chunk), lambda b, j: (0, b * nc + j)),
                  pl.BlockSpec((chunk, c), row),
                  pl.BlockSpec((chunk, c), lambda b, j: (b * nc + j, zcol)),
                  pl.BlockSpec((1, c), lambda b, j: (0, 0)),
                  pl.BlockSpec((1, c), lambda b, j: (0, 0))],
        out_specs=pl.BlockSpec((chunk, c), row),
        out_shape=jax.ShapeDtypeStruct((n, c), BF16),
        scratch_shapes=[pltpu.VMEM((M_HEADS, M_HEAD_DIM, M_HEAD_DIM), F32),
                        pltpu.VMEM((M_HEADS, 1, M_HEAD_DIM), F32),
                        pltpu.VMEM((M_HEADS, 8, 128), F32)],
        compiler_params=_cparams(("parallel", "arbitrary")),
        name="mlstm_chunk",
    )(q, k, v, g, gt, xc, proj, norm_w.reshape(1, c), skip.reshape(1, c))


def _gelu_tanh(x):
    return 0.5 * x * (1.0 + jnp.tanh(math.sqrt(2.0 / math.pi) * (x + 0.044715 * (x * x * x))))


def _rglru_kernel(xr_ref, halo_ref, xg_ref, cw_ref, cb_ref, wa_ref, ba_ref, wx_ref, bx_ref, lam_ref,
                  o_ref, xp_ref, a_ref, b_ref, h_ref, *, tm, cw):
    t = pl.program_id(2)

    @pl.when(t == 0)
    def _():
        h_ref[...] = jnp.zeros_like(h_ref)

    halo = jnp.where(t == 0, 0.0, halo_ref[...])
    xc = _causal_conv(xp_ref, halo, xr_ref[...], cw_ref, cb_ref, tm)
    xcb = xc.astype(BF16)
    nblk = cw // R_BLOCK
    ra = []
    rx = []
    for g in range(nblk):
        sl = slice(g * R_BLOCK, (g + 1) * R_BLOCK)
        ra.append(jnp.dot(xcb[:, sl], wa_ref[g], preferred_element_type=F32))
        rx.append(jnp.dot(xcb[:, sl], wx_ref[g], preferred_element_type=F32))
    r = jax.nn.sigmoid(jnp.concatenate(ra, axis=1) + ba_ref[...])
    ig = jax.nn.sigmoid(jnp.concatenate(rx, axis=1) + bx_ref[...])
    nl = -lam_ref[...]
    softplus = jnp.maximum(nl, 0.0) + jnp.log1p(jnp.exp(-jnp.abs(nl)))
    log_a = (-LRU_C * softplus) * r
    a = jnp.exp(log_a)
    b = jnp.sqrt(1.0 - jnp.exp(2.0 * log_a)) * (ig * xc)

    sub = lax.broadcasted_iota(I32, (tm, cw), 0) % 8
    for d in (1, 2, 4):
        keep = sub >= d
        a_sh = pltpu.roll(a, d, axis=0)
        b_sh = pltpu.roll(b, d, axis=0)
        b = jnp.where(keep, a * b_sh + b, b)
        a = jnp.where(keep, a * a_sh, a)
    a_ref[...] = a
    b_ref[...] = b

    def body(g, h):
        r0 = pl.multiple_of(g * 8, 8)
        hh = b_ref[pl.ds(r0, 8), :] + a_ref[pl.ds(r0, 8), :] * h
        b_ref[pl.ds(r0, 8), :] = hh
        return hh[7:8, :]

    h_ref[...] = lax.fori_loop(0, tm // 8, body, h_ref[...], unroll=8)
    o_ref[...] = (b_ref[...] * _gelu_tanh(xg_ref[...])).astype(o_ref.dtype)


def _rglru(proj, batch, seq, xr_off, xg_off, conv_w, conv_b, wa, ba, wx, bx, lam, tm, cw):
    n = proj.shape[0]
    ncol = R_WIDTH // cw
    nt = seq // tm
    per = cw // R_BLOCK
    xr_cb = xr_off // cw
    xg_cb = xg_off // cw
    hb = tm // HALO
    colv = lambda shape: pl.BlockSpec(shape, lambda b, j, t: (0, j))
    return pl.pallas_call(
        functools.partial(_rglru_kernel, tm=tm, cw=cw),
        grid=(batch, ncol, nt),
        in_specs=[pl.BlockSpec((tm, cw), lambda b, j, t: (b * nt + t, xr_cb + j)),
                  pl.BlockSpec((HALO, cw), lambda b, j, t: (jnp.maximum((b * nt + t) * hb - 1, 0), xr_cb + j)),
                  pl.BlockSpec((tm, cw), lambda b, j, t: (b * nt + t, xg_cb + j)),
                  colv((CONV_WIDTH, cw)), colv((1, cw)),
                  pl.BlockSpec((per, R_BLOCK, R_BLOCK), lambda b, j, t: (j, 0, 0)), colv((1, cw)),
                  pl.BlockSpec((per, R_BLOCK, R_BLOCK), lambda b, j, t: (j, 0, 0)), colv((1, cw)),
                  colv((1, cw))],
        out_specs=pl.BlockSpec((tm, cw), lambda b, j, t: (b * nt + t, j)),
        out_shape=jax.ShapeDtypeStruct((n, R_WIDTH), BF16),
        scratch_shapes=[pltpu.VMEM((HALO + tm, cw), F32), pltpu.VMEM((tm, cw), F32),
                        pltpu.VMEM((tm, cw), F32), pltpu.VMEM((1, cw), F32)],
        compiler_params=_cparams(("parallel", "parallel", "arbitrary")),
        name="rglru",
    )(proj, proj, proj, conv_w, conv_b.reshape(1, R_WIDTH), wa.astype(BF16), ba.reshape(1, R_WIDTH),
      wx.astype(BF16), bx.reshape(1, R_WIDTH), lam.reshape(1, R_WIDTH))


def _merge_kernel(ym_ref, yr_ref, wpm_ref, wpr_ref, g0_ref, g1_ref, bg_ref, o_ref):
    g0 = jax.nn.sigmoid(g0_ref[...] + bg_ref[0:1, :])
    g1 = jax.nn.sigmoid(g1_ref[...] + bg_ref[1:2, :])
    pm = jnp.dot(ym_ref[...], wpm_ref[...], preferred_element_type=F32)
    pr = jnp.dot(yr_ref[...], wpr_ref[...], preferred_element_type=F32)
    o_ref[...] = (g0 * pm + g1 * pr).astype(o_ref.dtype)


def _merge(ym, yr, w_pm, w_pr, proj, gate_off, b_gate, tm, bn):
    n = ym.shape[0]
    d = w_pm.shape[1]
    g0_cb = gate_off // bn
    g1_cb = (gate_off + d) // bn
    return pl.pallas_call(
        _merge_kernel,
        grid=(d // bn, n // tm),
        in_specs=[pl.BlockSpec((tm, ym.shape[1]), lambda j, i: (i, 0)),
                  pl.BlockSpec((tm, yr.shape[1]), lambda j, i: (i, 0)),
                  pl.BlockSpec((w_pm.shape[0], bn), lambda j, i: (0, j)),
                  pl.BlockSpec((w_pr.shape[0], bn), lambda j, i: (0, j)),
                  pl.BlockSpec((tm, bn), lambda j, i: (i, g0_cb + j)),
                  pl.BlockSpec((tm, bn), lambda j, i: (i, g1_cb + j)),
                  pl.BlockSpec((2, bn), lambda j, i: (0, j))],
        out_specs=pl.BlockSpec((tm, bn), lambda j, i: (i, j)),
        out_shape=jax.ShapeDtypeStruct((n, d), BF16),
        compiler_params=_cparams(("parallel", "parallel")),
        name="merge",
    )(ym, yr, w_pm.astype(BF16), w_pr.astype(BF16), proj, proj, b_gate)


def _layer_norm(y, g, b):
    mu = jnp.mean(y, axis=1, keepdims=True)
    cen = y - mu
    var = jnp.mean(cen * cen, axis=1, keepdims=True)
    return cen * lax.rsqrt(var + LN_EPS) * g + b


def _oproj_kernel(mg_ref, wo_ref, x_ref, g_ref, b_ref, rwt_ref, x1_ref, lt_ref):
    y = ALPHA * x_ref[...] + jnp.dot(mg_ref[...], wo_ref[...], preferred_element_type=F32)
    x1 = _layer_norm(y, g_ref[...], b_ref[...])
    x1_ref[...] = x1
    lt_ref[...] = lax.dot_general(rwt_ref[...], x1, (((1,), (1,)), ((), ())),
                                  precision=lax.Precision.HIGHEST, preferred_element_type=F32)


def _oproj(merged, w_o, x, ln_g, ln_b, router_w, tm):
    n, d = x.shape
    e = router_w.shape[1]
    full = lambda shape: pl.BlockSpec(shape, lambda i: (0,) * len(shape))
    return pl.pallas_call(
        _oproj_kernel,
        grid=(n // tm,),
        in_specs=[pl.BlockSpec((tm, d), lambda i: (i, 0)), full((d, d)),
                  pl.BlockSpec((tm, d), lambda i: (i, 0)), full((1, d)), full((1, d)), full((e, d))],
        out_specs=[pl.BlockSpec((tm, d), lambda i: (i, 0)), pl.BlockSpec((e, tm), lambda i: (0, i))],
        out_shape=[jax.ShapeDtypeStruct((n, d), F32), jax.ShapeDtypeStruct((e, n), F32)],
        compiler_params=_cparams(("parallel",)),
        name="out_proj_ln",
    )(merged, w_o.astype(BF16), x, ln_g.reshape(1, d), ln_b.reshape(1, d), router_w.T)


def _first_max(v, idx, sentinel):
    m = jnp.max(v, axis=0, keepdims=True)
    am = jnp.min(jnp.where(v == m, idx, sentinel), axis=0, keepdims=True)
    return m, am


def _route_kernel(lt_ref, bias_ref, tri_ref, te_ref, w_ref, rk_ref, cnt_ref, carry_ref, *, tm):
    @pl.when(pl.program_id(0) == 0)
    def _():
        carry_ref[...] = jnp.zeros_like(carry_ref)

    e = N_EXPERTS
    gs = e // N_GROUPS
    scores = jax.nn.sigmoid(lt_ref[...])
    biased = scores + bias_ref[...]
    sub = lax.broadcasted_iota(I32, (gs, tm), 0)
    grp_rows = []
    for g in range(N_GROUPS):
        slab = biased[g * gs:(g + 1) * gs, :]
        m1, a1 = _first_max(slab, sub, gs)
        m2 = jnp.max(jnp.where(sub == a1, NEG_INF, slab), axis=0, keepdims=True)
        grp_rows.append(m1 + m2)
    grp = jnp.concatenate(grp_rows, axis=0)
    gidx = lax.broadcasted_iota(I32, (N_GROUPS, tm), 0)
    gsel = jnp.zeros((N_GROUPS, tm), F32)
    for _ in range(TOPK_GROUPS):
        _, am = _first_max(grp, gidx, N_GROUPS)
        hit = gidx == am
        gsel = jnp.where(hit, 1.0, gsel)
        grp = jnp.where(hit, NEG_INF, grp)
    masked = jnp.concatenate(
        [jnp.where(gsel[g:g + 1, :] > 0.0, biased[g * gs:(g + 1) * gs, :], NEG_INF) for g in range(N_GROUPS)],
        axis=0)
    eidx = lax.broadcasted_iota(I32, (e, tm), 0)
    member = jnp.zeros((e, tm), F32)
    tops = []
    ws = []
    for _ in range(TOP_K):
        _, am = _first_max(masked, eidx, e)
        hit = eidx == am
        tops.append(am)
        ws.append(jnp.sum(jnp.where(hit, scores, 0.0), axis=0, keepdims=True))
        member = jnp.where(hit, 1.0, member)
        masked = jnp.where(hit, NEG_INF, masked)
    wsum = ws[0]
    for k in range(1, TOP_K):
        wsum = wsum + ws[k]
    te_ref[...] = jnp.concatenate(tops, axis=0)
    w_ref[...] = jnp.concatenate(ws, axis=0) / wsum * ROUTED_SCALE

    cum = jnp.dot(member.astype(BF16), tri_ref[...], preferred_element_type=F32)
    carry = carry_ref[:, 0:1]
    rank = carry + cum - member
    rks = []
    for k in range(TOP_K):
        rks.append(jnp.sum(jnp.where(eidx == tops[k], rank, 0.0), axis=0, keepdims=True))
    rk_ref[...] = jnp.concatenate(rks, axis=0).astype(I32)
    new_carry = carry + cum[:, tm - 1:tm]
    carry_ref[...] = jnp.broadcast_to(new_carry, carry_ref.shape)
    cnt_ref[...] = jnp.broadcast_to(new_carry, cnt_ref.shape).astype(I32)


def _route(logits_t, router_bias, tm):
    e, n = logits_t.shape
    tri = jnp.triu(jnp.ones((tm, tm), F32)).astype(BF16)
    return pl.pallas_call(
        functools.partial(_route_kernel, tm=tm),
        grid=(n // tm,),
        in_specs=[pl.BlockSpec((e, tm), lambda i: (0, i)),
                  pl.BlockSpec((e, 1), lambda i: (0, 0)),
                  pl.BlockSpec((tm, tm), lambda i: (0, 0))],
        out_specs=[pl.BlockSpec((TOP_K, tm), lambda i: (0, i)),
                   pl.BlockSpec((TOP_K, tm), lambda i: (0, i)),
                   pl.BlockSpec((TOP_K, tm), lambda i: (0, i)),
                   pl.BlockSpec((e, 128), lambda i: (0, 0))],
        out_shape=[jax.ShapeDtypeStruct((TOP_K, n), I32), jax.ShapeDtypeStruct((TOP_K, n), F32),
                   jax.ShapeDtypeStruct((TOP_K, n), I32), jax.ShapeDtypeStruct((e, 128), I32)],
        scratch_shapes=[pltpu.VMEM((e, 128), F32)],
        compiler_params=_cparams(("arbitrary",)),
        name="route",
    )(logits_t, router_bias.reshape(e, 1), tri)


def _dest_kernel(te_ref, rk_ref, ps_ref, d_ref):
    te = te_ref[...]
    e = N_EXPERTS
    tm = te.shape[1]
    eidx = lax.broadcasted_iota(I32, (e, tm), 0)
    ps = ps_ref[...]
    rows = []
    for k in range(TOP_K):
        rows.append(jnp.sum(jnp.where(eidx == te[k:k + 1, :], ps, 0), axis=0, keepdims=True))
    d_ref[...] = jnp.concatenate(rows, axis=0) + rk_ref[...]


def _dest(top_e, rank, pad_starts, tm):
    n = top_e.shape[1]
    return pl.pallas_call(
        _dest_kernel,
        grid=(n // tm,),
        in_specs=[pl.BlockSpec((TOP_K, tm), lambda i: (0, i)),
                  pl.BlockSpec((TOP_K, tm), lambda i: (0, i)),
                  pl.BlockSpec((N_EXPERTS, 1), lambda i: (0, 0))],
        out_specs=pl.BlockSpec((TOP_K, tm), lambda i: (0, i)),
        out_shape=jax.ShapeDtypeStruct((TOP_K, n), I32),
        compiler_params=_cparams(("parallel",)),
        name="dest",
    )(top_e, rank, pad_starts.reshape(N_EXPERTS, 1))


def _invert_kernel(dest_ref, inv_ref):
    def body(j, c):
        inv_ref[dest_ref[j]] = j
        return c

    lax.fori_loop(0, dest_ref.shape[0], body, 0, unroll=8)


def _invert_permutation(dest_flat):
    m = dest_flat.shape[0]
    return pl.pallas_call(
        _invert_kernel,
        in_specs=[pl.BlockSpec(memory_space=pltpu.SMEM)],
        out_specs=pl.BlockSpec(memory_space=pltpu.SMEM),
        out_shape=jax.ShapeDtypeStruct((m,), I32),
        name="invert_perm",
    )(dest_flat)


def _expert_ffn(x, wgb_ref, wub_ref, wdb_ref):
    xb = x.astype(BF16)
    hg = jnp.dot(xb, wgb_ref[...], preferred_element_type=F32)
    hu = jnp.dot(xb, wub_ref[...], preferred_element_type=F32)
    hh = (hg * jax.nn.sigmoid(hg)) * hu
    return jnp.dot(hh.astype(BF16), wdb_ref[...], preferred_element_type=F32)


def _expert_kernel(vt_ref, ve_ref, lo_ref, hi_ref, inv_ref, x_hbm, wg_ref, wu_ref, wd_ref, out_hbm,
                   wgb_ref, wub_ref, wdb_ref, xbuf, obuf, cur_ref, gsem, ssem, *, n_tok, n_tiles):
    v = pl.program_id(0)
    rows = xbuf.shape[1]
    tile = vt_ref[v]
    lo = lo_ref[v]
    hi = hi_ref[v]
    nonempty = hi > lo
    first = jnp.logical_and(nonempty, lo == 0)
    slot = tile % 2
    other = 1 - slot

    def gather_row(t, s, r):
        tok = inv_ref[t * rows + r] & (n_tok - 1)
        return pltpu.make_async_copy(x_hbm.at[pl.ds(tok, 1)], xbuf.at[s, pl.ds(r, 1)], gsem.at[s])

    def scatter_row(t, s, r):
        j = inv_ref[t * rows + r]
        return pltpu.make_async_copy(obuf.at[s, pl.ds(r, 1)], out_hbm.at[pl.ds(j, 1)], ssem.at[s])

    def wait_gathered_tile(s):
        pltpu.make_async_copy(x_hbm.at[pl.ds(0, rows)], xbuf.at[s], gsem.at[s]).wait()

    def wait_scattered_tile(s):
        pltpu.make_async_copy(obuf.at[s], out_hbm.at[pl.ds(0, rows)], ssem.at[s]).wait()

    @pl.when(v == 0)
    def _():
        cur_ref[0] = -1

    @pl.when(jnp.logical_and(nonempty, cur_ref[0] != ve_ref[v]))
    def _():
        wgb_ref[...] = wg_ref[0].astype(BF16)
        wub_ref[...] = wu_ref[0].astype(BF16)
        wdb_ref[...] = wd_ref[0].astype(BF16)
        cur_ref[0] = ve_ref[v]

    @pl.when(jnp.logical_and(first, tile == 0))
    def _():
        def issue(r, c):
            gather_row(0, 0, r).start()
            return c

        lax.fori_loop(0, rows, issue, 0)

    @pl.when(jnp.logical_and(first, tile >= 2))
    def _():
        wait_scattered_tile(slot)

    @pl.when(first)
    def _():
        wait_gathered_tile(slot)

    nxt = jnp.minimum(tile + 1, n_tiles - 1)

    @pl.when(jnp.logical_and(first, tile == 0))
    def _():
        for r in range(rows):
            gather_row(nxt, other, r).start()
        obuf[slot] = _expert_ffn(xbuf[slot], wgb_ref, wub_ref, wdb_ref)

    @pl.when(jnp.logical_and(first, tile >= 1))
    def _():
        for r in range(rows):
            gather_row(nxt, other, r).start()
            scatter_row(tile - 1, other, r).start()
        obuf[slot] = _expert_ffn(xbuf[slot], wgb_ref, wub_ref, wdb_ref)

    @pl.when(jnp.logical_and(nonempty, lo > 0))
    def _():
        y = _expert_ffn(xbuf[slot], wgb_ref, wub_ref, wdb_ref)
        row = lax.broadcasted_iota(I32, (rows, 1), 0)
        mine = jnp.logical_and(row >= lo, row < hi)
        obuf[slot] = jnp.where(mine, y, obuf[slot])

    @pl.when(v == pl.num_programs(0) - 1)
    def _():
        last = n_tiles - 1
        ls = last % 2

        def issue(r, c):
            scatter_row(last, ls, r).start()
            return c

        lax.fori_loop(0, rows, issue, 0)
        wait_scattered_tile(1 - ls)
        wait_scattered_tile(ls)
        wait_gathered_tile(1 - ls)


def _experts(x1, inv, visit_tile, visit_expert, visit_lo, visit_hi, w_gate, w_up, w_down):
    n_tok, d = x1.shape
    m = inv.shape[0]
    de = w_gate.shape[2]
    n_tiles = m // EXPERT_TILE
    grid_spec = pltpu.PrefetchScalarGridSpec(
        num_scalar_prefetch=5,
        grid=(visit_tile.shape[0],),
        in_specs=[pl.BlockSpec(memory_space=pl.ANY),
                  pl.BlockSpec((1, d, de), lambda v, vt, ve, lo, hi, inv: (ve[v], 0, 0)),
                  pl.BlockSpec((1, d, de), lambda v, vt, ve, lo, hi, inv: (ve[v], 0, 0)),
                  pl.BlockSpec((1, de, d), lambda v, vt, ve, lo, hi, inv: (ve[v], 0, 0))],
        out_specs=pl.BlockSpec(memory_space=pl.ANY),
        scratch_shapes=[pltpu.VMEM((d, de), BF16), pltpu.VMEM((d, de), BF16), pltpu.VMEM((de, d), BF16),
                        pltpu.VMEM((2, EXPERT_TILE, d), F32), pltpu.VMEM((2, EXPERT_TILE, d), F32),
                        pltpu.SMEM((1,), I32), pltpu.SemaphoreType.DMA((2,)), pltpu.SemaphoreType.DMA((2,))],
    )
    return pl.pallas_call(
        functools.partial(_expert_kernel, n_tok=n_tok, n_tiles=n_tiles),
        grid_spec=grid_spec,
        out_shape=jax.ShapeDtypeStruct((m, d), F32),
        compiler_params=_cparams(("arbitrary",)),
        name="experts",
    )(visit_tile, visit_expert, visit_lo, visit_hi, inv, x1, w_gate, w_up, w_down)


def _combine_kernel(x1_ref, w_ref, y_ref, sg_ref, su_ref, sd_ref, g_ref, b_ref, o_ref):
    x1 = x1_ref[...]
    xb = x1.astype(BF16)
    hg = jnp.dot(xb, sg_ref[...], preferred_element_type=F32)
    hu = jnp.dot(xb, su_ref[...], preferred_element_type=F32)
    hh = (hg * jax.nn.sigmoid(hg)) * hu
    shared = jnp.dot(hh.astype(BF16), sd_ref[...], preferred_element_type=F32)
    w = w_ref[...]
    routed = y_ref[0] * w[:, 0:1]
    for k in range(1, TOP_K):
        routed = routed + y_ref[k] * w[:, k:k + 1]
    o_ref[...] = _layer_norm(ALPHA * x1 + (routed + shared), g_ref[...], b_ref[...])


def _combine(x1, w_tok, y8, s_gate, s_up, s_down, ln_g, ln_b, tm):
    n, d = x1.shape
    de = s_gate.shape[1]
    full = lambda shape: pl.BlockSpec(shape, lambda i: (0,) * len(shape))
    return pl.pallas_call(
        _combine_kernel,
        grid=(n // tm,),
        in_specs=[pl.BlockSpec((tm, d), lambda i: (i, 0)),
                  pl.BlockSpec((tm, TOP_K), lambda i: (i, 0)),
                  pl.BlockSpec((TOP_K, tm, d), lambda i: (0, i, 0)),
                  full((d, de)), full((d, de)), full((de, d)), full((1, d)), full((1, d))],
        out_specs=pl.BlockSpec((tm, d), lambda i: (i, 0)),
        out_shape=jax.ShapeDtypeStruct((n, d), F32),
        compiler_params=_cparams(("parallel",)),
        name="combine",
    )(x1, w_tok, y8, s_gate.astype(BF16), s_up.astype(BF16), s_down.astype(BF16),
      ln_g.reshape(1, d), ln_b.reshape(1, d))


def _visit_plan(counts, n_rows):
    e = counts.shape[0]
    n_tiles = n_rows // EXPERT_TILE
    ends = jnp.cumsum(counts)
    starts = ends - counts
    pos = jnp.sort(jnp.concatenate([jnp.arange(n_tiles, dtype=I32) * EXPERT_TILE, starts]))
    nxt = jnp.concatenate([pos[1:], jnp.full((1,), n_rows, I32)])
    tile = jnp.minimum(pos // EXPERT_TILE, n_tiles - 1)
    expert = jnp.minimum(jnp.sum((ends[None, :] <= pos[:, None]).astype(I32), axis=1), e - 1)
    return starts, tile, expert, pos - tile * EXPERT_TILE, nxt - tile * EXPERT_TILE


def _moe(x1, logits_t, router_bias, e_w_gate, e_w_up, e_w_down, s_w_gate, s_w_up, s_w_down, ln_g, ln_b):
    n, d = x1.shape
    top_e, w_t, rank, counts = _route(logits_t, router_bias, tm=512)
    starts, v_tile, v_expert, v_lo, v_hi = _visit_plan(counts[:, 0], n * TOP_K)
    dest = _dest(top_e, rank, starts, tm=1024)
    inv = _invert_permutation(dest.reshape(n * TOP_K))
    y8 = _experts(x1, inv, v_tile, v_expert, v_lo, v_hi, e_w_gate, e_w_up, e_w_down)
    return _combine(x1, w_t.T, y8.reshape(TOP_K, n, d), s_w_gate, s_w_up, s_w_down, ln_g, ln_b, tm=128)


def _layer(x, w_in, b_gate, m_conv_w, m_conv_b, m_wq, m_wk, m_wv, m_w_if, m_b_if, m_norm_w, m_skip,
           r_conv_w, r_conv_b, r_wa, r_ba, r_wx, r_bx, r_lambda, w_pm, w_pr, w_o, ln1_g, ln1_b,
           router_w, router_bias, e_w_gate, e_w_up, e_w_down, s_w_gate, s_w_up, s_w_down, ln2_g, ln2_b):
    batch, seq, d = x.shape
    n = batch * seq
    xt = x.reshape(n, d)
    proj = _matmul(xt.astype(BF16), w_in.astype(BF16), 1024, 1024, F32)
    o_z = M_WIDTH
    o_xr = 2 * M_WIDTH
    o_xg = o_xr + R_WIDTH
    o_gate = o_xg + R_WIDTH
    del o_z
    xc, q, k, v, g, gt = _mlstm_prep(proj, seq, m_conv_w, m_conv_b, m_wq, m_wk, m_wv, m_w_if, m_b_if, tm=256)
    y_m = _mlstm(q, k, v, g, gt, xc, proj, m_norm_w, m_skip, batch, seq, MLSTM_CHUNK)
    y_r = _rglru(proj, batch, seq, o_xr, o_xg, r_conv_w, r_conv_b, r_wa, r_ba, r_wx, r_bx, r_lambda,
                 tm=512, cw=512)
    merged = _merge(y_m, y_r, w_pm, w_pr, proj, o_gate, b_gate, tm=512, bn=1024)
    x1, logits_t = _oproj(merged, w_o, xt, ln1_g, ln1_b, router_w, tm=256)
    out = _moe(x1, logits_t, router_bias, e_w_gate, e_w_up, e_w_down, s_w_gate, s_w_up, s_w_down, ln2_g, ln2_b)
    return out.reshape(batch, seq, d)


def kernel(x, w_in, b_gate, m_conv_w, m_conv_b, m_wq, m_wk, m_wv, m_w_if, m_b_if, m_norm_w, m_skip, r_conv_w, r_conv_b, r_wa, r_ba, r_wx, r_bx, r_lambda, w_pm, w_pr, w_o, ln1_g, ln1_b, router_w, router_bias, e_w_gate, e_w_up, e_w_down, s_w_gate, s_w_up, s_w_down, ln2_g, ln2_b):
    for l in range(DEPTH):
        x = _layer(x, w_in[l], b_gate[l], m_conv_w[l], m_conv_b[l], m_wq[l], m_wk[l], m_wv[l],
                   m_w_if[l], m_b_if[l], m_norm_w[l], m_skip[l], r_conv_w[l], r_conv_b[l],
                   r_wa[l], r_ba[l], r_wx[l], r_bx[l], r_lambda[l], w_pm[l], w_pr[l], w_o[l],
                   ln1_g[l], ln1_b[l], router_w[l], router_bias[l], e_w_gate[l], e_w_up[l],
                   e_w_down[l], s_w_gate[l], s_w_up[l], s_w_down[l], ln2_g[l], ln2_b[l])
    return x
```

```python
import functools
import math

import jax
import jax.numpy as jnp
from jax import lax
from jax.experimental import pallas as pl
from jax.experimental.pallas import tpu as pltpu

F32 = jnp.float32
BF16 = jnp.bfloat16
I32 = jnp.int32
U32 = jnp.uint32

D_MODEL = 2048
M_WIDTH = 2048
M_HEADS = 8
M_HEAD_DIM = 256
M_QKV_BLOCK = 4
CONV_WIDTH = 4
R_WIDTH = 2560
R_BLOCK = 256
LRU_C = 8.0
N_EXPERTS = 64
TOP_K = 8
N_GROUPS = 8
TOPK_GROUPS = 4
D_EXPERT = 512
ROUTED_SCALE = 2.5
DEPTH = 1
ALPHA = (2.0 * DEPTH) ** 0.25
LN_EPS = 1e-5

V7X_VMEM_LIMIT = 56 * 1024 * 1024
HALO = 8
MLSTM_CHUNK = 256
EXPERT_TILE = 256
NEG_INF = float("-inf")


def _cparams(sem, vmem=V7X_VMEM_LIMIT):
    return pltpu.CompilerParams(dimension_semantics=sem, vmem_limit_bytes=vmem)


def _mm_kernel(a_ref, b_ref, o_ref):
    o_ref[...] = jnp.dot(a_ref[...], b_ref[...], preferred_element_type=F32).astype(o_ref.dtype)


def _matmul(a, b, bm, bn, out_dtype):
    m, k = a.shape
    n = b.shape[1]
    return pl.pallas_call(
        _mm_kernel,
        grid=(m // bm, n // bn),
        in_specs=[pl.BlockSpec((bm, k), lambda i, j: (i, 0)),
                  pl.BlockSpec((k, bn), lambda i, j: (0, j))],
        out_specs=pl.BlockSpec((bm, bn), lambda i, j: (i, j)),
        out_shape=jax.ShapeDtypeStruct((m, n), out_dtype),
        compiler_params=_cparams(("parallel", "parallel")),
        name="in_proj",
    )(a, b)


def _log_sigmoid(x):
    return jnp.minimum(x, 0.0) - jnp.log1p(jnp.exp(-jnp.abs(x)))


def _causal_conv(xp_ref, halo, x, cw_ref, cb_ref, tm):
    xp_ref[0:HALO, :] = halo
    xp_ref[HALO:HALO + tm, :] = x
    base = HALO - (CONV_WIDTH - 1)
    y = cb_ref[...] + xp_ref[base:base + tm, :] * cw_ref[0:1, :]
    for j in range(1, CONV_WIDTH):
        y = y + xp_ref[base + j:base + j + tm, :] * cw_ref[j:j + 1, :]
    return y


def _mprep_kernel(xm_ref, halo_ref, cw_ref, cb_ref, wq_ref, wk_ref, wv_ref, wif_ref, wift_ref,
                  bif_ref, bift_ref, xc_ref, q_ref, k_ref, v_ref, g_ref, gt_ref, xp_ref,
                  *, tm, tiles_per_seq):
    i = pl.program_id(0)
    first = (i % tiles_per_seq) == 0
    halo = jnp.where(first, 0.0, halo_ref[...])
    xm = xm_ref[...]
    y = _causal_conv(xp_ref, halo, xm, cw_ref, cb_ref, tm)
    xc = y * jax.nn.sigmoid(y)
    xc_ref[...] = xc
    xcb = xc.astype(BF16)
    xmb = xm.astype(BF16)
    nblk = M_WIDTH // M_HEAD_DIM
    for g in range(nblk):
        sl = slice(g * M_HEAD_DIM, (g + 1) * M_HEAD_DIM)
        q_ref[:, sl] = jnp.dot(xcb[:, sl], wq_ref[g], preferred_element_type=F32).astype(BF16)
        k_ref[:, sl] = jnp.dot(xcb[:, sl], wk_ref[g], preferred_element_type=F32).astype(BF16)
        v_ref[:, sl] = jnp.dot(xmb[:, sl], wv_ref[g], preferred_element_type=F32).astype(BF16)
    qb, kb, vb = q_ref[...], k_ref[...], v_ref[...]
    w = M_WIDTH
    g = (jnp.dot(qb, wif_ref[0:w, :], preferred_element_type=F32)
         + jnp.dot(kb, wif_ref[w:2 * w, :], preferred_element_type=F32)
         + jnp.dot(vb, wif_ref[2 * w:3 * w, :], preferred_element_type=F32) + bif_ref[...])
    nt = (((1,), (1,)), ((), ()))
    gt = (lax.dot_general(wift_ref[:, 0:w], qb, nt, preferred_element_type=F32)
          + lax.dot_general(wift_ref[:, w:2 * w], kb, nt, preferred_element_type=F32)
          + lax.dot_general(wift_ref[:, 2 * w:3 * w], vb, nt, preferred_element_type=F32) + bift_ref[...])
    col = lax.broadcasted_iota(I32, g.shape, 1)
    g_ref[...] = jnp.where(col >= M_HEADS, _log_sigmoid(g), g)
    row = lax.broadcasted_iota(I32, gt.shape, 0)
    gt_ref[...] = jnp.where(row >= M_HEADS, _log_sigmoid(gt), gt)


def _block_diag_dense(w, group):
    nb, bi, bo = w.shape
    per = group // bi
    w4 = w.reshape(nb // per, per, bi, bo)
    eye = jnp.eye(per, dtype=w.dtype)
    return jnp.einsum("gaio,ab->gaibo", w4, eye).reshape(nb // per, group, group)


def _mlstm_prep(proj, seq, conv_w, conv_b, wq, wk, wv, w_if, b_if, tm):
    n = proj.shape[0]
    c = M_WIDTH
    nblk = c // M_HEAD_DIM
    tiles_per_seq = seq // tm
    wqd = _block_diag_dense(wq, M_HEAD_DIM).astype(BF16)
    wkd = _block_diag_dense(wk, M_HEAD_DIM).astype(BF16)
    wvd = _block_diag_dense(wv, M_HEAD_DIM).astype(BF16)
    wif = w_if.astype(BF16)
    wift = w_if.T.astype(BF16)
    ng = 2 * M_HEADS
    hb = tm // HALO
    full = lambda shape: pl.BlockSpec(shape, lambda i: (0,) * len(shape))
    return pl.pallas_call(
        functools.partial(_mprep_kernel, tm=tm, tiles_per_seq=tiles_per_seq),
        grid=(n // tm,),
        in_specs=[pl.BlockSpec((tm, c), lambda i: (i, 0)),
                  pl.BlockSpec((HALO, c), lambda i: (jnp.maximum(i * hb - 1, 0), 0)),
                  full((CONV_WIDTH, c)), full((1, c)),
                  full((nblk, M_HEAD_DIM, M_HEAD_DIM)), full((nblk, M_HEAD_DIM, M_HEAD_DIM)),
                  full((nblk, M_HEAD_DIM, M_HEAD_DIM)),
                  full((3 * c, ng)), full((ng, 3 * c)), full((1, ng)), full((ng, 1))],
        out_specs=[pl.BlockSpec((tm, c), lambda i: (i, 0)),
                   pl.BlockSpec((tm, c), lambda i: (i, 0)),
                   pl.BlockSpec((tm, c), lambda i: (i, 0)),
                   pl.BlockSpec((tm, c), lambda i: (i, 0)),
                   pl.BlockSpec((tm, ng), lambda i: (i, 0)),
                   pl.BlockSpec((ng, tm), lambda i: (0, i))],
        out_shape=[jax.ShapeDtypeStruct((n, c), F32),
                   jax.ShapeDtypeStruct((n, c), BF16),
                   jax.ShapeDtypeStruct((n, c), BF16),
                   jax.ShapeDtypeStruct((n, c), BF16),
                   jax.ShapeDtypeStruct((n, ng), F32),
                   jax.ShapeDtypeStruct((ng, n), F32)],
        scratch_shapes=[pltpu.VMEM((HALO + tm, c), F32)],
        compiler_params=_cparams(("parallel",)),
        name="mlstm_prep",
    )(proj, proj, conv_w, conv_b.reshape(1, c), wqd, wkd, wvd, wif, wift,
      b_if.reshape(1, ng), b_if.reshape(ng, 1))


def _mlstm_kernel(q_ref, k_ref, v_ref, g_ref, gt_ref, xc_ref, z_ref, nw_ref, sk_ref, o_ref,
                  c_ref, n_ref, m_ref, *, chunk):
    L = chunk
    hd = M_HEAD_DIM

    @pl.when(pl.program_id(1) == 0)
    def _():
        c_ref[...] = jnp.zeros_like(c_ref)
        n_ref[...] = jnp.zeros_like(n_ref)
        m_ref[...] = jnp.zeros_like(m_ref)

    rows = lax.broadcasted_iota(I32, (L, L), 0)
    cols = lax.broadcasted_iota(I32, (L, L), 1)
    causal = cols <= rows
    tril = jnp.where(causal, 1.0, 0.0).astype(F32)
    triu = jnp.where(rows <= cols, 1.0, 0.0).astype(F32)
    g = g_ref[...]
    gt = gt_ref[...]
    hi = lax.Precision.HIGHEST
    bcol_all = jnp.dot(tril, g, precision=hi, preferred_element_type=F32)
    brow_all = jnp.dot(gt, triu, precision=hi, preferred_element_type=F32)
    k_scale = hd ** -0.5
    nt = (((1,), (1,)), ((), ()))
    tn = (((0,), (0,)), ((), ()))

    for h in range(M_HEADS):
        sl = slice(h * hd, (h + 1) * hd)
        qh = q_ref[:, sl]
        kh = k_ref[:, sl]
        vh = v_ref[:, sl]
        i_col = g[:, h:h + 1]
        b_col = bcol_all[:, M_HEADS + h:M_HEADS + h + 1]
        i_row = gt[h:h + 1, :]
        b_row = brow_all[M_HEADS + h:M_HEADS + h + 1, :]
        m_prev = m_ref[h, 0:1, 0:1]
        c_prev = c_ref[h]
        n_prev = n_ref[h]

        dmat = jnp.where(causal, b_col - b_row + i_row, NEG_INF)
        a_col = b_col + m_prev
        m_row = jnp.maximum(a_col, jnp.max(dmat, axis=1, keepdims=True))
        s = lax.dot_general(qh, kh, nt, preferred_element_type=F32) * k_scale
        s = s * jnp.exp(dmat - m_row)
        inter = jnp.exp(a_col - m_row)
        num = inter * jnp.dot(qh, c_prev.astype(BF16), preferred_element_type=F32) \
            + jnp.dot(s.astype(BF16), vh, preferred_element_type=F32)
        qn = jnp.sum(qh.astype(F32) * n_prev, axis=1, keepdims=True)
        den = inter * qn + jnp.sum(s, axis=1, keepdims=True)
        hval = num * (1.0 / jnp.maximum(jnp.abs(den), jnp.exp(-m_row)))

        mu = jnp.mean(hval, axis=1, keepdims=True)
        cen = hval - mu
        var = jnp.mean(cen * cen, axis=1, keepdims=True)
        hn = cen * lax.rsqrt(var + LN_EPS) * nw_ref[:, sl]
        zz = z_ref[:, sl]
        o_ref[:, sl] = ((hn + sk_ref[:, sl] * xc_ref[:, sl]) * (zz * jax.nn.sigmoid(zz))).astype(o_ref.dtype)

        b_last = b_col[L - 1:L, :]
        w_log = b_last - b_col + i_col
        m_new = jnp.maximum(b_last + m_prev, jnp.max(w_log, axis=0, keepdims=True))
        decay = jnp.exp(b_last + m_prev - m_new)
        kw = kh.astype(F32) * (jnp.exp(w_log - m_new) * k_scale)
        c_ref[h] = decay * c_prev + lax.dot_general(kw.astype(BF16), vh, tn, preferred_element_type=F32)
        n_ref[h] = decay * n_prev + jnp.sum(kw, axis=0, keepdims=True)
        m_ref[h] = jnp.broadcast_to(m_new, m_ref.shape[1:])


def _mlstm(q, k, v, g, gt, xc, proj, norm_w, skip, batch, seq, chunk):
    n, c = q.shape
    nc = seq // chunk
    ng = 2 * M_HEADS
    zcol = M_WIDTH // c
    row = lambda b, j: (b * nc + j, 0)
    return pl.pallas_call(
        functools.partial(_mlstm_kernel, chunk=chunk),
        grid=(batch, nc),
        in_specs=[pl.BlockSpec((chunk, c), row), pl.BlockSpec((chunk, c), row), pl.BlockSpec((chunk, c), row),
                  pl.BlockSpec((chunk, ng), row),
                  pl.BlockSpec((ng, chunk), lambda b, j: (0, b * nc + j)),
                  pl.BlockSpec((chunk, c), row),
                  pl.BlockSpec((chunk, c), lambda b, j: (b * nc + j, zcol)),
                  pl.BlockSpec((1, c), lambda b, j: (0, 0)),
                  pl.BlockSpec((1, c), lambda b, j: (0, 0))],
        out_specs=pl.BlockSpec((chunk, c), row),
        out_shape=jax.ShapeDtypeStruct((n, c), BF16),
        scratch_shapes=[pltpu.VMEM((M_HEADS, M_HEAD_DIM, M_HEAD_DIM), F32),
                        pltpu.VMEM((M_HEADS, 1, M_HEAD_DIM), F32),
                        pltpu.VMEM((M_HEADS, 8, 128), F32)],
        compiler_params=_cparams(("parallel", "arbitrary")),
        name="mlstm_chunk",
    )(q, k, v, g, gt, xc, proj, norm_w.reshape(1, c), skip.reshape(1, c))


def _gelu_tanh(x):
    return 0.5 * x * (1.0 + jnp.tanh(math.sqrt(2.0 / math.pi) * (x + 0.044715 * (x * x * x))))


def _rglru_kernel(xr_ref, halo_ref, xg_ref, cw_ref, cb_ref, wa_ref, ba_ref, wx_ref, bx_ref, lam_ref,
                  o_ref, xp_ref, a_ref, b_ref, h_ref, *, tm, cw):
    t = pl.program_id(2)

    @pl.when(t == 0)
    def _():
        h_ref[...] = jnp.zeros_like(h_ref)

    halo = jnp.where(t == 0, 0.0, halo_ref[...])
    xc = _causal_conv(xp_ref, halo, xr_ref[...], cw_ref, cb_ref, tm)
    xcb = xc.astype(BF16)
    nblk = cw // R_BLOCK
    ra = []
    rx = []
    for g in range(nblk):
        sl = slice(g * R_BLOCK, (g + 1) * R_BLOCK)
        ra.append(jnp.dot(xcb[:, sl], wa_ref[g], preferred_element_type=F32))
        rx.append(jnp.dot(xcb[:, sl], wx_ref[g], preferred_element_type=F32))
    r = jax.nn.sigmoid(jnp.concatenate(ra, axis=1) + ba_ref[...])
    ig = jax.nn.sigmoid(jnp.concatenate(rx, axis=1) + bx_ref[...])
    nl = -lam_ref[...]
    softplus = jnp.maximum(nl, 0.0) + jnp.log1p(jnp.exp(-jnp.abs(nl)))
    log_a = (-LRU_C * softplus) * r
    a = jnp.exp(log_a)
    b = jnp.sqrt(1.0 - jnp.exp(2.0 * log_a)) * (ig * xc)

    sub = lax.broadcasted_iota(I32, (tm, cw), 0) % 8
    for d in (1, 2, 4):
        keep = sub >= d
        a_sh = pltpu.roll(a, d, axis=0)
        b_sh = pltpu.roll(b, d, axis=0)
        b = jnp.where(keep, a * b_sh + b, b)
        a = jnp.where(keep, a * a_sh, a)
    a_ref[...] = a
    b_ref[...] = b

    def body(g, h):
        r0 = pl.multiple_of(g * 8, 8)
        hh = b_ref[pl.ds(r0, 8), :] + a_ref[pl.ds(r0, 8), :] * h
        b_ref[pl.ds(r0, 8), :] = hh
        return hh[7:8, :]

    h_ref[...] = lax.fori_loop(0, tm // 8, body, h_ref[...], unroll=8)
    o_ref[...] = (b_ref[...] * _gelu_tanh(xg_ref[...])).astype(o_ref.dtype)


def _rglru(proj, batch, seq, xr_off, xg_off, conv_w, conv_b, wa, ba, wx, bx, lam, tm, cw):
    n = proj.shape[0]
    ncol = R_WIDTH // cw
    nt = seq // tm
    per = cw // R_BLOCK
    xr_cb = xr_off // cw
    xg_cb = xg_off // cw
    hb = tm // HALO
    colv = lambda shape: pl.BlockSpec(shape, lambda b, j, t: (0, j))
    return pl.pallas_call(
        functools.partial(_rglru_kernel, tm=tm, cw=cw),
        grid=(batch, ncol, nt),
        in_specs=[pl.BlockSpec((tm, cw), lambda b, j, t: (b * nt + t, xr_cb + j)),
                  pl.BlockSpec((HALO, cw), lambda b, j, t: (jnp.maximum((b * nt + t) * hb - 1, 0), xr_cb + j)),
                  pl.BlockSpec((tm, cw), lambda b, j, t: (b * nt + t, xg_cb + j)),
                  colv((CONV_WIDTH, cw)), colv((1, cw)),
                  pl.BlockSpec((per, R_BLOCK, R_BLOCK), lambda b, j, t: (j, 0, 0)), colv((1, cw)),
                  pl.BlockSpec((per, R_BLOCK, R_BLOCK), lambda b, j, t: (j, 0, 0)), colv((1, cw)),
                  colv((1, cw))],
        out_specs=pl.BlockSpec((tm, cw), lambda b, j, t: (b * nt + t, j)),
        out_shape=jax.ShapeDtypeStruct((n, R_WIDTH), BF16),
        scratch_shapes=[pltpu.VMEM((HALO + tm, cw), F32), pltpu.VMEM((tm, cw), F32),
                        pltpu.VMEM((tm, cw), F32), pltpu.VMEM((1, cw), F32)],
        compiler_params=_cparams(("parallel", "parallel", "arbitrary")),
        name="rglru",
    )(proj, proj, proj, conv_w, conv_b.reshape(1, R_WIDTH), wa.astype(BF16), ba.reshape(1, R_WIDTH),
      wx.astype(BF16), bx.reshape(1, R_WIDTH), lam.reshape(1, R_WIDTH))


def _merge_kernel(ym_ref, yr_ref, wpm_ref, wpr_ref, g0_ref, g1_ref, bg_ref, o_ref):
    g0 = jax.nn.sigmoid(g0_ref[...] + bg_ref[0:1, :])
    g1 = jax.nn.sigmoid(g1_ref[...] + bg_ref[1:2, :])
    pm = jnp.dot(ym_ref[...], wpm_ref[...], preferred_element_type=F32)
    pr = jnp.dot(yr_ref[...], wpr_ref[...], preferred_element_type=F32)
    o_ref[...] = (g0 * pm + g1 * pr).astype(o_ref.dtype)


def _merge(ym, yr, w_pm, w_pr, proj, gate_off, b_gate, tm, bn):
    n = ym.shape[0]
    d = w_pm.shape[1]
    g0_cb = gate_off // bn
    g1_cb = (gate_off + d) // bn
    return pl.pallas_call(
        _merge_kernel,
        grid=(d // bn, n // tm),
        in_specs=[pl.BlockSpec((tm, ym.shape[1]), lambda j, i: (i, 0)),
                  pl.BlockSpec((tm, yr.shape[1]), lambda j, i: (i, 0)),
                  pl.BlockSpec((w_pm.shape[0], bn), lambda j, i: (0, j)),
                  pl.BlockSpec((w_pr.shape[0], bn), lambda j, i: (0, j)),
                  pl.BlockSpec((tm, bn), lambda j, i: (i, g0_cb + j)),
                  pl.BlockSpec((tm, bn), lambda j, i: (i, g1_cb + j)),
                  pl.BlockSpec((2, bn), lambda j, i: (0, j))],
        out_specs=pl.BlockSpec((tm, bn), lambda j, i: (i, j)),
        out_shape=jax.ShapeDtypeStruct((n, d), BF16),
        compiler_params=_cparams(("parallel", "parallel")),
        name="merge",
    )(ym, yr, w_pm.astype(BF16), w_pr.astype(BF16), proj, proj, b_gate)


def _layer_norm(y, g, b):
    mu = jnp.mean(y, axis=1, keepdims=True)
    cen = y - mu
    var = jnp.mean(cen * cen, axis=1, keepdims=True)
    return cen * lax.rsqrt(var + LN_EPS) * g + b


def _oproj_kernel(mg_ref, wo_ref, x_ref, g_ref, b_ref, rwt_ref, x1_ref, lt_ref):
    y = ALPHA * x_ref[...] + jnp.dot(mg_ref[...], wo_ref[...], preferred_element_type=F32)
    x1 = _layer_norm(y, g_ref[...], b_ref[...])
    x1_ref[...] = x1
    lt_ref[...] = lax.dot_general(rwt_ref[...], x1, (((1,), (1,)), ((), ())),
                                  precision=lax.Precision.HIGHEST, preferred_element_type=F32)


def _oproj(merged, w_o, x, ln_g, ln_b, router_w, tm):
    n, d = x.shape
    e = router_w.shape[1]
    full = lambda shape: pl.BlockSpec(shape, lambda i: (0,) * len(shape))
    return pl.pallas_call(
        _oproj_kernel,
        grid=(n // tm,),
        in_specs=[pl.BlockSpec((tm, d), lambda i: (i, 0)), full((d, d)),
                  pl.BlockSpec((tm, d), lambda i: (i, 0)), full((1, d)), full((1, d)), full((e, d))],
        out_specs=[pl.BlockSpec((tm, d), lambda i: (i, 0)), pl.BlockSpec((e, tm), lambda i: (0, i))],
        out_shape=[jax.ShapeDtypeStruct((n, d), F32), jax.ShapeDtypeStruct((e, n), F32)],
        compiler_params=_cparams(("parallel",)),
        name="out_proj_ln",
    )(merged, w_o.astype(BF16), x, ln_g.reshape(1, d), ln_b.reshape(1, d), router_w.T)


def _first_max(v, idx, sentinel):
    m = jnp.max(v, axis=0, keepdims=True)
    am = jnp.min(jnp.where(v == m, idx, sentinel), axis=0, keepdims=True)
    return m, am


def _route_kernel(lt_ref, bias_ref, tri_ref, te_ref, w_ref, rk_ref, cnt_ref, carry_ref, *, tm):
    @pl.when(pl.program_id(0) == 0)
    def _():
        carry_ref[...] = jnp.zeros_like(carry_ref)

    e = N_EXPERTS
    gs = e // N_GROUPS
    scores = jax.nn.sigmoid(lt_ref[...])
    biased = scores + bias_ref[...]
    sub = lax.broadcasted_iota(I32, (gs, tm), 0)
    grp_rows = []
    for g in range(N_GROUPS):
        slab = biased[g * gs:(g + 1) * gs, :]
        m1, a1 = _first_max(slab, sub, gs)
        m2 = jnp.max(jnp.where(sub == a1, NEG_INF, slab), axis=0, keepdims=True)
        grp_rows.append(m1 + m2)
    grp = jnp.concatenate(grp_rows, axis=0)
    gidx = lax.broadcasted_iota(I32, (N_GROUPS, tm), 0)
    gsel = jnp.zeros((N_GROUPS, tm), F32)
    for _ in range(TOPK_GROUPS):
        _, am = _first_max(grp, gidx, N_GROUPS)
        hit = gidx == am
        gsel = jnp.where(hit, 1.0, gsel)
        grp = jnp.where(hit, NEG_INF, grp)
    masked = jnp.concatenate(
        [jnp.where(gsel[g:g + 1, :] > 0.0, biased[g * gs:(g + 1) * gs, :], NEG_INF) for g in range(N_GROUPS)],
        axis=0)
    eidx = lax.broadcasted_iota(I32, (e, tm), 0)
    member = jnp.zeros((e, tm), F32)
    tops = []
    ws = []
    for _ in range(TOP_K):
        _, am = _first_max(masked, eidx, e)
        hit = eidx == am
        tops.append(am)
        ws.append(jnp.sum(jnp.where(hit, scores, 0.0), axis=0, keepdims=True))
        member = jnp.where(hit, 1.0, member)
        masked = jnp.where(hit, NEG_INF, masked)
    wsum = ws[0]
    for k in range(1, TOP_K):
        wsum = wsum + ws[k]
    te_ref[...] = jnp.concatenate(tops, axis=0)
    w_ref[...] = jnp.concatenate(ws, axis=0) / wsum * ROUTED_SCALE

    cum = jnp.dot(member.astype(BF16), tri_ref[...], preferred_element_type=F32)
    carry = carry_ref[:, 0:1]
    rank = carry + cum - member
    rks = []
    for k in range(TOP_K):
        rks.append(jnp.sum(jnp.where(eidx == tops[k], rank, 0.0), axis=0, keepdims=True))
    rk_ref[...] = jnp.concatenate(rks, axis=0).astype(I32)
    new_carry = carry + cum[:, tm - 1:tm]
    carry_ref[...] = jnp.broadcast_to(new_carry, carry_ref.shape)
    cnt_ref[...] = jnp.broadcast_to(new_carry, cnt_ref.shape).astype(I32)


def _route(logits_t, router_bias, tm):
    e, n = logits_t.shape
    tri = jnp.triu(jnp.ones((tm, tm), F32)).astype(BF16)
    return pl.pallas_call(
        functools.partial(_route_kernel, tm=tm),
        grid=(n // tm,),
        in_specs=[pl.BlockSpec((e, tm), lambda i: (0, i)),
                  pl.BlockSpec((e, 1), lambda i: (0, 0)),
                  pl.BlockSpec((tm, tm), lambda i: (0, 0))],
        out_specs=[pl.BlockSpec((TOP_K, tm), lambda i: (0, i)),
                   pl.BlockSpec((TOP_K, tm), lambda i: (0, i)),
                   pl.BlockSpec((TOP_K, tm), lambda i: (0, i)),
                   pl.BlockSpec((e, 128), lambda i: (0, 0))],
        out_shape=[jax.ShapeDtypeStruct((TOP_K, n), I32), jax.ShapeDtypeStruct((TOP_K, n), F32),
                   jax.ShapeDtypeStruct((TOP_K, n), I32), jax.ShapeDtypeStruct((e, 128), I32)],
        scratch_shapes=[pltpu.VMEM((e, 128), F32)],
        compiler_params=_cparams(("arbitrary",)),
        name="route",
    )(logits_t, router_bias.reshape(e, 1), tri)


def _dest_kernel(te_ref, rk_ref, ps_ref, d_ref):
    te = te_ref[...]
    e = N_EXPERTS
    tm = te.shape[1]
    eidx = lax.broadcasted_iota(I32, (e, tm), 0)
    ps = ps_ref[...]
    rows = []
    for k in range(TOP_K):
        rows.append(jnp.sum(jnp.where(eidx == te[k:k + 1, :], ps, 0), axis=0, keepdims=True))
    d_ref[...] = jnp.concatenate(rows, axis=0) + rk_ref[...]


def _dest(top_e, rank, pad_starts, tm):
    n = top_e.shape[1]
    return pl.pallas_call(
        _dest_kernel,
        grid=(n // tm,),
        in_specs=[pl.BlockSpec((TOP_K, tm), lambda i: (0, i)),
                  pl.BlockSpec((TOP_K, tm), lambda i: (0, i)),
                  pl.BlockSpec((N_EXPERTS, 1), lambda i: (0, 0))],
        out_specs=pl.BlockSpec((TOP_K, tm), lambda i: (0, i)),
        out_shape=jax.ShapeDtypeStruct((TOP_K, n), I32),
        compiler_params=_cparams(("parallel",)),
        name="dest",
    )(top_e, rank, pad_starts.reshape(N_EXPERTS, 1))


def _invert_kernel(dest_ref, inv_ref):
    def body(j, c):
        inv_ref[dest_ref[j]] = j
        return c

    lax.fori_loop(0, dest_ref.shape[0], body, 0, unroll=8)


def _invert_permutation(dest_flat):
    m = dest_flat.shape[0]
    return pl.pallas_call(
        _invert_kernel,
        in_specs=[pl.BlockSpec(memory_space=pltpu.SMEM)],
        out_specs=pl.BlockSpec(memory_space=pltpu.SMEM),
        out_shape=jax.ShapeDtypeStruct((m,), I32),
        name="invert_perm",
    )(dest_flat)


def _expert_ffn(xb, wgb_ref, wub_ref, wdb_ref):
    hg = jnp.dot(xb, wgb_ref[...], preferred_element_type=F32)
    hu = jnp.dot(xb, wub_ref[...], preferred_element_type=F32)
    hh = (hg * jax.nn.sigmoid(hg)) * hu
    return jnp.dot(hh.astype(BF16), wdb_ref[...], preferred_element_type=F32)


def _expert_kernel(vt_ref, ve_ref, lo_ref, hi_ref, inv_ref, x_hbm, wg_ref, wu_ref, wd_ref, out_hbm,
                   wgb_ref, wub_ref, wdb_ref, xbuf, obuf, xb_ref, cur_ref, gsem, ssem, *, n_tok, n_tiles):
    v = pl.program_id(0)
    rows = xbuf.shape[1]
    tile = vt_ref[v]
    lo = lo_ref[v]
    hi = hi_ref[v]
    nonempty = hi > lo
    first = jnp.logical_and(nonempty, lo == 0)
    slot = tile % 2
    other = 1 - slot

    def gather_row(t, s, r):
        tok = inv_ref[t * rows + r] & (n_tok - 1)
        return pltpu.make_async_copy(x_hbm.at[pl.ds(tok, 1)], xbuf.at[s, pl.ds(r, 1)], gsem.at[s])

    def scatter_row(t, s, r):
        j = inv_ref[t * rows + r]
        return pltpu.make_async_copy(obuf.at[s, pl.ds(r, 1)], out_hbm.at[pl.ds(j, 1)], ssem.at[s])

    def wait_gathered_tile(s):
        pltpu.make_async_copy(x_hbm.at[pl.ds(0, rows)], xbuf.at[s], gsem.at[s]).wait()

    def wait_scattered_tile(s):
        pltpu.make_async_copy(obuf.at[s], out_hbm.at[pl.ds(0, rows)], ssem.at[s]).wait()

    @pl.when(v == 0)
    def _():
        cur_ref[0] = -1

    @pl.when(jnp.logical_and(nonempty, cur_ref[0] != ve_ref[v]))
    def _():
        wgb_ref[...] = wg_ref[0].astype(BF16)
        wub_ref[...] = wu_ref[0].astype(BF16)
        wdb_ref[...] = wd_ref[0].astype(BF16)
        cur_ref[0] = ve_ref[v]

    @pl.when(jnp.logical_and(first, tile == 0))
    def _():
        def issue(r, c):
            gather_row(0, 0, r).start()
            return c

        lax.fori_loop(0, rows, issue, 0)

    @pl.when(jnp.logical_and(first, tile >= 2))
    def _():
        wait_scattered_tile(slot)

    @pl.when(first)
    def _():
        wait_gathered_tile(slot)

    nxt = jnp.minimum(tile + 1, n_tiles - 1)

    @pl.when(jnp.logical_and(first, tile == 0))
    def _():
        xb_ref[...] = xbuf[slot].astype(BF16)
        for r in range(rows):
            gather_row(nxt, other, r).start(priority=r % 2)
        obuf[slot] = _expert_ffn(xb_ref[...], wgb_ref, wub_ref, wdb_ref)

    @pl.when(jnp.logical_and(first, tile >= 1))
    def _():
        xb_ref[...] = xbuf[slot].astype(BF16)
        for r in range(rows):
            gather_row(nxt, other, r).start(priority=r % 2)
            scatter_row(tile - 1, other, r).start(priority=(r + 1) % 2)
        obuf[slot] = _expert_ffn(xb_ref[...], wgb_ref, wub_ref, wdb_ref)

    @pl.when(jnp.logical_and(nonempty, lo > 0))
    def _():
        y = _expert_ffn(xbuf[slot].astype(BF16), wgb_ref, wub_ref, wdb_ref)
        row = lax.broadcasted_iota(I32, (rows, 1), 0)
        mine = jnp.logical_and(row >= lo, row < hi)
        obuf[slot] = jnp.where(mine, y, obuf[slot])

    @pl.when(v == pl.num_programs(0) - 1)
    def _():
        last = n_tiles - 1
        ls = last % 2

        def issue(r, c):
            scatter_row(last, ls, r).start()
            return c

        lax.fori_loop(0, rows, issue, 0)
        wait_scattered_tile(1 - ls)
        wait_scattered_tile(ls)
        wait_gathered_tile(1 - ls)


def _experts(x1, inv, visit_tile, visit_expert, visit_lo, visit_hi, w_gate, w_up, w_down):
    n_tok, d = x1.shape
    m = inv.shape[0]
    de = w_gate.shape[2]
    n_tiles = m // EXPERT_TILE
    grid_spec = pltpu.PrefetchScalarGridSpec(
        num_scalar_prefetch=5,
        grid=(visit_tile.shape[0],),
        in_specs=[pl.BlockSpec(memory_space=pl.ANY),
                  pl.BlockSpec((1, d, de), lambda v, vt, ve, lo, hi, inv: (ve[v], 0, 0)),
                  pl.BlockSpec((1, d, de), lambda v, vt, ve, lo, hi, inv: (ve[v], 0, 0)),
                  pl.BlockSpec((1, de, d), lambda v, vt, ve, lo, hi, inv: (ve[v], 0, 0))],
        out_specs=pl.BlockSpec(memory_space=pl.ANY),
        scratch_shapes=[pltpu.VMEM((d, de), BF16), pltpu.VMEM((d, de), BF16), pltpu.VMEM((de, d), BF16),
                        pltpu.VMEM((2, EXPERT_TILE, d), F32), pltpu.VMEM((2, EXPERT_TILE, d), F32),
                        pltpu.VMEM((EXPERT_TILE, d), BF16), pltpu.SMEM((1,), I32), pltpu.SemaphoreType.DMA((2,)), pltpu.SemaphoreType.DMA((2,))],
    )
    return pl.pallas_call(
        functools.partial(_expert_kernel, n_tok=n_tok, n_tiles=n_tiles),
        grid_spec=grid_spec,
        out_shape=jax.ShapeDtypeStruct((m, d), F32),
        compiler_params=_cparams(("arbitrary",)),
        name="experts",
    )(visit_tile, visit_expert, visit_lo, visit_hi, inv, x1, w_gate, w_up, w_down)


def _combine_kernel(x1_ref, w_ref, y_ref, sg_ref, su_ref, sd_ref, g_ref, b_ref, o_ref):
    x1 = x1_ref[...]
    xb = x1.astype(BF16)
    hg = jnp.dot(xb, sg_ref[...], preferred_element_type=F32)
    hu = jnp.dot(xb, su_ref[...], preferred_element_type=F32)
    hh = (hg * jax.nn.sigmoid(hg)) * hu
    shared = jnp.dot(hh.astype(BF16), sd_ref[...], preferred_element_type=F32)
    w = w_ref[...]
    routed = y_ref[0] * w[:, 0:1]
    for k in range(1, TOP_K):
        routed = routed + y_ref[k] * w[:, k:k + 1]
    o_ref[...] = _layer_norm(ALPHA * x1 + (routed + shared), g_ref[...], b_ref[...])


def _combine(x1, w_tok, y8, s_gate, s_up, s_down, ln_g, ln_b, tm):
    n, d = x1.shape
    de = s_gate.shape[1]
    full = lambda shape: pl.BlockSpec(shape, lambda i: (0,) * len(shape))
    return pl.pallas_call(
        _combine_kernel,
        grid=(n // tm,),
        in_specs=[pl.BlockSpec((tm, d), lambda i: (i, 0)),
                  pl.BlockSpec((tm, TOP_K), lambda i: (i, 0)),
                  pl.BlockSpec((TOP_K, tm, d), lambda i: (0, i, 0)),
                  full((d, de)), full((d, de)), full((de, d)), full((1, d)), full((1, d))],
        out_specs=pl.BlockSpec((tm, d), lambda i: (i, 0)),
        out_shape=jax.ShapeDtypeStruct((n, d), F32),
        compiler_params=_cparams(("parallel",)),
        name="combine",
    )(x1, w_tok, y8, s_gate.astype(BF16), s_up.astype(BF16), s_down.astype(BF16),
      ln_g.reshape(1, d), ln_b.reshape(1, d))


def _visit_plan(counts, n_rows):
    e = counts.shape[0]
    n_tiles = n_rows // EXPERT_TILE
    ends = jnp.cumsum(counts)
    starts = ends - counts
    pos = jnp.sort(jnp.concatenate([jnp.arange(n_tiles, dtype=I32) * EXPERT_TILE, starts]))
    nxt = jnp.concatenate([pos[1:], jnp.full((1,), n_rows, I32)])
    tile = jnp.minimum(pos // EXPERT_TILE, n_tiles - 1)
    expert = jnp.minimum(jnp.sum((ends[None, :] <= pos[:, None]).astype(I32), axis=1), e - 1)
    return starts, tile, expert, pos - tile * EXPERT_TILE, nxt - tile * EXPERT_TILE


def _moe(x1, logits_t, router_bias, e_w_gate, e_w_up, e_w_down, s_w_gate, s_w_up, s_w_down, ln_g, ln_b):
    n, d = x1.shape
    top_e, w_t, rank, counts = _route(logits_t, router_bias, tm=512)
    starts, v_tile, v_expert, v_lo, v_hi = _visit_plan(counts[:, 0], n * TOP_K)
    dest = _dest(top_e, rank, starts, tm=1024)
    inv = _invert_permutation(dest.reshape(n * TOP_K))
    y8 = _experts(x1, inv, v_tile, v_expert, v_lo, v_hi, e_w_gate, e_w_up, e_w_down)
    return _combine(x1, w_t.T, y8.reshape(TOP_K, n, d), s_w_gate, s_w_up, s_w_down, ln_g, ln_b, tm=128)


def _layer(x, w_in, b_gate, m_conv_w, m_conv_b, m_wq, m_wk, m_wv, m_w_if, m_b_if, m_norm_w, m_skip,
           r_conv_w, r_conv_b, r_wa, r_ba, r_wx, r_bx, r_lambda, w_pm, w_pr, w_o, ln1_g, ln1_b,
           router_w, router_bias, e_w_gate, e_w_up, e_w_down, s_w_gate, s_w_up, s_w_down, ln2_g, ln2_b):
    batch, seq, d = x.shape
    n = batch * seq
    xt = x.reshape(n, d)
    proj = _matmul(xt.astype(BF16), w_in.astype(BF16), 1024, 1024, F32)
    o_z = M_WIDTH
    o_xr = 2 * M_WIDTH
    o_xg = o_xr + R_WIDTH
    o_gate = o_xg + R_WIDTH
    del o_z
    xc, q, k, v, g, gt = _mlstm_prep(proj, seq, m_conv_w, m_conv_b, m_wq, m_wk, m_wv, m_w_if, m_b_if, tm=256)
    y_m = _mlstm(q, k, v, g, gt, xc, proj, m_norm_w, m_skip, batch, seq, MLSTM_CHUNK)
    y_r = _rglru(proj, batch, seq, o_xr, o_xg, r_conv_w, r_conv_b, r_wa, r_ba, r_wx, r_bx, r_lambda,
                 tm=512, cw=512)
    merged = _merge(y_m, y_r, w_pm, w_pr, proj, o_gate, b_gate, tm=512, bn=1024)
    x1, logits_t = _oproj(merged, w_o, xt, ln1_g, ln1_b, router_w, tm=256)
    out = _moe(x1, logits_t, router_bias, e_w_gate, e_w_up, e_w_down, s_w_gate, s_w_up, s_w_down, ln2_g, ln2_b)
    return out.reshape(batch, seq, d)


def kernel(x, w_in, b_gate, m_conv_w, m_conv_b, m_wq, m_wk, m_wv, m_w_if, m_b_if, m_norm_w, m_skip, r_conv_w, r_conv_b, r_wa, r_ba, r_wx, r_bx, r_lambda, w_pm, w_pr, w_o, ln1_g, ln1_b, router_w, router_bias, e_w_gate, e_w_up, e_w_down, s_w_gate, s_w_up, s_w_down, ln2_g, ln2_b):
    for l in range(DEPTH):
        x = _layer(x, w_in[l], b_gate[l], m_conv_w[l], m_conv_b[l], m_wq[l], m_wk[l], m_wv[l],
                   m_w_if[l], m_b_if[l], m_norm_w[l], m_skip[l], r_conv_w[l], r_conv_b[l],
                   r_wa[l], r_ba[l], r_wx[l], r_bx[l], r_lambda[l], w_pm[l], w_pr[l], w_o[l],
                   ln1_g[l], ln1_b[l], router_w[l], router_bias[l], e_w_gate[l], e_w_up[l],
                   e_w_down[l], s_w_gate[l], s_w_up[l], s_w_down[l], ln2_g[l], ln2_b[l])
    return x
```

```python
import functools
import math

import jax
import jax.numpy as jnp
from jax import lax
from jax.experimental import pallas as pl
from jax.experimental.pallas import tpu as pltpu

F32 = jnp.float32
BF16 = jnp.bfloat16
I32 = jnp.int32
U32 = jnp.uint32

D_MODEL = 2048
M_WIDTH = 2048
M_HEADS = 8
M_HEAD_DIM = 256
M_QKV_BLOCK = 4
CONV_WIDTH = 4
R_WIDTH = 2560
R_BLOCK = 256
LRU_C = 8.0
N_EXPERTS = 64
TOP_K = 8
N_GROUPS = 8
TOPK_GROUPS = 4
D_EXPERT = 512
ROUTED_SCALE = 2.5
DEPTH = 1
ALPHA = (2.0 * DEPTH) ** 0.25
LN_EPS = 1e-5

V7X_VMEM_LIMIT = 56 * 1024 * 1024
HALO = 8
MLSTM_CHUNK = 256
EXPERT_TILE = 256
NEG_INF = float("-inf")
ROW_COPY_PRIORITY = 1


def _cparams(sem, vmem=V7X_VMEM_LIMIT):
    return pltpu.CompilerParams(dimension_semantics=sem, vmem_limit_bytes=vmem)


def _mm_kernel(a_ref, b_ref, o_ref):
    o_ref[...] = jnp.dot(a_ref[...], b_ref[...], preferred_element_type=F32).astype(o_ref.dtype)


def _matmul(a, b, bm, bn, out_dtype):
    m, k = a.shape
    n = b.shape[1]
    return pl.pallas_call(
        _mm_kernel,
        grid=(m // bm, n // bn),
        in_specs=[pl.BlockSpec((bm, k), lambda i, j: (i, 0)),
                  pl.BlockSpec((k, bn), lambda i, j: (0, j))],
        out_specs=pl.BlockSpec((bm, bn), lambda i, j: (i, j)),
        out_shape=jax.ShapeDtypeStruct((m, n), out_dtype),
        compiler_params=_cparams(("parallel", "parallel")),
        name="in_proj",
    )(a, b)


def _log_sigmoid(x):
    return jnp.minimum(x, 0.0) - jnp.log1p(jnp.exp(-jnp.abs(x)))


def _causal_conv(xp_ref, halo, x, cw_ref, cb_ref, tm):
    xp_ref[0:HALO, :] = halo
    xp_ref[HALO:HALO + tm, :] = x
    base = HALO - (CONV_WIDTH - 1)
    y = cb_ref[...] + xp_ref[base:base + tm, :] * cw_ref[0:1, :]
    for j in range(1, CONV_WIDTH):
        y = y + xp_ref[base + j:base + j + tm, :] * cw_ref[j:j + 1, :]
    return y


def _mprep_kernel(xm_ref, halo_ref, cw_ref, cb_ref, wq_ref, wk_ref, wv_ref, wif_ref, wift_ref,
                  bif_ref, bift_ref, xc_ref, q_ref, k_ref, v_ref, g_ref, gt_ref, xp_ref,
                  *, tm, tiles_per_seq):
    i = pl.program_id(0)
    first = (i % tiles_per_seq) == 0
    halo = jnp.where(first, 0.0, halo_ref[...])
    xm = xm_ref[...]
    y = _causal_conv(xp_ref, halo, xm, cw_ref, cb_ref, tm)
    xc = y * jax.nn.sigmoid(y)
    xc_ref[...] = xc
    xcb = xc.astype(BF16)
    xmb = xm.astype(BF16)
    nblk = M_WIDTH // M_HEAD_DIM
    for g in range(nblk):
        sl = slice(g * M_HEAD_DIM, (g + 1) * M_HEAD_DIM)
        q_ref[:, sl] = jnp.dot(xcb[:, sl], wq_ref[g], preferred_element_type=F32).astype(BF16)
        k_ref[:, sl] = jnp.dot(xcb[:, sl], wk_ref[g], preferred_element_type=F32).astype(BF16)
        v_ref[:, sl] = jnp.dot(xmb[:, sl], wv_ref[g], preferred_element_type=F32).astype(BF16)
    qb, kb, vb = q_ref[...], k_ref[...], v_ref[...]
    w = M_WIDTH
    g = (jnp.dot(qb, wif_ref[0:w, :], preferred_element_type=F32)
         + jnp.dot(kb, wif_ref[w:2 * w, :], preferred_element_type=F32)
         + jnp.dot(vb, wif_ref[2 * w:3 * w, :], preferred_element_type=F32) + bif_ref[...])
    nt = (((1,), (1,)), ((), ()))
    gt = (lax.dot_general(wift_ref[:, 0:w], qb, nt, preferred_element_type=F32)
          + lax.dot_general(wift_ref[:, w:2 * w], kb, nt, preferred_element_type=F32)
          + lax.dot_general(wift_ref[:, 2 * w:3 * w], vb, nt, preferred_element_type=F32) + bift_ref[...])
    col = lax.broadcasted_iota(I32, g.shape, 1)
    g_ref[...] = jnp.where(col >= M_HEADS, _log_sigmoid(g), g)
    row = lax.broadcasted_iota(I32, gt.shape, 0)
    gt_ref[...] = jnp.where(row >= M_HEADS, _log_sigmoid(gt), gt)


def _block_diag_dense(w, group):
    nb, bi, bo = w.shape
    per = group // bi
    w4 = w.reshape(nb // per, per, bi, bo)
    eye = jnp.eye(per, dtype=w.dtype)
    return jnp.einsum("gaio,ab->gaibo", w4, eye).reshape(nb // per, group, group)


def _mlstm_prep(proj, seq, conv_w, conv_b, wq, wk, wv, w_if, b_if, tm):
    n = proj.shape[0]
    c = M_WIDTH
    nblk = c // M_HEAD_DIM
    tiles_per_seq = seq // tm
    wqd = _block_diag_dense(wq, M_HEAD_DIM).astype(BF16)
    wkd = _block_diag_dense(wk, M_HEAD_DIM).astype(BF16)
    wvd = _block_diag_dense(wv, M_HEAD_DIM).astype(BF16)
    wif = w_if.astype(BF16)
    wift = w_if.T.astype(BF16)
    ng = 2 * M_HEADS
    hb = tm // HALO
    full = lambda shape: pl.BlockSpec(shape, lambda i: (0,) * len(shape))
    return pl.pallas_call(
        functools.partial(_mprep_kernel, tm=tm, tiles_per_seq=tiles_per_seq),
        grid=(n // tm,),
        in_specs=[pl.BlockSpec((tm, c), lambda i: (i, 0)),
                  pl.BlockSpec((HALO, c), lambda i: (jnp.maximum(i * hb - 1, 0), 0)),
                  full((CONV_WIDTH, c)), full((1, c)),
                  full((nblk, M_HEAD_DIM, M_HEAD_DIM)), full((nblk, M_HEAD_DIM, M_HEAD_DIM)),
                  full((nblk, M_HEAD_DIM, M_HEAD_DIM)),
                  full((3 * c, ng)), full((ng, 3 * c)), full((1, ng)), full((ng, 1))],
        out_specs=[pl.BlockSpec((tm, c), lambda i: (i, 0)),
                   pl.BlockSpec((tm, c), lambda i: (i, 0)),
                   pl.BlockSpec((tm, c), lambda i: (i, 0)),
                   pl.BlockSpec((tm, c), lambda i: (i, 0)),
                   pl.BlockSpec((tm, ng), lambda i: (i, 0)),
                   pl.BlockSpec((ng, tm), lambda i: (0, i))],
        out_shape=[jax.ShapeDtypeStruct((n, c), F32),
                   jax.ShapeDtypeStruct((n, c), BF16),
                   jax.ShapeDtypeStruct((n, c), BF16),
                   jax.ShapeDtypeStruct((n, c), BF16),
                   jax.ShapeDtypeStruct((n, ng), F32),
                   jax.ShapeDtypeStruct((ng, n), F32)],
        scratch_shapes=[pltpu.VMEM((HALO + tm, c), F32)],
        compiler_params=_cparams(("parallel",)),
        name="mlstm_prep",
    )(proj, proj, conv_w, conv_b.reshape(1, c), wqd, wkd, wvd, wif, wift,
      b_if.reshape(1, ng), b_if.reshape(ng, 1))


def _mlstm_kernel(q_ref, k_ref, v_ref, g_ref, gt_ref, xc_ref, z_ref, nw_ref, sk_ref, o_ref,
                  c_ref, n_ref, m_ref, *, chunk):
    L = chunk
    hd = M_HEAD_DIM

    @pl.when(pl.program_id(1) == 0)
    def _():
        c_ref[...] = jnp.zeros_like(c_ref)
        n_ref[...] = jnp.zeros_like(n_ref)
        m_ref[...] = jnp.zeros_like(m_ref)

    rows = lax.broadcasted_iota(I32, (L, L), 0)
    cols = lax.broadcasted_iota(I32, (L, L), 1)
    causal = cols <= rows
    tril = jnp.where(causal, 1.0, 0.0).astype(F32)
    triu = jnp.where(rows <= cols, 1.0, 0.0).astype(F32)
    g = g_ref[...]
    gt = gt_ref[...]
    hi = lax.Precision.HIGHEST
    bcol_all = jnp.dot(tril, g, precision=hi, preferred_element_type=F32)
    brow_all = jnp.dot(gt, triu, precision=hi, preferred_element_type=F32)
    k_scale = hd ** -0.5
    nt = (((1,), (1,)), ((), ()))
    tn = (((0,), (0,)), ((), ()))

    for h in range(M_HEADS):
        sl = slice(h * hd, (h + 1) * hd)
        qh = q_ref[:, sl]
        kh = k_ref[:, sl]
        vh = v_ref[:, sl]
        i_col = g[:, h:h + 1]
        b_col = bcol_all[:, M_HEADS + h:M_HEADS + h + 1]
        i_row = gt[h:h + 1, :]
        b_row = brow_all[M_HEADS + h:M_HEADS + h + 1, :]
        m_prev = m_ref[h, 0:1, 0:1]
        c_prev = c_ref[h]
        n_prev = n_ref[h]

        dmat = jnp.where(causal, b_col - b_row + i_row, NEG_INF)
        a_col = b_col + m_prev
        m_row = jnp.maximum(a_col, jnp.max(dmat, axis=1, keepdims=True))
        s = lax.dot_general(qh, kh, nt, preferred_element_type=F32) * k_scale
        s = s * jnp.exp(dmat - m_row)
        inter = jnp.exp(a_col - m_row)
        num = inter * jnp.dot(qh, c_prev.astype(BF16), preferred_element_type=F32) \
            + jnp.dot(s.astype(BF16), vh, preferred_element_type=F32)
        qn = jnp.sum(qh.astype(F32) * n_prev, axis=1, keepdims=True)
        den = inter * qn + jnp.sum(s, axis=1, keepdims=True)
        hval = num * (1.0 / jnp.maximum(jnp.abs(den), jnp.exp(-m_row)))

        mu = jnp.mean(hval, axis=1, keepdims=True)
        cen = hval - mu
        var = jnp.mean(cen * cen, axis=1, keepdims=True)
        hn = cen * lax.rsqrt(var + LN_EPS) * nw_ref[:, sl]
        zz = z_ref[:, sl]
        o_ref[:, sl] = ((hn + sk_ref[:, sl] * xc_ref[:, sl]) * (zz * jax.nn.sigmoid(zz))).astype(o_ref.dtype)

        b_last = b_col[L - 1:L, :]
        w_log = b_last - b_col + i_col
        m_new = jnp.maximum(b_last + m_prev, jnp.max(w_log, axis=0, keepdims=True))
        decay = jnp.exp(b_last + m_prev - m_new)
        kw = kh.astype(F32) * (jnp.exp(w_log - m_new) * k_scale)
        c_ref[h] = decay * c_prev + lax.dot_general(kw.astype(BF16), vh, tn, preferred_element_type=F32)
        n_ref[h] = decay * n_prev + jnp.sum(kw, axis=0, keepdims=True)
        m_ref[h] = jnp.broadcast_to(m_new, m_ref.shape[1:])


def _mlstm(q, k, v, g, gt, xc, proj, norm_w, skip, batch, seq, chunk):
    n, c = q.shape
    nc = seq // chunk
    ng = 2 * M_HEADS
    zcol = M_WIDTH // c
    row = lambda b, j: (b * nc + j, 0)
    return pl.pallas_call(
        functools.partial(_mlstm_kernel, chunk=chunk),
        grid=(batch, nc),
        in_specs=[pl.BlockSpec((chunk, c), row), pl.BlockSpec((chunk, c), row), pl.BlockSpec((chunk, c), row),
                  pl.BlockSpec((chunk, ng), row),
                  pl.BlockSpec((ng, chunk), lambda b, j: (0, b * nc + j)),
                  pl.BlockSpec((chunk, c), row),
                  pl.BlockSpec((chunk, c), lambda b, j: (b * nc + j, zcol)),
                  pl.BlockSpec((1, c), lambda b, j: (0, 0)),
                  pl.BlockSpec((1, c), lambda b, j: (0, 0))],
        out_specs=pl.BlockSpec((chunk, c), row),
        out_shape=jax.ShapeDtypeStruct((n, c), BF16),
        scratch_shapes=[pltpu.VMEM((M_HEADS, M_HEAD_DIM, M_HEAD_DIM), F32),
                        pltpu.VMEM((M_HEADS, 1, M_HEAD_DIM), F32),
                        pltpu.VMEM((M_HEADS, 8, 128), F32)],
        compiler_params=_cparams(("parallel", "arbitrary")),
        name="mlstm_chunk",
    )(q, k, v, g, gt, xc, proj, norm_w.reshape(1, c), skip.reshape(1, c))


def _gelu_tanh(x):
    return 0.5 * x * (1.0 + jnp.tanh(math.sqrt(2.0 / math.pi) * (x + 0.044715 * (x * x * x))))


def _rglru_kernel(xr_ref, halo_ref, xg_ref, cw_ref, cb_ref, wa_ref, ba_ref, wx_ref, bx_ref, lam_ref,
                  o_ref, xp_ref, a_ref, b_ref, h_ref, *, tm, cw):
    t = pl.program_id(2)

    @pl.when(t == 0)
    def _():
        h_ref[...] = jnp.zeros_like(h_ref)

    halo = jnp.where(t == 0, 0.0, halo_ref[...])
    xc = _causal_conv(xp_ref, halo, xr_ref[...], cw_ref, cb_ref, tm)
    xcb = xc.astype(BF16)
    nblk = cw // R_BLOCK
    ra = []
    rx = []
    for g in range(nblk):
        sl = slice(g * R_BLOCK, (g + 1) * R_BLOCK)
        ra.append(jnp.dot(xcb[:, sl], wa_ref[g], preferred_element_type=F32))
        rx.append(jnp.dot(xcb[:, sl], wx_ref[g], preferred_element_type=F32))
    r = jax.nn.sigmoid(jnp.concatenate(ra, axis=1) + ba_ref[...])
    ig = jax.nn.sigmoid(jnp.concatenate(rx, axis=1) + bx_ref[...])
    nl = -lam_ref[...]
    softplus = jnp.maximum(nl, 0.0) + jnp.log1p(jnp.exp(-jnp.abs(nl)))
    log_a = (-LRU_C * softplus) * r
    a = jnp.exp(log_a)
    b = jnp.sqrt(1.0 - jnp.exp(2.0 * log_a)) * (ig * xc)

    sub = lax.broadcasted_iota(I32, (tm, cw), 0) % 8
    for d in (1, 2, 4):
        keep = sub >= d
        a_sh = pltpu.roll(a, d, axis=0)
        b_sh = pltpu.roll(b, d, axis=0)
        b = jnp.where(keep, a * b_sh + b, b)
        a = jnp.where(keep, a * a_sh, a)
    a_ref[...] = a
    b_ref[...] = b

    def body(g, h):
        r0 = pl.multiple_of(g * 8, 8)
        hh = b_ref[pl.ds(r0, 8), :] + a_ref[pl.ds(r0, 8), :] * h
        b_ref[pl.ds(r0, 8), :] = hh
        return hh[7:8, :]

    h_ref[...] = lax.fori_loop(0, tm // 8, body, h_ref[...], unroll=8)
    o_ref[...] = (b_ref[...] * _gelu_tanh(xg_ref[...])).astype(o_ref.dtype)


def _rglru(proj, batch, seq, xr_off, xg_off, conv_w, conv_b, wa, ba, wx, bx, lam, tm, cw):
    n = proj.shape[0]
    ncol = R_WIDTH // cw
    nt = seq // tm
    per = cw // R_BLOCK
    xr_cb = xr_off // cw
    xg_cb = xg_off // cw
    hb = tm // HALO
    colv = lambda shape: pl.BlockSpec(shape, lambda b, j, t: (0, j))
    return pl.pallas_call(
        functools.partial(_rglru_kernel, tm=tm, cw=cw),
        grid=(batch, ncol, nt),
        in_specs=[pl.BlockSpec((tm, cw), lambda b, j, t: (b * nt + t, xr_cb + j)),
                  pl.BlockSpec((HALO, cw), lambda b, j, t: (jnp.maximum((b * nt + t) * hb - 1, 0), xr_cb + j)),
                  pl.BlockSpec((tm, cw), lambda b, j, t: (b * nt + t, xg_cb + j)),
                  colv((CONV_WIDTH, cw)), colv((1, cw)),
                  pl.BlockSpec((per, R_BLOCK, R_BLOCK), lambda b, j, t: (j, 0, 0)), colv((1, cw)),
                  pl.BlockSpec((per, R_BLOCK, R_BLOCK), lambda b, j, t: (j, 0, 0)), colv((1, cw)),
                  colv((1, cw))],
        out_specs=pl.BlockSpec((tm, cw), lambda b, j, t: (b * nt + t, j)),
        out_shape=jax.ShapeDtypeStruct((n, R_WIDTH), BF16),
        scratch_shapes=[pltpu.VMEM((HALO + tm, cw), F32), pltpu.VMEM((tm, cw), F32),
                        pltpu.VMEM((tm, cw), F32), pltpu.VMEM((1, cw), F32)],
        compiler_params=_cparams(("parallel", "parallel", "arbitrary")),
        name="rglru",
    )(proj, proj, proj, conv_w, conv_b.reshape(1, R_WIDTH), wa.astype(BF16), ba.reshape(1, R_WIDTH),
      wx.astype(BF16), bx.reshape(1, R_WIDTH), lam.reshape(1, R_WIDTH))


def _merge_kernel(ym_ref, yr_ref, wpm_ref, wpr_ref, g0_ref, g1_ref, bg_ref, o_ref):
    g0 = jax.nn.sigmoid(g0_ref[...] + bg_ref[0:1, :])
    g1 = jax.nn.sigmoid(g1_ref[...] + bg_ref[1:2, :])
    pm = jnp.dot(ym_ref[...], wpm_ref[...], preferred_element_type=F32)
    pr = jnp.dot(yr_ref[...], wpr_ref[...], preferred_element_type=F32)
    o_ref[...] = (g0 * pm + g1 * pr).astype(o_ref.dtype)


def _merge(ym, yr, w_pm, w_pr, proj, gate_off, b_gate, tm, bn):
    n = ym.shape[0]
    d = w_pm.shape[1]
    g0_cb = gate_off // bn
    g1_cb = (gate_off + d) // bn
    return pl.pallas_call(
        _merge_kernel,
        grid=(d // bn, n // tm),
        in_specs=[pl.BlockSpec((tm, ym.shape[1]), lambda j, i: (i, 0)),
                  pl.BlockSpec((tm, yr.shape[1]), lambda j, i: (i, 0)),
                  pl.BlockSpec((w_pm.shape[0], bn), lambda j, i: (0, j)),
                  pl.BlockSpec((w_pr.shape[0], bn), lambda j, i: (0, j)),
                  pl.BlockSpec((tm, bn), lambda j, i: (i, g0_cb + j)),
                  pl.BlockSpec((tm, bn), lambda j, i: (i, g1_cb + j)),
                  pl.BlockSpec((2, bn), lambda j, i: (0, j))],
        out_specs=pl.BlockSpec((tm, bn), lambda j, i: (i, j)),
        out_shape=jax.ShapeDtypeStruct((n, d), BF16),
        compiler_params=_cparams(("parallel", "parallel")),
        name="merge",
    )(ym, yr, w_pm.astype(BF16), w_pr.astype(BF16), proj, proj, b_gate)


def _layer_norm(y, g, b):
    mu = jnp.mean(y, axis=1, keepdims=True)
    cen = y - mu
    var = jnp.mean(cen * cen, axis=1, keepdims=True)
    return cen * lax.rsqrt(var + LN_EPS) * g + b


def _oproj_kernel(mg_ref, wo_ref, x_ref, g_ref, b_ref, rwt_ref, x1_ref, lt_ref):
    y = ALPHA * x_ref[...] + jnp.dot(mg_ref[...], wo_ref[...], preferred_element_type=F32)
    x1 = _layer_norm(y, g_ref[...], b_ref[...])
    x1_ref[...] = x1
    lt_ref[...] = lax.dot_general(rwt_ref[...], x1, (((1,), (1,)), ((), ())),
                                  precision=lax.Precision.HIGHEST, preferred_element_type=F32)


def _oproj(merged, w_o, x, ln_g, ln_b, router_w, tm):
    n, d = x.shape
    e = router_w.shape[1]
    full = lambda shape: pl.BlockSpec(shape, lambda i: (0,) * len(shape))
    return pl.pallas_call(
        _oproj_kernel,
        grid=(n // tm,),
        in_specs=[pl.BlockSpec((tm, d), lambda i: (i, 0)), full((d, d)),
                  pl.BlockSpec((tm, d), lambda i: (i, 0)), full((1, d)), full((1, d)), full((e, d))],
        out_specs=[pl.BlockSpec((tm, d), lambda i: (i, 0)), pl.BlockSpec((e, tm), lambda i: (0, i))],
        out_shape=[jax.ShapeDtypeStruct((n, d), F32), jax.ShapeDtypeStruct((e, n), F32)],
        compiler_params=_cparams(("parallel",)),
        name="out_proj_ln",
    )(merged, w_o.astype(BF16), x, ln_g.reshape(1, d), ln_b.reshape(1, d), router_w.T)


def _first_max(v, idx, sentinel):
    m = jnp.max(v, axis=0, keepdims=True)
    am = jnp.min(jnp.where(v == m, idx, sentinel), axis=0, keepdims=True)
    return m, am


def _route_kernel(lt_ref, bias_ref, tri_ref, te_ref, w_ref, rk_ref, cnt_ref, carry_ref, *, tm):
    @pl.when(pl.program_id(0) == 0)
    def _():
        carry_ref[...] = jnp.zeros_like(carry_ref)

    e = N_EXPERTS
    gs = e // N_GROUPS
    scores = jax.nn.sigmoid(lt_ref[...])
    biased = scores + bias_ref[...]
    sub = lax.broadcasted_iota(I32, (gs, tm), 0)
    grp_rows = []
    for g in range(N_GROUPS):
        slab = biased[g * gs:(g + 1) * gs, :]
        m1, a1 = _first_max(slab, sub, gs)
        m2 = jnp.max(jnp.where(sub == a1, NEG_INF, slab), axis=0, keepdims=True)
        grp_rows.append(m1 + m2)
    grp = jnp.concatenate(grp_rows, axis=0)
    gidx = lax.broadcasted_iota(I32, (N_GROUPS, tm), 0)
    gsel = jnp.zeros((N_GROUPS, tm), F32)
    for _ in range(TOPK_GROUPS):
        _, am = _first_max(grp, gidx, N_GROUPS)
        hit = gidx == am
        gsel = jnp.where(hit, 1.0, gsel)
        grp = jnp.where(hit, NEG_INF, grp)
    masked = jnp.concatenate(
        [jnp.where(gsel[g:g + 1, :] > 0.0, biased[g * gs:(g + 1) * gs, :], NEG_INF) for g in range(N_GROUPS)],
        axis=0)
    eidx = lax.broadcasted_iota(I32, (e, tm), 0)
    member = jnp.zeros((e, tm), F32)
    tops = []
    ws = []
    for _ in range(TOP_K):
        _, am = _first_max(masked, eidx, e)
        hit = eidx == am
        tops.append(am)
        ws.append(jnp.sum(jnp.where(hit, scores, 0.0), axis=0, keepdims=True))
        member = jnp.where(hit, 1.0, member)
        masked = jnp.where(hit, NEG_INF, masked)
    wsum = ws[0]
    for k in range(1, TOP_K):
        wsum = wsum + ws[k]
    te_ref[...] = jnp.concatenate(tops, axis=0)
    w_ref[...] = jnp.concatenate(ws, axis=0) / wsum * ROUTED_SCALE

    cum = jnp.dot(member.astype(BF16), tri_ref[...], preferred_element_type=F32)
    carry = carry_ref[:, 0:1]
    rank = carry + cum - member
    rks = []
    for k in range(TOP_K):
        rks.append(jnp.sum(jnp.where(eidx == tops[k], rank, 0.0), axis=0, keepdims=True))
    rk_ref[...] = jnp.concatenate(rks, axis=0).astype(I32)
    new_carry = carry + cum[:, tm - 1:tm]
    carry_ref[...] = jnp.broadcast_to(new_carry, carry_ref.shape)
    cnt_ref[...] = jnp.broadcast_to(new_carry, cnt_ref.shape).astype(I32)


def _route(logits_t, router_bias, tm):
    e, n = logits_t.shape
    tri = jnp.triu(jnp.ones((tm, tm), F32)).astype(BF16)
    return pl.pallas_call(
        functools.partial(_route_kernel, tm=tm),
        grid=(n // tm,),
        in_specs=[pl.BlockSpec((e, tm), lambda i: (0, i)),
                  pl.BlockSpec((e, 1), lambda i: (0, 0)),
                  pl.BlockSpec((tm, tm), lambda i: (0, 0))],
        out_specs=[pl.BlockSpec((TOP_K, tm), lambda i: (0, i)),
                   pl.BlockSpec((TOP_K, tm), lambda i: (0, i)),
                   pl.BlockSpec((TOP_K, tm), lambda i: (0, i)),
                   pl.BlockSpec((e, 128), lambda i: (0, 0))],
        out_shape=[jax.ShapeDtypeStruct((TOP_K, n), I32), jax.ShapeDtypeStruct((TOP_K, n), F32),
                   jax.ShapeDtypeStruct((TOP_K, n), I32), jax.ShapeDtypeStruct((e, 128), I32)],
        scratch_shapes=[pltpu.VMEM((e, 128), F32)],
        compiler_params=_cparams(("arbitrary",)),
        name="route",
    )(logits_t, router_bias.reshape(e, 1), tri)


def _dest_kernel(te_ref, rk_ref, ps_ref, d_ref):
    te = te_ref[...]
    e = N_EXPERTS
    tm = te.shape[1]
    eidx = lax.broadcasted_iota(I32, (e, tm), 0)
    ps = ps_ref[...]
    rows = []
    for k in range(TOP_K):
        rows.append(jnp.sum(jnp.where(eidx == te[k:k + 1, :], ps, 0), axis=0, keepdims=True))
    d_ref[...] = jnp.concatenate(rows, axis=0) + rk_ref[...]


def _dest(top_e, rank, pad_starts, tm):
    n = top_e.shape[1]
    return pl.pallas_call(
        _dest_kernel,
        grid=(n // tm,),
        in_specs=[pl.BlockSpec((TOP_K, tm), lambda i: (0, i)),
                  pl.BlockSpec((TOP_K, tm), lambda i: (0, i)),
                  pl.BlockSpec((N_EXPERTS, 1), lambda i: (0, 0))],
        out_specs=pl.BlockSpec((TOP_K, tm), lambda i: (0, i)),
        out_shape=jax.ShapeDtypeStruct((TOP_K, n), I32),
        compiler_params=_cparams(("parallel",)),
        name="dest",
    )(top_e, rank, pad_starts.reshape(N_EXPERTS, 1))


def _invert_kernel(dest_ref, inv_ref):
    def body(j, c):
        inv_ref[dest_ref[j]] = j
        return c

    lax.fori_loop(0, dest_ref.shape[0], body, 0, unroll=8)


def _invert_permutation(dest_flat):
    m = dest_flat.shape[0]
    return pl.pallas_call(
        _invert_kernel,
        in_specs=[pl.BlockSpec(memory_space=pltpu.SMEM)],
        out_specs=pl.BlockSpec(memory_space=pltpu.SMEM),
        out_shape=jax.ShapeDtypeStruct((m,), I32),
        name="invert_perm",
    )(dest_flat)


def _expert_ffn(xb, wgb_ref, wub_ref, wdb_ref):
    hg = jnp.dot(xb, wgb_ref[...], preferred_element_type=F32)
    hu = jnp.dot(xb, wub_ref[...], preferred_element_type=F32)
    hh = (hg * jax.nn.sigmoid(hg)) * hu
    return jnp.dot(hh.astype(BF16), wdb_ref[...], preferred_element_type=F32)


def _expert_kernel(vt_ref, ve_ref, lo_ref, hi_ref, inv_ref, x_hbm, wg_ref, wu_ref, wd_ref, out_hbm,
                   wgb_ref, wub_ref, wdb_ref, xbuf, obuf, xb_ref, cur_ref, gsem, ssem, *, n_tok, n_tiles):
    v = pl.program_id(0)
    rows = xbuf.shape[1]
    tile = vt_ref[v]
    lo = lo_ref[v]
    hi = hi_ref[v]
    nonempty = hi > lo
    first = jnp.logical_and(nonempty, lo == 0)
    slot = tile % 2
    other = 1 - slot

    def gather_row(t, s, r):
        tok = inv_ref[t * rows + r] & (n_tok - 1)
        return pltpu.make_async_copy(x_hbm.at[pl.ds(tok, 1)], xbuf.at[s, pl.ds(r, 1)], gsem.at[s])

    def scatter_row(t, s, r):
        j = inv_ref[t * rows + r]
        return pltpu.make_async_copy(obuf.at[s, pl.ds(r, 1)], out_hbm.at[pl.ds(j, 1)], ssem.at[s])

    def wait_gathered_tile(s):
        pltpu.make_async_copy(x_hbm.at[pl.ds(0, rows)], xbuf.at[s], gsem.at[s]).wait()

    def wait_scattered_tile(s):
        pltpu.make_async_copy(obuf.at[s], out_hbm.at[pl.ds(0, rows)], ssem.at[s]).wait()

    @pl.when(v == 0)
    def _():
        cur_ref[0] = -1

    @pl.when(jnp.logical_and(nonempty, cur_ref[0] != ve_ref[v]))
    def _():
        wgb_ref[...] = wg_ref[0].astype(BF16)
        wub_ref[...] = wu_ref[0].astype(BF16)
        wdb_ref[...] = wd_ref[0].astype(BF16)
        cur_ref[0] = ve_ref[v]

    @pl.when(jnp.logical_and(first, tile == 0))
    def _():
        def issue(r, c):
            gather_row(0, 0, r).start()
            return c

        lax.fori_loop(0, rows, issue, 0)

    @pl.when(jnp.logical_and(first, tile >= 2))
    def _():
        wait_scattered_tile(slot)

    @pl.when(first)
    def _():
        wait_gathered_tile(slot)

    nxt = jnp.minimum(tile + 1, n_tiles - 1)

    @pl.when(jnp.logical_and(first, tile == 0))
    def _():
        xb_ref[...] = xbuf[slot].astype(BF16)
        for r in range(rows):
            gather_row(nxt, other, r).start(priority=ROW_COPY_PRIORITY)
        obuf[slot] = _expert_ffn(xb_ref[...], wgb_ref, wub_ref, wdb_ref)

    @pl.when(jnp.logical_and(first, tile >= 1))
    def _():
        xb_ref[...] = xbuf[slot].astype(BF16)
        for r in range(rows):
            gather_row(nxt, other, r).start(priority=ROW_COPY_PRIORITY)
            scatter_row(tile - 1, other, r).start(priority=ROW_COPY_PRIORITY)
        obuf[slot] = _expert_ffn(xb_ref[...], wgb_ref, wub_ref, wdb_ref)

    @pl.when(jnp.logical_and(nonempty, lo > 0))
    def _():
        y = _expert_ffn(xbuf[slot].astype(BF16), wgb_ref, wub_ref, wdb_ref)
        row = lax.broadcasted_iota(I32, (rows, 1), 0)
        mine = jnp.logical_and(row >= lo, row < hi)
        obuf[slot] = jnp.where(mine, y, obuf[slot])

    @pl.when(v == pl.num_programs(0) - 1)
    def _():
        last = n_tiles - 1
        ls = last % 2

        def issue(r, c):
            scatter_row(last, ls, r).start()
            return c

        lax.fori_loop(0, rows, issue, 0)
        wait_scattered_tile(1 - ls)
        wait_scattered_tile(ls)
        wait_gathered_tile(1 - ls)


def _experts(x1, inv, visit_tile, visit_expert, visit_lo, visit_hi, w_gate, w_up, w_down):
    n_tok, d = x1.shape
    m = inv.shape[0]
    de = w_gate.shape[2]
    n_tiles = m // EXPERT_TILE
    grid_spec = pltpu.PrefetchScalarGridSpec(
        num_scalar_prefetch=5,
        grid=(visit_tile.shape[0],),
        in_specs=[pl.BlockSpec(memory_space=pl.ANY),
                  pl.BlockSpec((1, d, de), lambda v, vt, ve, lo, hi, inv: (ve[v], 0, 0)),
                  pl.BlockSpec((1, d, de), lambda v, vt, ve, lo, hi, inv: (ve[v], 0, 0)),
                  pl.BlockSpec((1, de, d), lambda v, vt, ve, lo, hi, inv: (ve[v], 0, 0))],
        out_specs=pl.BlockSpec(memory_space=pl.ANY),
        scratch_shapes=[pltpu.VMEM((d, de), BF16), pltpu.VMEM((d, de), BF16), pltpu.VMEM((de, d), BF16),
                        pltpu.VMEM((2, EXPERT_TILE, d), F32), pltpu.VMEM((2, EXPERT_TILE, d), F32),
                        pltpu.VMEM((EXPERT_TILE, d), BF16), pltpu.SMEM((1,), I32), pltpu.SemaphoreType.DMA((2,)), pltpu.SemaphoreType.DMA((2,))],
    )
    return pl.pallas_call(
        functools.partial(_expert_kernel, n_tok=n_tok, n_tiles=n_tiles),
        grid_spec=grid_spec,
        out_shape=jax.ShapeDtypeStruct((m, d), F32),
        compiler_params=_cparams(("arbitrary",)),
        name="experts",
    )(visit_tile, visit_expert, visit_lo, visit_hi, inv, x1, w_gate, w_up, w_down)


def _combine_kernel(x1_ref, w_ref, y_ref, sg_ref, su_ref, sd_ref, g_ref, b_ref, o_ref):
    x1 = x1_ref[...]
    xb = x1.astype(BF16)
    hg = jnp.dot(xb, sg_ref[...], preferred_element_type=F32)
    hu = jnp.dot(xb, su_ref[...], preferred_element_type=F32)
    hh = (hg * jax.nn.sigmoid(hg)) * hu
    shared = jnp.dot(hh.astype(BF16), sd_ref[...], preferred_element_type=F32)
    w = w_ref[...]
    routed = y_ref[0] * w[:, 0:1]
    for k in range(1, TOP_K):
        routed = routed + y_ref[k] * w[:, k:k + 1]
    o_ref[...] = _layer_norm(ALPHA * x1 + (routed + shared), g_ref[...], b_ref[...])


def _combine(x1, w_tok, y8, s_gate, s_up, s_down, ln_g, ln_b, tm):
    n, d = x1.shape
    de = s_gate.shape[1]
    full = lambda shape: pl.BlockSpec(shape, lambda i: (0,) * len(shape))
    return pl.pallas_call(
        _combine_kernel,
        grid=(n // tm,),
        in_specs=[pl.BlockSpec((tm, d), lambda i: (i, 0)),
                  pl.BlockSpec((tm, TOP_K), lambda i: (i, 0)),
                  pl.BlockSpec((TOP_K, tm, d), lambda i: (0, i, 0)),
                  full((d, de)), full((d, de)), full((de, d)), full((1, d)), full((1, d))],
        out_specs=pl.BlockSpec((tm, d), lambda i: (i, 0)),
        out_shape=jax.ShapeDtypeStruct((n, d), F32),
        compiler_params=_cparams(("parallel",)),
        name="combine",
    )(x1, w_tok, y8, s_gate.astype(BF16), s_up.astype(BF16), s_down.astype(BF16),
      ln_g.reshape(1, d), ln_b.reshape(1, d))


def _visit_plan(counts, n_rows):
    e = counts.shape[0]
    n_tiles = n_rows // EXPERT_TILE
    ends = jnp.cumsum(counts)
    starts = ends - counts
    pos = jnp.sort(jnp.concatenate([jnp.arange(n_tiles, dtype=I32) * EXPERT_TILE, starts]))
    nxt = jnp.concatenate([pos[1:], jnp.full((1,), n_rows, I32)])
    tile = jnp.minimum(pos // EXPERT_TILE, n_tiles - 1)
    expert = jnp.minimum(jnp.sum((ends[None, :] <= pos[:, None]).astype(I32), axis=1), e - 1)
    return starts, tile, expert, pos - tile * EXPERT_TILE, nxt - tile * EXPERT_TILE


def _moe(x1, logits_t, router_bias, e_w_gate, e_w_up, e_w_down, s_w_gate, s_w_up, s_w_down, ln_g, ln_b):
    n, d = x1.shape
    top_e, w_t, rank, counts = _route(logits_t, router_bias, tm=512)
    starts, v_tile, v_expert, v_lo, v_hi = _visit_plan(counts[:, 0], n * TOP_K)
    dest = _dest(top_e, rank, starts, tm=1024)
    inv = _invert_permutation(dest.reshape(n * TOP_K))
    y8 = _experts(x1, inv, v_tile, v_expert, v_lo, v_hi, e_w_gate, e_w_up, e_w_down)
    return _combine(x1, w_t.T, y8.reshape(TOP_K, n, d), s_w_gate, s_w_up, s_w_down, ln_g, ln_b, tm=128)


def _layer(x, w_in, b_gate, m_conv_w, m_conv_b, m_wq, m_wk, m_wv, m_w_if, m_b_if, m_norm_w, m_skip,
           r_conv_w, r_conv_b, r_wa, r_ba, r_wx, r_bx, r_lambda, w_pm, w_pr, w_o, ln1_g, ln1_b,
           router_w, router_bias, e_w_gate, e_w_up, e_w_down, s_w_gate, s_w_up, s_w_down, ln2_g, ln2_b):
    batch, seq, d = x.shape
    n = batch * seq
    xt = x.reshape(n, d)
    proj = _matmul(xt.astype(BF16), w_in.astype(BF16), 1024, 1024, F32)
    o_z = M_WIDTH
    o_xr = 2 * M_WIDTH
    o_xg = o_xr + R_WIDTH
    o_gate = o_xg + R_WIDTH
    del o_z
    xc, q, k, v, g, gt = _mlstm_prep(proj, seq, m_conv_w, m_conv_b, m_wq, m_wk, m_wv, m_w_if, m_b_if, tm=256)
    y_m = _mlstm(q, k, v, g, gt, xc, proj, m_norm_w, m_skip, batch, seq, MLSTM_CHUNK)
    y_r = _rglru(proj, batch, seq, o_xr, o_xg, r_conv_w, r_conv_b, r_wa, r_ba, r_wx, r_bx, r_lambda,
                 tm=512, cw=512)
    merged = _merge(y_m, y_r, w_pm, w_pr, proj, o_gate, b_gate, tm=512, bn=1024)
    x1, logits_t = _oproj(merged, w_o, xt, ln1_g, ln1_b, router_w, tm=256)
    out = _moe(x1, logits_t, router_bias, e_w_gate, e_w_up, e_w_down, s_w_gate, s_w_up, s_w_down, ln2_g, ln2_b)
    return out.reshape(batch, seq, d)


def kernel(x, w_in, b_gate, m_conv_w, m_conv_b, m_wq, m_wk, m_wv, m_w_if, m_b_if, m_norm_w, m_skip, r_conv_w, r_conv_b, r_wa, r_ba, r_wx, r_bx, r_lambda, w_pm, w_pr, w_o, ln1_g, ln1_b, router_w, router_bias, e_w_gate, e_w_up, e_w_down, s_w_gate, s_w_up, s_w_down, ln2_g, ln2_b):
    for l in range(DEPTH):
        x = _layer(x, w_in[l], b_gate[l], m_conv_w[l], m_conv_b[l], m_wq[l], m_wk[l], m_wv[l],
                   m_w_if[l], m_b_if[l], m_norm_w[l], m_skip[l], r_conv_w[l], r_conv_b[l],
                   r_wa[l], r_ba[l], r_wx[l], r_bx[l], r_lambda[l], w_pm[l], w_pr[l], w_o[l],
                   ln1_g[l], ln1_b[l], router_w[l], router_bias[l], e_w_gate[l], e_w_up[l],
                   e_w_down[l], s_w_gate[l], s_w_up[l], s_w_down[l], ln2_g[l], ln2_b[l])
    return x
```

```python
import functools
import math

import jax
import jax.numpy as jnp
from jax import lax
from jax.experimental import pallas as pl
from jax.experimental.pallas import tpu as pltpu

F32 = jnp.float32
BF16 = jnp.bfloat16
I32 = jnp.int32
U32 = jnp.uint32

D_MODEL = 2048
M_WIDTH = 2048
M_HEADS = 8
M_HEAD_DIM = 256
M_QKV_BLOCK = 4
CONV_WIDTH = 4
R_WIDTH = 2560
R_BLOCK = 256
LRU_C = 8.0
N_EXPERTS = 64
TOP_K = 8
N_GROUPS = 8
TOPK_GROUPS = 4
D_EXPERT = 512
ROUTED_SCALE = 2.5
DEPTH = 1
ALPHA = (2.0 * DEPTH) ** 0.25
LN_EPS = 1e-5

V7X_VMEM_LIMIT = 56 * 1024 * 1024
HALO = 8
MLSTM_CHUNK = 256
EXPERT_TILE = 256
NEG_INF = float("-inf")
ROUTER_LANES = 128
ROW_COPY_PRIORITY = 1


def _cparams(sem, vmem=V7X_VMEM_LIMIT):
    return pltpu.CompilerParams(dimension_semantics=sem, vmem_limit_bytes=vmem)


def _inproj_kernel(a_ref, w_ref, o_ref, wb_ref):
    @pl.when(pl.program_id(1) == 0)
    def _():
        wb_ref[...] = w_ref[...].astype(BF16)

    o_ref[...] = jnp.dot(a_ref[...], wb_ref[...], preferred_element_type=F32)


def _in_proj(a, w, bm, bn):
    m, k = a.shape
    n = w.shape[1]
    return pl.pallas_call(
        _inproj_kernel,
        grid=(n // bn, m // bm),
        in_specs=[pl.BlockSpec((bm, k), lambda j, i: (i, 0)),
                  pl.BlockSpec((k, bn), lambda j, i: (0, j))],
        out_specs=pl.BlockSpec((bm, bn), lambda j, i: (i, j)),
        out_shape=jax.ShapeDtypeStruct((m, n), F32),
        scratch_shapes=[pltpu.VMEM((k, bn), BF16)],
        compiler_params=_cparams(("parallel", "arbitrary")),
        name="in_proj",
    )(a, w)


def _log_sigmoid(x):
    return jnp.minimum(x, 0.0) - jnp.log1p(jnp.exp(-jnp.abs(x)))


def _shift_rows(x3, prev_group, j):
    rot = pltpu.roll(x3, j, axis=1)
    prev = jnp.concatenate([pltpu.roll(prev_group, j, axis=1), rot[:-1]], axis=0)
    sub = lax.broadcasted_iota(I32, x3.shape, 1)
    return jnp.where(sub >= j, rot, prev)


def _causal_conv(halo, x, cw_ref, cb_ref):
    tm, c = x.shape
    x3 = x.reshape(tm // HALO, HALO, c)
    h3 = halo.reshape(1, HALO, c)
    last = CONV_WIDTH - 1
    y = cb_ref[...] + x3 * cw_ref[last:last + 1, :]
    for j in range(1, CONV_WIDTH):
        y = y + _shift_rows(x3, h3, j) * cw_ref[last - j:last - j + 1, :]
    return y


def _mprep_kernel(xm_ref, halo_ref, cw_ref, cb_ref, wq_ref, wk_ref, wv_ref, wif_ref, wift_ref,
                  bif_ref, bift_ref, xc_ref, q_ref, k_ref, v_ref, g_ref, gt_ref,
                  *, tm, tiles_per_seq):
    i = pl.program_id(0)
    first = (i % tiles_per_seq) == 0
    halo = jnp.where(first, 0.0, halo_ref[...])
    xm = xm_ref[...]
    y = _causal_conv(halo, xm, cw_ref, cb_ref).reshape(tm, M_WIDTH)
    xc = y * jax.nn.sigmoid(y)
    xc_ref[...] = xc
    xcb = xc.astype(BF16)
    xmb = xm.astype(BF16)
    nblk = M_WIDTH // M_HEAD_DIM
    for g in range(nblk):
        sl = slice(g * M_HEAD_DIM, (g + 1) * M_HEAD_DIM)
        q_ref[:, sl] = jnp.dot(xcb[:, sl], wq_ref[g], preferred_element_type=F32).astype(BF16)
        k_ref[:, sl] = jnp.dot(xcb[:, sl], wk_ref[g], preferred_element_type=F32).astype(BF16)
        v_ref[:, sl] = jnp.dot(xmb[:, sl], wv_ref[g], preferred_element_type=F32).astype(BF16)
    qb, kb, vb = q_ref[...], k_ref[...], v_ref[...]
    w = M_WIDTH
    g = (jnp.dot(qb, wif_ref[0:w, :], preferred_element_type=F32)
         + jnp.dot(kb, wif_ref[w:2 * w, :], preferred_element_type=F32)
         + jnp.dot(vb, wif_ref[2 * w:3 * w, :], preferred_element_type=F32) + bif_ref[...])
    nt = (((1,), (1,)), ((), ()))
    gt = (lax.dot_general(wift_ref[:, 0:w], qb, nt, preferred_element_type=F32)
          + lax.dot_general(wift_ref[:, w:2 * w], kb, nt, preferred_element_type=F32)
          + lax.dot_general(wift_ref[:, 2 * w:3 * w], vb, nt, preferred_element_type=F32) + bift_ref[...])
    col = lax.broadcasted_iota(I32, g.shape, 1)
    g_ref[...] = jnp.where(col >= M_HEADS, _log_sigmoid(g), g)
    row = lax.broadcasted_iota(I32, gt.shape, 0)
    gt_ref[...] = jnp.where(row >= M_HEADS, _log_sigmoid(gt), gt)


def _block_diag_dense(w, group):
    nb, bi, bo = w.shape
    per = group // bi
    w4 = w.reshape(nb // per, per, bi, bo)
    eye = jnp.eye(per, dtype=w.dtype)
    return jnp.einsum("gaio,ab->gaibo", w4, eye).reshape(nb // per, group, group)


def _mlstm_prep(proj, seq, conv_w, conv_b, wq, wk, wv, w_if, b_if, tm):
    n = proj.shape[0]
    c = M_WIDTH
    nblk = c // M_HEAD_DIM
    tiles_per_seq = seq // tm
    wqd = _block_diag_dense(wq, M_HEAD_DIM).astype(BF16)
    wkd = _block_diag_dense(wk, M_HEAD_DIM).astype(BF16)
    wvd = _block_diag_dense(wv, M_HEAD_DIM).astype(BF16)
    wif = w_if.astype(BF16)
    wift = w_if.T.astype(BF16)
    ng = 2 * M_HEADS
    hb = tm // HALO
    full = lambda shape: pl.BlockSpec(shape, lambda i: (0,) * len(shape))
    return pl.pallas_call(
        functools.partial(_mprep_kernel, tm=tm, tiles_per_seq=tiles_per_seq),
        grid=(n // tm,),
        in_specs=[pl.BlockSpec((tm, c), lambda i: (i, 0)),
                  pl.BlockSpec((HALO, c), lambda i: (jnp.maximum(i * hb - 1, 0), 0)),
                  full((CONV_WIDTH, c)), full((1, c)),
                  full((nblk, M_HEAD_DIM, M_HEAD_DIM)), full((nblk, M_HEAD_DIM, M_HEAD_DIM)),
                  full((nblk, M_HEAD_DIM, M_HEAD_DIM)),
                  full((3 * c, ng)), full((ng, 3 * c)), full((1, ng)), full((ng, 1))],
        out_specs=[pl.BlockSpec((tm, c), lambda i: (i, 0)),
                   pl.BlockSpec((tm, c), lambda i: (i, 0)),
                   pl.BlockSpec((tm, c), lambda i: (i, 0)),
                   pl.BlockSpec((tm, c), lambda i: (i, 0)),
                   pl.BlockSpec((tm, ng), lambda i: (i, 0)),
                   pl.BlockSpec((ng, tm), lambda i: (0, i))],
        out_shape=[jax.ShapeDtypeStruct((n, c), F32),
                   jax.ShapeDtypeStruct((n, c), BF16),
                   jax.ShapeDtypeStruct((n, c), BF16),
                   jax.ShapeDtypeStruct((n, c), BF16),
                   jax.ShapeDtypeStruct((n, ng), F32),
                   jax.ShapeDtypeStruct((ng, n), F32)],
        compiler_params=_cparams(("parallel",)),
        name="mlstm_prep",
    )(proj, proj, conv_w, conv_b.reshape(1, c), wqd, wkd, wvd, wif, wift,
      b_if.reshape(1, ng), b_if.reshape(ng, 1))


def _mlstm_kernel(q_ref, k_ref, v_ref, g_ref, gt_ref, xc_ref, z_ref, nw_ref, sk_ref, o_ref,
                  c_ref, n_ref, m_ref, *, chunk):
    L = chunk
    hd = M_HEAD_DIM

    @pl.when(pl.program_id(1) == 0)
    def _():
        c_ref[...] = jnp.zeros_like(c_ref)
        n_ref[...] = jnp.zeros_like(n_ref)
        m_ref[...] = jnp.zeros_like(m_ref)

    rows = lax.broadcasted_iota(I32, (L, L), 0)
    cols = lax.broadcasted_iota(I32, (L, L), 1)
    causal = cols <= rows
    tril = jnp.where(causal, 1.0, 0.0).astype(F32)
    triu = jnp.where(rows <= cols, 1.0, 0.0).astype(F32)
    g = g_ref[...]
    gt = gt_ref[...]
    hi = lax.Precision.HIGHEST
    bcol_all = jnp.dot(tril, g, precision=hi, preferred_element_type=F32)
    brow_all = jnp.dot(gt, triu, precision=hi, preferred_element_type=F32)
    k_scale = hd ** -0.5
    nt = (((1,), (1,)), ((), ()))
    tn = (((0,), (0,)), ((), ()))

    for h in range(M_HEADS):
        sl = slice(h * hd, (h + 1) * hd)
        qh = q_ref[:, sl]
        kh = k_ref[:, sl]
        vh = v_ref[:, sl]
        i_col = g[:, h:h + 1]
        b_col = bcol_all[:, M_HEADS + h:M_HEADS + h + 1]
        i_row = gt[h:h + 1, :]
        b_row = brow_all[M_HEADS + h:M_HEADS + h + 1, :]
        m_prev = m_ref[h, 0:1, 0:1]
        c_prev = c_ref[h]
        n_prev = n_ref[h]

        dmat = jnp.where(causal, b_col - b_row + i_row, NEG_INF)
        a_col = b_col + m_prev
        m_row = jnp.maximum(a_col, jnp.max(dmat, axis=1, keepdims=True))
        s = lax.dot_general(qh, kh, nt, preferred_element_type=F32) * k_scale
        s = s * jnp.exp(dmat - m_row)
        inter = jnp.exp(a_col - m_row)
        num = inter * jnp.dot(qh, c_prev.astype(BF16), preferred_element_type=F32) \
            + jnp.dot(s.astype(BF16), vh, preferred_element_type=F32)
        qn = jnp.sum(qh.astype(F32) * n_prev, axis=1, keepdims=True)
        den = inter * qn + jnp.sum(s, axis=1, keepdims=True)
        hval = num * (1.0 / jnp.maximum(jnp.abs(den), jnp.exp(-m_row)))

        mu = jnp.mean(hval, axis=1, keepdims=True)
        cen = hval - mu
        var = jnp.mean(cen * cen, axis=1, keepdims=True)
        hn = cen * lax.rsqrt(var + LN_EPS) * nw_ref[:, sl]
        zz = z_ref[:, sl]
        o_ref[:, sl] = ((hn + sk_ref[:, sl] * xc_ref[:, sl]) * (zz * jax.nn.sigmoid(zz))).astype(o_ref.dtype)

        b_last = b_col[L - 1:L, :]
        w_log = b_last - b_col + i_col
        m_new = jnp.maximum(b_last + m_prev, jnp.max(w_log, axis=0, keepdims=True))
        decay = jnp.exp(b_last + m_prev - m_new)
        kw = kh.astype(F32) * (jnp.exp(w_log - m_new) * k_scale)
        c_ref[h] = decay * c_prev + lax.dot_general(kw.astype(BF16), vh, tn, preferred_element_type=F32)
        n_ref[h] = decay * n_prev + jnp.sum(kw, axis=0, keepdims=True)
        m_ref[h] = jnp.broadcast_to(m_new, m_ref.shape[1:])


def _mlstm(q, k, v, g, gt, xc, proj, norm_w, skip, batch, seq, chunk):
    n, c = q.shape
    nc = seq // chunk
    ng = 2 * M_HEADS
    zcol = M_WIDTH // c
    row = lambda b, j: (b * nc + j, 0)
    return pl.pallas_call(
        functools.partial(_mlstm_kernel, chunk=chunk),
        grid=(batch, nc),
        in_specs=[pl.BlockSpec((chunk, c), row), pl.BlockSpec((chunk, c), row), pl.BlockSpec((chunk, c), row),
                  pl.BlockSpec((chunk, ng), row),
                  pl.BlockSpec((ng, chunk), lambda b, j: (0, b * nc + j)),
                  pl.BlockSpec((chunk, c), row),
                  pl.BlockSpec((chunk, c), lambda b, j: (b * nc + j, zcol)),
                  pl.BlockSpec((1, c), lambda b, j: (0, 0)),
                  pl.BlockSpec((1, c), lambda b, j: (0, 0))],
        out_specs=pl.BlockSpec((chunk, c), row),
        out_shape=jax.ShapeDtypeStruct((n, c), BF16),
        scratch_shapes=[pltpu.VMEM((M_HEADS, M_HEAD_DIM, M_HEAD_DIM), F32),
                        pltpu.VMEM((M_HEADS, 1, M_HEAD_DIM), F32),
                        pltpu.VMEM((M_HEADS, 8, 128), F32)],
        compiler_params=_cparams(("parallel", "arbitrary")),
        name="mlstm_chunk",
    )(q, k, v, g, gt, xc, proj, norm_w.reshape(1, c), skip.reshape(1, c))


def _gelu_tanh(x):
    return 0.5 * x * (1.0 + jnp.tanh(math.sqrt(2.0 / math.pi) * (x + 0.044715 * (x * x * x))))


def _rglru_kernel(xr_ref, halo_ref, xg_ref, cw_ref, cb_ref, wa_ref, ba_ref, wx_ref, bx_ref, lam_ref,
                  o_ref, a_ref, b_ref, h_ref, *, tm, cw):
    t = pl.program_id(2)

    @pl.when(t == 0)
    def _():
        h_ref[...] = jnp.zeros_like(h_ref)

    halo = jnp.where(t == 0, 0.0, halo_ref[...])
    xc = _causal_conv(halo, xr_ref[...], cw_ref, cb_ref).reshape(tm, cw)
    xcb = xc.astype(BF16)
    nblk = cw // R_BLOCK
    ra = []
    rx = []
    for g in range(nblk):
        sl = slice(g * R_BLOCK, (g + 1) * R_BLOCK)
        ra.append(jnp.dot(xcb[:, sl], wa_ref[g], preferred_element_type=F32))
        rx.append(jnp.dot(xcb[:, sl], wx_ref[g], preferred_element_type=F32))
    r = jax.nn.sigmoid(jnp.concatenate(ra, axis=1) + ba_ref[...])
    ig = jax.nn.sigmoid(jnp.concatenate(rx, axis=1) + bx_ref[...])
    nl = -lam_ref[...]
    softplus = jnp.maximum(nl, 0.0) + jnp.log1p(jnp.exp(-jnp.abs(nl)))
    a = jnp.exp((-LRU_C * softplus) * r)
    b = jnp.sqrt(1.0 - a * a) * (ig * xc)

    a = a.reshape(tm // 8, 8, cw)
    b = b.reshape(tm // 8, 8, cw)
    sub = lax.broadcasted_iota(I32, a.shape, 1)
    for d in (1, 2, 4):
        keep = sub >= d
        a_sh = pltpu.roll(a, d, axis=1)
        b_sh = pltpu.roll(b, d, axis=1)
        b = jnp.where(keep, a * b_sh + b, b)
        a = jnp.where(keep, a * a_sh, a)
    a_ref[...] = a.reshape(tm, cw)
    b_ref[...] = b.reshape(tm, cw)

    def body(g, h):
        r0 = pl.multiple_of(g * 8, 8)
        hh = b_ref[pl.ds(r0, 8), :] + a_ref[pl.ds(r0, 8), :] * h
        b_ref[pl.ds(r0, 8), :] = hh
        return hh[7:8, :]

    h_ref[...] = lax.fori_loop(0, tm // 8, body, h_ref[...], unroll=8)
    o_ref[...] = (b_ref[...] * _gelu_tanh(xg_ref[...])).astype(o_ref.dtype)


def _rglru(proj, batch, seq, xr_off, xg_off, conv_w, conv_b, wa, ba, wx, bx, lam, tm, cw):
    n = proj.shape[0]
    ncol = R_WIDTH // cw
    nt = seq // tm
    per = cw // R_BLOCK
    xr_cb = xr_off // cw
    xg_cb = xg_off // cw
    hb = tm // HALO
    colv = lambda shape: pl.BlockSpec(shape, lambda b, j, t: (0, j))
    return pl.pallas_call(
        functools.partial(_rglru_kernel, tm=tm, cw=cw),
        grid=(batch, ncol, nt),
        in_specs=[pl.BlockSpec((tm, cw), lambda b, j, t: (b * nt + t, xr_cb + j)),
                  pl.BlockSpec((HALO, cw), lambda b, j, t: (jnp.maximum((b * nt + t) * hb - 1, 0), xr_cb + j)),
                  pl.BlockSpec((tm, cw), lambda b, j, t: (b * nt + t, xg_cb + j)),
                  colv((CONV_WIDTH, cw)), colv((1, cw)),
                  pl.BlockSpec((per, R_BLOCK, R_BLOCK), lambda b, j, t: (j, 0, 0)), colv((1, cw)),
                  pl.BlockSpec((per, R_BLOCK, R_BLOCK), lambda b, j, t: (j, 0, 0)), colv((1, cw)),
                  colv((1, cw))],
        out_specs=pl.BlockSpec((tm, cw), lambda b, j, t: (b * nt + t, j)),
        out_shape=jax.ShapeDtypeStruct((n, R_WIDTH), BF16),
        scratch_shapes=[pltpu.VMEM((tm, cw), F32), pltpu.VMEM((tm, cw), F32), pltpu.VMEM((1, cw), F32)],
        compiler_params=_cparams(("parallel", "parallel", "arbitrary")),
        name="rglru",
    )(proj, proj, proj, conv_w, conv_b.reshape(1, R_WIDTH), wa.astype(BF16), ba.reshape(1, R_WIDTH),
      wx.astype(BF16), bx.reshape(1, R_WIDTH), lam.reshape(1, R_WIDTH))


def _merge_kernel(ym_ref, yr_ref, wpm_ref, wpr_ref, g0_ref, g1_ref, bg_ref, o_ref):
    g0 = jax.nn.sigmoid(g0_ref[...] + bg_ref[0:1, :])
    g1 = jax.nn.sigmoid(g1_ref[...] + bg_ref[1:2, :])
    pm = jnp.dot(ym_ref[...], wpm_ref[...], preferred_element_type=F32)
    pr = jnp.dot(yr_ref[...], wpr_ref[...], preferred_element_type=F32)
    o_ref[...] = (g0 * pm + g1 * pr).astype(o_ref.dtype)


def _merge(ym, yr, w_pm, w_pr, proj, gate_off, b_gate, tm, bn):
    n = ym.shape[0]
    d = w_pm.shape[1]
    g0_cb = gate_off // bn
    g1_cb = (gate_off + d) // bn
    return pl.pallas_call(
        _merge_kernel,
        grid=(d // bn, n // tm),
        in_specs=[pl.BlockSpec((tm, ym.shape[1]), lambda j, i: (i, 0)),
                  pl.BlockSpec((tm, yr.shape[1]), lambda j, i: (i, 0)),
                  pl.BlockSpec((w_pm.shape[0], bn), lambda j, i: (0, j)),
                  pl.BlockSpec((w_pr.shape[0], bn), lambda j, i: (0, j)),
                  pl.BlockSpec((tm, bn), lambda j, i: (i, g0_cb + j)),
                  pl.BlockSpec((tm, bn), lambda j, i: (i, g1_cb + j)),
                  pl.BlockSpec((2, bn), lambda j, i: (0, j))],
        out_specs=pl.BlockSpec((tm, bn), lambda j, i: (i, j)),
        out_shape=jax.ShapeDtypeStruct((n, d), BF16),
        compiler_params=_cparams(("parallel", "parallel")),
        name="merge",
    )(ym, yr, w_pm.astype(BF16), w_pr.astype(BF16), proj, proj, b_gate)


def _layer_norm(y, g, b):
    mu = jnp.mean(y, axis=1, keepdims=True)
    cen = y - mu
    var = jnp.mean(cen * cen, axis=1, keepdims=True)
    return cen * lax.rsqrt(var + LN_EPS) * g + b


def _oproj_kernel(mg_ref, wo_ref, x_ref, g_ref, b_ref, rwh_ref, rwl_ref, x1_ref, lg_ref):
    y = ALPHA * x_ref[...] + jnp.dot(mg_ref[...], wo_ref[...], preferred_element_type=F32)
    x1 = _layer_norm(y, g_ref[...], b_ref[...])
    x1_ref[...] = x1
    xh = x1.astype(BF16)
    xl = (x1 - xh.astype(F32)).astype(BF16)
    lg_ref[...] = (jnp.dot(xh, rwh_ref[...], preferred_element_type=F32)
                   + (jnp.dot(xl, rwh_ref[...], preferred_element_type=F32)
                      + jnp.dot(xh, rwl_ref[...], preferred_element_type=F32)))


def _oproj(merged, w_o, x, ln_g, ln_b, router_w, tm):
    n, d = x.shape
    e = router_w.shape[1]
    rw = jnp.pad(router_w, ((0, 0), (0, ROUTER_LANES - e)))
    rw_hi = rw.astype(BF16)
    rw_lo = (rw - rw_hi.astype(F32)).astype(BF16)
    full = lambda shape: pl.BlockSpec(shape, lambda i: (0,) * len(shape))
    return pl.pallas_call(
        _oproj_kernel,
        grid=(n // tm,),
        in_specs=[pl.BlockSpec((tm, d), lambda i: (i, 0)), full((d, d)),
                  pl.BlockSpec((tm, d), lambda i: (i, 0)), full((1, d)), full((1, d)),
                  full((d, ROUTER_LANES)), full((d, ROUTER_LANES))],
        out_specs=[pl.BlockSpec((tm, d), lambda i: (i, 0)), pl.BlockSpec((tm, ROUTER_LANES), lambda i: (i, 0))],
        out_shape=[jax.ShapeDtypeStruct((n, d), F32), jax.ShapeDtypeStruct((n, ROUTER_LANES), F32)],
        compiler_params=_cparams(("parallel",)),
        name="out_proj_ln",
    )(merged, w_o.astype(BF16), x, ln_g.reshape(1, d), ln_b.reshape(1, d), rw_hi, rw_lo)


def _first_max(v, idx, sentinel):
    m = jnp.max(v, axis=0, keepdims=True)
    am = jnp.min(jnp.where(v == m, idx, sentinel), axis=0, keepdims=True)
    return m, am


def _route_kernel(lg_ref, bias_ref, tri_ref, te_ref, w_ref, rk_ref, cnt_ref, carry_ref, *, tm):
    @pl.when(pl.program_id(0) == 0)
    def _():
        carry_ref[...] = jnp.zeros_like(carry_ref)

    e = N_EXPERTS
    gs = e // N_GROUPS
    scores = jax.nn.sigmoid(jnp.transpose(lg_ref[...])[0:e, :])
    biased = scores + bias_ref[...]
    sub = lax.broadcasted_iota(I32, (gs, tm), 0)
    grp_rows = []
    for g in range(N_GROUPS):
        slab = biased[g * gs:(g + 1) * gs, :]
        m1, a1 = _first_max(slab, sub, gs)
        m2 = jnp.max(jnp.where(sub == a1, NEG_INF, slab), axis=0, keepdims=True)
        grp_rows.append(m1 + m2)
    grp = jnp.concatenate(grp_rows, axis=0)
    gidx = lax.broadcasted_iota(I32, (N_GROUPS, tm), 0)
    gsel = jnp.zeros((N_GROUPS, tm), F32)
    for _ in range(TOPK_GROUPS):
        _, am = _first_max(grp, gidx, N_GROUPS)
        hit = gidx == am
        gsel = jnp.where(hit, 1.0, gsel)
        grp = jnp.where(hit, NEG_INF, grp)
    masked = jnp.concatenate(
        [jnp.where(gsel[g:g + 1, :] > 0.0, biased[g * gs:(g + 1) * gs, :], NEG_INF) for g in range(N_GROUPS)],
        axis=0)
    eidx = lax.broadcasted_iota(I32, (e, tm), 0)
    member = jnp.zeros((e, tm), F32)
    tops = []
    ws = []
    for _ in range(TOP_K):
        _, am = _first_max(masked, eidx, e)
        hit = eidx == am
        tops.append(am)
        ws.append(jnp.sum(jnp.where(hit, scores, 0.0), axis=0, keepdims=True))
        member = jnp.where(hit, 1.0, member)
        masked = jnp.where(hit, NEG_INF, masked)
    wsum = ws[0]
    for k in range(1, TOP_K):
        wsum = wsum + ws[k]
    te_ref[...] = jnp.concatenate(tops, axis=0)
    w_ref[...] = jnp.concatenate(ws, axis=0) / wsum * ROUTED_SCALE

    cum = jnp.dot(member.astype(BF16), tri_ref[...], preferred_element_type=F32)
    carry = carry_ref[:, 0:1]
    rank = carry + cum - member
    rks = []
    for k in range(TOP_K):
        rks.append(jnp.sum(jnp.where(eidx == tops[k], rank, 0.0), axis=0, keepdims=True))
    rk_ref[...] = jnp.concatenate(rks, axis=0).astype(I32)
    new_carry = carry + cum[:, tm - 1:tm]
    carry_ref[...] = jnp.broadcast_to(new_carry, carry_ref.shape)
    cnt_ref[...] = jnp.broadcast_to(new_carry, cnt_ref.shape).astype(I32)


def _route(logits, router_bias, tm):
    n = logits.shape[0]
    e = router_bias.shape[0]
    tri = jnp.triu(jnp.ones((tm, tm), F32)).astype(BF16)
    return pl.pallas_call(
        functools.partial(_route_kernel, tm=tm),
        grid=(n // tm,),
        in_specs=[pl.BlockSpec((tm, ROUTER_LANES), lambda i: (i, 0)),
                  pl.BlockSpec((e, 1), lambda i: (0, 0)),
                  pl.BlockSpec((tm, tm), lambda i: (0, 0))],
        out_specs=[pl.BlockSpec((TOP_K, tm), lambda i: (0, i)),
                   pl.BlockSpec((TOP_K, tm), lambda i: (0, i)),
                   pl.BlockSpec((TOP_K, tm), lambda i: (0, i)),
                   pl.BlockSpec((e, 128), lambda i: (0, 0))],
        out_shape=[jax.ShapeDtypeStruct((TOP_K, n), I32), jax.ShapeDtypeStruct((TOP_K, n), F32),
                   jax.ShapeDtypeStruct((TOP_K, n), I32), jax.ShapeDtypeStruct((e, 128), I32)],
        scratch_shapes=[pltpu.VMEM((e, 128), F32)],
        compiler_params=_cparams(("arbitrary",)),
        name="route",
    )(logits, router_bias.reshape(e, 1), tri)


def _dest_kernel(te_ref, rk_ref, ps_ref, d_ref):
    te = te_ref[...]
    e = N_EXPERTS
    tm = te.shape[1]
    eidx = lax.broadcasted_iota(I32, (e, tm), 0)
    ps = ps_ref[...]
    rows = []
    for k in range(TOP_K):
        rows.append(jnp.sum(jnp.where(eidx == te[k:k + 1, :], ps, 0), axis=0, keepdims=True))
    d_ref[...] = jnp.concatenate(rows, axis=0) + rk_ref[...]


def _dest(top_e, rank, pad_starts, tm):
    n = top_e.shape[1]
    return pl.pallas_call(
        _dest_kernel,
        grid=(n // tm,),
        in_specs=[pl.BlockSpec((TOP_K, tm), lambda i: (0, i)),
                  pl.BlockSpec((TOP_K, tm), lambda i: (0, i)),
                  pl.BlockSpec((N_EXPERTS, 1), lambda i: (0, 0))],
        out_specs=pl.BlockSpec((TOP_K, tm), lambda i: (0, i)),
        out_shape=jax.ShapeDtypeStruct((TOP_K, n), I32),
        compiler_params=_cparams(("parallel",)),
        name="dest",
    )(top_e, rank, pad_starts.reshape(N_EXPERTS, 1))


def _invert_kernel(dest_ref, inv_ref):
    def body(j, c):
        inv_ref[dest_ref[j]] = j
        return c

    lax.fori_loop(0, dest_ref.shape[0], body, 0, unroll=8)


def _invert_permutation(dest_flat):
    m = dest_flat.shape[0]
    return pl.pallas_call(
        _invert_kernel,
        in_specs=[pl.BlockSpec(memory_space=pltpu.SMEM)],
        out_specs=pl.BlockSpec(memory_space=pltpu.SMEM),
        out_shape=jax.ShapeDtypeStruct((m,), I32),
        name="invert_perm",
    )(dest_flat)


def _expert_ffn(xb, wgb_ref, wub_ref, wdb_ref):
    hg = jnp.dot(xb, wgb_ref[...], preferred_element_type=F32)
    hu = jnp.dot(xb, wub_ref[...], preferred_element_type=F32)
    hh = (hg * jax.nn.sigmoid(hg)) * hu
    return jnp.dot(hh.astype(BF16), wdb_ref[...], preferred_element_type=F32)


def _expert_kernel(vt_ref, ve_ref, lo_ref, hi_ref, inv_ref, x_hbm, wg_ref, wu_ref, wd_ref, out_hbm,
                   wgb_ref, wub_ref, wdb_ref, xbuf, obuf, xb_ref, cur_ref, gsem, ssem, *, n_tok, n_tiles):
    v = pl.program_id(0)
    rows = xbuf.shape[1]
    tile = vt_ref[v]
    lo = lo_ref[v]
    hi = hi_ref[v]
    nonempty = hi > lo
    first = jnp.logical_and(nonempty, lo == 0)
    slot = tile % 2
    other = 1 - slot

    def gather_row(t, s, r):
        tok = inv_ref[t * rows + r] & (n_tok - 1)
        return pltpu.make_async_copy(x_hbm.at[pl.ds(tok, 1)], xbuf.at[s, pl.ds(r, 1)], gsem.at[s])

    def scatter_row(t, s, r):
        j = inv_ref[t * rows + r]
        return pltpu.make_async_copy(obuf.at[s, pl.ds(r, 1)], out_hbm.at[pl.ds(j, 1)], ssem.at[s])

    def wait_gathered_tile(s):
        pltpu.make_async_copy(x_hbm.at[pl.ds(0, rows)], xbuf.at[s], gsem.at[s]).wait()

    def wait_scattered_tile(s):
        pltpu.make_async_copy(obuf.at[s], out_hbm.at[pl.ds(0, rows)], ssem.at[s]).wait()

    @pl.when(v == 0)
    def _():
        cur_ref[0] = -1

    @pl.when(jnp.logical_and(nonempty, cur_ref[0] != ve_ref[v]))
    def _():
        wgb_ref[...] = wg_ref[0].astype(BF16)
        wub_ref[...] = wu_ref[0].astype(BF16)
        wdb_ref[...] = wd_ref[0].astype(BF16)
        cur_ref[0] = ve_ref[v]

    @pl.when(jnp.logical_and(first, tile == 0))
    def _():
        def issue(r, c):
            gather_row(0, 0, r).start()
            return c

        lax.fori_loop(0, rows, issue, 0)

    @pl.when(jnp.logical_and(first, tile >= 2))
    def _():
        wait_scattered_tile(slot)

    @pl.when(first)
    def _():
        wait_gathered_tile(slot)

    nxt = jnp.minimum(tile + 1, n_tiles - 1)

    @pl.when(jnp.logical_and(first, tile == 0))
    def _():
        xb_ref[...] = xbuf[slot].astype(BF16)
        for r in range(rows):
            gather_row(nxt, other, r).start(priority=ROW_COPY_PRIORITY)
        obuf[slot] = _expert_ffn(xb_ref[...], wgb_ref, wub_ref, wdb_ref)

    @pl.when(jnp.logical_and(first, tile >= 1))
    def _():
        xb_ref[...] = xbuf[slot].astype(BF16)
        for r in range(rows):
            gather_row(nxt, other, r).start(priority=ROW_COPY_PRIORITY)
            scatter_row(tile - 1, other, r).start(priority=ROW_COPY_PRIORITY)
        obuf[slot] = _expert_ffn(xb_ref[...], wgb_ref, wub_ref, wdb_ref)

    @pl.when(jnp.logical_and(nonempty, lo > 0))
    def _():
        y = _expert_ffn(xbuf[slot].astype(BF16), wgb_ref, wub_ref, wdb_ref)
        row = lax.broadcasted_iota(I32, (rows, 1), 0)
        mine = jnp.logical_and(row >= lo, row < hi)
        obuf[slot] = jnp.where(mine, y, obuf[slot])

    @pl.when(v == pl.num_programs(0) - 1)
    def _():
        last = n_tiles - 1
        ls = last % 2

        def issue(r, c):
            scatter_row(last, ls, r).start()
            return c

        lax.fori_loop(0, rows, issue, 0)
        wait_scattered_tile(1 - ls)
        wait_scattered_tile(ls)
        wait_gathered_tile(1 - ls)


def _experts(x1, inv, visit_tile, visit_expert, visit_lo, visit_hi, w_gate, w_up, w_down):
    n_tok, d = x1.shape
    m = inv.shape[0]
    de = w_gate.shape[2]
    n_tiles = m // EXPERT_TILE
    grid_spec = pltpu.PrefetchScalarGridSpec(
        num_scalar_prefetch=5,
        grid=(visit_tile.shape[0],),
        in_specs=[pl.BlockSpec(memory_space=pl.ANY),
                  pl.BlockSpec((1, d, de), lambda v, vt, ve, lo, hi, inv: (ve[v], 0, 0)),
                  pl.BlockSpec((1, d, de), lambda v, vt, ve, lo, hi, inv: (ve[v], 0, 0)),
                  pl.BlockSpec((1, de, d), lambda v, vt, ve, lo, hi, inv: (ve[v], 0, 0))],
        out_specs=pl.BlockSpec(memory_space=pl.ANY),
        scratch_shapes=[pltpu.VMEM((d, de), BF16), pltpu.VMEM((d, de), BF16), pltpu.VMEM((de, d), BF16),
                        pltpu.VMEM((2, EXPERT_TILE, d), F32), pltpu.VMEM((2, EXPERT_TILE, d), F32),
                        pltpu.VMEM((EXPERT_TILE, d), BF16), pltpu.SMEM((1,), I32), pltpu.SemaphoreType.DMA((2,)), pltpu.SemaphoreType.DMA((2,))],
    )
    return pl.pallas_call(
        functools.partial(_expert_kernel, n_tok=n_tok, n_tiles=n_tiles),
        grid_spec=grid_spec,
        out_shape=jax.ShapeDtypeStruct((m, d), F32),
        compiler_params=_cparams(("arbitrary",)),
        name="experts",
    )(visit_tile, visit_expert, visit_lo, visit_hi, inv, x1, w_gate, w_up, w_down)


def _combine_kernel(x1_ref, w_ref, y_ref, sg_ref, su_ref, sd_ref, g_ref, b_ref, o_ref):
    x1 = x1_ref[...]
    xb = x1.astype(BF16)
    hg = jnp.dot(xb, sg_ref[...], preferred_element_type=F32)
    hu = jnp.dot(xb, su_ref[...], preferred_element_type=F32)
    hh = (hg * jax.nn.sigmoid(hg)) * hu
    shared = jnp.dot(hh.astype(BF16), sd_ref[...], preferred_element_type=F32)
    w = w_ref[...]
    routed = y_ref[0] * w[:, 0:1]
    for k in range(1, TOP_K):
        routed = routed + y_ref[k] * w[:, k:k + 1]
    o_ref[...] = _layer_norm(ALPHA * x1 + (routed + shared), g_ref[...], b_ref[...])


def _combine(x1, w_tok, y8, s_gate, s_up, s_down, ln_g, ln_b, tm):
    n, d = x1.shape
    de = s_gate.shape[1]
    full = lambda shape: pl.BlockSpec(shape, lambda i: (0,) * len(shape))
    return pl.pallas_call(
        _combine_kernel,
        grid=(n // tm,),
        in_specs=[pl.BlockSpec((tm, d), lambda i: (i, 0)),
                  pl.BlockSpec((tm, TOP_K), lambda i: (i, 0)),
                  pl.BlockSpec((TOP_K, tm, d), lambda i: (0, i, 0)),
                  full((d, de)), full((d, de)), full((de, d)), full((1, d)), full((1, d))],
        out_specs=pl.BlockSpec((tm, d), lambda i: (i, 0)),
        out_shape=jax.ShapeDtypeStruct((n, d), F32),
        compiler_params=_cparams(("parallel",)),
        name="combine",
    )(x1, w_tok, y8, s_gate.astype(BF16), s_up.astype(BF16), s_down.astype(BF16),
      ln_g.reshape(1, d), ln_b.reshape(1, d))


def _visit_plan(counts, n_rows):
    e = counts.shape[0]
    n_tiles = n_rows // EXPERT_TILE
    ends = jnp.cumsum(counts)
    starts = ends - counts
    pos = jnp.sort(jnp.concatenate([jnp.arange(n_tiles, dtype=I32) * EXPERT_TILE, starts]))
    nxt = jnp.concatenate([pos[1:], jnp.full((1,), n_rows, I32)])
    tile = jnp.minimum(pos // EXPERT_TILE, n_tiles - 1)
    expert = jnp.minimum(jnp.sum((ends[None, :] <= pos[:, None]).astype(I32), axis=1), e - 1)
    return starts, tile, expert, pos - tile * EXPERT_TILE, nxt - tile * EXPERT_TILE


def _moe(x1, logits, router_bias, e_w_gate, e_w_up, e_w_down, s_w_gate, s_w_up, s_w_down, ln_g, ln_b):
    n, d = x1.shape
    top_e, w_t, rank, counts = _route(logits, router_bias, tm=512)
    starts, v_tile, v_expert, v_lo, v_hi = _visit_plan(counts[:, 0], n * TOP_K)
    dest = _dest(top_e, rank, starts, tm=1024)
    inv = _invert_permutation(dest.reshape(n * TOP_K))
    y8 = _experts(x1, inv, v_tile, v_expert, v_lo, v_hi, e_w_gate, e_w_up, e_w_down)
    return _combine(x1, w_t.T, y8.reshape(TOP_K, n, d), s_w_gate, s_w_up, s_w_down, ln_g, ln_b, tm=128)


def _layer(x, w_in, b_gate, m_conv_w, m_conv_b, m_wq, m_wk, m_wv, m_w_if, m_b_if, m_norm_w, m_skip,
           r_conv_w, r_conv_b, r_wa, r_ba, r_wx, r_bx, r_lambda, w_pm, w_pr, w_o, ln1_g, ln1_b,
           router_w, router_bias, e_w_gate, e_w_up, e_w_down, s_w_gate, s_w_up, s_w_down, ln2_g, ln2_b):
    batch, seq, d = x.shape
    n = batch * seq
    xt = x.reshape(n, d)
    proj = _in_proj(xt.astype(BF16), w_in, bm=1024, bn=1024)
    o_z = M_WIDTH
    o_xr = 2 * M_WIDTH
    o_xg = o_xr + R_WIDTH
    o_gate = o_xg + R_WIDTH
    del o_z
    xc, q, k, v, g, gt = _mlstm_prep(proj, seq, m_conv_w, m_conv_b, m_wq, m_wk, m_wv, m_w_if, m_b_if, tm=256)
    y_m = _mlstm(q, k, v, g, gt, xc, proj, m_norm_w, m_skip, batch, seq, MLSTM_CHUNK)
    y_r = _rglru(proj, batch, seq, o_xr, o_xg, r_conv_w, r_conv_b, r_wa, r_ba, r_wx, r_bx, r_lambda,
                 tm=512, cw=512)
    merged = _merge(y_m, y_r, w_pm, w_pr, proj, o_gate, b_gate, tm=512, bn=1024)
    x1, logits = _oproj(merged, w_o, xt, ln1_g, ln1_b, router_w, tm=256)
    out = _moe(x1, logits, router_bias, e_w_gate, e_w_up, e_w_down, s_w_gate, s_w_up, s_w_down, ln2_g, ln2_b)
    return out.reshape(batch, seq, d)


def kernel(x, w_in, b_gate, m_conv_w, m_conv_b, m_wq, m_wk, m_wv, m_w_if, m_b_if, m_norm_w, m_skip, r_conv_w, r_conv_b, r_wa, r_ba, r_wx, r_bx, r_lambda, w_pm, w_pr, w_o, ln1_g, ln1_b, router_w, router_bias, e_w_gate, e_w_up, e_w_down, s_w_gate, s_w_up, s_w_down, ln2_g, ln2_b):
    for l in range(DEPTH):
        x = _layer(x, w_in[l], b_gate[l], m_conv_w[l], m_conv_b[l], m_wq[l], m_wk[l], m_wv[l],
                   m_w_if[l], m_b_if[l], m_norm_w[l], m_skip[l], r_conv_w[l], r_conv_b[l],
                   r_wa[l], r_ba[l], r_wx[l], r_bx[l], r_lambda[l], w_pm[l], w_pr[l], w_o[l],
                   ln1_g[l], ln1_b[l], router_w[l], router_bias[l], e_w_gate[l], e_w_up[l],
                   e_w_down[l], s_w_gate[l], s_w_up[l], s_w_down[l], ln2_g[l], ln2_b[l])
    return x
```

```python
import functools
import math

import jax
import jax.numpy as jnp
from jax import lax
from jax.experimental import pallas as pl
from jax.experimental.pallas import tpu as pltpu

F32 = jnp.float32
BF16 = jnp.bfloat16
I32 = jnp.int32
U32 = jnp.uint32

D_MODEL = 2048
M_WIDTH = 2048
M_HEADS = 8
M_HEAD_DIM = 256
M_QKV_BLOCK = 4
CONV_WIDTH = 4
R_WIDTH = 2560
R_BLOCK = 256
LRU_C = 8.0
N_EXPERTS = 64
TOP_K = 8
N_GROUPS = 8
TOPK_GROUPS = 4
D_EXPERT = 512
ROUTED_SCALE = 2.5
DEPTH = 1
ALPHA = (2.0 * DEPTH) ** 0.25
LN_EPS = 1e-5

V7X_VMEM_LIMIT = 56 * 1024 * 1024
HALO = 8
MLSTM_CHUNK = 256
EXPERT_TILE = 256
NEG_INF = float("-inf")
ROUTER_LANES = 128
GATHER_SLOTS = 3
ROW_COPY_PRIORITY = 1


def _cparams(sem, vmem=V7X_VMEM_LIMIT):
    return pltpu.CompilerParams(dimension_semantics=sem, vmem_limit_bytes=vmem)


def _inproj_kernel(a_ref, w_ref, o_ref, wb_ref):
    @pl.when(pl.program_id(1) == 0)
    def _():
        wb_ref[...] = w_ref[...].astype(BF16)

    o_ref[...] = jnp.dot(a_ref[...], wb_ref[...], preferred_element_type=F32)


def _in_proj(a, w, bm, bn):
    m, k = a.shape
    n = w.shape[1]
    return pl.pallas_call(
        _inproj_kernel,
        grid=(n // bn, m // bm),
        in_specs=[pl.BlockSpec((bm, k), lambda j, i: (i, 0)),
                  pl.BlockSpec((k, bn), lambda j, i: (0, j))],
        out_specs=pl.BlockSpec((bm, bn), lambda j, i: (i, j)),
        out_shape=jax.ShapeDtypeStruct((m, n), F32),
        scratch_shapes=[pltpu.VMEM((k, bn), BF16)],
        compiler_params=_cparams(("parallel", "arbitrary")),
        name="in_proj",
    )(a, w)


def _log_sigmoid(x):
    return jnp.minimum(x, 0.0) - jnp.log1p(jnp.exp(-jnp.abs(x)))


def _shift_rows(x3, prev_group, j):
    rot = pltpu.roll(x3, j, axis=1)
    prev = jnp.concatenate([pltpu.roll(prev_group, j, axis=1), rot[:-1]], axis=0)
    sub = lax.broadcasted_iota(I32, x3.shape, 1)
    return jnp.where(sub >= j, rot, prev)


def _causal_conv(halo, x, cw_ref, cb_ref):
    tm, c = x.shape
    x3 = x.reshape(tm // HALO, HALO, c)
    h3 = halo.reshape(1, HALO, c)
    last = CONV_WIDTH - 1
    y = cb_ref[...] + x3 * cw_ref[last:last + 1, :]
    for j in range(1, CONV_WIDTH):
        y = y + _shift_rows(x3, h3, j) * cw_ref[last - j:last - j + 1, :]
    return y


def _mprep_kernel(xm_ref, halo_ref, cw_ref, cb_ref, wq_ref, wk_ref, wv_ref, wif_ref, wift_ref,
                  bif_ref, bift_ref, xc_ref, q_ref, k_ref, v_ref, g_ref, gt_ref,
                  *, tm, tiles_per_seq):
    i = pl.program_id(0)
    first = (i % tiles_per_seq) == 0
    halo = jnp.where(first, 0.0, halo_ref[...])
    xm = xm_ref[...]
    y = _causal_conv(halo, xm, cw_ref, cb_ref).reshape(tm, M_WIDTH)
    xc = y * jax.nn.sigmoid(y)
    xc_ref[...] = xc
    xcb = xc.astype(BF16)
    xmb = xm.astype(BF16)
    nblk = M_WIDTH // M_HEAD_DIM
    for g in range(nblk):
        sl = slice(g * M_HEAD_DIM, (g + 1) * M_HEAD_DIM)
        q_ref[:, sl] = jnp.dot(xcb[:, sl], wq_ref[g], preferred_element_type=F32).astype(BF16)
        k_ref[:, sl] = jnp.dot(xcb[:, sl], wk_ref[g], preferred_element_type=F32).astype(BF16)
        v_ref[:, sl] = jnp.dot(xmb[:, sl], wv_ref[g], preferred_element_type=F32).astype(BF16)
    qb, kb, vb = q_ref[...], k_ref[...], v_ref[...]
    w = M_WIDTH
    g = (jnp.dot(qb, wif_ref[0:w, :], preferred_element_type=F32)
         + jnp.dot(kb, wif_ref[w:2 * w, :], preferred_element_type=F32)
         + jnp.dot(vb, wif_ref[2 * w:3 * w, :], preferred_element_type=F32) + bif_ref[...])
    nt = (((1,), (1,)), ((), ()))
    gt = (lax.dot_general(wift_ref[:, 0:w], qb, nt, preferred_element_type=F32)
          + lax.dot_general(wift_ref[:, w:2 * w], kb, nt, preferred_element_type=F32)
          + lax.dot_general(wift_ref[:, 2 * w:3 * w], vb, nt, preferred_element_type=F32) + bift_ref[...])
    col = lax.broadcasted_iota(I32, g.shape, 1)
    g_ref[...] = jnp.where(col >= M_HEADS, _log_sigmoid(g), g)
    row = lax.broadcasted_iota(I32, gt.shape, 0)
    gt_ref[...] = jnp.where(row >= M_HEADS, _log_sigmoid(gt), gt)


def _block_diag_dense(w, group):
    nb, bi, bo = w.shape
    per = group // bi
    w4 = w.reshape(nb // per, per, bi, bo)
    eye = jnp.eye(per, dtype=w.dtype)
    return jnp.einsum("gaio,ab->gaibo", w4, eye).reshape(nb // per, group, group)


def _mlstm_prep(proj, seq, conv_w, conv_b, wq, wk, wv, w_if, b_if, tm):
    n = proj.shape[0]
    c = M_WIDTH
    nblk = c // M_HEAD_DIM
    tiles_per_seq = seq // tm
    wqd = _block_diag_dense(wq, M_HEAD_DIM).astype(BF16)
    wkd = _block_diag_dense(wk, M_HEAD_DIM).astype(BF16)
    wvd = _block_diag_dense(wv, M_HEAD_DIM).astype(BF16)
    wif = w_if.astype(BF16)
    wift = w_if.T.astype(BF16)
    ng = 2 * M_HEADS
    hb = tm // HALO
    full = lambda shape: pl.BlockSpec(shape, lambda i: (0,) * len(shape))
    return pl.pallas_call(
        functools.partial(_mprep_kernel, tm=tm, tiles_per_seq=tiles_per_seq),
        grid=(n // tm,),
        in_specs=[pl.BlockSpec((tm, c), lambda i: (i, 0)),
                  pl.BlockSpec((HALO, c), lambda i: (jnp.maximum(i * hb - 1, 0), 0)),
                  full((CONV_WIDTH, c)), full((1, c)),
                  full((nblk, M_HEAD_DIM, M_HEAD_DIM)), full((nblk, M_HEAD_DIM, M_HEAD_DIM)),
                  full((nblk, M_HEAD_DIM, M_HEAD_DIM)),
                  full((3 * c, ng)), full((ng, 3 * c)), full((1, ng)), full((ng, 1))],
        out_specs=[pl.BlockSpec((tm, c), lambda i: (i, 0)),
                   pl.BlockSpec((tm, c), lambda i: (i, 0)),
                   pl.BlockSpec((tm, c), lambda i: (i, 0)),
                   pl.BlockSpec((tm, c), lambda i: (i, 0)),
                   pl.BlockSpec((tm, ng), lambda i: (i, 0)),
                   pl.BlockSpec((ng, tm), lambda i: (0, i))],
        out_shape=[jax.ShapeDtypeStruct((n, c), F32),
                   jax.ShapeDtypeStruct((n, c), BF16),
                   jax.ShapeDtypeStruct((n, c), BF16),
                   jax.ShapeDtypeStruct((n, c), BF16),
                   jax.ShapeDtypeStruct((n, ng), F32),
                   jax.ShapeDtypeStruct((ng, n), F32)],
        compiler_params=_cparams(("parallel",)),
        name="mlstm_prep",
    )(proj, proj, conv_w, conv_b.reshape(1, c), wqd, wkd, wvd, wif, wift,
      b_if.reshape(1, ng), b_if.reshape(ng, 1))


def _mlstm_kernel(q_ref, k_ref, v_ref, g_ref, gt_ref, xc_ref, z_ref, nw_ref, sk_ref, o_ref,
                  c_ref, n_ref, m_ref, *, chunk):
    L = chunk
    hd = M_HEAD_DIM

    @pl.when(pl.program_id(1) == 0)
    def _():
        c_ref[...] = jnp.zeros_like(c_ref)
        n_ref[...] = jnp.zeros_like(n_ref)
        m_ref[...] = jnp.zeros_like(m_ref)

    rows = lax.broadcasted_iota(I32, (L, L), 0)
    cols = lax.broadcasted_iota(I32, (L, L), 1)
    causal = cols <= rows
    tril = jnp.where(causal, 1.0, 0.0).astype(F32)
    triu = jnp.where(rows <= cols, 1.0, 0.0).astype(F32)
    g = g_ref[...]
    gt = gt_ref[...]
    hi = lax.Precision.HIGHEST
    bcol_all = jnp.dot(tril, g, precision=hi, preferred_element_type=F32)
    brow_all = jnp.dot(gt, triu, precision=hi, preferred_element_type=F32)
    k_scale = hd ** -0.5
    nt = (((1,), (1,)), ((), ()))
    tn = (((0,), (0,)), ((), ()))

    for h in range(M_HEADS):
        sl = slice(h * hd, (h + 1) * hd)
        qh = q_ref[:, sl]
        kh = k_ref[:, sl]
        vh = v_ref[:, sl]
        i_col = g[:, h:h + 1]
        b_col = bcol_all[:, M_HEADS + h:M_HEADS + h + 1]
        i_row = gt[h:h + 1, :]
        b_row = brow_all[M_HEADS + h:M_HEADS + h + 1, :]
        m_prev = m_ref[h, 0:1, 0:1]
        c_prev = c_ref[h]
        n_prev = n_ref[h]

        dmat = jnp.where(causal, b_col - b_row + i_row, NEG_INF)
        a_col = b_col + m_prev
        m_row = jnp.maximum(a_col, jnp.max(dmat, axis=1, keepdims=True))
        s = lax.dot_general(qh, kh, nt, preferred_element_type=F32) * k_scale
        s = s * jnp.exp(dmat - m_row)
        inter = jnp.exp(a_col - m_row)
        num = inter * jnp.dot(qh, c_prev.astype(BF16), preferred_element_type=F32) \
            + jnp.dot(s.astype(BF16), vh, preferred_element_type=F32)
        qn = jnp.sum(qh.astype(F32) * n_prev, axis=1, keepdims=True)
        den = inter * qn + jnp.sum(s, axis=1, keepdims=True)
        hval = num * (1.0 / jnp.maximum(jnp.abs(den), jnp.exp(-m_row)))

        mu = jnp.mean(hval, axis=1, keepdims=True)
        cen = hval - mu
        var = jnp.mean(cen * cen, axis=1, keepdims=True)
        hn = cen * lax.rsqrt(var + LN_EPS) * nw_ref[:, sl]
        zz = z_ref[:, sl]
        o_ref[:, sl] = ((hn + sk_ref[:, sl] * xc_ref[:, sl]) * (zz * jax.nn.sigmoid(zz))).astype(o_ref.dtype)

        b_last = b_col[L - 1:L, :]
        w_log = b_last - b_col + i_col
        m_new = jnp.maximum(b_last + m_prev, jnp.max(w_log, axis=0, keepdims=True))
        decay = jnp.exp(b_last + m_prev - m_new)
        kw = kh.astype(F32) * (jnp.exp(w_log - m_new) * k_scale)
        c_ref[h] = decay * c_prev + lax.dot_general(kw.astype(BF16), vh, tn, preferred_element_type=F32)
        n_ref[h] = decay * n_prev + jnp.sum(kw, axis=0, keepdims=True)
        m_ref[h] = jnp.broadcast_to(m_new, m_ref.shape[1:])


def _mlstm(q, k, v, g, gt, xc, proj, norm_w, skip, batch, seq, chunk):
    n, c = q.shape
    nc = seq // chunk
    ng = 2 * M_HEADS
    zcol = M_WIDTH // c
    row = lambda b, j: (b * nc + j, 0)
    return pl.pallas_call(
        functools.partial(_mlstm_kernel, chunk=chunk),
        grid=(batch, nc),
        in_specs=[pl.BlockSpec((chunk, c), row), pl.BlockSpec((chunk, c), row), pl.BlockSpec((chunk, c), row),
                  pl.BlockSpec((chunk, ng), row),
                  pl.BlockSpec((ng, chunk), lambda b, j: (0, b * nc + j)),
                  pl.BlockSpec((chunk, c), row),
                  pl.BlockSpec((chunk, c), lambda b, j: (b * nc + j, zcol)),
                  pl.BlockSpec((1, c), lambda b, j: (0, 0)),
                  pl.BlockSpec((1, c), lambda b, j: (0, 0))],
        out_specs=pl.BlockSpec((chunk, c), row),
        out_shape=jax.ShapeDtypeStruct((n, c), BF16),
        scratch_shapes=[pltpu.VMEM((M_HEADS, M_HEAD_DIM, M_HEAD_DIM), F32),
                        pltpu.VMEM((M_HEADS, 1, M_HEAD_DIM), F32),
                        pltpu.VMEM((M_HEADS, 8, 128), F32)],
        compiler_params=_cparams(("parallel", "arbitrary")),
        name="mlstm_chunk",
    )(q, k, v, g, gt, xc, proj, norm_w.reshape(1, c), skip.reshape(1, c))


def _gelu_tanh(x):
    return 0.5 * x * (1.0 + jnp.tanh(math.sqrt(2.0 / math.pi) * (x + 0.044715 * (x * x * x))))


def _rglru_kernel(xr_ref, halo_ref, xg_ref, cw_ref, cb_ref, wa_ref, ba_ref, wx_ref, bx_ref, lam_ref,
                  o_ref, a_ref, b_ref, h_ref, *, tm, cw):
    t = pl.program_id(2)

    @pl.when(t == 0)
    def _():
        h_ref[...] = jnp.zeros_like(h_ref)

    halo = jnp.where(t == 0, 0.0, halo_ref[...])
    xc = _causal_conv(halo, xr_ref[...], cw_ref, cb_ref).reshape(tm, cw)
    xcb = xc.astype(BF16)
    nblk = cw // R_BLOCK
    ra = []
    rx = []
    for g in range(nblk):
        sl = slice(g * R_BLOCK, (g + 1) * R_BLOCK)
        ra.append(jnp.dot(xcb[:, sl], wa_ref[g], preferred_element_type=F32))
        rx.append(jnp.dot(xcb[:, sl], wx_ref[g], preferred_element_type=F32))
    r = jax.nn.sigmoid(jnp.concatenate(ra, axis=1) + ba_ref[...])
    ig = jax.nn.sigmoid(jnp.concatenate(rx, axis=1) + bx_ref[...])
    nl = -lam_ref[...]
    softplus = jnp.maximum(nl, 0.0) + jnp.log1p(jnp.exp(-jnp.abs(nl)))
    a = jnp.exp((-LRU_C * softplus) * r)
    b = jnp.sqrt(1.0 - a * a) * (ig * xc)

    a = a.reshape(tm // 8, 8, cw)
    b = b.reshape(tm // 8, 8, cw)
    sub = lax.broadcasted_iota(I32, a.shape, 1)
    for d in (1, 2, 4):
        keep = sub >= d
        a_sh = pltpu.roll(a, d, axis=1)
        b_sh = pltpu.roll(b, d, axis=1)
        b = jnp.where(keep, a * b_sh + b, b)
        a = jnp.where(keep, a * a_sh, a)
    a_ref[...] = a.reshape(tm, cw)
    b_ref[...] = b.reshape(tm, cw)

    def body(g, h):
        r0 = pl.multiple_of(g * 8, 8)
        hh = b_ref[pl.ds(r0, 8), :] + a_ref[pl.ds(r0, 8), :] * h
        b_ref[pl.ds(r0, 8), :] = hh
        return hh[7:8, :]

    h_ref[...] = lax.fori_loop(0, tm // 8, body, h_ref[...], unroll=8)
    o_ref[...] = (b_ref[...] * _gelu_tanh(xg_ref[...])).astype(o_ref.dtype)


def _rglru(proj, batch, seq, xr_off, xg_off, conv_w, conv_b, wa, ba, wx, bx, lam, tm, cw):
    n = proj.shape[0]
    ncol = R_WIDTH // cw
    nt = seq // tm
    per = cw // R_BLOCK
    xr_cb = xr_off // cw
    xg_cb = xg_off // cw
    hb = tm // HALO
    colv = lambda shape: pl.BlockSpec(shape, lambda b, j, t: (0, j))
    return pl.pallas_call(
        functools.partial(_rglru_kernel, tm=tm, cw=cw),
        grid=(batch, ncol, nt),
        in_specs=[pl.BlockSpec((tm, cw), lambda b, j, t: (b * nt + t, xr_cb + j)),
                  pl.BlockSpec((HALO, cw), lambda b, j, t: (jnp.maximum((b * nt + t) * hb - 1, 0), xr_cb + j)),
                  pl.BlockSpec((tm, cw), lambda b, j, t: (b * nt + t, xg_cb + j)),
                  colv((CONV_WIDTH, cw)), colv((1, cw)),
                  pl.BlockSpec((per, R_BLOCK, R_BLOCK), lambda b, j, t: (j, 0, 0)), colv((1, cw)),
                  pl.BlockSpec((per, R_BLOCK, R_BLOCK), lambda b, j, t: (j, 0, 0)), colv((1, cw)),
                  colv((1, cw))],
        out_specs=pl.BlockSpec((tm, cw), lambda b, j, t: (b * nt + t, j)),
        out_shape=jax.ShapeDtypeStruct((n, R_WIDTH), BF16),
        scratch_shapes=[pltpu.VMEM((tm, cw), F32), pltpu.VMEM((tm, cw), F32), pltpu.VMEM((1, cw), F32)],
        compiler_params=_cparams(("parallel", "parallel", "arbitrary")),
        name="rglru",
    )(proj, proj, proj, conv_w, conv_b.reshape(1, R_WIDTH), wa.astype(BF16), ba.reshape(1, R_WIDTH),
      wx.astype(BF16), bx.reshape(1, R_WIDTH), lam.reshape(1, R_WIDTH))


def _merge_kernel(ym_ref, yr_ref, wpm_ref, wpr_ref, g0_ref, g1_ref, bg_ref, o_ref):
    g0 = jax.nn.sigmoid(g0_ref[...] + bg_ref[0:1, :])
    g1 = jax.nn.sigmoid(g1_ref[...] + bg_ref[1:2, :])
    pm = jnp.dot(ym_ref[...], wpm_ref[...], preferred_element_type=F32)
    pr = jnp.dot(yr_ref[...], wpr_ref[...], preferred_element_type=F32)
    o_ref[...] = (g0 * pm + g1 * pr).astype(o_ref.dtype)


def _merge(ym, yr, w_pm, w_pr, proj, gate_off, b_gate, tm, bn):
    n = ym.shape[0]
    d = w_pm.shape[1]
    g0_cb = gate_off // bn
    g1_cb = (gate_off + d) // bn
    return pl.pallas_call(
        _merge_kernel,
        grid=(d // bn, n // tm),
        in_specs=[pl.BlockSpec((tm, ym.shape[1]), lambda j, i: (i, 0)),
                  pl.BlockSpec((tm, yr.shape[1]), lambda j, i: (i, 0)),
                  pl.BlockSpec((w_pm.shape[0], bn), lambda j, i: (0, j)),
                  pl.BlockSpec((w_pr.shape[0], bn), lambda j, i: (0, j)),
                  pl.BlockSpec((tm, bn), lambda j, i: (i, g0_cb + j)),
                  pl.BlockSpec((tm, bn), lambda j, i: (i, g1_cb + j)),
                  pl.BlockSpec((2, bn), lambda j, i: (0, j))],
        out_specs=pl.BlockSpec((tm, bn), lambda j, i: (i, j)),
        out_shape=jax.ShapeDtypeStruct((n, d), BF16),
        compiler_params=_cparams(("parallel", "parallel")),
        name="merge",
    )(ym, yr, w_pm.astype(BF16), w_pr.astype(BF16), proj, proj, b_gate)


def _layer_norm(y, g, b):
    mu = jnp.mean(y, axis=1, keepdims=True)
    cen = y - mu
    var = jnp.mean(cen * cen, axis=1, keepdims=True)
    return cen * lax.rsqrt(var + LN_EPS) * g + b


def _oproj_kernel(mg_ref, wo_ref, x_ref, g_ref, b_ref, rwh_ref, rwl_ref, x1_ref, lg_ref):
    y = ALPHA * x_ref[...] + jnp.dot(mg_ref[...], wo_ref[...], preferred_element_type=F32)
    x1 = _layer_norm(y, g_ref[...], b_ref[...])
    x1_ref[...] = x1
    xh = x1.astype(BF16)
    xl = (x1 - xh.astype(F32)).astype(BF16)
    lg_ref[...] = (jnp.dot(xh, rwh_ref[...], preferred_element_type=F32)
                   + (jnp.dot(xl, rwh_ref[...], preferred_element_type=F32)
                      + jnp.dot(xh, rwl_ref[...], preferred_element_type=F32)))


def _oproj(merged, w_o, x, ln_g, ln_b, router_w, tm):
    n, d = x.shape
    e = router_w.shape[1]
    rw = jnp.pad(router_w, ((0, 0), (0, ROUTER_LANES - e)))
    rw_hi = rw.astype(BF16)
    rw_lo = (rw - rw_hi.astype(F32)).astype(BF16)
    full = lambda shape: pl.BlockSpec(shape, lambda i: (0,) * len(shape))
    return pl.pallas_call(
        _oproj_kernel,
        grid=(n // tm,),
        in_specs=[pl.BlockSpec((tm, d), lambda i: (i, 0)), full((d, d)),
                  pl.BlockSpec((tm, d), lambda i: (i, 0)), full((1, d)), full((1, d)),
                  full((d, ROUTER_LANES)), full((d, ROUTER_LANES))],
        out_specs=[pl.BlockSpec((tm, d), lambda i: (i, 0)), pl.BlockSpec((tm, ROUTER_LANES), lambda i: (i, 0))],
        out_shape=[jax.ShapeDtypeStruct((n, d), F32), jax.ShapeDtypeStruct((n, ROUTER_LANES), F32)],
        compiler_params=_cparams(("parallel",)),
        name="out_proj_ln",
    )(merged, w_o.astype(BF16), x, ln_g.reshape(1, d), ln_b.reshape(1, d), rw_hi, rw_lo)


def _first_max(v, idx, sentinel):
    m = jnp.max(v, axis=0, keepdims=True)
    am = jnp.min(jnp.where(v == m, idx, sentinel), axis=0, keepdims=True)
    return m, am


def _route_kernel(lg_ref, bias_ref, tri_ref, te_ref, w_ref, rk_ref, cnt_ref, carry_ref, *, tm):
    @pl.when(pl.program_id(0) == 0)
    def _():
        carry_ref[...] = jnp.zeros_like(carry_ref)

    e = N_EXPERTS
    gs = e // N_GROUPS
    scores = jax.nn.sigmoid(jnp.transpose(lg_ref[...])[0:e, :])
    biased = scores + bias_ref[...]
    sub = lax.broadcasted_iota(I32, (gs, tm), 0)
    grp_rows = []
    for g in range(N_GROUPS):
        slab = biased[g * gs:(g + 1) * gs, :]
        m1, a1 = _first_max(slab, sub, gs)
        m2 = jnp.max(jnp.where(sub == a1, NEG_INF, slab), axis=0, keepdims=True)
        grp_rows.append(m1 + m2)
    grp = jnp.concatenate(grp_rows, axis=0)
    gidx = lax.broadcasted_iota(I32, (N_GROUPS, tm), 0)
    gsel = jnp.zeros((N_GROUPS, tm), F32)
    for _ in range(TOPK_GROUPS):
        _, am = _first_max(grp, gidx, N_GROUPS)
        hit = gidx == am
        gsel = jnp.where(hit, 1.0, gsel)
        grp = jnp.where(hit, NEG_INF, grp)
    masked = jnp.concatenate(
        [jnp.where(gsel[g:g + 1, :] > 0.0, biased[g * gs:(g + 1) * gs, :], NEG_INF) for g in range(N_GROUPS)],
        axis=0)
    eidx = lax.broadcasted_iota(I32, (e, tm), 0)
    member = jnp.zeros((e, tm), F32)
    tops = []
    ws = []
    for _ in range(TOP_K):
        _, am = _first_max(masked, eidx, e)
        hit = eidx == am
        tops.append(am)
        ws.append(jnp.sum(jnp.where(hit, scores, 0.0), axis=0, keepdims=True))
        member = jnp.where(hit, 1.0, member)
        masked = jnp.where(hit, NEG_INF, masked)
    wsum = ws[0]
    for k in range(1, TOP_K):
        wsum = wsum + ws[k]
    te_ref[...] = jnp.concatenate(tops, axis=0)
    w_ref[...] = jnp.concatenate(ws, axis=0) / wsum * ROUTED_SCALE

    cum = jnp.dot(member.astype(BF16), tri_ref[...], preferred_element_type=F32)
    carry = carry_ref[:, 0:1]
    rank = carry + cum - member
    rks = []
    for k in range(TOP_K):
        rks.append(jnp.sum(jnp.where(eidx == tops[k], rank, 0.0), axis=0, keepdims=True))
    rk_ref[...] = jnp.concatenate(rks, axis=0).astype(I32)
    new_carry = carry + cum[:, tm - 1:tm]
    carry_ref[...] = jnp.broadcast_to(new_carry, carry_ref.shape)
    cnt_ref[...] = jnp.broadcast_to(new_carry, cnt_ref.shape).astype(I32)


def _route(logits, router_bias, tm):
    n = logits.shape[0]
    e = router_bias.shape[0]
    tri = jnp.triu(jnp.ones((tm, tm), F32)).astype(BF16)
    return pl.pallas_call(
        functools.partial(_route_kernel, tm=tm),
        grid=(n // tm,),
        in_specs=[pl.BlockSpec((tm, ROUTER_LANES), lambda i: (i, 0)),
                  pl.BlockSpec((e, 1), lambda i: (0, 0)),
                  pl.BlockSpec((tm, tm), lambda i: (0, 0))],
        out_specs=[pl.BlockSpec((TOP_K, tm), lambda i: (0, i)),
                   pl.BlockSpec((TOP_K, tm), lambda i: (0, i)),
                   pl.BlockSpec((TOP_K, tm), lambda i: (0, i)),
                   pl.BlockSpec((e, 128), lambda i: (0, 0))],
        out_shape=[jax.ShapeDtypeStruct((TOP_K, n), I32), jax.ShapeDtypeStruct((TOP_K, n), F32),
                   jax.ShapeDtypeStruct((TOP_K, n), I32), jax.ShapeDtypeStruct((e, 128), I32)],
        scratch_shapes=[pltpu.VMEM((e, 128), F32)],
        compiler_params=_cparams(("arbitrary",)),
        name="route",
    )(logits, router_bias.reshape(e, 1), tri)


def _dest_kernel(te_ref, rk_ref, ps_ref, d_ref):
    te = te_ref[...]
    e = N_EXPERTS
    tm = te.shape[1]
    eidx = lax.broadcasted_iota(I32, (e, tm), 0)
    ps = ps_ref[...]
    rows = []
    for k in range(TOP_K):
        rows.append(jnp.sum(jnp.where(eidx == te[k:k + 1, :], ps, 0), axis=0, keepdims=True))
    d_ref[...] = jnp.concatenate(rows, axis=0) + rk_ref[...]


def _dest(top_e, rank, pad_starts, tm):
    n = top_e.shape[1]
    return pl.pallas_call(
        _dest_kernel,
        grid=(n // tm,),
        in_specs=[pl.BlockSpec((TOP_K, tm), lambda i: (0, i)),
                  pl.BlockSpec((TOP_K, tm), lambda i: (0, i)),
                  pl.BlockSpec((N_EXPERTS, 1), lambda i: (0, 0))],
        out_specs=pl.BlockSpec((TOP_K, tm), lambda i: (0, i)),
        out_shape=jax.ShapeDtypeStruct((TOP_K, n), I32),
        compiler_params=_cparams(("parallel",)),
        name="dest",
    )(top_e, rank, pad_starts.reshape(N_EXPERTS, 1))


def _invert_kernel(dest_ref, inv_ref):
    def body(j, c):
        inv_ref[dest_ref[j]] = j
        return c

    lax.fori_loop(0, dest_ref.shape[0], body, 0, unroll=8)


def _invert_permutation(dest_flat):
    m = dest_flat.shape[0]
    return pl.pallas_call(
        _invert_kernel,
        in_specs=[pl.BlockSpec(memory_space=pltpu.SMEM)],
        out_specs=pl.BlockSpec(memory_space=pltpu.SMEM),
        out_shape=jax.ShapeDtypeStruct((m,), I32),
        name="invert_perm",
    )(dest_flat)


def _expert_ffn(xb, wgb_ref, wub_ref, wdb_ref):
    hg = jnp.dot(xb, wgb_ref[...], preferred_element_type=F32)
    hu = jnp.dot(xb, wub_ref[...], preferred_element_type=F32)
    hh = (hg * jax.nn.sigmoid(hg)) * hu
    return jnp.dot(hh.astype(BF16), wdb_ref[...], preferred_element_type=F32)


def _expert_kernel(vt_ref, ve_ref, lo_ref, hi_ref, inv_ref, nxt_ref, e0_ref,
                   x_hbm, wg_hbm, wu_hbm, wd_hbm, out_hbm,
                   wgf_ref, wuf_ref, wdf_ref, wgb_ref, wub_ref, wdb_ref, xbuf, obuf, xb_ref, cur_ref,
                   gsem, ssem, wsem, *, n_tok, n_tiles):
    v = pl.program_id(0)
    rows = xbuf.shape[1]
    tile = vt_ref[v]
    lo = lo_ref[v]
    hi = hi_ref[v]
    nonempty = hi > lo
    first = jnp.logical_and(nonempty, lo == 0)
    slot = lax.rem(tile, GATHER_SLOTS)
    oslot = tile % 2
    ahead = GATHER_SLOTS - 1

    def gather_row(t, r):
        src_tile = jnp.minimum(t, n_tiles - 1)
        tok = inv_ref[src_tile * rows + r] & (n_tok - 1)
        s = lax.rem(t, GATHER_SLOTS)
        return pltpu.make_async_copy(x_hbm.at[pl.ds(tok, 1)], xbuf.at[s, pl.ds(r, 1)], gsem.at[s])

    def scatter_row(t, r):
        j = inv_ref[t * rows + r]
        s = t % 2
        return pltpu.make_async_copy(obuf.at[s, pl.ds(r, 1)], out_hbm.at[pl.ds(j, 1)], ssem.at[s])

    def wait_gathered_tile(s):
        pltpu.make_async_copy(x_hbm.at[pl.ds(0, rows)], xbuf.at[s], gsem.at[s]).wait()

    def wait_scattered_tile(s):
        pltpu.make_async_copy(obuf.at[s], out_hbm.at[pl.ds(0, rows)], ssem.at[s]).wait()

    def weight_copies(e):
        return (pltpu.make_async_copy(wg_hbm.at[e], wgf_ref, wsem.at[0]),
                pltpu.make_async_copy(wu_hbm.at[e], wuf_ref, wsem.at[1]),
                pltpu.make_async_copy(wd_hbm.at[e], wdf_ref, wsem.at[2]))

    @pl.when(v == 0)
    def _():
        cur_ref[0] = -1
        for c in weight_copies(e0_ref[0]):
            c.start()

    @pl.when(jnp.logical_and(nonempty, cur_ref[0] != ve_ref[v]))
    def _():
        e = ve_ref[v]
        for c in weight_copies(e):
            c.wait()
        wgb_ref[...] = wgf_ref[...].astype(BF16)
        wub_ref[...] = wuf_ref[...].astype(BF16)
        wdb_ref[...] = wdf_ref[...].astype(BF16)
        cur_ref[0] = e
        ne = nxt_ref[e]

        @pl.when(ne >= 0)
        def _():
            for c in weight_copies(ne):
                c.start()

    @pl.when(jnp.logical_and(first, tile == 0))
    def _():
        def issue(i, c):
            gather_row(i // rows, i % rows).start(priority=ROW_COPY_PRIORITY)
            return c

        lax.fori_loop(0, ahead * rows, issue, 0)

    @pl.when(jnp.logical_and(first, tile >= 2))
    def _():
        wait_scattered_tile(oslot)

    @pl.when(first)
    def _():
        wait_gathered_tile(slot)

    @pl.when(jnp.logical_and(first, tile == 0))
    def _():
        xb_ref[...] = xbuf[slot].astype(BF16)
        for r in range(rows):
            gather_row(tile + ahead, r).start(priority=ROW_COPY_PRIORITY)
        obuf[oslot] = _expert_ffn(xb_ref[...], wgb_ref, wub_ref, wdb_ref)

    @pl.when(jnp.logical_and(first, tile >= 1))
    def _():
        xb_ref[...] = xbuf[slot].astype(BF16)
        for r in range(rows):
            gather_row(tile + ahead, r).start(priority=ROW_COPY_PRIORITY)
            scatter_row(tile - 1, r).start(priority=ROW_COPY_PRIORITY)
        obuf[oslot] = _expert_ffn(xb_ref[...], wgb_ref, wub_ref, wdb_ref)

    @pl.when(jnp.logical_and(nonempty, lo > 0))
    def _():
        y = _expert_ffn(xbuf[slot].astype(BF16), wgb_ref, wub_ref, wdb_ref)
        row = lax.broadcasted_iota(I32, (rows, 1), 0)
        mine = jnp.logical_and(row >= lo, row < hi)
        obuf[oslot] = jnp.where(mine, y, obuf[oslot])

    @pl.when(v == pl.num_programs(0) - 1)
    def _():
        last = n_tiles - 1

        def issue(r, c):
            scatter_row(last, r).start(priority=ROW_COPY_PRIORITY)
            return c

        lax.fori_loop(0, rows, issue, 0)
        wait_scattered_tile((last - 1) % 2)
        wait_scattered_tile(last % 2)
        for t in range(last + 1, last + 1 + ahead):
            wait_gathered_tile(t % GATHER_SLOTS)


def _experts(x1, inv, visit_tile, visit_expert, visit_lo, visit_hi, next_expert, first_expert,
             w_gate, w_up, w_down):
    n_tok, d = x1.shape
    m = inv.shape[0]
    de = w_gate.shape[2]
    n_tiles = m // EXPERT_TILE
    any_spec = pl.BlockSpec(memory_space=pl.ANY)
    grid_spec = pltpu.PrefetchScalarGridSpec(
        num_scalar_prefetch=7,
        grid=(visit_tile.shape[0],),
        in_specs=[any_spec, any_spec, any_spec, any_spec],
        out_specs=any_spec,
        scratch_shapes=[pltpu.VMEM((d, de), F32), pltpu.VMEM((d, de), F32), pltpu.VMEM((de, d), F32),
                        pltpu.VMEM((d, de), BF16), pltpu.VMEM((d, de), BF16), pltpu.VMEM((de, d), BF16),
                        pltpu.VMEM((GATHER_SLOTS, EXPERT_TILE, d), F32), pltpu.VMEM((2, EXPERT_TILE, d), F32),
                        pltpu.VMEM((EXPERT_TILE, d), BF16), pltpu.SMEM((1,), I32),
                        pltpu.SemaphoreType.DMA((GATHER_SLOTS,)), pltpu.SemaphoreType.DMA((2,)),
                        pltpu.SemaphoreType.DMA((3,))],
    )
    return pl.pallas_call(
        functools.partial(_expert_kernel, n_tok=n_tok, n_tiles=n_tiles),
        grid_spec=grid_spec,
        out_shape=jax.ShapeDtypeStruct((m, d), F32),
        compiler_params=_cparams(("arbitrary",)),
        name="experts",
    )(visit_tile, visit_expert, visit_lo, visit_hi, inv, next_expert, first_expert, x1, w_gate, w_up, w_down)


def _combine_kernel(x1_ref, w_ref, y_ref, sg_ref, su_ref, sd_ref, g_ref, b_ref, o_ref):
    x1 = x1_ref[...]
    xb = x1.astype(BF16)
    hg = jnp.dot(xb, sg_ref[...], preferred_element_type=F32)
    hu = jnp.dot(xb, su_ref[...], preferred_element_type=F32)
    hh = (hg * jax.nn.sigmoid(hg)) * hu
    shared = jnp.dot(hh.astype(BF16), sd_ref[...], preferred_element_type=F32)
    w = w_ref[...]
    routed = y_ref[0] * w[:, 0:1]
    for k in range(1, TOP_K):
        routed = routed + y_ref[k] * w[:, k:k + 1]
    o_ref[...] = _layer_norm(ALPHA * x1 + (routed + shared), g_ref[...], b_ref[...])


def _combine(x1, w_tok, y8, s_gate, s_up, s_down, ln_g, ln_b, tm):
    n, d = x1.shape
    de = s_gate.shape[1]
    full = lambda shape: pl.BlockSpec(shape, lambda i: (0,) * len(shape))
    return pl.pallas_call(
        _combine_kernel,
        grid=(n // tm,),
        in_specs=[pl.BlockSpec((tm, d), lambda i: (i, 0)),
                  pl.BlockSpec((tm, TOP_K), lambda i: (i, 0)),
                  pl.BlockSpec((TOP_K, tm, d), lambda i: (0, i, 0)),
                  full((d, de)), full((d, de)), full((de, d)), full((1, d)), full((1, d))],
        out_specs=pl.BlockSpec((tm, d), lambda i: (i, 0)),
        out_shape=jax.ShapeDtypeStruct((n, d), F32),
        compiler_params=_cparams(("parallel",)),
        name="combine",
    )(x1, w_tok, y8, s_gate.astype(BF16), s_up.astype(BF16), s_down.astype(BF16),
      ln_g.reshape(1, d), ln_b.reshape(1, d))


def _visit_plan(counts, n_rows):
    e = counts.shape[0]
    n_tiles = n_rows // EXPERT_TILE
    ends = jnp.cumsum(counts)
    starts = ends - counts
    pos = jnp.sort(jnp.concatenate([jnp.arange(n_tiles, dtype=I32) * EXPERT_TILE, starts]))
    nxt = jnp.concatenate([pos[1:], jnp.full((1,), n_rows, I32)])
    tile = jnp.minimum(pos // EXPERT_TILE, n_tiles - 1)
    expert = jnp.minimum(jnp.sum((ends[None, :] <= pos[:, None]).astype(I32), axis=1), e - 1)
    ids = jnp.arange(e, dtype=I32)
    later = jnp.logical_and(ids[None, :] > ids[:, None], counts[None, :] > 0)
    next_expert = jnp.min(jnp.where(later, ids[None, :], e), axis=1)
    next_expert = jnp.where(next_expert < e, next_expert, -1).astype(I32)
    first_expert = jnp.min(jnp.where(counts > 0, ids, e - 1)).astype(I32).reshape(1)
    return starts, tile, expert, pos - tile * EXPERT_TILE, nxt - tile * EXPERT_TILE, next_expert, first_expert


def _moe(x1, logits, router_bias, e_w_gate, e_w_up, e_w_down, s_w_gate, s_w_up, s_w_down, ln_g, ln_b):
    n, d = x1.shape
    top_e, w_t, rank, counts = _route(logits, router_bias, tm=512)
    starts, v_tile, v_expert, v_lo, v_hi, next_e, first_e = _visit_plan(counts[:, 0], n * TOP_K)
    dest = _dest(top_e, rank, starts, tm=1024)
    inv = _invert_permutation(dest.reshape(n * TOP_K))
    y8 = _experts(x1, inv, v_tile, v_expert, v_lo, v_hi, next_e, first_e, e_w_gate, e_w_up, e_w_down)
    return _combine(x1, w_t.T, y8.reshape(TOP_K, n, d), s_w_gate, s_w_up, s_w_down, ln_g, ln_b, tm=128)


def _layer(x, w_in, b_gate, m_conv_w, m_conv_b, m_wq, m_wk, m_wv, m_w_if, m_b_if, m_norm_w, m_skip,
           r_conv_w, r_conv_b, r_wa, r_ba, r_wx, r_bx, r_lambda, w_pm, w_pr, w_o, ln1_g, ln1_b,
           router_w, router_bias, e_w_gate, e_w_up, e_w_down, s_w_gate, s_w_up, s_w_down, ln2_g, ln2_b):
    batch, seq, d = x.shape
    n = batch * seq
    xt = x.reshape(n, d)
    proj = _in_proj(xt.astype(BF16), w_in, bm=1024, bn=1024)
    o_z = M_WIDTH
    o_xr = 2 * M_WIDTH
    o_xg = o_xr + R_WIDTH
    o_gate = o_xg + R_WIDTH
    del o_z
    xc, q, k, v, g, gt = _mlstm_prep(proj, seq, m_conv_w, m_conv_b, m_wq, m_wk, m_wv, m_w_if, m_b_if, tm=256)
    y_m = _mlstm(q, k, v, g, gt, xc, proj, m_norm_w, m_skip, batch, seq, MLSTM_CHUNK)
    y_r = _rglru(proj, batch, seq, o_xr, o_xg, r_conv_w, r_conv_b, r_wa, r_ba, r_wx, r_bx, r_lambda,
                 tm=512, cw=512)
    merged = _merge(y_m, y_r, w_pm, w_pr, proj, o_gate, b_gate, tm=512, bn=1024)
    x1, logits = _oproj(merged, w_o, xt, ln1_g, ln1_b, router_w, tm=256)
    out = _moe(x1, logits, router_bias, e_w_gate, e_w_up, e_w_down, s_w_gate, s_w_up, s_w_down, ln2_g, ln2_b)
    return out.reshape(batch, seq, d)


def kernel(x, w_in, b_gate, m_conv_w, m_conv_b, m_wq, m_wk, m_wv, m_w_if, m_b_if, m_norm_w, m_skip, r_conv_w, r_conv_b, r_wa, r_ba, r_wx, r_bx, r_lambda, w_pm, w_pr, w_o, ln1_g, ln1_b, router_w, router_bias, e_w_gate, e_w_up, e_w_down, s_w_gate, s_w_up, s_w_down, ln2_g, ln2_b):
    for l in range(DEPTH):
        x = _layer(x, w_in[l], b_gate[l], m_conv_w[l], m_conv_b[l], m_wq[l], m_wk[l], m_wv[l],
                   m_w_if[l], m_b_if[l], m_norm_w[l], m_skip[l], r_conv_w[l], r_conv_b[l],
                   r_wa[l], r_ba[l], r_wx[l], r_bx[l], r_lambda[l], w_pm[l], w_pr[l], w_o[l],
                   ln1_g[l], ln1_b[l], router_w[l], router_bias[l], e_w_gate[l], e_w_up[l],
                   e_w_down[l], s_w_gate[l], s_w_up[l], s_w_down[l], ln2_g[l], ln2_b[l])
    return x
```

```python
import functools
import math

import jax
import jax.numpy as jnp
from jax import lax
from jax.experimental import pallas as pl
from jax.experimental.pallas import tpu as pltpu

F32 = jnp.float32
BF16 = jnp.bfloat16
I32 = jnp.int32
U32 = jnp.uint32

D_MODEL = 2048
M_WIDTH = 2048
M_HEADS = 8
M_HEAD_DIM = 256
M_QKV_BLOCK = 4
CONV_WIDTH = 4
R_WIDTH = 2560
R_BLOCK = 256
LRU_C = 8.0
N_EXPERTS = 64
TOP_K = 8
N_GROUPS = 8
TOPK_GROUPS = 4
D_EXPERT = 512
ROUTED_SCALE = 2.5
DEPTH = 1
ALPHA = (2.0 * DEPTH) ** 0.25
LN_EPS = 1e-5

V7X_VMEM_LIMIT = 56 * 1024 * 1024
HALO = 8
MLSTM_CHUNK = 256
EXPERT_TILE = 256
NEG_INF = float("-inf")
ROUTER_LANES = 128
ROW_SLAB_SUB = 16
GATHER_SLOTS = 3
ROW_COPY_PRIORITY = 1


def _cparams(sem, vmem=V7X_VMEM_LIMIT):
    return pltpu.CompilerParams(dimension_semantics=sem, vmem_limit_bytes=vmem)


def _inproj_kernel(a_ref, w_ref, o_ref, wb_ref):
    @pl.when(pl.program_id(1) == 0)
    def _():
        wb_ref[...] = w_ref[...].astype(BF16)

    o_ref[...] = jnp.dot(a_ref[...], wb_ref[...], preferred_element_type=F32)


def _in_proj(a, w, bm, bn):
    m, k = a.shape
    n = w.shape[1]
    return pl.pallas_call(
        _inproj_kernel,
        grid=(n // bn, m // bm),
        in_specs=[pl.BlockSpec((bm, k), lambda j, i: (i, 0)),
                  pl.BlockSpec((k, bn), lambda j, i: (0, j))],
        out_specs=pl.BlockSpec((bm, bn), lambda j, i: (i, j)),
        out_shape=jax.ShapeDtypeStruct((m, n), F32),
        scratch_shapes=[pltpu.VMEM((k, bn), BF16)],
        compiler_params=_cparams(("parallel", "arbitrary")),
        name="in_proj",
    )(a, w)


def _log_sigmoid(x):
    return jnp.minimum(x, 0.0) - jnp.log1p(jnp.exp(-jnp.abs(x)))


def _shift_rows(x3, prev_group, j):
    rot = pltpu.roll(x3, j, axis=1)
    prev = jnp.concatenate([pltpu.roll(prev_group, j, axis=1), rot[:-1]], axis=0)
    sub = lax.broadcasted_iota(I32, x3.shape, 1)
    return jnp.where(sub >= j, rot, prev)


def _causal_conv(halo, x, cw_ref, cb_ref):
    tm, c = x.shape
    x3 = x.reshape(tm // HALO, HALO, c)
    h3 = halo.reshape(1, HALO, c)
    last = CONV_WIDTH - 1
    y = cb_ref[...] + x3 * cw_ref[last:last + 1, :]
    for j in range(1, CONV_WIDTH):
        y = y + _shift_rows(x3, h3, j) * cw_ref[last - j:last - j + 1, :]
    return y


def _mprep_kernel(xm_ref, halo_ref, cw_ref, cb_ref, wq_ref, wk_ref, wv_ref, wif_ref, wift_ref,
                  bif_ref, bift_ref, xc_ref, q_ref, k_ref, v_ref, g_ref, gt_ref,
                  *, tm, tiles_per_seq):
    i = pl.program_id(0)
    first = (i % tiles_per_seq) == 0
    halo = jnp.where(first, 0.0, halo_ref[...])
    xm = xm_ref[...]
    y = _causal_conv(halo, xm, cw_ref, cb_ref).reshape(tm, M_WIDTH)
    xc = y * jax.nn.sigmoid(y)
    xc_ref[...] = xc
    xcb = xc.astype(BF16)
    xmb = xm.astype(BF16)
    nblk = M_WIDTH // M_HEAD_DIM
    for g in range(nblk):
        sl = slice(g * M_HEAD_DIM, (g + 1) * M_HEAD_DIM)
        q_ref[:, sl] = jnp.dot(xcb[:, sl], wq_ref[g], preferred_element_type=F32).astype(BF16)
        k_ref[:, sl] = jnp.dot(xcb[:, sl], wk_ref[g], preferred_element_type=F32).astype(BF16)
        v_ref[:, sl] = jnp.dot(xmb[:, sl], wv_ref[g], preferred_element_type=F32).astype(BF16)
    qb, kb, vb = q_ref[...], k_ref[...], v_ref[...]
    w = M_WIDTH
    g = (jnp.dot(qb, wif_ref[0:w, :], preferred_element_type=F32)
         + jnp.dot(kb, wif_ref[w:2 * w, :], preferred_element_type=F32)
         + jnp.dot(vb, wif_ref[2 * w:3 * w, :], preferred_element_type=F32) + bif_ref[...])
    nt = (((1,), (1,)), ((), ()))
    gt = (lax.dot_general(wift_ref[:, 0:w], qb, nt, preferred_element_type=F32)
          + lax.dot_general(wift_ref[:, w:2 * w], kb, nt, preferred_element_type=F32)
          + lax.dot_general(wift_ref[:, 2 * w:3 * w], vb, nt, preferred_element_type=F32) + bift_ref[...])
    col = lax.broadcasted_iota(I32, g.shape, 1)
    g_ref[...] = jnp.where(col >= M_HEADS, _log_sigmoid(g), g)
    row = lax.broadcasted_iota(I32, gt.shape, 0)
    gt_ref[...] = jnp.where(row >= M_HEADS, _log_sigmoid(gt), gt)


def _block_diag_dense(w, group):
    nb, bi, bo = w.shape
    per = group // bi
    w4 = w.reshape(nb // per, per, bi, bo)
    eye = jnp.eye(per, dtype=w.dtype)
    return jnp.einsum("gaio,ab->gaibo", w4, eye).reshape(nb // per, group, group)


def _mlstm_prep(proj, seq, conv_w, conv_b, wq, wk, wv, w_if, b_if, tm):
    n = proj.shape[0]
    c = M_WIDTH
    nblk = c // M_HEAD_DIM
    tiles_per_seq = seq // tm
    wqd = _block_diag_dense(wq, M_HEAD_DIM).astype(BF16)
    wkd = _block_diag_dense(wk, M_HEAD_DIM).astype(BF16)
    wvd = _block_diag_dense(wv, M_HEAD_DIM).astype(BF16)
    wif = w_if.astype(BF16)
    wift = w_if.T.astype(BF16)
    ng = 2 * M_HEADS
    hb = tm // HALO
    full = lambda shape: pl.BlockSpec(shape, lambda i: (0,) * len(shape))
    return pl.pallas_call(
        functools.partial(_mprep_kernel, tm=tm, tiles_per_seq=tiles_per_seq),
        grid=(n // tm,),
        in_specs=[pl.BlockSpec((tm, c), lambda i: (i, 0)),
                  pl.BlockSpec((HALO, c), lambda i: (jnp.maximum(i * hb - 1, 0), 0)),
                  full((CONV_WIDTH, c)), full((1, c)),
                  full((nblk, M_HEAD_DIM, M_HEAD_DIM)), full((nblk, M_HEAD_DIM, M_HEAD_DIM)),
                  full((nblk, M_HEAD_DIM, M_HEAD_DIM)),
                  full((3 * c, ng)), full((ng, 3 * c)), full((1, ng)), full((ng, 1))],
        out_specs=[pl.BlockSpec((tm, c), lambda i: (i, 0)),
                   pl.BlockSpec((tm, c), lambda i: (i, 0)),
                   pl.BlockSpec((tm, c), lambda i: (i, 0)),
                   pl.BlockSpec((tm, c), lambda i: (i, 0)),
                   pl.BlockSpec((tm, ng), lambda i: (i, 0)),
                   pl.BlockSpec((ng, tm), lambda i: (0, i))],
        out_shape=[jax.ShapeDtypeStruct((n, c), F32),
                   jax.ShapeDtypeStruct((n, c), BF16),
                   jax.ShapeDtypeStruct((n, c), BF16),
                   jax.ShapeDtypeStruct((n, c), BF16),
                   jax.ShapeDtypeStruct((n, ng), F32),
                   jax.ShapeDtypeStruct((ng, n), F32)],
        compiler_params=_cparams(("parallel",)),
        name="mlstm_prep",
    )(proj, proj, conv_w, conv_b.reshape(1, c), wqd, wkd, wvd, wif, wift,
      b_if.reshape(1, ng), b_if.reshape(ng, 1))


def _mlstm_kernel(q_ref, k_ref, v_ref, g_ref, gt_ref, xc_ref, z_ref, nw_ref, sk_ref, o_ref,
                  c_ref, n_ref, m_ref, *, chunk):
    L = chunk
    hd = M_HEAD_DIM

    @pl.when(pl.program_id(1) == 0)
    def _():
        c_ref[...] = jnp.zeros_like(c_ref)
        n_ref[...] = jnp.zeros_like(n_ref)
        m_ref[...] = jnp.zeros_like(m_ref)

    rows = lax.broadcasted_iota(I32, (L, L), 0)
    cols = lax.broadcasted_iota(I32, (L, L), 1)
    causal = cols <= rows
    tril = jnp.where(causal, 1.0, 0.0).astype(F32)
    triu = jnp.where(rows <= cols, 1.0, 0.0).astype(F32)
    g = g_ref[...]
    gt = gt_ref[...]
    hi = lax.Precision.HIGHEST
    bcol_all = jnp.dot(tril, g, precision=hi, preferred_element_type=F32)
    brow_all = jnp.dot(gt, triu, precision=hi, preferred_element_type=F32)
    k_scale = hd ** -0.5
    nt = (((1,), (1,)), ((), ()))
    tn = (((0,), (0,)), ((), ()))

    for h in range(M_HEADS):
        sl = slice(h * hd, (h + 1) * hd)
        qh = q_ref[:, sl]
        kh = k_ref[:, sl]
        vh = v_ref[:, sl]
        i_col = g[:, h:h + 1]
        b_col = bcol_all[:, M_HEADS + h:M_HEADS + h + 1]
        i_row = gt[h:h + 1, :]
        b_row = brow_all[M_HEADS + h:M_HEADS + h + 1, :]
        m_prev = m_ref[h, 0:1, 0:1]
        c_prev = c_ref[h]
        n_prev = n_ref[h]

        dmat = jnp.where(causal, b_col - b_row + i_row, NEG_INF)
        a_col = b_col + m_prev
        m_row = jnp.maximum(a_col, jnp.max(dmat, axis=1, keepdims=True))
        s = lax.dot_general(qh, kh, nt, preferred_element_type=F32) * k_scale
        s = s * jnp.exp(dmat - m_row)
        inter = jnp.exp(a_col - m_row)
        num = inter * jnp.dot(qh, c_prev.astype(BF16), preferred_element_type=F32) \
            + jnp.dot(s.astype(BF16), vh, preferred_element_type=F32)
        qn = jnp.sum(qh.astype(F32) * n_prev, axis=1, keepdims=True)
        den = inter * qn + jnp.sum(s, axis=1, keepdims=True)
        hval = num * (1.0 / jnp.maximum(jnp.abs(den), jnp.exp(-m_row)))

        mu = jnp.mean(hval, axis=1, keepdims=True)
        cen = hval - mu
        var = jnp.mean(cen * cen, axis=1, keepdims=True)
        hn = cen * lax.rsqrt(var + LN_EPS) * nw_ref[:, sl]
        zz = z_ref[:, sl]
        o_ref[:, sl] = ((hn + sk_ref[:, sl] * xc_ref[:, sl]) * (zz * jax.nn.sigmoid(zz))).astype(o_ref.dtype)

        b_last = b_col[L - 1:L, :]
        w_log = b_last - b_col + i_col
        m_new = jnp.maximum(b_last + m_prev, jnp.max(w_log, axis=0, keepdims=True))
        decay = jnp.exp(b_last + m_prev - m_new)
        kw = kh.astype(F32) * (jnp.exp(w_log - m_new) * k_scale)
        c_ref[h] = decay * c_prev + lax.dot_general(kw.astype(BF16), vh, tn, preferred_element_type=F32)
        n_ref[h] = decay * n_prev + jnp.sum(kw, axis=0, keepdims=True)
        m_ref[h] = jnp.broadcast_to(m_new, m_ref.shape[1:])


def _mlstm(q, k, v, g, gt, xc, proj, norm_w, skip, batch, seq, chunk):
    n, c = q.shape
    nc = seq // chunk
    ng = 2 * M_HEADS
    zcol = M_WIDTH // c
    row = lambda b, j: (b * nc + j, 0)
    return pl.pallas_call(
        functools.partial(_mlstm_kernel, chunk=chunk),
        grid=(batch, nc),
        in_specs=[pl.BlockSpec((chunk, c), row), pl.BlockSpec((chunk, c), row), pl.BlockSpec((chunk, c), row),
                  pl.BlockSpec((chunk, ng), row),
                  pl.BlockSpec((ng, chunk), lambda b, j: (0, b * nc + j)),
                  pl.BlockSpec((chunk, c), row),
                  pl.BlockSpec((chunk, c), lambda b, j: (b * nc + j, zcol)),
                  pl.BlockSpec((1, c), lambda b, j: (0, 0)),
                  pl.BlockSpec((1, c), lambda b, j: (0, 0))],
        out_specs=pl.BlockSpec((chunk, c), row),
        out_shape=jax.ShapeDtypeStruct((n, c), BF16),
        scratch_shapes=[pltpu.VMEM((M_HEADS, M_HEAD_DIM, M_HEAD_DIM), F32),
                        pltpu.VMEM((M_HEADS, 1, M_HEAD_DIM), F32),
                        pltpu.VMEM((M_HEADS, 8, 128), F32)],
        compiler_params=_cparams(("parallel", "arbitrary")),
        name="mlstm_chunk",
    )(q, k, v, g, gt, xc, proj, norm_w.reshape(1, c), skip.reshape(1, c))


def _gelu_tanh(x):
    return 0.5 * x * (1.0 + jnp.tanh(math.sqrt(2.0 / math.pi) * (x + 0.044715 * (x * x * x))))


def _rglru_kernel(xr_ref, halo_ref, xg_ref, cw_ref, cb_ref, wa_ref, ba_ref, wx_ref, bx_ref, lam_ref,
                  o_ref, a_ref, b_ref, h_ref, *, tm, cw):
    t = pl.program_id(2)

    @pl.when(t == 0)
    def _():
        h_ref[...] = jnp.zeros_like(h_ref)

    halo = jnp.where(t == 0, 0.0, halo_ref[...])
    xc = _causal_conv(halo, xr_ref[...], cw_ref, cb_ref).reshape(tm, cw)
    xcb = xc.astype(BF16)
    nblk = cw // R_BLOCK
    ra = []
    rx = []
    for g in range(nblk):
        sl = slice(g * R_BLOCK, (g + 1) * R_BLOCK)
        ra.append(jnp.dot(xcb[:, sl], wa_ref[g], preferred_element_type=F32))
        rx.append(jnp.dot(xcb[:, sl], wx_ref[g], preferred_element_type=F32))
    r = jax.nn.sigmoid(jnp.concatenate(ra, axis=1) + ba_ref[...])
    ig = jax.nn.sigmoid(jnp.concatenate(rx, axis=1) + bx_ref[...])
    nl = -lam_ref[...]
    softplus = jnp.maximum(nl, 0.0) + jnp.log1p(jnp.exp(-jnp.abs(nl)))
    a = jnp.exp((-LRU_C * softplus) * r)
    b = jnp.sqrt(1.0 - a * a) * (ig * xc)

    a = a.reshape(tm // 8, 8, cw)
    b = b.reshape(tm // 8, 8, cw)
    sub = lax.broadcasted_iota(I32, a.shape, 1)
    for d in (1, 2, 4):
        keep = sub >= d
        a_sh = pltpu.roll(a, d, axis=1)
        b_sh = pltpu.roll(b, d, axis=1)
        b = jnp.where(keep, a * b_sh + b, b)
        a = jnp.where(keep, a * a_sh, a)
    a_ref[...] = a.reshape(tm, cw)
    b_ref[...] = b.reshape(tm, cw)

    def body(g, h):
        r0 = pl.multiple_of(g * 8, 8)
        hh = b_ref[pl.ds(r0, 8), :] + a_ref[pl.ds(r0, 8), :] * h
        b_ref[pl.ds(r0, 8), :] = hh
        return hh[7:8, :]

    h_ref[...] = lax.fori_loop(0, tm // 8, body, h_ref[...], unroll=8)
    o_ref[...] = (b_ref[...] * _gelu_tanh(xg_ref[...])).astype(o_ref.dtype)


def _rglru(proj, batch, seq, xr_off, xg_off, conv_w, conv_b, wa, ba, wx, bx, lam, tm, cw):
    n = proj.shape[0]
    ncol = R_WIDTH // cw
    nt = seq // tm
    per = cw // R_BLOCK
    xr_cb = xr_off // cw
    xg_cb = xg_off // cw
    hb = tm // HALO
    colv = lambda shape: pl.BlockSpec(shape, lambda b, j, t: (0, j))
    return pl.pallas_call(
        functools.partial(_rglru_kernel, tm=tm, cw=cw),
        grid=(batch, ncol, nt),
        in_specs=[pl.BlockSpec((tm, cw), lambda b, j, t: (b * nt + t, xr_cb + j)),
                  pl.BlockSpec((HALO, cw), lambda b, j, t: (jnp.maximum((b * nt + t) * hb - 1, 0), xr_cb + j)),
                  pl.BlockSpec((tm, cw), lambda b, j, t: (b * nt + t, xg_cb + j)),
                  colv((CONV_WIDTH, cw)), colv((1, cw)),
                  pl.BlockSpec((per, R_BLOCK, R_BLOCK), lambda b, j, t: (j, 0, 0)), colv((1, cw)),
                  pl.BlockSpec((per, R_BLOCK, R_BLOCK), lambda b, j, t: (j, 0, 0)), colv((1, cw)),
                  colv((1, cw))],
        out_specs=pl.BlockSpec((tm, cw), lambda b, j, t: (b * nt + t, j)),
        out_shape=jax.ShapeDtypeStruct((n, R_WIDTH), BF16),
        scratch_shapes=[pltpu.VMEM((tm, cw), F32), pltpu.VMEM((tm, cw), F32), pltpu.VMEM((1, cw), F32)],
        compiler_params=_cparams(("parallel", "parallel", "arbitrary")),
        name="rglru",
    )(proj, proj, proj, conv_w, conv_b.reshape(1, R_WIDTH), wa.astype(BF16), ba.reshape(1, R_WIDTH),
      wx.astype(BF16), bx.reshape(1, R_WIDTH), lam.reshape(1, R_WIDTH))


def _merge_kernel(ym_ref, yr_ref, wpm_ref, wpr_ref, g0_ref, g1_ref, bg_ref, o_ref):
    g0 = jax.nn.sigmoid(g0_ref[...] + bg_ref[0:1, :])
    g1 = jax.nn.sigmoid(g1_ref[...] + bg_ref[1:2, :])
    pm = jnp.dot(ym_ref[...], wpm_ref[...], preferred_element_type=F32)
    pr = jnp.dot(yr_ref[...], wpr_ref[...], preferred_element_type=F32)
    o_ref[...] = (g0 * pm + g1 * pr).astype(o_ref.dtype)


def _merge(ym, yr, w_pm, w_pr, proj, gate_off, b_gate, tm, bn):
    n = ym.shape[0]
    d = w_pm.shape[1]
    g0_cb = gate_off // bn
    g1_cb = (gate_off + d) // bn
    return pl.pallas_call(
        _merge_kernel,
        grid=(d // bn, n // tm),
        in_specs=[pl.BlockSpec((tm, ym.shape[1]), lambda j, i: (i, 0)),
                  pl.BlockSpec((tm, yr.shape[1]), lambda j, i: (i, 0)),
                  pl.BlockSpec((w_pm.shape[0], bn), lambda j, i: (0, j)),
                  pl.BlockSpec((w_pr.shape[0], bn), lambda j, i: (0, j)),
                  pl.BlockSpec((tm, bn), lambda j, i: (i, g0_cb + j)),
                  pl.BlockSpec((tm, bn), lambda j, i: (i, g1_cb + j)),
                  pl.BlockSpec((2, bn), lambda j, i: (0, j))],
        out_specs=pl.BlockSpec((tm, bn), lambda j, i: (i, j)),
        out_shape=jax.ShapeDtypeStruct((n, d), BF16),
        compiler_params=_cparams(("parallel", "parallel")),
        name="merge",
    )(ym, yr, w_pm.astype(BF16), w_pr.astype(BF16), proj, proj, b_gate)


def _layer_norm(y, g, b):
    mu = jnp.mean(y, axis=1, keepdims=True)
    cen = y - mu
    var = jnp.mean(cen * cen, axis=1, keepdims=True)
    return cen * lax.rsqrt(var + LN_EPS) * g + b


def _to_slabs(ref, x):
    w = ref.shape[1]
    for a in range(ROW_SLAB_SUB):
        ref[pl.ds(a, x.shape[0], stride=ROW_SLAB_SUB), :] = x[:, a * w:(a + 1) * w]


def _oproj_kernel(mg_ref, wo_ref, x_ref, g_ref, b_ref, rw_ref, x1_ref, x1s_ref, lg_ref):
    y = ALPHA * x_ref[...] + jnp.dot(mg_ref[...], wo_ref[...], preferred_element_type=F32)
    x1 = _layer_norm(y, g_ref[...], b_ref[...])
    x1_ref[...] = x1
    _to_slabs(x1s_ref, x1)
    lg_ref[...] = jnp.dot(x1, rw_ref[...], precision=lax.Precision.HIGHEST, preferred_element_type=F32)


def _oproj(merged, w_o, x, ln_g, ln_b, router_w, tm):
    n, d = x.shape
    e = router_w.shape[1]
    rw = jnp.pad(router_w, ((0, 0), (0, ROUTER_LANES - e)))
    sw = d // ROW_SLAB_SUB
    full = lambda shape: pl.BlockSpec(shape, lambda i: (0,) * len(shape))
    return pl.pallas_call(
        _oproj_kernel,
        grid=(n // tm,),
        in_specs=[pl.BlockSpec((tm, d), lambda i: (i, 0)), full((d, d)),
                  pl.BlockSpec((tm, d), lambda i: (i, 0)), full((1, d)), full((1, d)),
                  full((d, ROUTER_LANES))],
        out_specs=[pl.BlockSpec((tm, d), lambda i: (i, 0)),
                   pl.BlockSpec((tm * ROW_SLAB_SUB, sw), lambda i: (i, 0)),
                   pl.BlockSpec((tm, ROUTER_LANES), lambda i: (i, 0))],
        out_shape=[jax.ShapeDtypeStruct((n, d), F32), jax.ShapeDtypeStruct((n * ROW_SLAB_SUB, sw), F32),
                   jax.ShapeDtypeStruct((n, ROUTER_LANES), F32)],
        compiler_params=_cparams(("parallel",)),
        name="out_proj_ln",
    )(merged, w_o.astype(BF16), x, ln_g.reshape(1, d), ln_b.reshape(1, d), rw)


def _first_max(v, idx, sentinel):
    m = jnp.max(v, axis=0, keepdims=True)
    am = jnp.min(jnp.where(v == m, idx, sentinel), axis=0, keepdims=True)
    return m, am


def _route_kernel(lg_ref, bias_ref, tri_ref, te_ref, w_ref, rk_ref, cnt_ref, carry_ref, *, tm):
    @pl.when(pl.program_id(0) == 0)
    def _():
        carry_ref[...] = jnp.zeros_like(carry_ref)

    e = N_EXPERTS
    gs = e // N_GROUPS
    scores = jax.nn.sigmoid(jnp.transpose(lg_ref[...])[0:e, :])
    biased = scores + bias_ref[...]
    sub = lax.broadcasted_iota(I32, (gs, tm), 0)
    grp_rows = []
    for g in range(N_GROUPS):
        slab = biased[g * gs:(g + 1) * gs, :]
        m1, a1 = _first_max(slab, sub, gs)
        m2 = jnp.max(jnp.where(sub == a1, NEG_INF, slab), axis=0, keepdims=True)
        grp_rows.append(m1 + m2)
    grp = jnp.concatenate(grp_rows, axis=0)
    gidx = lax.broadcasted_iota(I32, (N_GROUPS, tm), 0)
    gsel = jnp.zeros((N_GROUPS, tm), F32)
    for _ in range(TOPK_GROUPS):
        _, am = _first_max(grp, gidx, N_GROUPS)
        hit = gidx == am
        gsel = jnp.where(hit, 1.0, gsel)
        grp = jnp.where(hit, NEG_INF, grp)
    masked = jnp.concatenate(
        [jnp.where(gsel[g:g + 1, :] > 0.0, biased[g * gs:(g + 1) * gs, :], NEG_INF) for g in range(N_GROUPS)],
        axis=0)
    eidx = lax.broadcasted_iota(I32, (e, tm), 0)
    member = jnp.zeros((e, tm), F32)
    tops = []
    ws = []
    for _ in range(TOP_K):
        _, am = _first_max(masked, eidx, e)
        hit = eidx == am
        tops.append(am)
        ws.append(jnp.sum(jnp.where(hit, scores, 0.0), axis=0, keepdims=True))
        member = jnp.where(hit, 1.0, member)
        masked = jnp.where(hit, NEG_INF, masked)
    wsum = ws[0]
    for k in range(1, TOP_K):
        wsum = wsum + ws[k]
    te_ref[...] = jnp.concatenate(tops, axis=0)
    w_ref[...] = jnp.concatenate(ws, axis=0) / wsum * ROUTED_SCALE

    cum = jnp.dot(member.astype(BF16), tri_ref[...], preferred_element_type=F32)
    carry = carry_ref[:, 0:1]
    rank = carry + cum - member
    rks = []
    for k in range(TOP_K):
        rks.append(jnp.sum(jnp.where(eidx == tops[k], rank, 0.0), axis=0, keepdims=True))
    rk_ref[...] = jnp.concatenate(rks, axis=0).astype(I32)
    new_carry = carry + cum[:, tm - 1:tm]
    carry_ref[...] = jnp.broadcast_to(new_carry, carry_ref.shape)
    cnt_ref[...] = jnp.broadcast_to(new_carry, cnt_ref.shape).astype(I32)


def _route(logits, router_bias, tm):
    n = logits.shape[0]
    e = router_bias.shape[0]
    tri = jnp.triu(jnp.ones((tm, tm), F32)).astype(BF16)
    return pl.pallas_call(
        functools.partial(_route_kernel, tm=tm),
        grid=(n // tm,),
        in_specs=[pl.BlockSpec((tm, ROUTER_LANES), lambda i: (i, 0)),
                  pl.BlockSpec((e, 1), lambda i: (0, 0)),
                  pl.BlockSpec((tm, tm), lambda i: (0, 0))],
        out_specs=[pl.BlockSpec((TOP_K, tm), lambda i: (0, i)),
                   pl.BlockSpec((TOP_K, tm), lambda i: (0, i)),
                   pl.BlockSpec((TOP_K, tm), lambda i: (0, i)),
                   pl.BlockSpec((e, 128), lambda i: (0, 0))],
        out_shape=[jax.ShapeDtypeStruct((TOP_K, n), I32), jax.ShapeDtypeStruct((TOP_K, n), F32),
                   jax.ShapeDtypeStruct((TOP_K, n), I32), jax.ShapeDtypeStruct((e, 128), I32)],
        scratch_shapes=[pltpu.VMEM((e, 128), F32)],
        compiler_params=_cparams(("arbitrary",)),
        name="route",
    )(logits, router_bias.reshape(e, 1), tri)


def _dest_kernel(te_ref, rk_ref, ps_ref, d_ref):
    te = te_ref[...]
    e = N_EXPERTS
    tm = te.shape[1]
    eidx = lax.broadcasted_iota(I32, (e, tm), 0)
    ps = ps_ref[...]
    rows = []
    for k in range(TOP_K):
        rows.append(jnp.sum(jnp.where(eidx == te[k:k + 1, :], ps, 0), axis=0, keepdims=True))
    d_ref[...] = jnp.concatenate(rows, axis=0) + rk_ref[...]


def _dest(top_e, rank, pad_starts, tm):
    n = top_e.shape[1]
    return pl.pallas_call(
        _dest_kernel,
        grid=(n // tm,),
        in_specs=[pl.BlockSpec((TOP_K, tm), lambda i: (0, i)),
                  pl.BlockSpec((TOP_K, tm), lambda i: (0, i)),
                  pl.BlockSpec((N_EXPERTS, 1), lambda i: (0, 0))],
        out_specs=pl.BlockSpec((TOP_K, tm), lambda i: (0, i)),
        out_shape=jax.ShapeDtypeStruct((TOP_K, n), I32),
        compiler_params=_cparams(("parallel",)),
        name="dest",
    )(top_e, rank, pad_starts.reshape(N_EXPERTS, 1))


def _invert_kernel(dest_ref, inv_ref):
    def body(j, c):
        inv_ref[dest_ref[j]] = j
        return c

    lax.fori_loop(0, dest_ref.shape[0], body, 0, unroll=32)


def _invert_permutation(dest_flat):
    m = dest_flat.shape[0]
    return pl.pallas_call(
        _invert_kernel,
        in_specs=[pl.BlockSpec(memory_space=pltpu.SMEM)],
        out_specs=pl.BlockSpec(memory_space=pltpu.SMEM),
        out_shape=jax.ShapeDtypeStruct((m,), I32),
        name="invert_perm",
    )(dest_flat)


def _expert_ffn(xb, wgb_ref, wub_ref, wdb_ref):
    hg = jnp.dot(xb, wgb_ref[...], preferred_element_type=F32)
    hu = jnp.dot(xb, wub_ref[...], preferred_element_type=F32)
    hh = (hg * jax.nn.sigmoid(hg)) * hu
    return jnp.dot(hh.astype(BF16), wdb_ref[...], preferred_element_type=F32)


def _expert_kernel(vt_ref, ve_ref, lo_ref, hi_ref, inv_ref, nxt_ref, e0_ref,
                   x_hbm, wg_hbm, wu_hbm, wd_hbm, out_hbm,
                   wgf_ref, wuf_ref, wdf_ref, wgb_ref, wub_ref, wdb_ref, xbuf, obuf, xb_ref, cur_ref,
                   gsem, ssem, wsem, *, n_tok, n_tiles):
    v = pl.program_id(0)
    sub = ROW_SLAB_SUB
    rows = xbuf.shape[1] // sub
    tile = vt_ref[v]
    lo = lo_ref[v]
    hi = hi_ref[v]
    nonempty = hi > lo
    first = jnp.logical_and(nonempty, lo == 0)
    slot = lax.rem(tile, GATHER_SLOTS)
    oslot = tile % 2
    ahead = GATHER_SLOTS - 1

    def gather_row(t, r):
        src_tile = jnp.minimum(t, n_tiles - 1)
        tok = inv_ref[src_tile * rows + r] & (n_tok - 1)
        s = lax.rem(t, GATHER_SLOTS)
        return pltpu.make_async_copy(x_hbm.at[tok], xbuf.at[s, pl.ds(r * sub, sub)], gsem.at[s])

    def scatter_row(t, r):
        j = inv_ref[t * rows + r]
        s = t % 2
        return pltpu.make_async_copy(obuf.at[s, pl.ds(r, 1)], out_hbm.at[pl.ds(j, 1)], ssem.at[s])

    def wait_gathered_tile(s):
        pltpu.make_async_copy(xbuf.at[s], xbuf.at[s], gsem.at[s]).wait()

    def wait_scattered_tile(s):
        pltpu.make_async_copy(obuf.at[s], obuf.at[s], ssem.at[s]).wait()

    def weight_copies(e):
        return (pltpu.make_async_copy(wg_hbm.at[e], wgf_ref, wsem.at[0]),
                pltpu.make_async_copy(wu_hbm.at[e], wuf_ref, wsem.at[1]),
                pltpu.make_async_copy(wd_hbm.at[e], wdf_ref, wsem.at[2]))

    def load_tile_bf16():
        w = xbuf.shape[2]
        for a in range(sub):
            xb_ref[:, a * w:(a + 1) * w] = xbuf[slot, pl.ds(a, rows, stride=sub), :].astype(BF16)


    @pl.when(v == 0)
    def _():
        cur_ref[0] = -1
        for c in weight_copies(e0_ref[0]):
            c.start()

    @pl.when(jnp.logical_and(nonempty, cur_ref[0] != ve_ref[v]))
    def _():
        e = ve_ref[v]
        for c in weight_copies(e):
            c.wait()
        wgb_ref[...] = wgf_ref[...].astype(BF16)
        wub_ref[...] = wuf_ref[...].astype(BF16)
        wdb_ref[...] = wdf_ref[...].astype(BF16)
        cur_ref[0] = e
        ne = nxt_ref[e]

        @pl.when(ne >= 0)
        def _():
            for c in weight_copies(ne):
                c.start()

    @pl.when(jnp.logical_and(first, tile == 0))
    def _():
        def issue(i, c):
            gather_row(i // rows, i % rows).start(priority=ROW_COPY_PRIORITY)
            return c

        lax.fori_loop(0, ahead * rows, issue, 0)

    @pl.when(jnp.logical_and(first, tile >= 2))
    def _():
        wait_scattered_tile(oslot)

    @pl.when(first)
    def _():
        wait_gathered_tile(slot)

    @pl.when(jnp.logical_and(first, tile == 0))
    def _():
        load_tile_bf16()
        for r in range(rows):
            gather_row(tile + ahead, r).start(priority=ROW_COPY_PRIORITY)
        obuf[oslot] = _expert_ffn(xb_ref[...], wgb_ref, wub_ref, wdb_ref)

    @pl.when(jnp.logical_and(first, tile >= 1))
    def _():
        for r in range(rows):
            scatter_row(tile - 1, r).start(priority=ROW_COPY_PRIORITY)
        load_tile_bf16()
        for r in range(rows):
            gather_row(tile + ahead, r).start(priority=ROW_COPY_PRIORITY)
        obuf[oslot] = _expert_ffn(xb_ref[...], wgb_ref, wub_ref, wdb_ref)

    @pl.when(jnp.logical_and(nonempty, lo > 0))
    def _():
        load_tile_bf16()
        y = _expert_ffn(xb_ref[...], wgb_ref, wub_ref, wdb_ref)
        row = lax.broadcasted_iota(I32, (rows, 1), 0)
        mine = jnp.logical_and(row >= lo, row < hi)
        obuf[oslot] = jnp.where(mine, y, obuf[oslot])

    @pl.when(v == pl.num_programs(0) - 1)
    def _():
        last = n_tiles - 1

        def issue(r, c):
            scatter_row(last, r).start(priority=ROW_COPY_PRIORITY)
            return c

        lax.fori_loop(0, rows, issue, 0)
        wait_scattered_tile((last - 1) % 2)
        wait_scattered_tile(last % 2)
        for t in range(last + 1, last + 1 + ahead):
            wait_gathered_tile(t % GATHER_SLOTS)


def _experts(x1s, inv, visit_tile, visit_expert, visit_lo, visit_hi, next_expert, first_expert,
             w_gate, w_up, w_down):
    n_tok, sub, sw = x1s.shape
    d = sub * sw
    m = inv.shape[0]
    de = w_gate.shape[2]
    n_tiles = m // EXPERT_TILE
    any_spec = pl.BlockSpec(memory_space=pl.ANY)
    grid_spec = pltpu.PrefetchScalarGridSpec(
        num_scalar_prefetch=7,
        grid=(visit_tile.shape[0],),
        in_specs=[any_spec, any_spec, any_spec, any_spec],
        out_specs=any_spec,
        scratch_shapes=[pltpu.VMEM((d, de), F32), pltpu.VMEM((d, de), F32), pltpu.VMEM((de, d), F32),
                        pltpu.VMEM((d, de), BF16), pltpu.VMEM((d, de), BF16), pltpu.VMEM((de, d), BF16),
                        pltpu.VMEM((GATHER_SLOTS, EXPERT_TILE * sub, sw), F32),
                        pltpu.VMEM((2, EXPERT_TILE, d), F32),
                        pltpu.VMEM((EXPERT_TILE, d), BF16), pltpu.SMEM((1,), I32),
                        pltpu.SemaphoreType.DMA((GATHER_SLOTS,)), pltpu.SemaphoreType.DMA((2,)),
                        pltpu.SemaphoreType.DMA((3,))],
    )
    return pl.pallas_call(
        functools.partial(_expert_kernel, n_tok=n_tok, n_tiles=n_tiles),
        grid_spec=grid_spec,
        out_shape=jax.ShapeDtypeStruct((m, d), F32),
        compiler_params=_cparams(("arbitrary",)),
        name="experts",
    )(visit_tile, visit_expert, visit_lo, visit_hi, inv, next_expert, first_expert, x1s, w_gate, w_up, w_down)


def _combine_kernel(x1_ref, w_ref, y_ref, sg_ref, su_ref, sd_ref, g_ref, b_ref, o_ref):
    x1 = x1_ref[...]
    xb = x1.astype(BF16)
    hg = jnp.dot(xb, sg_ref[...], preferred_element_type=F32)
    hu = jnp.dot(xb, su_ref[...], preferred_element_type=F32)
    hh = (hg * jax.nn.sigmoid(hg)) * hu
    shared = jnp.dot(hh.astype(BF16), sd_ref[...], preferred_element_type=F32)
    w = w_ref[...]
    routed = y_ref[0] * w[:, 0:1]
    for k in range(1, TOP_K):
        routed = routed + y_ref[k] * w[:, k:k + 1]
    o_ref[...] = _layer_norm(ALPHA * x1 + (routed + shared), g_ref[...], b_ref[...])


def _combine(x1, w_tok, y8, s_gate, s_up, s_down, ln_g, ln_b, tm):
    n, d = x1.shape
    de = s_gate.shape[1]
    full = lambda shape: pl.BlockSpec(shape, lambda i: (0,) * len(shape))
    return pl.pallas_call(
        _combine_kernel,
        grid=(n // tm,),
        in_specs=[pl.BlockSpec((tm, d), lambda i: (i, 0)),
                  pl.BlockSpec((tm, TOP_K), lambda i: (i, 0)),
                  pl.BlockSpec((TOP_K, tm, d), lambda i: (0, i, 0)),
                  full((d, de)), full((d, de)), full((de, d)), full((1, d)), full((1, d))],
        out_specs=pl.BlockSpec((tm, d), lambda i: (i, 0)),
        out_shape=jax.ShapeDtypeStruct((n, d), F32),
        compiler_params=_cparams(("parallel",)),
        name="combine",
    )(x1, w_tok, y8, s_gate.astype(BF16), s_up.astype(BF16), s_down.astype(BF16),
      ln_g.reshape(1, d), ln_b.reshape(1, d))


def _visit_plan(counts, n_rows):
    e = counts.shape[0]
    n_tiles = n_rows // EXPERT_TILE
    ends = jnp.cumsum(counts)
    starts = ends - counts
    pos = jnp.sort(jnp.concatenate([jnp.arange(n_tiles, dtype=I32) * EXPERT_TILE, starts]))
    nxt = jnp.concatenate([pos[1:], jnp.full((1,), n_rows, I32)])
    tile = jnp.minimum(pos // EXPERT_TILE, n_tiles - 1)
    expert = jnp.minimum(jnp.sum((ends[None, :] <= pos[:, None]).astype(I32), axis=1), e - 1)
    ids = jnp.arange(e, dtype=I32)
    later = jnp.logical_and(ids[None, :] > ids[:, None], counts[None, :] > 0)
    next_expert = jnp.min(jnp.where(later, ids[None, :], e), axis=1)
    next_expert = jnp.where(next_expert < e, next_expert, -1).astype(I32)
    first_expert = jnp.min(jnp.where(counts > 0, ids, e - 1)).astype(I32).reshape(1)
    return starts, tile, expert, pos - tile * EXPERT_TILE, nxt - tile * EXPERT_TILE, next_expert, first_expert


def _moe(x1, x1s, logits, router_bias, e_w_gate, e_w_up, e_w_down, s_w_gate, s_w_up, s_w_down, ln_g, ln_b):
    n, d = x1.shape
    sub = ROW_SLAB_SUB
    top_e, w_t, rank, counts = _route(logits, router_bias, tm=512)
    starts, v_tile, v_expert, v_lo, v_hi, next_e, first_e = _visit_plan(counts[:, 0], n * TOP_K)
    dest = _dest(top_e, rank, starts, tm=1024)
    inv = _invert_permutation(dest.reshape(n * TOP_K))
    y8 = _experts(x1s.reshape(n, sub, d // sub), inv, v_tile, v_expert, v_lo, v_hi, next_e, first_e,
                  e_w_gate, e_w_up, e_w_down)
    return _combine(x1, w_t.T, y8.reshape(TOP_K, n, d), s_w_gate, s_w_up, s_w_down, ln_g, ln_b, tm=128)


def _layer(x, w_in, b_gate, m_conv_w, m_conv_b, m_wq, m_wk, m_wv, m_w_if, m_b_if, m_norm_w, m_skip,
           r_conv_w, r_conv_b, r_wa, r_ba, r_wx, r_bx, r_lambda, w_pm, w_pr, w_o, ln1_g, ln1_b,
           router_w, router_bias, e_w_gate, e_w_up, e_w_down, s_w_gate, s_w_up, s_w_down, ln2_g, ln2_b):
    batch, seq, d = x.shape
    n = batch * seq
    xt = x.reshape(n, d)
    proj = _in_proj(xt.astype(BF16), w_in, bm=1024, bn=1024)
    o_z = M_WIDTH
    o_xr = 2 * M_WIDTH
    o_xg = o_xr + R_WIDTH
    o_gate = o_xg + R_WIDTH
    del o_z
    xc, q, k, v, g, gt = _mlstm_prep(proj, seq, m_conv_w, m_conv_b, m_wq, m_wk, m_wv, m_w_if, m_b_if, tm=256)
    y_m = _mlstm(q, k, v, g, gt, xc, proj, m_norm_w, m_skip, batch, seq, MLSTM_CHUNK)
    y_r = _rglru(proj, batch, seq, o_xr, o_xg, r_conv_w, r_conv_b, r_wa, r_ba, r_wx, r_bx, r_lambda,
                 tm=512, cw=512)
    merged = _merge(y_m, y_r, w_pm, w_pr, proj, o_gate, b_gate, tm=512, bn=1024)
    x1, x1s, logits = _oproj(merged, w_o, xt, ln1_g, ln1_b, router_w, tm=256)
    out = _moe(x1, x1s, logits, router_bias, e_w_gate, e_w_up, e_w_down, s_w_gate, s_w_up, s_w_down, ln2_g, ln2_b)
    return out.reshape(batch, seq, d)


def kernel(x, w_in, b_gate, m_conv_w, m_conv_b, m_wq, m_wk, m_wv, m_w_if, m_b_if, m_norm_w, m_skip, r_conv_w, r_conv_b, r_wa, r_ba, r_wx, r_bx, r_lambda, w_pm, w_pr, w_o, ln1_g, ln1_b, router_w, router_bias, e_w_gate, e_w_up, e_w_down, s_w_gate, s_w_up, s_w_down, ln2_g, ln2_b):
    for l in range(DEPTH):
        x = _layer(x, w_in[l], b_gate[l], m_conv_w[l], m_conv_b[l], m_wq[l], m_wk[l], m_wv[l],
                   m_w_if[l], m_b_if[l], m_norm_w[l], m_skip[l], r_conv_w[l], r_conv_b[l],
                   r_wa[l], r_ba[l], r_wx[l], r_bx[l], r_lambda[l], w_pm[l], w_pr[l], w_o[l],
                   ln1_g[l], ln1_b[l], router_w[l], router_bias[l], e_w_gate[l], e_w_up[l],
                   e_w_down[l], s_w_gate[l], s_w_up[l], s_w_down[l], ln2_g[l], ln2_b[l])
    return x
```

```python
import functools
import math

import jax
import jax.numpy as jnp
from jax import lax
from jax.experimental import pallas as pl
from jax.experimental.pallas import tpu as pltpu

F32 = jnp.float32
BF16 = jnp.bfloat16
I32 = jnp.int32
U32 = jnp.uint32

D_MODEL = 2048
M_WIDTH = 2048
M_HEADS = 8
M_HEAD_DIM = 256
M_QKV_BLOCK = 4
CONV_WIDTH = 4
R_WIDTH = 2560
R_BLOCK = 256
LRU_C = 8.0
N_EXPERTS = 64
TOP_K = 8
N_GROUPS = 8
TOPK_GROUPS = 4
D_EXPERT = 512
ROUTED_SCALE = 2.5
DEPTH = 1
ALPHA = (2.0 * DEPTH) ** 0.25
LN_EPS = 1e-5

V7X_VMEM_LIMIT = 56 * 1024 * 1024
HALO = 8
MLSTM_CHUNK = 256
EXPERT_TILE = 256
NEG_INF = float("-inf")
ROUTER_LANES = 128
GATHER_SLOTS = 3
ROW_COPY_PRIORITY = 1


def _cparams(sem, vmem=V7X_VMEM_LIMIT):
    return pltpu.CompilerParams(dimension_semantics=sem, vmem_limit_bytes=vmem)


def _inproj_kernel(a_ref, w_ref, o_ref, wb_ref):
    @pl.when(pl.program_id(1) == 0)
    def _():
        wb_ref[...] = w_ref[...].astype(BF16)

    o_ref[...] = jnp.dot(a_ref[...], wb_ref[...], preferred_element_type=F32)


def _in_proj(a, w, bm, bn):
    m, k = a.shape
    n = w.shape[1]
    return pl.pallas_call(
        _inproj_kernel,
        grid=(n // bn, m // bm),
        in_specs=[pl.BlockSpec((bm, k), lambda j, i: (i, 0)),
                  pl.BlockSpec((k, bn), lambda j, i: (0, j))],
        out_specs=pl.BlockSpec((bm, bn), lambda j, i: (i, j)),
        out_shape=jax.ShapeDtypeStruct((m, n), F32),
        scratch_shapes=[pltpu.VMEM((k, bn), BF16)],
        compiler_params=_cparams(("parallel", "arbitrary")),
        name="in_proj",
    )(a, w)


def _log_sigmoid(x):
    return jnp.minimum(x, 0.0) - jnp.log1p(jnp.exp(-jnp.abs(x)))


def _shift_rows(x3, prev_group, j):
    rot = pltpu.roll(x3, j, axis=1)
    prev = jnp.concatenate([pltpu.roll(prev_group, j, axis=1), rot[:-1]], axis=0)
    sub = lax.broadcasted_iota(I32, x3.shape, 1)
    return jnp.where(sub >= j, rot, prev)


def _causal_conv(halo, x, cw_ref, cb_ref):
    tm, c = x.shape
    x3 = x.reshape(tm // HALO, HALO, c)
    h3 = halo.reshape(1, HALO, c)
    last = CONV_WIDTH - 1
    y = cb_ref[...] + x3 * cw_ref[last:last + 1, :]
    for j in range(1, CONV_WIDTH):
        y = y + _shift_rows(x3, h3, j) * cw_ref[last - j:last - j + 1, :]
    return y


def _mprep_kernel(xm_ref, halo_ref, cw_ref, cb_ref, wq_ref, wk_ref, wv_ref, wif_ref, wift_ref,
                  bif_ref, bift_ref, xc_ref, q_ref, k_ref, v_ref, g_ref, gt_ref,
                  *, tm, tiles_per_seq):
    i = pl.program_id(0)
    first = (i % tiles_per_seq) == 0
    halo = jnp.where(first, 0.0, halo_ref[...])
    xm = xm_ref[...]
    y = _causal_conv(halo, xm, cw_ref, cb_ref).reshape(tm, M_WIDTH)
    xc = y * jax.nn.sigmoid(y)
    xc_ref[...] = xc
    xcb = xc.astype(BF16)
    xmb = xm.astype(BF16)
    nblk = M_WIDTH // M_HEAD_DIM
    for g in range(nblk):
        sl = slice(g * M_HEAD_DIM, (g + 1) * M_HEAD_DIM)
        q_ref[:, sl] = jnp.dot(xcb[:, sl], wq_ref[g], preferred_element_type=F32).astype(BF16)
        k_ref[:, sl] = jnp.dot(xcb[:, sl], wk_ref[g], preferred_element_type=F32).astype(BF16)
        v_ref[:, sl] = jnp.dot(xmb[:, sl], wv_ref[g], preferred_element_type=F32).astype(BF16)
    qb, kb, vb = q_ref[...], k_ref[...], v_ref[...]
    w = M_WIDTH
    g = (jnp.dot(qb, wif_ref[0:w, :], preferred_element_type=F32)
         + jnp.dot(kb, wif_ref[w:2 * w, :], preferred_element_type=F32)
         + jnp.dot(vb, wif_ref[2 * w:3 * w, :], preferred_element_type=F32) + bif_ref[...])
    nt = (((1,), (1,)), ((), ()))
    gt = (lax.dot_general(wift_ref[:, 0:w], qb, nt, preferred_element_type=F32)
          + lax.dot_general(wift_ref[:, w:2 * w], kb, nt, preferred_element_type=F32)
          + lax.dot_general(wift_ref[:, 2 * w:3 * w], vb, nt, preferred_element_type=F32) + bift_ref[...])
    col = lax.broadcasted_iota(I32, g.shape, 1)
    g_ref[...] = jnp.where(col >= M_HEADS, _log_sigmoid(g), g)
    row = lax.broadcasted_iota(I32, gt.shape, 0)
    gt_ref[...] = jnp.where(row >= M_HEADS, _log_sigmoid(gt), gt)


def _block_diag_dense(w, group):
    nb, bi, bo = w.shape
    per = group // bi
    w4 = w.reshape(nb // per, per, bi, bo)
    eye = jnp.eye(per, dtype=w.dtype)
    return jnp.einsum("gaio,ab->gaibo", w4, eye).reshape(nb // per, group, group)


def _mlstm_prep(proj, seq, conv_w, conv_b, wq, wk, wv, w_if, b_if, tm):
    n = proj.shape[0]
    c = M_WIDTH
    nblk = c // M_HEAD_DIM
    tiles_per_seq = seq // tm
    wqd = _block_diag_dense(wq, M_HEAD_DIM).astype(BF16)
    wkd = _block_diag_dense(wk, M_HEAD_DIM).astype(BF16)
    wvd = _block_diag_dense(wv, M_HEAD_DIM).astype(BF16)
    wif = w_if.astype(BF16)
    wift = w_if.T.astype(BF16)
    ng = 2 * M_HEADS
    hb = tm // HALO
    full = lambda shape: pl.BlockSpec(shape, lambda i: (0,) * len(shape))
    return pl.pallas_call(
        functools.partial(_mprep_kernel, tm=tm, tiles_per_seq=tiles_per_seq),
        grid=(n // tm,),
        in_specs=[pl.BlockSpec((tm, c), lambda i: (i, 0)),
                  pl.BlockSpec((HALO, c), lambda i: (jnp.maximum(i * hb - 1, 0), 0)),
                  full((CONV_WIDTH, c)), full((1, c)),
                  full((nblk, M_HEAD_DIM, M_HEAD_DIM)), full((nblk, M_HEAD_DIM, M_HEAD_DIM)),
                  full((nblk, M_HEAD_DIM, M_HEAD_DIM)),
                  full((3 * c, ng)), full((ng, 3 * c)), full((1, ng)), full((ng, 1))],
        out_specs=[pl.BlockSpec((tm, c), lambda i: (i, 0)),
                   pl.BlockSpec((tm, c), lambda i: (i, 0)),
                   pl.BlockSpec((tm, c), lambda i: (i, 0)),
                   pl.BlockSpec((tm, c), lambda i: (i, 0)),
                   pl.BlockSpec((tm, ng), lambda i: (i, 0)),
                   pl.BlockSpec((ng, tm), lambda i: (0, i))],
        out_shape=[jax.ShapeDtypeStruct((n, c), F32),
                   jax.ShapeDtypeStruct((n, c), BF16),
                   jax.ShapeDtypeStruct((n, c), BF16),
                   jax.ShapeDtypeStruct((n, c), BF16),
                   jax.ShapeDtypeStruct((n, ng), F32),
                   jax.ShapeDtypeStruct((ng, n), F32)],
        compiler_params=_cparams(("parallel",)),
        name="mlstm_prep",
    )(proj, proj, conv_w, conv_b.reshape(1, c), wqd, wkd, wvd, wif, wift,
      b_if.reshape(1, ng), b_if.reshape(ng, 1))


def _mlstm_kernel(q_ref, k_ref, v_ref, g_ref, gt_ref, xc_ref, z_ref, nw_ref, sk_ref, o_ref,
                  c_ref, n_ref, m_ref, *, chunk):
    L = chunk
    hd = M_HEAD_DIM

    @pl.when(pl.program_id(1) == 0)
    def _():
        c_ref[...] = jnp.zeros_like(c_ref)
        n_ref[...] = jnp.zeros_like(n_ref)
        m_ref[...] = jnp.zeros_like(m_ref)

    rows = lax.broadcasted_iota(I32, (L, L), 0)
    cols = lax.broadcasted_iota(I32, (L, L), 1)
    causal = cols <= rows
    tril = jnp.where(causal, 1.0, 0.0).astype(F32)
    triu = jnp.where(rows <= cols, 1.0, 0.0).astype(F32)
    g = g_ref[...]
    gt = gt_ref[...]
    hi = lax.Precision.HIGHEST
    bcol_all = jnp.dot(tril, g, precision=hi, preferred_element_type=F32)
    brow_all = jnp.dot(gt, triu, precision=hi, preferred_element_type=F32)
    k_scale = hd ** -0.5
    nt = (((1,), (1,)), ((), ()))
    tn = (((0,), (0,)), ((), ()))

    for h in range(M_HEADS):
        sl = slice(h * hd, (h + 1) * hd)
        qh = q_ref[:, sl]
        kh = k_ref[:, sl]
        vh = v_ref[:, sl]
        i_col = g[:, h:h + 1]
        b_col = bcol_all[:, M_HEADS + h:M_HEADS + h + 1]
        i_row = gt[h:h + 1, :]
        b_row = brow_all[M_HEADS + h:M_HEADS + h + 1, :]
        m_prev = m_ref[h, 0:1, 0:1]
        c_prev = c_ref[h]
        n_prev = n_ref[h]

        dmat = jnp.where(causal, b_col - b_row + i_row, NEG_INF)
        a_col = b_col + m_prev
        m_row = jnp.maximum(a_col, jnp.max(dmat, axis=1, keepdims=True))
        s = lax.dot_general(qh, kh, nt, preferred_element_type=F32) * k_scale
        s = s * jnp.exp(dmat - m_row)
        inter = jnp.exp(a_col - m_row)
        num = inter * jnp.dot(qh, c_prev.astype(BF16), preferred_element_type=F32) \
            + jnp.dot(s.astype(BF16), vh, preferred_element_type=F32)
        qn = jnp.sum(qh.astype(F32) * n_prev, axis=1, keepdims=True)
        den = inter * qn + jnp.sum(s, axis=1, keepdims=True)
        hval = num * (1.0 / jnp.maximum(jnp.abs(den), jnp.exp(-m_row)))

        mu = jnp.mean(hval, axis=1, keepdims=True)
        cen = hval - mu
        var = jnp.mean(cen * cen, axis=1, keepdims=True)
        hn = cen * lax.rsqrt(var + LN_EPS) * nw_ref[:, sl]
        zz = z_ref[:, sl]
        o_ref[:, sl] = ((hn + sk_ref[:, sl] * xc_ref[:, sl]) * (zz * jax.nn.sigmoid(zz))).astype(o_ref.dtype)

        b_last = b_col[L - 1:L, :]
        w_log = b_last - b_col + i_col
        m_new = jnp.maximum(b_last + m_prev, jnp.max(w_log, axis=0, keepdims=True))
        decay = jnp.exp(b_last + m_prev - m_new)
        kw = kh.astype(F32) * (jnp.exp(w_log - m_new) * k_scale)
        c_ref[h] = decay * c_prev + lax.dot_general(kw.astype(BF16), vh, tn, preferred_element_type=F32)
        n_ref[h] = decay * n_prev + jnp.sum(kw, axis=0, keepdims=True)
        m_ref[h] = jnp.broadcast_to(m_new, m_ref.shape[1:])


def _mlstm(q, k, v, g, gt, xc, proj, norm_w, skip, batch, seq, chunk):
    n, c = q.shape
    nc = seq // chunk
    ng = 2 * M_HEADS
    zcol = M_WIDTH // c
    row = lambda b, j: (b * nc + j, 0)
    return pl.pallas_call(
        functools.partial(_mlstm_kernel, chunk=chunk),
        grid=(batch, nc),
        in_specs=[pl.BlockSpec((chunk, c), row), pl.BlockSpec((chunk, c), row), pl.BlockSpec((chunk, c), row),
                  pl.BlockSpec((chunk, ng), row),
                  pl.BlockSpec((ng, chunk), lambda b, j: (0, b * nc + j)),
                  pl.BlockSpec((chunk, c), row),
                  pl.BlockSpec((chunk, c), lambda b, j: (b * nc + j, zcol)),
                  pl.BlockSpec((1, c), lambda b, j: (0, 0)),
                  pl.BlockSpec((1, c), lambda b, j: (0, 0))],
        out_specs=pl.BlockSpec((chunk, c), row),
        out_shape=jax.ShapeDtypeStruct((n, c), BF16),
        scratch_shapes=[pltpu.VMEM((M_HEADS, M_HEAD_DIM, M_HEAD_DIM), F32),
                        pltpu.VMEM((M_HEADS, 1, M_HEAD_DIM), F32),
                        pltpu.VMEM((M_HEADS, 8, 128), F32)],
        compiler_params=_cparams(("parallel", "arbitrary")),
        name="mlstm_chunk",
    )(q, k, v, g, gt, xc, proj, norm_w.reshape(1, c), skip.reshape(1, c))


def _gelu_tanh(x):
    return 0.5 * x * (1.0 + jnp.tanh(math.sqrt(2.0 / math.pi) * (x + 0.044715 * (x * x * x))))


def _rglru_kernel(xr_ref, halo_ref, xg_ref, cw_ref, cb_ref, wa_ref, ba_ref, wx_ref, bx_ref, lam_ref,
                  o_ref, a_ref, b_ref, h_ref, *, tm, cw):
    t = pl.program_id(2)

    @pl.when(t == 0)
    def _():
        h_ref[...] = jnp.zeros_like(h_ref)

    halo = jnp.where(t == 0, 0.0, halo_ref[...])
    xc = _causal_conv(halo, xr_ref[...], cw_ref, cb_ref).reshape(tm, cw)
    xcb = xc.astype(BF16)
    nblk = cw // R_BLOCK
    ra = []
    rx = []
    for g in range(nblk):
        sl = slice(g * R_BLOCK, (g + 1) * R_BLOCK)
        ra.append(jnp.dot(xcb[:, sl], wa_ref[g], preferred_element_type=F32))
        rx.append(jnp.dot(xcb[:, sl], wx_ref[g], preferred_element_type=F32))
    r = jax.nn.sigmoid(jnp.concatenate(ra, axis=1) + ba_ref[...])
    ig = jax.nn.sigmoid(jnp.concatenate(rx, axis=1) + bx_ref[...])
    nl = -lam_ref[...]
    softplus = jnp.maximum(nl, 0.0) + jnp.log1p(jnp.exp(-jnp.abs(nl)))
    a = jnp.exp((-LRU_C * softplus) * r)
    b = jnp.sqrt(1.0 - a * a) * (ig * xc)

    a = a.reshape(tm // 8, 8, cw)
    b = b.reshape(tm // 8, 8, cw)
    sub = lax.broadcasted_iota(I32, a.shape, 1)
    for d in (1, 2, 4):
        keep = sub >= d
        a_sh = pltpu.roll(a, d, axis=1)
        b_sh = pltpu.roll(b, d, axis=1)
        b = jnp.where(keep, a * b_sh + b, b)
        a = jnp.where(keep, a * a_sh, a)
    a_ref[...] = a.reshape(tm, cw)
    b_ref[...] = b.reshape(tm, cw)

    def body(g, h):
        r0 = pl.multiple_of(g * 8, 8)
        hh = b_ref[pl.ds(r0, 8), :] + a_ref[pl.ds(r0, 8), :] * h
        b_ref[pl.ds(r0, 8), :] = hh
        return hh[7:8, :]

    h_ref[...] = lax.fori_loop(0, tm // 8, body, h_ref[...], unroll=8)
    o_ref[...] = (b_ref[...] * _gelu_tanh(xg_ref[...])).astype(o_ref.dtype)


def _rglru(proj, batch, seq, xr_off, xg_off, conv_w, conv_b, wa, ba, wx, bx, lam, tm, cw):
    n = proj.shape[0]
    ncol = R_WIDTH // cw
    nt = seq // tm
    per = cw // R_BLOCK
    xr_cb = xr_off // cw
    xg_cb = xg_off // cw
    hb = tm // HALO
    colv = lambda shape: pl.BlockSpec(shape, lambda b, j, t: (0, j))
    return pl.pallas_call(
        functools.partial(_rglru_kernel, tm=tm, cw=cw),
        grid=(batch, ncol, nt),
        in_specs=[pl.BlockSpec((tm, cw), lambda b, j, t: (b * nt + t, xr_cb + j)),
                  pl.BlockSpec((HALO, cw), lambda b, j, t: (jnp.maximum((b * nt + t) * hb - 1, 0), xr_cb + j)),
                  pl.BlockSpec((tm, cw), lambda b, j, t: (b * nt + t, xg_cb + j)),
                  colv((CONV_WIDTH, cw)), colv((1, cw)),
                  pl.BlockSpec((per, R_BLOCK, R_BLOCK), lambda b, j, t: (j, 0, 0)), colv((1, cw)),
                  pl.BlockSpec((per, R_BLOCK, R_BLOCK), lambda b, j, t: (j, 0, 0)), colv((1, cw)),
                  colv((1, cw))],
        out_specs=pl.BlockSpec((tm, cw), lambda b, j, t: (b * nt + t, j)),
        out_shape=jax.ShapeDtypeStruct((n, R_WIDTH), BF16),
        scratch_shapes=[pltpu.VMEM((tm, cw), F32), pltpu.VMEM((tm, cw), F32), pltpu.VMEM((1, cw), F32)],
        compiler_params=_cparams(("parallel", "parallel", "arbitrary")),
        name="rglru",
    )(proj, proj, proj, conv_w, conv_b.reshape(1, R_WIDTH), wa.astype(BF16), ba.reshape(1, R_WIDTH),
      wx.astype(BF16), bx.reshape(1, R_WIDTH), lam.reshape(1, R_WIDTH))


def _merge_kernel(ym_ref, yr_ref, wpm_ref, wpr_ref, g0_ref, g1_ref, bg_ref, o_ref):
    g0 = jax.nn.sigmoid(g0_ref[...] + bg_ref[0:1, :])
    g1 = jax.nn.sigmoid(g1_ref[...] + bg_ref[1:2, :])
    pm = jnp.dot(ym_ref[...], wpm_ref[...], preferred_element_type=F32)
    pr = jnp.dot(yr_ref[...], wpr_ref[...], preferred_element_type=F32)
    o_ref[...] = (g0 * pm + g1 * pr).astype(o_ref.dtype)


def _merge(ym, yr, w_pm, w_pr, proj, gate_off, b_gate, tm, bn):
    n = ym.shape[0]
    d = w_pm.shape[1]
    g0_cb = gate_off // bn
    g1_cb = (gate_off + d) // bn
    return pl.pallas_call(
        _merge_kernel,
        grid=(d // bn, n // tm),
        in_specs=[pl.BlockSpec((tm, ym.shape[1]), lambda j, i: (i, 0)),
                  pl.BlockSpec((tm, yr.shape[1]), lambda j, i: (i, 0)),
                  pl.BlockSpec((w_pm.shape[0], bn), lambda j, i: (0, j)),
                  pl.BlockSpec((w_pr.shape[0], bn), lambda j, i: (0, j)),
                  pl.BlockSpec((tm, bn), lambda j, i: (i, g0_cb + j)),
                  pl.BlockSpec((tm, bn), lambda j, i: (i, g1_cb + j)),
                  pl.BlockSpec((2, bn), lambda j, i: (0, j))],
        out_specs=pl.BlockSpec((tm, bn), lambda j, i: (i, j)),
        out_shape=jax.ShapeDtypeStruct((n, d), BF16),
        compiler_params=_cparams(("parallel", "parallel")),
        name="merge",
    )(ym, yr, w_pm.astype(BF16), w_pr.astype(BF16), proj, proj, b_gate)


def _layer_norm(y, g, b):
    mu = jnp.mean(y, axis=1, keepdims=True)
    cen = y - mu
    var = jnp.mean(cen * cen, axis=1, keepdims=True)
    return cen * lax.rsqrt(var + LN_EPS) * g + b


def _split_hi_lo(x):
    hi = lax.bitcast_convert_type(lax.bitcast_convert_type(x, U32) & jnp.uint32(0xFFFF0000), F32)
    return hi.astype(BF16), (x - hi).astype(BF16)


def _oproj_kernel(mg_ref, wo_ref, x_ref, g_ref, b_ref, rwh_ref, rwl_ref, x1_ref, lg_ref):
    y = ALPHA * x_ref[...] + jnp.dot(mg_ref[...], wo_ref[...], preferred_element_type=F32)
    x1 = _layer_norm(y, g_ref[...], b_ref[...])
    x1_ref[...] = x1
    xh, xl = _split_hi_lo(x1)
    wh = rwh_ref[...]
    wl = rwl_ref[...]
    lg_ref[...] = ((jnp.dot(xh, wh, preferred_element_type=F32) + jnp.dot(xl, wl, preferred_element_type=F32))
                   + (jnp.dot(xl, wh, preferred_element_type=F32) + jnp.dot(xh, wl, preferred_element_type=F32)))


def _oproj(merged, w_o, x, ln_g, ln_b, router_w, tm):
    n, d = x.shape
    e = router_w.shape[1]
    rw_hi, rw_lo = _split_hi_lo(jnp.pad(router_w, ((0, 0), (0, ROUTER_LANES - e))))
    full = lambda shape: pl.BlockSpec(shape, lambda i: (0,) * len(shape))
    return pl.pallas_call(
        _oproj_kernel,
        grid=(n // tm,),
        in_specs=[pl.BlockSpec((tm, d), lambda i: (i, 0)), full((d, d)),
                  pl.BlockSpec((tm, d), lambda i: (i, 0)), full((1, d)), full((1, d)),
                  full((d, ROUTER_LANES)), full((d, ROUTER_LANES))],
        out_specs=[pl.BlockSpec((tm, d), lambda i: (i, 0)), pl.BlockSpec((tm, ROUTER_LANES), lambda i: (i, 0))],
        out_shape=[jax.ShapeDtypeStruct((n, d), F32), jax.ShapeDtypeStruct((n, ROUTER_LANES), F32)],
        compiler_params=_cparams(("parallel",)),
        name="out_proj_ln",
    )(merged, w_o.astype(BF16), x, ln_g.reshape(1, d), ln_b.reshape(1, d), rw_hi, rw_lo)


def _first_max(v, idx, sentinel):
    m = jnp.max(v, axis=0, keepdims=True)
    am = jnp.min(jnp.where(v == m, idx, sentinel), axis=0, keepdims=True)
    return m, am


def _route_kernel(lg_ref, bias_ref, tri_ref, te_ref, w_ref, rk_ref, cnt_ref, carry_ref, *, tm):
    @pl.when(pl.program_id(0) == 0)
    def _():
        carry_ref[...] = jnp.zeros_like(carry_ref)

    e = N_EXPERTS
    gs = e // N_GROUPS
    scores = jax.nn.sigmoid(jnp.transpose(lg_ref[...])[0:e, :])
    biased = scores + bias_ref[...]
    sub = lax.broadcasted_iota(I32, (gs, tm), 0)
    grp_rows = []
    for g in range(N_GROUPS):
        slab = biased[g * gs:(g + 1) * gs, :]
        m1, a1 = _first_max(slab, sub, gs)
        m2 = jnp.max(jnp.where(sub == a1, NEG_INF, slab), axis=0, keepdims=True)
        grp_rows.append(m1 + m2)
    grp = jnp.concatenate(grp_rows, axis=0)
    gidx = lax.broadcasted_iota(I32, (N_GROUPS, tm), 0)
    gsel = jnp.zeros((N_GROUPS, tm), F32)
    for _ in range(TOPK_GROUPS):
        _, am = _first_max(grp, gidx, N_GROUPS)
        hit = gidx == am
        gsel = jnp.where(hit, 1.0, gsel)
        grp = jnp.where(hit, NEG_INF, grp)
    masked = jnp.concatenate(
        [jnp.where(gsel[g:g + 1, :] > 0.0, biased[g * gs:(g + 1) * gs, :], NEG_INF) for g in range(N_GROUPS)],
        axis=0)
    eidx = lax.broadcasted_iota(I32, (e, tm), 0)
    member = jnp.zeros((e, tm), F32)
    tops = []
    ws = []
    for _ in range(TOP_K):
        _, am = _first_max(masked, eidx, e)
        hit = eidx == am
        tops.append(am)
        ws.append(jnp.sum(jnp.where(hit, scores, 0.0), axis=0, keepdims=True))
        member = jnp.where(hit, 1.0, member)
        masked = jnp.where(hit, NEG_INF, masked)
    wsum = ws[0]
    for k in range(1, TOP_K):
        wsum = wsum + ws[k]
    te_ref[...] = jnp.concatenate(tops, axis=0)
    w_ref[...] = jnp.concatenate(ws, axis=0) / wsum * ROUTED_SCALE

    cum = jnp.dot(member.astype(BF16), tri_ref[...], preferred_element_type=F32)
    carry = carry_ref[:, 0:1]
    rank = carry + cum - member
    rks = []
    for k in range(TOP_K):
        rks.append(jnp.sum(jnp.where(eidx == tops[k], rank, 0.0), axis=0, keepdims=True))
    rk_ref[...] = jnp.concatenate(rks, axis=0).astype(I32)
    new_carry = carry + cum[:, tm - 1:tm]
    carry_ref[...] = jnp.broadcast_to(new_carry, carry_ref.shape)
    cnt_ref[...] = jnp.broadcast_to(new_carry, cnt_ref.shape).astype(I32)


def _route(logits, router_bias, tm):
    n = logits.shape[0]
    e = router_bias.shape[0]
    tri = jnp.triu(jnp.ones((tm, tm), F32)).astype(BF16)
    return pl.pallas_call(
        functools.partial(_route_kernel, tm=tm),
        grid=(n // tm,),
        in_specs=[pl.BlockSpec((tm, ROUTER_LANES), lambda i: (i, 0)),
                  pl.BlockSpec((e, 1), lambda i: (0, 0)),
                  pl.BlockSpec((tm, tm), lambda i: (0, 0))],
        out_specs=[pl.BlockSpec((TOP_K, tm), lambda i: (0, i)),
                   pl.BlockSpec((TOP_K, tm), lambda i: (0, i)),
                   pl.BlockSpec((TOP_K, tm), lambda i: (0, i)),
                   pl.BlockSpec((e, 128), lambda i: (0, 0))],
        out_shape=[jax.ShapeDtypeStruct((TOP_K, n), I32), jax.ShapeDtypeStruct((TOP_K, n), F32),
                   jax.ShapeDtypeStruct((TOP_K, n), I32), jax.ShapeDtypeStruct((e, 128), I32)],
        scratch_shapes=[pltpu.VMEM((e, 128), F32)],
        compiler_params=_cparams(("arbitrary",)),
        name="route",
    )(logits, router_bias.reshape(e, 1), tri)


def _dest_kernel(te_ref, rk_ref, ps_ref, d_ref):
    te = te_ref[...]
    e = N_EXPERTS
    tm = te.shape[1]
    eidx = lax.broadcasted_iota(I32, (e, tm), 0)
    ps = ps_ref[...]
    rows = []
    for k in range(TOP_K):
        rows.append(jnp.sum(jnp.where(eidx == te[k:k + 1, :], ps, 0), axis=0, keepdims=True))
    d_ref[...] = jnp.concatenate(rows, axis=0) + rk_ref[...]


def _dest(top_e, rank, pad_starts, tm):
    n = top_e.shape[1]
    return pl.pallas_call(
        _dest_kernel,
        grid=(n // tm,),
        in_specs=[pl.BlockSpec((TOP_K, tm), lambda i: (0, i)),
                  pl.BlockSpec((TOP_K, tm), lambda i: (0, i)),
                  pl.BlockSpec((N_EXPERTS, 1), lambda i: (0, 0))],
        out_specs=pl.BlockSpec((TOP_K, tm), lambda i: (0, i)),
        out_shape=jax.ShapeDtypeStruct((TOP_K, n), I32),
        compiler_params=_cparams(("parallel",)),
        name="dest",
    )(top_e, rank, pad_starts.reshape(N_EXPERTS, 1))


def _invert_kernel(dest_ref, inv_ref):
    def body(j, c):
        inv_ref[dest_ref[j]] = j
        return c

    lax.fori_loop(0, dest_ref.shape[0], body, 0, unroll=32)


def _invert_permutation(dest_flat):
    m = dest_flat.shape[0]
    return pl.pallas_call(
        _invert_kernel,
        in_specs=[pl.BlockSpec(memory_space=pltpu.SMEM)],
        out_specs=pl.BlockSpec(memory_space=pltpu.SMEM),
        out_shape=jax.ShapeDtypeStruct((m,), I32),
        name="invert_perm",
    )(dest_flat)


def _expert_ffn(xb, wgb_ref, wub_ref, wdb_ref):
    hg = jnp.dot(xb, wgb_ref[...], preferred_element_type=F32)
    hu = jnp.dot(xb, wub_ref[...], preferred_element_type=F32)
    hh = (hg * jax.nn.sigmoid(hg)) * hu
    return jnp.dot(hh.astype(BF16), wdb_ref[...], preferred_element_type=F32)


def _expert_kernel(vt_ref, ve_ref, lo_ref, hi_ref, inv_ref, nxt_ref, e0_ref,
                   x_hbm, wg_hbm, wu_hbm, wd_hbm, out_hbm,
                   wgf_ref, wuf_ref, wdf_ref, wgb_ref, wub_ref, wdb_ref, xbuf, obuf, xb_ref, cur_ref,
                   gsem, ssem, wsem, *, n_tok, n_tiles):
    v = pl.program_id(0)
    rows = xbuf.shape[1]
    tile = vt_ref[v]
    lo = lo_ref[v]
    hi = hi_ref[v]
    nonempty = hi > lo
    first = jnp.logical_and(nonempty, lo == 0)
    slot = lax.rem(tile, GATHER_SLOTS)
    oslot = tile % 2
    ahead = GATHER_SLOTS - 1

    def gather_row(t, r):
        src_tile = jnp.minimum(t, n_tiles - 1)
        tok = inv_ref[src_tile * rows + r] & (n_tok - 1)
        s = lax.rem(t, GATHER_SLOTS)
        return pltpu.make_async_copy(x_hbm.at[pl.ds(tok, 1)], xbuf.at[s, pl.ds(r, 1)], gsem.at[s])

    def scatter_row(t, r):
        j = inv_ref[t * rows + r]
        s = t % 2
        return pltpu.make_async_copy(obuf.at[s, pl.ds(r, 1)], out_hbm.at[pl.ds(j, 1)], ssem.at[s])

    def wait_gathered_tile(s):
        pltpu.make_async_copy(x_hbm.at[pl.ds(0, rows)], xbuf.at[s], gsem.at[s]).wait()

    def wait_scattered_tile(s):
        pltpu.make_async_copy(obuf.at[s], out_hbm.at[pl.ds(0, rows)], ssem.at[s]).wait()

    def weight_copies(e):
        return (pltpu.make_async_copy(wg_hbm.at[e], wgf_ref, wsem.at[0]),
                pltpu.make_async_copy(wu_hbm.at[e], wuf_ref, wsem.at[1]),
                pltpu.make_async_copy(wd_hbm.at[e], wdf_ref, wsem.at[2]))

    @pl.when(v == 0)
    def _():
        cur_ref[0] = -1
        for c in weight_copies(e0_ref[0]):
            c.start()

    @pl.when(jnp.logical_and(nonempty, cur_ref[0] != ve_ref[v]))
    def _():
        e = ve_ref[v]
        for c in weight_copies(e):
            c.wait()
        wgb_ref[...] = wgf_ref[...].astype(BF16)
        wub_ref[...] = wuf_ref[...].astype(BF16)
        wdb_ref[...] = wdf_ref[...].astype(BF16)
        cur_ref[0] = e
        ne = nxt_ref[e]

        @pl.when(ne >= 0)
        def _():
            for c in weight_copies(ne):
                c.start()

    @pl.when(jnp.logical_and(first, tile == 0))
    def _():
        def issue(i, c):
            gather_row(i // rows, i % rows).start(priority=ROW_COPY_PRIORITY)
            return c

        lax.fori_loop(0, ahead * rows, issue, 0)

    @pl.when(jnp.logical_and(first, tile >= 2))
    def _():
        wait_scattered_tile(oslot)

    @pl.when(first)
    def _():
        wait_gathered_tile(slot)

    @pl.when(jnp.logical_and(first, tile == 0))
    def _():
        xb_ref[...] = xbuf[slot].astype(BF16)
        for r in range(rows):
            gather_row(tile + ahead, r).start(priority=ROW_COPY_PRIORITY)
        obuf[oslot] = _expert_ffn(xb_ref[...], wgb_ref, wub_ref, wdb_ref)

    @pl.when(jnp.logical_and(first, tile >= 1))
    def _():
        xb_ref[...] = xbuf[slot].astype(BF16)
        for r in range(rows):
            gather_row(tile + ahead, r).start(priority=ROW_COPY_PRIORITY)
            scatter_row(tile - 1, r).start(priority=ROW_COPY_PRIORITY)
        obuf[oslot] = _expert_ffn(xb_ref[...], wgb_ref, wub_ref, wdb_ref)

    @pl.when(jnp.logical_and(nonempty, lo > 0))
    def _():
        y = _expert_ffn(xbuf[slot].astype(BF16), wgb_ref, wub_ref, wdb_ref)
        row = lax.broadcasted_iota(I32, (rows, 1), 0)
        mine = jnp.logical_and(row >= lo, row < hi)
        obuf[oslot] = jnp.where(mine, y, obuf[oslot])

    @pl.when(v == pl.num_programs(0) - 1)
    def _():
        last = n_tiles - 1

        def issue(r, c):
            scatter_row(last, r).start(priority=ROW_COPY_PRIORITY)
            return c

        lax.fori_loop(0, rows, issue, 0)
        wait_scattered_tile((last - 1) % 2)
        wait_scattered_tile(last % 2)
        for t in range(last + 1, last + 1 + ahead):
            wait_gathered_tile(t % GATHER_SLOTS)


def _experts(x1, inv, visit_tile, visit_expert, visit_lo, visit_hi, next_expert, first_expert,
             w_gate, w_up, w_down):
    n_tok, d = x1.shape
    m = inv.shape[0]
    de = w_gate.shape[2]
    n_tiles = m // EXPERT_TILE
    any_spec = pl.BlockSpec(memory_space=pl.ANY)
    grid_spec = pltpu.PrefetchScalarGridSpec(
        num_scalar_prefetch=7,
        grid=(visit_tile.shape[0],),
        in_specs=[any_spec, any_spec, any_spec, any_spec],
        out_specs=any_spec,
        scratch_shapes=[pltpu.VMEM((d, de), F32), pltpu.VMEM((d, de), F32), pltpu.VMEM((de, d), F32),
                        pltpu.VMEM((d, de), BF16), pltpu.VMEM((d, de), BF16), pltpu.VMEM((de, d), BF16),
                        pltpu.VMEM((GATHER_SLOTS, EXPERT_TILE, d), F32), pltpu.VMEM((2, EXPERT_TILE, d), F32),
                        pltpu.VMEM((EXPERT_TILE, d), BF16), pltpu.SMEM((1,), I32),
                        pltpu.SemaphoreType.DMA((GATHER_SLOTS,)), pltpu.SemaphoreType.DMA((2,)),
                        pltpu.SemaphoreType.DMA((3,))],
    )
    return pl.pallas_call(
        functools.partial(_expert_kernel, n_tok=n_tok, n_tiles=n_tiles),
        grid_spec=grid_spec,
        out_shape=jax.ShapeDtypeStruct((m, d), F32),
        compiler_params=_cparams(("arbitrary",)),
        name="experts",
    )(visit_tile, visit_expert, visit_lo, visit_hi, inv, next_expert, first_expert, x1, w_gate, w_up, w_down)


def _combine_kernel(x1_ref, w_ref, y_ref, sg_ref, su_ref, sd_ref, g_ref, b_ref, o_ref):
    x1 = x1_ref[...]
    xb = x1.astype(BF16)
    hg = jnp.dot(xb, sg_ref[...], preferred_element_type=F32)
    hu = jnp.dot(xb, su_ref[...], preferred_element_type=F32)
    hh = (hg * jax.nn.sigmoid(hg)) * hu
    shared = jnp.dot(hh.astype(BF16), sd_ref[...], preferred_element_type=F32)
    w = w_ref[...]
    routed = y_ref[0] * w[:, 0:1]
    for k in range(1, TOP_K):
        routed = routed + y_ref[k] * w[:, k:k + 1]
    o_ref[...] = _layer_norm(ALPHA * x1 + (routed + shared), g_ref[...], b_ref[...])


def _combine(x1, w_tok, y8, s_gate, s_up, s_down, ln_g, ln_b, tm):
    n, d = x1.shape
    de = s_gate.shape[1]
    full = lambda shape: pl.BlockSpec(shape, lambda i: (0,) * len(shape))
    return pl.pallas_call(
        _combine_kernel,
        grid=(n // tm,),
        in_specs=[pl.BlockSpec((tm, d), lambda i: (i, 0)),
                  pl.BlockSpec((tm, TOP_K), lambda i: (i, 0)),
                  pl.BlockSpec((TOP_K, tm, d), lambda i: (0, i, 0)),
                  full((d, de)), full((d, de)), full((de, d)), full((1, d)), full((1, d))],
        out_specs=pl.BlockSpec((tm, d), lambda i: (i, 0)),
        out_shape=jax.ShapeDtypeStruct((n, d), F32),
        compiler_params=_cparams(("parallel",)),
        name="combine",
    )(x1, w_tok, y8, s_gate.astype(BF16), s_up.astype(BF16), s_down.astype(BF16),
      ln_g.reshape(1, d), ln_b.reshape(1, d))


def _visit_plan(counts, n_rows):
    e = counts.shape[0]
    n_tiles = n_rows // EXPERT_TILE
    ends = jnp.cumsum(counts)
    starts = ends - counts
    pos = jnp.sort(jnp.concatenate([jnp.arange(n_tiles, dtype=I32) * EXPERT_TILE, starts]))
    nxt = jnp.concatenate([pos[1:], jnp.full((1,), n_rows, I32)])
    tile = jnp.minimum(pos // EXPERT_TILE, n_tiles - 1)
    expert = jnp.minimum(jnp.sum((ends[None, :] <= pos[:, None]).astype(I32), axis=1), e - 1)
    ids = jnp.arange(e, dtype=I32)
    later = jnp.logical_and(ids[None, :] > ids[:, None], counts[None, :] > 0)
    next_expert = jnp.min(jnp.where(later, ids[None, :], e), axis=1)
    next_expert = jnp.where(next_expert < e, next_expert, -1).astype(I32)
    first_expert = jnp.min(jnp.where(counts > 0, ids, e - 1)).astype(I32).reshape(1)
    return starts, tile, expert, pos - tile * EXPERT_TILE, nxt - tile * EXPERT_TILE, next_expert, first_expert


def _moe(x1, logits, router_bias, e_w_gate, e_w_up, e_w_down, s_w_gate, s_w_up, s_w_down, ln_g, ln_b):
    n, d = x1.shape
    top_e, w_t, rank, counts = _route(logits, router_bias, tm=512)
    starts, v_tile, v_expert, v_lo, v_hi, next_e, first_e = _visit_plan(counts[:, 0], n * TOP_K)
    dest = _dest(top_e, rank, starts, tm=1024)
    inv = _invert_permutation(dest.reshape(n * TOP_K))
    y8 = _experts(x1, inv, v_tile, v_expert, v_lo, v_hi, next_e, first_e, e_w_gate, e_w_up, e_w_down)
    return _combine(x1, w_t.T, y8.reshape(TOP_K, n, d), s_w_gate, s_w_up, s_w_down, ln_g, ln_b, tm=128)


def _layer(x, w_in, b_gate, m_conv_w, m_conv_b, m_wq, m_wk, m_wv, m_w_if, m_b_if, m_norm_w, m_skip,
           r_conv_w, r_conv_b, r_wa, r_ba, r_wx, r_bx, r_lambda, w_pm, w_pr, w_o, ln1_g, ln1_b,
           router_w, router_bias, e_w_gate, e_w_up, e_w_down, s_w_gate, s_w_up, s_w_down, ln2_g, ln2_b):
    batch, seq, d = x.shape
    n = batch * seq
    xt = x.reshape(n, d)
    proj = _in_proj(xt.astype(BF16), w_in, bm=1024, bn=1024)
    o_xr = 2 * M_WIDTH
    o_xg = o_xr + R_WIDTH
    o_gate = o_xg + R_WIDTH
    xc, q, k, v, g, gt = _mlstm_prep(proj, seq, m_conv_w, m_conv_b, m_wq, m_wk, m_wv, m_w_if, m_b_if, tm=256)
    y_m = _mlstm(q, k, v, g, gt, xc, proj, m_norm_w, m_skip, batch, seq, MLSTM_CHUNK)
    y_r = _rglru(proj, batch, seq, o_xr, o_xg, r_conv_w, r_conv_b, r_wa, r_ba, r_wx, r_bx, r_lambda,
                 tm=512, cw=512)
    merged = _merge(y_m, y_r, w_pm, w_pr, proj, o_gate, b_gate, tm=512, bn=1024)
    x1, logits = _oproj(merged, w_o, xt, ln1_g, ln1_b, router_w, tm=256)
    out = _moe(x1, logits, router_bias, e_w_gate, e_w_up, e_w_down, s_w_gate, s_w_up, s_w_down, ln2_g, ln2_b)
    return out.reshape(batch, seq, d)


def kernel(x, w_in, b_gate, m_conv_w, m_conv_b, m_wq, m_wk, m_wv, m_w_if, m_b_if, m_norm_w, m_skip, r_conv_w, r_conv_b, r_wa, r_ba, r_wx, r_bx, r_lambda, w_pm, w_pr, w_o, ln1_g, ln1_b, router_w, router_bias, e_w_gate, e_w_up, e_w_down, s_w_gate, s_w_up, s_w_down, ln2_g, ln2_b):
    for l in range(DEPTH):
        x = _layer(x, w_in[l], b_gate[l], m_conv_w[l], m_conv_b[l], m_wq[l], m_wk[l], m_wv[l],
                   m_w_if[l], m_b_if[l], m_norm_w[l], m_skip[l], r_conv_w[l], r_conv_b[l],
                   r_wa[l], r_ba[l], r_wx[l], r_bx[l], r_lambda[l], w_pm[l], w_pr[l], w_o[l],
                   ln1_g[l], ln1_b[l], router_w[l], router_bias[l], e_w_gate[l], e_w_up[l],
                   e_w_down[l], s_w_gate[l], s_w_up[l], s_w_down[l], ln2_g[l], ln2_b[l])
    return x
```

```python
import functools
import math

import jax
import jax.numpy as jnp
from jax import lax
from jax.experimental import pallas as pl
from jax.experimental.pallas import tpu as pltpu

F32 = jnp.float32
BF16 = jnp.bfloat16
I32 = jnp.int32
U32 = jnp.uint32

D_MODEL = 2048
M_WIDTH = 2048
M_HEADS = 8
M_HEAD_DIM = 256
M_QKV_BLOCK = 4
CONV_WIDTH = 4
R_WIDTH = 2560
R_BLOCK = 256
LRU_C = 8.0
N_EXPERTS = 64
TOP_K = 8
N_GROUPS = 8
TOPK_GROUPS = 4
D_EXPERT = 512
ROUTED_SCALE = 2.5
DEPTH = 1
ALPHA = (2.0 * DEPTH) ** 0.25
LN_EPS = 1e-5

V7X_VMEM_LIMIT = 56 * 1024 * 1024
HALO = 8
MLSTM_CHUNK = 256
EXPERT_TILE = 256
NEG_INF = float("-inf")
ROUTER_LANES = 128
GATHER_SLOTS = 3
ROW_COPY_PRIORITY = 1


def _cparams(sem, vmem=V7X_VMEM_LIMIT):
    return pltpu.CompilerParams(dimension_semantics=sem, vmem_limit_bytes=vmem)


def _inproj_kernel(a_ref, w_ref, o_ref, wb_ref):
    @pl.when(pl.program_id(1) == 0)
    def _():
        wb_ref[...] = w_ref[...].astype(BF16)

    o_ref[...] = jnp.dot(a_ref[...], wb_ref[...], preferred_element_type=F32)


def _in_proj(a, w, bm, bn):
    m, k = a.shape
    n = w.shape[1]
    return pl.pallas_call(
        _inproj_kernel,
        grid=(n // bn, m // bm),
        in_specs=[pl.BlockSpec((bm, k), lambda j, i: (i, 0)),
                  pl.BlockSpec((k, bn), lambda j, i: (0, j))],
        out_specs=pl.BlockSpec((bm, bn), lambda j, i: (i, j)),
        out_shape=jax.ShapeDtypeStruct((m, n), F32),
        scratch_shapes=[pltpu.VMEM((k, bn), BF16)],
        compiler_params=_cparams(("parallel", "arbitrary")),
        name="in_proj",
    )(a, w)


def _sigmoid(x):
    return 0.5 * jnp.tanh(0.5 * x) + 0.5


def _log_sigmoid(x):
    return jnp.minimum(x, 0.0) - jnp.log1p(jnp.exp(-jnp.abs(x)))


def _shift_rows(x3, prev_group, j):
    rot = pltpu.roll(x3, j, axis=1)
    prev = jnp.concatenate([pltpu.roll(prev_group, j, axis=1), rot[:-1]], axis=0)
    sub = lax.broadcasted_iota(I32, x3.shape, 1)
    return jnp.where(sub >= j, rot, prev)


def _causal_conv(halo, x, cw_ref, cb_ref):
    tm, c = x.shape
    x3 = x.reshape(tm // HALO, HALO, c)
    h3 = halo.reshape(1, HALO, c)
    last = CONV_WIDTH - 1
    y = cb_ref[...] + x3 * cw_ref[last:last + 1, :]
    for j in range(1, CONV_WIDTH):
        y = y + _shift_rows(x3, h3, j) * cw_ref[last - j:last - j + 1, :]
    return y


def _mprep_kernel(xm_ref, halo_ref, cw_ref, cb_ref, wq_ref, wk_ref, wv_ref, wif_ref, wift_ref,
                  bif_ref, bift_ref, xc_ref, q_ref, k_ref, v_ref, g_ref, gt_ref,
                  *, tm, tiles_per_seq):
    i = pl.program_id(0)
    first = (i % tiles_per_seq) == 0
    halo = jnp.where(first, 0.0, halo_ref[...])
    xm = xm_ref[...]
    y = _causal_conv(halo, xm, cw_ref, cb_ref).reshape(tm, M_WIDTH)
    xc = y * _sigmoid(y)
    xc_ref[...] = xc
    xcb = xc.astype(BF16)
    xmb = xm.astype(BF16)
    nblk = M_WIDTH // M_HEAD_DIM
    for g in range(nblk):
        sl = slice(g * M_HEAD_DIM, (g + 1) * M_HEAD_DIM)
        q_ref[:, sl] = jnp.dot(xcb[:, sl], wq_ref[g], preferred_element_type=F32).astype(BF16)
        k_ref[:, sl] = jnp.dot(xcb[:, sl], wk_ref[g], preferred_element_type=F32).astype(BF16)
        v_ref[:, sl] = jnp.dot(xmb[:, sl], wv_ref[g], preferred_element_type=F32).astype(BF16)
    qb, kb, vb = q_ref[...], k_ref[...], v_ref[...]
    w = M_WIDTH
    g = (jnp.dot(qb, wif_ref[0:w, :], preferred_element_type=F32)
         + jnp.dot(kb, wif_ref[w:2 * w, :], preferred_element_type=F32)
         + jnp.dot(vb, wif_ref[2 * w:3 * w, :], preferred_element_type=F32) + bif_ref[...])
    nt = (((1,), (1,)), ((), ()))
    gt = (lax.dot_general(wift_ref[:, 0:w], qb, nt, preferred_element_type=F32)
          + lax.dot_general(wift_ref[:, w:2 * w], kb, nt, preferred_element_type=F32)
          + lax.dot_general(wift_ref[:, 2 * w:3 * w], vb, nt, preferred_element_type=F32) + bift_ref[...])
    col = lax.broadcasted_iota(I32, g.shape, 1)
    g_ref[...] = jnp.where(col >= M_HEADS, _log_sigmoid(g), g)
    row = lax.broadcasted_iota(I32, gt.shape, 0)
    gt_ref[...] = jnp.where(row >= M_HEADS, _log_sigmoid(gt), gt)


def _block_diag_kernel(w_ref, o_ref, *, bi):
    w = w_ref[0]
    group = o_ref.shape[1]
    row = lax.broadcasted_iota(I32, (group, group), 0)
    col = lax.broadcasted_iota(I32, (group, group), 1)
    dense = jnp.zeros((group, group), F32)
    for o in range(w.shape[1]):
        dense = jnp.where(col % w.shape[1] == o, w[:, o:o + 1], dense)
    o_ref[0] = jnp.where(row // bi == col // w.shape[1], dense, 0.0).astype(o_ref.dtype)


def _block_diag_dense(ws, group):
    nb, bi, bo = ws[0].shape
    slabs = len(ws) * nb * bi // group
    w3 = jnp.concatenate(ws, axis=0).reshape(slabs, group, bo)
    return pl.pallas_call(
        functools.partial(_block_diag_kernel, bi=bi),
        grid=(slabs,),
        in_specs=[pl.BlockSpec((1, group, bo), lambda i: (i, 0, 0))],
        out_specs=pl.BlockSpec((1, group, group), lambda i: (i, 0, 0)),
        out_shape=jax.ShapeDtypeStruct((slabs, group, group), BF16),
        compiler_params=_cparams(("parallel",)),
        name="block_diag",
    )(w3)


def _mlstm_prep(proj, seq, conv_w, conv_b, wq, wk, wv, w_if, b_if, tm):
    n = proj.shape[0]
    c = M_WIDTH
    nblk = c // M_HEAD_DIM
    tiles_per_seq = seq // tm
    wd = _block_diag_dense([wq, wk, wv], M_HEAD_DIM)
    wif = w_if.astype(BF16)
    wift = w_if.T.astype(BF16)
    ng = 2 * M_HEADS
    hb = tm // HALO
    full = lambda shape: pl.BlockSpec(shape, lambda i: (0,) * len(shape))
    return pl.pallas_call(
        functools.partial(_mprep_kernel, tm=tm, tiles_per_seq=tiles_per_seq),
        grid=(n // tm,),
        in_specs=[pl.BlockSpec((tm, c), lambda i: (i, 0)),
                  pl.BlockSpec((HALO, c), lambda i: (jnp.maximum(i * hb - 1, 0), 0)),
                  full((CONV_WIDTH, c)), full((1, c)),
                  pl.BlockSpec((nblk, M_HEAD_DIM, M_HEAD_DIM), lambda i: (0, 0, 0)),
                  pl.BlockSpec((nblk, M_HEAD_DIM, M_HEAD_DIM), lambda i: (1, 0, 0)),
                  pl.BlockSpec((nblk, M_HEAD_DIM, M_HEAD_DIM), lambda i: (2, 0, 0)),
                  full((3 * c, ng)), full((ng, 3 * c)), full((1, ng)), full((ng, 1))],
        out_specs=[pl.BlockSpec((tm, c), lambda i: (i, 0)),
                   pl.BlockSpec((tm, c), lambda i: (i, 0)),
                   pl.BlockSpec((tm, c), lambda i: (i, 0)),
                   pl.BlockSpec((tm, c), lambda i: (i, 0)),
                   pl.BlockSpec((tm, ng), lambda i: (i, 0)),
                   pl.BlockSpec((ng, tm), lambda i: (0, i))],
        out_shape=[jax.ShapeDtypeStruct((n, c), F32),
                   jax.ShapeDtypeStruct((n, c), BF16),
                   jax.ShapeDtypeStruct((n, c), BF16),
                   jax.ShapeDtypeStruct((n, c), BF16),
                   jax.ShapeDtypeStruct((n, ng), F32),
                   jax.ShapeDtypeStruct((ng, n), F32)],
        compiler_params=_cparams(("parallel",)),
        name="mlstm_prep",
    )(proj, proj, conv_w, conv_b.reshape(1, c), wd, wd, wd, wif, wift,
      b_if.reshape(1, ng), b_if.reshape(ng, 1))


def _mlstm_kernel(q_ref, k_ref, v_ref, g_ref, gt_ref, xc_ref, z_ref, nw_ref, sk_ref, o_ref,
                  c_ref, n_ref, m_ref, *, chunk):
    L = chunk
    hd = M_HEAD_DIM

    @pl.when(pl.program_id(1) == 0)
    def _():
        c_ref[...] = jnp.zeros_like(c_ref)
        n_ref[...] = jnp.zeros_like(n_ref)
        m_ref[...] = jnp.zeros_like(m_ref)

    rows = lax.broadcasted_iota(I32, (L, L), 0)
    cols = lax.broadcasted_iota(I32, (L, L), 1)
    causal = cols <= rows
    tril = jnp.where(causal, 1.0, 0.0).astype(F32)
    triu = jnp.where(rows <= cols, 1.0, 0.0).astype(F32)
    g = g_ref[...]
    gt = gt_ref[...]
    hi = lax.Precision.HIGHEST
    bcol_all = jnp.dot(tril, g, precision=hi, preferred_element_type=F32)
    brow_all = jnp.dot(gt, triu, precision=hi, preferred_element_type=F32)
    k_scale = hd ** -0.5
    nt = (((1,), (1,)), ((), ()))
    tn = (((0,), (0,)), ((), ()))

    for h in range(M_HEADS):
        sl = slice(h * hd, (h + 1) * hd)
        qh = q_ref[:, sl]
        kh = k_ref[:, sl]
        vh = v_ref[:, sl]
        i_col = g[:, h:h + 1]
        b_col = bcol_all[:, M_HEADS + h:M_HEADS + h + 1]
        i_row = gt[h:h + 1, :]
        b_row = brow_all[M_HEADS + h:M_HEADS + h + 1, :]
        m_prev = m_ref[h, 0:1, 0:1]
        c_prev = c_ref[h]
        n_prev = n_ref[h]

        dmat = jnp.where(causal, b_col - b_row + i_row, NEG_INF)
        a_col = b_col + m_prev
        m_row = jnp.maximum(a_col, jnp.max(dmat, axis=1, keepdims=True))
        s = lax.dot_general(qh, kh, nt, preferred_element_type=F32) * k_scale
        s = s * jnp.exp(dmat - m_row)
        inter = jnp.exp(a_col - m_row)
        num = inter * jnp.dot(qh, c_prev.astype(BF16), preferred_element_type=F32) \
            + jnp.dot(s.astype(BF16), vh, preferred_element_type=F32)
        qn = jnp.sum(qh.astype(F32) * n_prev, axis=1, keepdims=True)
        den = inter * qn + jnp.sum(s, axis=1, keepdims=True)
        hval = num * (1.0 / jnp.maximum(jnp.abs(den), jnp.exp(-m_row)))

        mu = jnp.mean(hval, axis=1, keepdims=True)
        cen = hval - mu
        var = jnp.mean(cen * cen, axis=1, keepdims=True)
        hn = cen * lax.rsqrt(var + LN_EPS) * nw_ref[:, sl]
        zz = z_ref[:, sl]
        o_ref[:, sl] = ((hn + sk_ref[:, sl] * xc_ref[:, sl]) * (zz * _sigmoid(zz))).astype(o_ref.dtype)

        b_last = b_col[L - 1:L, :]
        w_log = b_last - b_col + i_col
        m_new = jnp.maximum(b_last + m_prev, jnp.max(w_log, axis=0, keepdims=True))
        decay = jnp.exp(b_last + m_prev - m_new)
        kw = kh.astype(F32) * (jnp.exp(w_log - m_new) * k_scale)
        c_ref[h] = decay * c_prev + lax.dot_general(kw.astype(BF16), vh, tn, preferred_element_type=F32)
        n_ref[h] = decay * n_prev + jnp.sum(kw, axis=0, keepdims=True)
        m_ref[h] = jnp.broadcast_to(m_new, m_ref.shape[1:])


def _mlstm(q, k, v, g, gt, xc, proj, norm_w, skip, batch, seq, chunk):
    n, c = q.shape
    nc = seq // chunk
    ng = 2 * M_HEADS
    zcol = M_WIDTH // c
    row = lambda b, j: (b * nc + j, 0)
    return pl.pallas_call(
        functools.partial(_mlstm_kernel, chunk=chunk),
        grid=(batch, nc),
        in_specs=[pl.BlockSpec((chunk, c), row), pl.BlockSpec((chunk, c), row), pl.BlockSpec((chunk, c), row),
                  pl.BlockSpec((chunk, ng), row),
                  pl.BlockSpec((ng, chunk), lambda b, j: (0, b * nc + j)),
                  pl.BlockSpec((chunk, c), row),
                  pl.BlockSpec((chunk, c), lambda b, j: (b * nc + j, zcol)),
                  pl.BlockSpec((1, c), lambda b, j: (0, 0)),
                  pl.BlockSpec((1, c), lambda b, j: (0, 0))],
        out_specs=pl.BlockSpec((chunk, c), row),
        out_shape=jax.ShapeDtypeStruct((n, c), BF16),
        scratch_shapes=[pltpu.VMEM((M_HEADS, M_HEAD_DIM, M_HEAD_DIM), F32),
                        pltpu.VMEM((M_HEADS, 1, M_HEAD_DIM), F32),
                        pltpu.VMEM((M_HEADS, 8, 128), F32)],
        compiler_params=_cparams(("parallel", "arbitrary")),
        name="mlstm_chunk",
    )(q, k, v, g, gt, xc, proj, norm_w.reshape(1, c), skip.reshape(1, c))


def _gelu_tanh(x):
    return 0.5 * x * (1.0 + jnp.tanh(math.sqrt(2.0 / math.pi) * (x + 0.044715 * (x * x * x))))


def _rglru_kernel(xr_ref, halo_ref, xg_ref, cw_ref, cb_ref, wa_ref, ba_ref, wx_ref, bx_ref, lam_ref,
                  o_ref, a_ref, b_ref, h_ref, *, tm, cw):
    t = pl.program_id(2)

    @pl.when(t == 0)
    def _():
        h_ref[...] = jnp.zeros_like(h_ref)

    halo = jnp.where(t == 0, 0.0, halo_ref[...])
    xc = _causal_conv(halo, xr_ref[...], cw_ref, cb_ref).reshape(tm, cw)
    xcb = xc.astype(BF16)
    nblk = cw // R_BLOCK
    ra = []
    rx = []
    for g in range(nblk):
        sl = slice(g * R_BLOCK, (g + 1) * R_BLOCK)
        ra.append(jnp.dot(xcb[:, sl], wa_ref[g], preferred_element_type=F32))
        rx.append(jnp.dot(xcb[:, sl], wx_ref[g], preferred_element_type=F32))
    r = _sigmoid(jnp.concatenate(ra, axis=1) + ba_ref[...])
    ig = _sigmoid(jnp.concatenate(rx, axis=1) + bx_ref[...])
    nl = -lam_ref[...]
    softplus = jnp.maximum(nl, 0.0) + jnp.log1p(jnp.exp(-jnp.abs(nl)))
    a = jnp.exp((-LRU_C * softplus) * r)
    v = 1.0 - a * a
    b = jnp.where(v > 0.0, v * lax.rsqrt(v), 0.0) * (ig * xc)

    a = a.reshape(tm // 8, 8, cw)
    b = b.reshape(tm // 8, 8, cw)
    sub = lax.broadcasted_iota(I32, a.shape, 1)
    for d in (1, 2, 4):
        keep = sub >= d
        a_sh = pltpu.roll(a, d, axis=1)
        b_sh = pltpu.roll(b, d, axis=1)
        b = jnp.where(keep, a * b_sh + b, b)
        a = jnp.where(keep, a * a_sh, a)
    a_ref[...] = a.reshape(tm, cw)
    b_ref[...] = b.reshape(tm, cw)

    def body(g, h):
        r0 = pl.multiple_of(g * 8, 8)
        hh = b_ref[pl.ds(r0, 8), :] + a_ref[pl.ds(r0, 8), :] * h
        b_ref[pl.ds(r0, 8), :] = hh
        return hh[7:8, :]

    h_ref[...] = lax.fori_loop(0, tm // 8, body, h_ref[...], unroll=8)
    o_ref[...] = (b_ref[...] * _gelu_tanh(xg_ref[...])).astype(o_ref.dtype)


def _rglru(proj, batch, seq, xr_off, xg_off, conv_w, conv_b, wa, ba, wx, bx, lam, tm, cw):
    n = proj.shape[0]
    ncol = R_WIDTH // cw
    nt = seq // tm
    per = cw // R_BLOCK
    xr_cb = xr_off // cw
    xg_cb = xg_off // cw
    hb = tm // HALO
    colv = lambda shape: pl.BlockSpec(shape, lambda b, j, t: (0, j))
    return pl.pallas_call(
        functools.partial(_rglru_kernel, tm=tm, cw=cw),
        grid=(batch, ncol, nt),
        in_specs=[pl.BlockSpec((tm, cw), lambda b, j, t: (b * nt + t, xr_cb + j)),
                  pl.BlockSpec((HALO, cw), lambda b, j, t: (jnp.maximum((b * nt + t) * hb - 1, 0), xr_cb + j)),
                  pl.BlockSpec((tm, cw), lambda b, j, t: (b * nt + t, xg_cb + j)),
                  colv((CONV_WIDTH, cw)), colv((1, cw)),
                  pl.BlockSpec((per, R_BLOCK, R_BLOCK), lambda b, j, t: (j, 0, 0)), colv((1, cw)),
                  pl.BlockSpec((per, R_BLOCK, R_BLOCK), lambda b, j, t: (j, 0, 0)), colv((1, cw)),
                  colv((1, cw))],
        out_specs=pl.BlockSpec((tm, cw), lambda b, j, t: (b * nt + t, j)),
        out_shape=jax.ShapeDtypeStruct((n, R_WIDTH), BF16),
        scratch_shapes=[pltpu.VMEM((tm, cw), F32), pltpu.VMEM((tm, cw), F32), pltpu.VMEM((1, cw), F32)],
        compiler_params=_cparams(("parallel", "parallel", "arbitrary")),
        name="rglru",
    )(proj, proj, proj, conv_w, conv_b.reshape(1, R_WIDTH), wa.astype(BF16), ba.reshape(1, R_WIDTH),
      wx.astype(BF16), bx.reshape(1, R_WIDTH), lam.reshape(1, R_WIDTH))


def _merge_kernel(ym_ref, yr_ref, wpm_ref, wpr_ref, g0_ref, g1_ref, bg_ref, o_ref):
    g0 = jax.nn.sigmoid(g0_ref[...] + bg_ref[0:1, :])
    g1 = jax.nn.sigmoid(g1_ref[...] + bg_ref[1:2, :])
    pm = jnp.dot(ym_ref[...], wpm_ref[...], preferred_element_type=F32)
    pr = jnp.dot(yr_ref[...], wpr_ref[...], preferred_element_type=F32)
    o_ref[...] = (g0 * pm + g1 * pr).astype(o_ref.dtype)


def _merge(ym, yr, w_pm, w_pr, proj, gate_off, b_gate, tm, bn):
    n = ym.shape[0]
    d = w_pm.shape[1]
    g0_cb = gate_off // bn
    g1_cb = (gate_off + d) // bn
    return pl.pallas_call(
        _merge_kernel,
        grid=(d // bn, n // tm),
        in_specs=[pl.BlockSpec((tm, ym.shape[1]), lambda j, i: (i, 0)),
                  pl.BlockSpec((tm, yr.shape[1]), lambda j, i: (i, 0)),
                  pl.BlockSpec((w_pm.shape[0], bn), lambda j, i: (0, j)),
                  pl.BlockSpec((w_pr.shape[0], bn), lambda j, i: (0, j)),
                  pl.BlockSpec((tm, bn), lambda j, i: (i, g0_cb + j)),
                  pl.BlockSpec((tm, bn), lambda j, i: (i, g1_cb + j)),
                  pl.BlockSpec((2, bn), lambda j, i: (0, j))],
        out_specs=pl.BlockSpec((tm, bn), lambda j, i: (i, j)),
        out_shape=jax.ShapeDtypeStruct((n, d), BF16),
        compiler_params=_cparams(("parallel", "parallel")),
        name="merge",
    )(ym, yr, w_pm.astype(BF16), w_pr.astype(BF16), proj, proj, b_gate)


def _layer_norm(y, g, b):
    mu = jnp.mean(y, axis=1, keepdims=True)
    cen = y - mu
    var = jnp.mean(cen * cen, axis=1, keepdims=True)
    return cen * lax.rsqrt(var + LN_EPS) * g + b


def _split_hi_lo(x):
    hi = lax.bitcast_convert_type(lax.bitcast_convert_type(x, U32) & jnp.uint32(0xFFFF0000), F32)
    return hi.astype(BF16), (x - hi).astype(BF16)


def _oproj_kernel(mg_ref, wo_ref, x_ref, g_ref, b_ref, rwh_ref, rwl_ref, x1_ref, lg_ref):
    y = ALPHA * x_ref[...] + jnp.dot(mg_ref[...], wo_ref[...], preferred_element_type=F32)
    x1 = _layer_norm(y, g_ref[...], b_ref[...])
    x1_ref[...] = x1
    xh, xl = _split_hi_lo(x1)
    wh = rwh_ref[...]
    wl = rwl_ref[...]
    lg_ref[...] = ((jnp.dot(xh, wh, preferred_element_type=F32) + jnp.dot(xl, wl, preferred_element_type=F32))
                   + (jnp.dot(xl, wh, preferred_element_type=F32) + jnp.dot(xh, wl, preferred_element_type=F32)))


def _oproj(merged, w_o, x, ln_g, ln_b, router_w, tm):
    n, d = x.shape
    e = router_w.shape[1]
    rw_hi, rw_lo = _split_hi_lo(jnp.pad(router_w, ((0, 0), (0, ROUTER_LANES - e))))
    full = lambda shape: pl.BlockSpec(shape, lambda i: (0,) * len(shape))
    return pl.pallas_call(
        _oproj_kernel,
        grid=(n // tm,),
        in_specs=[pl.BlockSpec((tm, d), lambda i: (i, 0)), full((d, d)),
                  pl.BlockSpec((tm, d), lambda i: (i, 0)), full((1, d)), full((1, d)),
                  full((d, ROUTER_LANES)), full((d, ROUTER_LANES))],
        out_specs=[pl.BlockSpec((tm, d), lambda i: (i, 0)), pl.BlockSpec((tm, ROUTER_LANES), lambda i: (i, 0))],
        out_shape=[jax.ShapeDtypeStruct((n, d), F32), jax.ShapeDtypeStruct((n, ROUTER_LANES), F32)],
        compiler_params=_cparams(("parallel",)),
        name="out_proj_ln",
    )(merged, w_o.astype(BF16), x, ln_g.reshape(1, d), ln_b.reshape(1, d), rw_hi, rw_lo)


def _first_max(v, idx, sentinel):
    m = jnp.max(v, axis=0, keepdims=True)
    am = jnp.min(jnp.where(v == m, idx, sentinel), axis=0, keepdims=True)
    return m, am


def _route_kernel(lg_ref, bias_ref, tri_ref, te_ref, w_ref, rk_ref, cnt_ref, carry_ref, *, tm):
    @pl.when(pl.program_id(0) == 0)
    def _():
        carry_ref[...] = jnp.zeros_like(carry_ref)

    e = N_EXPERTS
    gs = e // N_GROUPS
    scores = jax.nn.sigmoid(jnp.transpose(lg_ref[...])[0:e, :])
    biased = scores + bias_ref[...]
    sub = lax.broadcasted_iota(I32, (gs, tm), 0)
    grp_rows = []
    for g in range(N_GROUPS):
        slab = biased[g * gs:(g + 1) * gs, :]
        m1, a1 = _first_max(slab, sub, gs)
        m2 = jnp.max(jnp.where(sub == a1, NEG_INF, slab), axis=0, keepdims=True)
        grp_rows.append(m1 + m2)
    grp = jnp.concatenate(grp_rows, axis=0)
    gidx = lax.broadcasted_iota(I32, (N_GROUPS, tm), 0)
    gsel = jnp.zeros((N_GROUPS, tm), F32)
    for _ in range(TOPK_GROUPS):
        _, am = _first_max(grp, gidx, N_GROUPS)
        hit = gidx == am
        gsel = jnp.where(hit, 1.0, gsel)
        grp = jnp.where(hit, NEG_INF, grp)
    masked = jnp.concatenate(
        [jnp.where(gsel[g:g + 1, :] > 0.0, biased[g * gs:(g + 1) * gs, :], NEG_INF) for g in range(N_GROUPS)],
        axis=0)
    eidx = lax.broadcasted_iota(I32, (e, tm), 0)
    member = jnp.zeros((e, tm), F32)
    tops = []
    ws = []
    for _ in range(TOP_K):
        _, am = _first_max(masked, eidx, e)
        hit = eidx == am
        tops.append(am)
        ws.append(jnp.sum(jnp.where(hit, scores, 0.0), axis=0, keepdims=True))
        member = jnp.where(hit, 1.0, member)
        masked = jnp.where(hit, NEG_INF, masked)
    wsum = ws[0]
    for k in range(1, TOP_K):
        wsum = wsum + ws[k]
    te_ref[...] = jnp.concatenate(tops, axis=0)
    w_ref[...] = jnp.concatenate(ws, axis=0) / wsum * ROUTED_SCALE

    cum = jnp.dot(member.astype(BF16), tri_ref[...], preferred_element_type=F32)
    carry = carry_ref[:, 0:1]
    rank = carry + cum - member
    rks = []
    for k in range(TOP_K):
        rks.append(jnp.sum(jnp.where(eidx == tops[k], rank, 0.0), axis=0, keepdims=True))
    rk_ref[...] = jnp.concatenate(rks, axis=0).astype(I32)
    new_carry = carry + cum[:, tm - 1:tm]
    carry_ref[...] = jnp.broadcast_to(new_carry, carry_ref.shape)
    cnt_ref[...] = jnp.broadcast_to(new_carry, cnt_ref.shape).astype(I32)


def _route(logits, router_bias, tm):
    n = logits.shape[0]
    e = router_bias.shape[0]
    tri = jnp.triu(jnp.ones((tm, tm), F32)).astype(BF16)
    return pl.pallas_call(
        functools.partial(_route_kernel, tm=tm),
        grid=(n // tm,),
        in_specs=[pl.BlockSpec((tm, ROUTER_LANES), lambda i: (i, 0)),
                  pl.BlockSpec((e, 1), lambda i: (0, 0)),
                  pl.BlockSpec((tm, tm), lambda i: (0, 0))],
        out_specs=[pl.BlockSpec((TOP_K, tm), lambda i: (0, i)),
                   pl.BlockSpec((TOP_K, tm), lambda i: (0, i)),
                   pl.BlockSpec((TOP_K, tm), lambda i: (0, i)),
                   pl.BlockSpec((e, 128), lambda i: (0, 0))],
        out_shape=[jax.ShapeDtypeStruct((TOP_K, n), I32), jax.ShapeDtypeStruct((TOP_K, n), F32),
                   jax.ShapeDtypeStruct((TOP_K, n), I32), jax.ShapeDtypeStruct((e, 128), I32)],
        scratch_shapes=[pltpu.VMEM((e, 128), F32)],
        compiler_params=_cparams(("arbitrary",)),
        name="route",
    )(logits, router_bias.reshape(e, 1), tri)


def _dest_kernel(te_ref, rk_ref, ps_ref, d_ref):
    te = te_ref[...]
    e = N_EXPERTS
    tm = te.shape[1]
    eidx = lax.broadcasted_iota(I32, (e, tm), 0)
    ps = ps_ref[...]
    rows = []
    for k in range(TOP_K):
        rows.append(jnp.sum(jnp.where(eidx == te[k:k + 1, :], ps, 0), axis=0, keepdims=True))
    d_ref[...] = jnp.concatenate(rows, axis=0) + rk_ref[...]


def _dest(top_e, rank, pad_starts, tm):
    n = top_e.shape[1]
    return pl.pallas_call(
        _dest_kernel,
        grid=(n // tm,),
        in_specs=[pl.BlockSpec((TOP_K, tm), lambda i: (0, i)),
                  pl.BlockSpec((TOP_K, tm), lambda i: (0, i)),
                  pl.BlockSpec((N_EXPERTS, 1), lambda i: (0, 0))],
        out_specs=pl.BlockSpec((TOP_K, tm), lambda i: (0, i)),
        out_shape=jax.ShapeDtypeStruct((TOP_K, n), I32),
        compiler_params=_cparams(("parallel",)),
        name="dest",
    )(top_e, rank, pad_starts.reshape(N_EXPERTS, 1))


def _invert_kernel(dest_ref, inv_ref):
    def body(j, c):
        inv_ref[dest_ref[j]] = j
        return c

    lax.fori_loop(0, dest_ref.shape[0], body, 0, unroll=32)


def _invert_permutation(dest_flat):
    m = dest_flat.shape[0]
    return pl.pallas_call(
        _invert_kernel,
        in_specs=[pl.BlockSpec(memory_space=pltpu.SMEM)],
        out_specs=pl.BlockSpec(memory_space=pltpu.SMEM),
        out_shape=jax.ShapeDtypeStruct((m,), I32),
        name="invert_perm",
    )(dest_flat)


def _expert_ffn(xb, wgb_ref, wub_ref, wdb_ref):
    hg = jnp.dot(xb, wgb_ref[...], preferred_element_type=F32)
    hu = jnp.dot(xb, wub_ref[...], preferred_element_type=F32)
    hh = (hg * jax.nn.sigmoid(hg)) * hu
    return jnp.dot(hh.astype(BF16), wdb_ref[...], preferred_element_type=F32)


def _expert_kernel(vt_ref, ve_ref, lo_ref, hi_ref, inv_ref, nxt_ref, e0_ref,
                   x_hbm, wg_hbm, wu_hbm, wd_hbm, out_hbm,
                   wgf_ref, wuf_ref, wdf_ref, wgb_ref, wub_ref, wdb_ref, xbuf, obuf, xb_ref, cur_ref,
                   gsem, ssem, wsem, *, n_tok, n_tiles):
    v = pl.program_id(0)
    rows = xbuf.shape[1]
    tile = vt_ref[v]
    lo = lo_ref[v]
    hi = hi_ref[v]
    nonempty = hi > lo
    first = jnp.logical_and(nonempty, lo == 0)
    slot = lax.rem(tile, GATHER_SLOTS)
    oslot = tile % 2
    ahead = GATHER_SLOTS - 1

    def gather_row(t, r):
        src_tile = jnp.minimum(t, n_tiles - 1)
        tok = inv_ref[src_tile * rows + r] & (n_tok - 1)
        s = lax.rem(t, GATHER_SLOTS)
        return pltpu.make_async_copy(x_hbm.at[pl.ds(tok, 1)], xbuf.at[s, pl.ds(r, 1)], gsem.at[s])

    def scatter_row(t, r):
        j = inv_ref[t * rows + r]
        s = t % 2
        return pltpu.make_async_copy(obuf.at[s, pl.ds(r, 1)], out_hbm.at[pl.ds(j, 1)], ssem.at[s])

    def wait_gathered_tile(s):
        pltpu.make_async_copy(x_hbm.at[pl.ds(0, rows)], xbuf.at[s], gsem.at[s]).wait()

    def wait_scattered_tile(s):
        pltpu.make_async_copy(obuf.at[s], out_hbm.at[pl.ds(0, rows)], ssem.at[s]).wait()

    def weight_copies(e):
        return (pltpu.make_async_copy(wg_hbm.at[e], wgf_ref, wsem.at[0]),
                pltpu.make_async_copy(wu_hbm.at[e], wuf_ref, wsem.at[1]),
                pltpu.make_async_copy(wd_hbm.at[e], wdf_ref, wsem.at[2]))

    @pl.when(v == 0)
    def _():
        cur_ref[0] = -1
        for c in weight_copies(e0_ref[0]):
            c.start()

    @pl.when(jnp.logical_and(nonempty, cur_ref[0] != ve_ref[v]))
    def _():
        e = ve_ref[v]
        for c in weight_copies(e):
            c.wait()
        wgb_ref[...] = wgf_ref[...].astype(BF16)
        wub_ref[...] = wuf_ref[...].astype(BF16)
        wdb_ref[...] = wdf_ref[...].astype(BF16)
        cur_ref[0] = e
        ne = nxt_ref[e]

        @pl.when(ne >= 0)
        def _():
            for c in weight_copies(ne):
                c.start()

    @pl.when(jnp.logical_and(first, tile == 0))
    def _():
        def issue(i, c):
            gather_row(i // rows, i % rows).start(priority=ROW_COPY_PRIORITY)
            return c

        lax.fori_loop(0, ahead * rows, issue, 0)

    @pl.when(jnp.logical_and(first, tile >= 2))
    def _():
        wait_scattered_tile(oslot)

    @pl.when(first)
    def _():
        wait_gathered_tile(slot)

    @pl.when(jnp.logical_and(first, tile == 0))
    def _():
        xb_ref[...] = xbuf[slot].astype(BF16)
        for r in range(rows):
            gather_row(tile + ahead, r).start(priority=ROW_COPY_PRIORITY)
        obuf[oslot] = _expert_ffn(xb_ref[...], wgb_ref, wub_ref, wdb_ref)

    @pl.when(jnp.logical_and(first, tile >= 1))
    def _():
        xb_ref[...] = xbuf[slot].astype(BF16)
        for r in range(rows):
            gather_row(tile + ahead, r).start(priority=ROW_COPY_PRIORITY)
            scatter_row(tile - 1, r).start(priority=ROW_COPY_PRIORITY)
        obuf[oslot] = _expert_ffn(xb_ref[...], wgb_ref, wub_ref, wdb_ref)

    @pl.when(jnp.logical_and(nonempty, lo > 0))
    def _():
        y = _expert_ffn(xbuf[slot].astype(BF16), wgb_ref, wub_ref, wdb_ref)
        row = lax.broadcasted_iota(I32, (rows, 1), 0)
        mine = jnp.logical_and(row >= lo, row < hi)
        obuf[oslot] = jnp.where(mine, y, obuf[oslot])

    @pl.when(v == pl.num_programs(0) - 1)
    def _():
        last = n_tiles - 1

        def issue(r, c):
            scatter_row(last, r).start(priority=ROW_COPY_PRIORITY)
            return c

        lax.fori_loop(0, rows, issue, 0)
        wait_scattered_tile((last - 1) % 2)
        wait_scattered_tile(last % 2)
        for t in range(last + 1, last + 1 + ahead):
            wait_gathered_tile(t % GATHER_SLOTS)


def _experts(x1, inv, visit_tile, visit_expert, visit_lo, visit_hi, next_expert, first_expert,
             w_gate, w_up, w_down):
    n_tok, d = x1.shape
    m = inv.shape[0]
    de = w_gate.shape[2]
    n_tiles = m // EXPERT_TILE
    any_spec = pl.BlockSpec(memory_space=pl.ANY)
    grid_spec = pltpu.PrefetchScalarGridSpec(
        num_scalar_prefetch=7,
        grid=(visit_tile.shape[0],),
        in_specs=[any_spec, any_spec, any_spec, any_spec],
        out_specs=any_spec,
        scratch_shapes=[pltpu.VMEM((d, de), F32), pltpu.VMEM((d, de), F32), pltpu.VMEM((de, d), F32),
                        pltpu.VMEM((d, de), BF16), pltpu.VMEM((d, de), BF16), pltpu.VMEM((de, d), BF16),
                        pltpu.VMEM((GATHER_SLOTS, EXPERT_TILE, d), F32), pltpu.VMEM((2, EXPERT_TILE, d), F32),
                        pltpu.VMEM((EXPERT_TILE, d), BF16), pltpu.SMEM((1,), I32),
                        pltpu.SemaphoreType.DMA((GATHER_SLOTS,)), pltpu.SemaphoreType.DMA((2,)),
                        pltpu.SemaphoreType.DMA((3,))],
    )
    return pl.pallas_call(
        functools.partial(_expert_kernel, n_tok=n_tok, n_tiles=n_tiles),
        grid_spec=grid_spec,
        out_shape=jax.ShapeDtypeStruct((m, d), F32),
        compiler_params=_cparams(("arbitrary",)),
        name="experts",
    )(visit_tile, visit_expert, visit_lo, visit_hi, inv, next_expert, first_expert, x1, w_gate, w_up, w_down)


def _combine_kernel(x1_ref, w_ref, y_ref, sg_ref, su_ref, sd_ref, g_ref, b_ref, o_ref):
    x1 = x1_ref[...]
    xb = x1.astype(BF16)
    hg = jnp.dot(xb, sg_ref[...], preferred_element_type=F32)
    hu = jnp.dot(xb, su_ref[...], preferred_element_type=F32)
    hh = (hg * jax.nn.sigmoid(hg)) * hu
    shared = jnp.dot(hh.astype(BF16), sd_ref[...], preferred_element_type=F32)
    w = w_ref[...]
    routed = y_ref[0] * w[:, 0:1]
    for k in range(1, TOP_K):
        routed = routed + y_ref[k] * w[:, k:k + 1]
    o_ref[...] = _layer_norm(ALPHA * x1 + (routed + shared), g_ref[...], b_ref[...])


def _combine(x1, w_tok, y8, s_gate, s_up, s_down, ln_g, ln_b, tm):
    n, d = x1.shape
    de = s_gate.shape[1]
    full = lambda shape: pl.BlockSpec(shape, lambda i: (0,) * len(shape))
    return pl.pallas_call(
        _combine_kernel,
        grid=(n // tm,),
        in_specs=[pl.BlockSpec((tm, d), lambda i: (i, 0)),
                  pl.BlockSpec((tm, TOP_K), lambda i: (i, 0)),
                  pl.BlockSpec((TOP_K, tm, d), lambda i: (0, i, 0)),
                  full((d, de)), full((d, de)), full((de, d)), full((1, d)), full((1, d))],
        out_specs=pl.BlockSpec((tm, d), lambda i: (i, 0)),
        out_shape=jax.ShapeDtypeStruct((n, d), F32),
        compiler_params=_cparams(("parallel",)),
        name="combine",
    )(x1, w_tok, y8, s_gate.astype(BF16), s_up.astype(BF16), s_down.astype(BF16),
      ln_g.reshape(1, d), ln_b.reshape(1, d))


def _visit_plan(counts, n_rows):
    e = counts.shape[0]
    n_tiles = n_rows // EXPERT_TILE
    ends = jnp.cumsum(counts)
    starts = ends - counts
    pos = jnp.sort(jnp.concatenate([jnp.arange(n_tiles, dtype=I32) * EXPERT_TILE, starts]))
    nxt = jnp.concatenate([pos[1:], jnp.full((1,), n_rows, I32)])
    tile = jnp.minimum(pos // EXPERT_TILE, n_tiles - 1)
    expert = jnp.minimum(jnp.sum((ends[None, :] <= pos[:, None]).astype(I32), axis=1), e - 1)
    ids = jnp.arange(e, dtype=I32)
    later = jnp.logical_and(ids[None, :] > ids[:, None], counts[None, :] > 0)
    next_expert = jnp.min(jnp.where(later, ids[None, :], e), axis=1)
    next_expert = jnp.where(next_expert < e, next_expert, -1).astype(I32)
    first_expert = jnp.min(jnp.where(counts > 0, ids, e - 1)).astype(I32).reshape(1)
    return starts, tile, expert, pos - tile * EXPERT_TILE, nxt - tile * EXPERT_TILE, next_expert, first_expert


def _moe(x1, logits, router_bias, e_w_gate, e_w_up, e_w_down, s_w_gate, s_w_up, s_w_down, ln_g, ln_b):
    n, d = x1.shape
    top_e, w_t, rank, counts = _route(logits, router_bias, tm=512)
    starts, v_tile, v_expert, v_lo, v_hi, next_e, first_e = _visit_plan(counts[:, 0], n * TOP_K)
    dest = _dest(top_e, rank, starts, tm=1024)
    inv = _invert_permutation(dest.reshape(n * TOP_K))
    y8 = _experts(x1, inv, v_tile, v_expert, v_lo, v_hi, next_e, first_e, e_w_gate, e_w_up, e_w_down)
    return _combine(x1, w_t.T, y8.reshape(TOP_K, n, d), s_w_gate, s_w_up, s_w_down, ln_g, ln_b, tm=128)


def _layer(x, w_in, b_gate, m_conv_w, m_conv_b, m_wq, m_wk, m_wv, m_w_if, m_b_if, m_norm_w, m_skip,
           r_conv_w, r_conv_b, r_wa, r_ba, r_wx, r_bx, r_lambda, w_pm, w_pr, w_o, ln1_g, ln1_b,
           router_w, router_bias, e_w_gate, e_w_up, e_w_down, s_w_gate, s_w_up, s_w_down, ln2_g, ln2_b):
    batch, seq, d = x.shape
    n = batch * seq
    xt = x.reshape(n, d)
    proj = _in_proj(xt.astype(BF16), w_in, bm=1024, bn=1024)
    o_xr = 2 * M_WIDTH
    o_xg = o_xr + R_WIDTH
    o_gate = o_xg + R_WIDTH
    xc, q, k, v, g, gt = _mlstm_prep(proj, seq, m_conv_w, m_conv_b, m_wq, m_wk, m_wv, m_w_if, m_b_if, tm=256)
    y_m = _mlstm(q, k, v, g, gt, xc, proj, m_norm_w, m_skip, batch, seq, MLSTM_CHUNK)
    y_r = _rglru(proj, batch, seq, o_xr, o_xg, r_conv_w, r_conv_b, r_wa, r_ba, r_wx, r_bx, r_lambda,
                 tm=512, cw=512)
    merged = _merge(y_m, y_r, w_pm, w_pr, proj, o_gate, b_gate, tm=512, bn=1024)
    x1, logits = _oproj(merged, w_o, xt, ln1_g, ln1_b, router_w, tm=256)
    out = _moe(x1, logits, router_bias, e_w_gate, e_w_up, e_w_down, s_w_gate, s_w_up, s_w_down, ln2_g, ln2_b)
    return out.reshape(batch, seq, d)


def kernel(x, w_in, b_gate, m_conv_w, m_conv_b, m_wq, m_wk, m_wv, m_w_if, m_b_if, m_norm_w, m_skip, r_conv_w, r_conv_b, r_wa, r_ba, r_wx, r_bx, r_lambda, w_pm, w_pr, w_o, ln1_g, ln1_b, router_w, router_bias, e_w_gate, e_w_up, e_w_down, s_w_gate, s_w_up, s_w_down, ln2_g, ln2_b):
    for l in range(DEPTH):
        x = _layer(x, w_in[l], b_gate[l], m_conv_w[l], m_conv_b[l], m_wq[l], m_wk[l], m_wv[l],
                   m_w_if[l], m_b_if[l], m_norm_w[l], m_skip[l], r_conv_w[l], r_conv_b[l],
                   r_wa[l], r_ba[l], r_wx[l], r_bx[l], r_lambda[l], w_pm[l], w_pr[l], w_o[l],
                   ln1_g[l], ln1_b[l], router_w[l], router_bias[l], e_w_gate[l], e_w_up[l],
                   e_w_down[l], s_w_gate[l], s_w_up[l], s_w_down[l], ln2_g[l], ln2_b[l])
    return x
```

```python
import functools
import math

import jax
import jax.numpy as jnp
from jax import lax
from jax.experimental import pallas as pl
from jax.experimental.pallas import tpu as pltpu

F32 = jnp.float32
BF16 = jnp.bfloat16
I32 = jnp.int32
U32 = jnp.uint32

D_MODEL = 2048
M_WIDTH = 2048
M_HEADS = 8
M_HEAD_DIM = 256
M_QKV_BLOCK = 4
CONV_WIDTH = 4
R_WIDTH = 2560
R_BLOCK = 256
LRU_C = 8.0
N_EXPERTS = 64
TOP_K = 8
N_GROUPS = 8
TOPK_GROUPS = 4
D_EXPERT = 512
ROUTED_SCALE = 2.5
DEPTH = 1
ALPHA = (2.0 * DEPTH) ** 0.25
LN_EPS = 1e-5

V7X_VMEM_LIMIT = 56 * 1024 * 1024
HALO = 8
MLSTM_CHUNK = 256
EXPERT_TILE = 256
NEG_INF = float("-inf")
ROUTER_LANES = 128
GATHER_SLOTS = 3
COMBINE_PHASES = 4
ROW_COPY_PRIORITY = 1


def _cparams(sem, vmem=V7X_VMEM_LIMIT):
    return pltpu.CompilerParams(dimension_semantics=sem, vmem_limit_bytes=vmem)


def _inproj_kernel(a_ref, w_ref, o_ref, wb_ref):
    @pl.when(pl.program_id(1) == 0)
    def _():
        wb_ref[...] = w_ref[...].astype(BF16)

    o_ref[...] = jnp.dot(a_ref[...], wb_ref[...], preferred_element_type=F32)


def _in_proj(a, w, bm, bn):
    m, k = a.shape
    n = w.shape[1]
    return pl.pallas_call(
        _inproj_kernel,
        grid=(n // bn, m // bm),
        in_specs=[pl.BlockSpec((bm, k), lambda j, i: (i, 0)),
                  pl.BlockSpec((k, bn), lambda j, i: (0, j))],
        out_specs=pl.BlockSpec((bm, bn), lambda j, i: (i, j)),
        out_shape=jax.ShapeDtypeStruct((m, n), F32),
        scratch_shapes=[pltpu.VMEM((k, bn), BF16)],
        compiler_params=_cparams(("parallel", "arbitrary")),
        name="in_proj",
    )(a, w)


def _sigmoid(x):
    return 0.5 * jnp.tanh(0.5 * x) + 0.5


def _log_sigmoid(x):
    return jnp.minimum(x, 0.0) - jnp.log1p(jnp.exp(-jnp.abs(x)))


def _shift_rows(x3, prev_group, j):
    rot = pltpu.roll(x3, j, axis=1)
    prev = jnp.concatenate([pltpu.roll(prev_group, j, axis=1), rot[:-1]], axis=0)
    sub = lax.broadcasted_iota(I32, x3.shape, 1)
    return jnp.where(sub >= j, rot, prev)


def _causal_conv(halo, x, cw_ref, cb_ref):
    tm, c = x.shape
    x3 = x.reshape(tm // HALO, HALO, c)
    h3 = halo.reshape(1, HALO, c)
    last = CONV_WIDTH - 1
    y = cb_ref[...] + x3 * cw_ref[last:last + 1, :]
    for j in range(1, CONV_WIDTH):
        y = y + _shift_rows(x3, h3, j) * cw_ref[last - j:last - j + 1, :]
    return y


def _mprep_kernel(xm_ref, halo_ref, cw_ref, cb_ref, wq_ref, wk_ref, wv_ref, wif_ref, wift_ref,
                  bif_ref, bift_ref, xc_ref, q_ref, k_ref, v_ref, g_ref, gt_ref,
                  *, tm, tiles_per_seq):
    i = pl.program_id(0)
    first = (i % tiles_per_seq) == 0
    halo = jnp.where(first, 0.0, halo_ref[...])
    xm = xm_ref[...]
    y = _causal_conv(halo, xm, cw_ref, cb_ref).reshape(tm, M_WIDTH)
    xc = y * _sigmoid(y)
    xc_ref[...] = xc
    xcb = xc.astype(BF16)
    xmb = xm.astype(BF16)
    nblk = M_WIDTH // M_HEAD_DIM
    for g in range(nblk):
        sl = slice(g * M_HEAD_DIM, (g + 1) * M_HEAD_DIM)
        q_ref[:, sl] = jnp.dot(xcb[:, sl], wq_ref[g], preferred_element_type=F32).astype(BF16)
        k_ref[:, sl] = jnp.dot(xcb[:, sl], wk_ref[g], preferred_element_type=F32).astype(BF16)
        v_ref[:, sl] = jnp.dot(xmb[:, sl], wv_ref[g], preferred_element_type=F32).astype(BF16)
    qb, kb, vb = q_ref[...], k_ref[...], v_ref[...]
    w = M_WIDTH
    g = (jnp.dot(qb, wif_ref[0:w, :], preferred_element_type=F32)
         + jnp.dot(kb, wif_ref[w:2 * w, :], preferred_element_type=F32)
         + jnp.dot(vb, wif_ref[2 * w:3 * w, :], preferred_element_type=F32) + bif_ref[...])
    nt = (((1,), (1,)), ((), ()))
    gt = (lax.dot_general(wift_ref[:, 0:w], qb, nt, preferred_element_type=F32)
          + lax.dot_general(wift_ref[:, w:2 * w], kb, nt, preferred_element_type=F32)
          + lax.dot_general(wift_ref[:, 2 * w:3 * w], vb, nt, preferred_element_type=F32) + bift_ref[...])
    col = lax.broadcasted_iota(I32, g.shape, 1)
    g_ref[...] = jnp.where(col >= M_HEADS, _log_sigmoid(g), g)
    row = lax.broadcasted_iota(I32, gt.shape, 0)
    gt_ref[...] = jnp.where(row >= M_HEADS, _log_sigmoid(gt), gt)


def _block_diag_kernel(w_ref, o_ref, *, bi):
    w = w_ref[0]
    group = o_ref.shape[1]
    row = lax.broadcasted_iota(I32, (group, group), 0)
    col = lax.broadcasted_iota(I32, (group, group), 1)
    dense = jnp.zeros((group, group), F32)
    for o in range(w.shape[1]):
        dense = jnp.where(col % w.shape[1] == o, w[:, o:o + 1], dense)
    o_ref[0] = jnp.where(row // bi == col // w.shape[1], dense, 0.0).astype(o_ref.dtype)


def _block_diag_dense(ws, group):
    nb, bi, bo = ws[0].shape
    slabs = len(ws) * nb * bi // group
    w3 = jnp.concatenate(ws, axis=0).reshape(slabs, group, bo)
    return pl.pallas_call(
        functools.partial(_block_diag_kernel, bi=bi),
        grid=(slabs,),
        in_specs=[pl.BlockSpec((1, group, bo), lambda i: (i, 0, 0))],
        out_specs=pl.BlockSpec((1, group, group), lambda i: (i, 0, 0)),
        out_shape=jax.ShapeDtypeStruct((slabs, group, group), BF16),
        compiler_params=_cparams(("parallel",)),
        name="block_diag",
    )(w3)


def _mlstm_prep(proj, seq, conv_w, conv_b, wq, wk, wv, w_if, b_if, tm):
    n = proj.shape[0]
    c = M_WIDTH
    nblk = c // M_HEAD_DIM
    tiles_per_seq = seq // tm
    wd = _block_diag_dense([wq, wk, wv], M_HEAD_DIM)
    wif = w_if.astype(BF16)
    wift = w_if.T.astype(BF16)
    ng = 2 * M_HEADS
    hb = tm // HALO
    full = lambda shape: pl.BlockSpec(shape, lambda i: (0,) * len(shape))
    return pl.pallas_call(
        functools.partial(_mprep_kernel, tm=tm, tiles_per_seq=tiles_per_seq),
        grid=(n // tm,),
        in_specs=[pl.BlockSpec((tm, c), lambda i: (i, 0)),
                  pl.BlockSpec((HALO, c), lambda i: (jnp.maximum(i * hb - 1, 0), 0)),
                  full((CONV_WIDTH, c)), full((1, c)),
                  pl.BlockSpec((nblk, M_HEAD_DIM, M_HEAD_DIM), lambda i: (0, 0, 0)),
                  pl.BlockSpec((nblk, M_HEAD_DIM, M_HEAD_DIM), lambda i: (1, 0, 0)),
                  pl.BlockSpec((nblk, M_HEAD_DIM, M_HEAD_DIM), lambda i: (2, 0, 0)),
                  full((3 * c, ng)), full((ng, 3 * c)), full((1, ng)), full((ng, 1))],
        out_specs=[pl.BlockSpec((tm, c), lambda i: (i, 0)),
                   pl.BlockSpec((tm, c), lambda i: (i, 0)),
                   pl.BlockSpec((tm, c), lambda i: (i, 0)),
                   pl.BlockSpec((tm, c), lambda i: (i, 0)),
                   pl.BlockSpec((tm, ng), lambda i: (i, 0)),
                   pl.BlockSpec((ng, tm), lambda i: (0, i))],
        out_shape=[jax.ShapeDtypeStruct((n, c), F32),
                   jax.ShapeDtypeStruct((n, c), BF16),
                   jax.ShapeDtypeStruct((n, c), BF16),
                   jax.ShapeDtypeStruct((n, c), BF16),
                   jax.ShapeDtypeStruct((n, ng), F32),
                   jax.ShapeDtypeStruct((ng, n), F32)],
        compiler_params=_cparams(("parallel",)),
        name="mlstm_prep",
    )(proj, proj, conv_w, conv_b.reshape(1, c), wd, wd, wd, wif, wift,
      b_if.reshape(1, ng), b_if.reshape(ng, 1))


def _mlstm_kernel(q_ref, k_ref, v_ref, g_ref, gt_ref, xc_ref, z_ref, nw_ref, sk_ref, o_ref,
                  c_ref, n_ref, m_ref, *, chunk):
    L = chunk
    hd = M_HEAD_DIM

    @pl.when(pl.program_id(1) == 0)
    def _():
        c_ref[...] = jnp.zeros_like(c_ref)
        n_ref[...] = jnp.zeros_like(n_ref)
        m_ref[...] = jnp.zeros_like(m_ref)

    rows = lax.broadcasted_iota(I32, (L, L), 0)
    cols = lax.broadcasted_iota(I32, (L, L), 1)
    causal = cols <= rows
    tril = jnp.where(causal, 1.0, 0.0).astype(F32)
    triu = jnp.where(rows <= cols, 1.0, 0.0).astype(F32)
    g = g_ref[...]
    gt = gt_ref[...]
    hi = lax.Precision.HIGHEST
    bcol_all = jnp.dot(tril, g, precision=hi, preferred_element_type=F32)
    brow_all = jnp.dot(gt, triu, precision=hi, preferred_element_type=F32)
    k_scale = hd ** -0.5
    nt = (((1,), (1,)), ((), ()))
    tn = (((0,), (0,)), ((), ()))

    for h in range(M_HEADS):
        sl = slice(h * hd, (h + 1) * hd)
        qh = q_ref[:, sl]
        kh = k_ref[:, sl]
        vh = v_ref[:, sl]
        i_col = g[:, h:h + 1]
        b_col = bcol_all[:, M_HEADS + h:M_HEADS + h + 1]
        i_row = gt[h:h + 1, :]
        b_row = brow_all[M_HEADS + h:M_HEADS + h + 1, :]
        m_prev = m_ref[h, 0:1, 0:1]
        c_prev = c_ref[h]
        n_prev = n_ref[h]

        dmat = jnp.where(causal, b_col - b_row + i_row, NEG_INF)
        a_col = b_col + m_prev
        m_row = jnp.maximum(a_col, jnp.max(dmat, axis=1, keepdims=True))
        s = lax.dot_general(qh, kh, nt, preferred_element_type=F32) * k_scale
        s = s * jnp.exp(dmat - m_row)
        inter = jnp.exp(a_col - m_row)
        num = inter * jnp.dot(qh, c_prev.astype(BF16), preferred_element_type=F32) \
            + jnp.dot(s.astype(BF16), vh, preferred_element_type=F32)
        qn = jnp.sum(qh.astype(F32) * n_prev, axis=1, keepdims=True)
        den = inter * qn + jnp.sum(s, axis=1, keepdims=True)
        hval = num * (1.0 / jnp.maximum(jnp.abs(den), jnp.exp(-m_row)))

        mu = jnp.mean(hval, axis=1, keepdims=True)
        cen = hval - mu
        var = jnp.mean(cen * cen, axis=1, keepdims=True)
        hn = cen * lax.rsqrt(var + LN_EPS) * nw_ref[:, sl]
        zz = z_ref[:, sl]
        o_ref[:, sl] = ((hn + sk_ref[:, sl] * xc_ref[:, sl]) * (zz * _sigmoid(zz))).astype(o_ref.dtype)

        b_last = b_col[L - 1:L, :]
        w_log = b_last - b_col + i_col
        m_new = jnp.maximum(b_last + m_prev, jnp.max(w_log, axis=0, keepdims=True))
        decay = jnp.exp(b_last + m_prev - m_new)
        kw = kh.astype(F32) * (jnp.exp(w_log - m_new) * k_scale)
        c_ref[h] = decay * c_prev + lax.dot_general(kw.astype(BF16), vh, tn, preferred_element_type=F32)
        n_ref[h] = decay * n_prev + jnp.sum(kw, axis=0, keepdims=True)
        m_ref[h] = jnp.broadcast_to(m_new, m_ref.shape[1:])


def _mlstm(q, k, v, g, gt, xc, proj, norm_w, skip, batch, seq, chunk):
    n, c = q.shape
    nc = seq // chunk
    ng = 2 * M_HEADS
    zcol = M_WIDTH // c
    row = lambda b, j: (b * nc + j, 0)
    return pl.pallas_call(
        functools.partial(_mlstm_kernel, chunk=chunk),
        grid=(batch, nc),
        in_specs=[pl.BlockSpec((chunk, c), row), pl.BlockSpec((chunk, c), row), pl.BlockSpec((chunk, c), row),
                  pl.BlockSpec((chunk, ng), row),
                  pl.BlockSpec((ng, chunk), lambda b, j: (0, b * nc + j)),
                  pl.BlockSpec((chunk, c), row),
                  pl.BlockSpec((chunk, c), lambda b, j: (b * nc + j, zcol)),
                  pl.BlockSpec((1, c), lambda b, j: (0, 0)),
                  pl.BlockSpec((1, c), lambda b, j: (0, 0))],
        out_specs=pl.BlockSpec((chunk, c), row),
        out_shape=jax.ShapeDtypeStruct((n, c), BF16),
        scratch_shapes=[pltpu.VMEM((M_HEADS, M_HEAD_DIM, M_HEAD_DIM), F32),
                        pltpu.VMEM((M_HEADS, 1, M_HEAD_DIM), F32),
                        pltpu.VMEM((M_HEADS, 8, 128), F32)],
        compiler_params=_cparams(("parallel", "arbitrary")),
        name="mlstm_chunk",
    )(q, k, v, g, gt, xc, proj, norm_w.reshape(1, c), skip.reshape(1, c))


def _gelu_tanh(x):
    return 0.5 * x * (1.0 + jnp.tanh(math.sqrt(2.0 / math.pi) * (x + 0.044715 * (x * x * x))))


def _rglru_kernel(xr_ref, halo_ref, xg_ref, cw_ref, cb_ref, wa_ref, ba_ref, wx_ref, bx_ref, lam_ref,
                  o_ref, a_ref, b_ref, h_ref, *, tm, cw):
    t = pl.program_id(2)

    @pl.when(t == 0)
    def _():
        h_ref[...] = jnp.zeros_like(h_ref)

    halo = jnp.where(t == 0, 0.0, halo_ref[...])
    xc = _causal_conv(halo, xr_ref[...], cw_ref, cb_ref).reshape(tm, cw)
    xcb = xc.astype(BF16)
    nblk = cw // R_BLOCK
    ra = []
    rx = []
    for g in range(nblk):
        sl = slice(g * R_BLOCK, (g + 1) * R_BLOCK)
        ra.append(jnp.dot(xcb[:, sl], wa_ref[g], preferred_element_type=F32))
        rx.append(jnp.dot(xcb[:, sl], wx_ref[g], preferred_element_type=F32))
    r = _sigmoid(jnp.concatenate(ra, axis=1) + ba_ref[...])
    ig = _sigmoid(jnp.concatenate(rx, axis=1) + bx_ref[...])
    nl = -lam_ref[...]
    softplus = jnp.maximum(nl, 0.0) + jnp.log1p(jnp.exp(-jnp.abs(nl)))
    a = jnp.exp((-LRU_C * softplus) * r)
    v = 1.0 - a * a
    b = jnp.where(v > 0.0, v * lax.rsqrt(v), 0.0) * (ig * xc)

    a = a.reshape(tm // 8, 8, cw)
    b = b.reshape(tm // 8, 8, cw)
    sub = lax.broadcasted_iota(I32, a.shape, 1)
    for d in (1, 2, 4):
        keep = sub >= d
        a_sh = pltpu.roll(a, d, axis=1)
        b_sh = pltpu.roll(b, d, axis=1)
        b = jnp.where(keep, a * b_sh + b, b)
        a = jnp.where(keep, a * a_sh, a)
    a_ref[...] = a.reshape(tm, cw)
    b_ref[...] = b.reshape(tm, cw)

    def body(g, h):
        r0 = pl.multiple_of(g * 8, 8)
        hh = b_ref[pl.ds(r0, 8), :] + a_ref[pl.ds(r0, 8), :] * h
        b_ref[pl.ds(r0, 8), :] = hh
        return hh[7:8, :]

    h_ref[...] = lax.fori_loop(0, tm // 8, body, h_ref[...], unroll=8)
    o_ref[...] = (b_ref[...] * _gelu_tanh(xg_ref[...])).astype(o_ref.dtype)


def _rglru(proj, batch, seq, xr_off, xg_off, conv_w, conv_b, wa, ba, wx, bx, lam, tm, cw):
    n = proj.shape[0]
    ncol = R_WIDTH // cw
    nt = seq // tm
    per = cw // R_BLOCK
    xr_cb = xr_off // cw
    xg_cb = xg_off // cw
    hb = tm // HALO
    colv = lambda shape: pl.BlockSpec(shape, lambda b, j, t: (0, j))
    return pl.pallas_call(
        functools.partial(_rglru_kernel, tm=tm, cw=cw),
        grid=(batch, ncol, nt),
        in_specs=[pl.BlockSpec((tm, cw), lambda b, j, t: (b * nt + t, xr_cb + j)),
                  pl.BlockSpec((HALO, cw), lambda b, j, t: (jnp.maximum((b * nt + t) * hb - 1, 0), xr_cb + j)),
                  pl.BlockSpec((tm, cw), lambda b, j, t: (b * nt + t, xg_cb + j)),
                  colv((CONV_WIDTH, cw)), colv((1, cw)),
                  pl.BlockSpec((per, R_BLOCK, R_BLOCK), lambda b, j, t: (j, 0, 0)), colv((1, cw)),
                  pl.BlockSpec((per, R_BLOCK, R_BLOCK), lambda b, j, t: (j, 0, 0)), colv((1, cw)),
                  colv((1, cw))],
        out_specs=pl.BlockSpec((tm, cw), lambda b, j, t: (b * nt + t, j)),
        out_shape=jax.ShapeDtypeStruct((n, R_WIDTH), BF16),
        scratch_shapes=[pltpu.VMEM((tm, cw), F32), pltpu.VMEM((tm, cw), F32), pltpu.VMEM((1, cw), F32)],
        compiler_params=_cparams(("parallel", "parallel", "arbitrary")),
        name="rglru",
    )(proj, proj, proj, conv_w, conv_b.reshape(1, R_WIDTH), wa.astype(BF16), ba.reshape(1, R_WIDTH),
      wx.astype(BF16), bx.reshape(1, R_WIDTH), lam.reshape(1, R_WIDTH))


def _merge_kernel(ym_ref, yr_ref, wpm_ref, wpr_ref, g0_ref, g1_ref, bg_ref, o_ref):
    g0 = jax.nn.sigmoid(g0_ref[...] + bg_ref[0:1, :])
    g1 = jax.nn.sigmoid(g1_ref[...] + bg_ref[1:2, :])
    pm = jnp.dot(ym_ref[...], wpm_ref[...], preferred_element_type=F32)
    pr = jnp.dot(yr_ref[...], wpr_ref[...], preferred_element_type=F32)
    o_ref[...] = (g0 * pm + g1 * pr).astype(o_ref.dtype)


def _merge(ym, yr, w_pm, w_pr, proj, gate_off, b_gate, tm, bn):
    n = ym.shape[0]
    d = w_pm.shape[1]
    g0_cb = gate_off // bn
    g1_cb = (gate_off + d) // bn
    return pl.pallas_call(
        _merge_kernel,
        grid=(d // bn, n // tm),
        in_specs=[pl.BlockSpec((tm, ym.shape[1]), lambda j, i: (i, 0)),
                  pl.BlockSpec((tm, yr.shape[1]), lambda j, i: (i, 0)),
                  pl.BlockSpec((w_pm.shape[0], bn), lambda j, i: (0, j)),
                  pl.BlockSpec((w_pr.shape[0], bn), lambda j, i: (0, j)),
                  pl.BlockSpec((tm, bn), lambda j, i: (i, g0_cb + j)),
                  pl.BlockSpec((tm, bn), lambda j, i: (i, g1_cb + j)),
                  pl.BlockSpec((2, bn), lambda j, i: (0, j))],
        out_specs=pl.BlockSpec((tm, bn), lambda j, i: (i, j)),
        out_shape=jax.ShapeDtypeStruct((n, d), BF16),
        compiler_params=_cparams(("parallel", "parallel")),
        name="merge",
    )(ym, yr, w_pm.astype(BF16), w_pr.astype(BF16), proj, proj, b_gate)


def _layer_norm(y, g, b):
    mu = jnp.mean(y, axis=1, keepdims=True)
    cen = y - mu
    var = jnp.mean(cen * cen, axis=1, keepdims=True)
    return cen * lax.rsqrt(var + LN_EPS) * g + b


def _split_hi_lo(x):
    hi = lax.bitcast_convert_type(lax.bitcast_convert_type(x, U32) & jnp.uint32(0xFFFF0000), F32)
    return hi.astype(BF16), (x - hi).astype(BF16)


def _oproj_kernel(mg_ref, wo_ref, x_ref, g_ref, b_ref, rwh_ref, rwl_ref, x1_ref, lg_ref):
    y = ALPHA * x_ref[...] + jnp.dot(mg_ref[...], wo_ref[...], preferred_element_type=F32)
    x1 = _layer_norm(y, g_ref[...], b_ref[...])
    x1_ref[...] = x1
    xh, xl = _split_hi_lo(x1)
    wh = rwh_ref[...]
    wl = rwl_ref[...]
    lg_ref[...] = ((jnp.dot(xh, wh, preferred_element_type=F32) + jnp.dot(xl, wl, preferred_element_type=F32))
                   + (jnp.dot(xl, wh, preferred_element_type=F32) + jnp.dot(xh, wl, preferred_element_type=F32)))


def _oproj(merged, w_o, x, ln_g, ln_b, router_w, tm):
    n, d = x.shape
    e = router_w.shape[1]
    rw_hi, rw_lo = _split_hi_lo(jnp.pad(router_w, ((0, 0), (0, ROUTER_LANES - e))))
    full = lambda shape: pl.BlockSpec(shape, lambda i: (0,) * len(shape))
    return pl.pallas_call(
        _oproj_kernel,
        grid=(n // tm,),
        in_specs=[pl.BlockSpec((tm, d), lambda i: (i, 0)), full((d, d)),
                  pl.BlockSpec((tm, d), lambda i: (i, 0)), full((1, d)), full((1, d)),
                  full((d, ROUTER_LANES)), full((d, ROUTER_LANES))],
        out_specs=[pl.BlockSpec((tm, d), lambda i: (i, 0)), pl.BlockSpec((tm, ROUTER_LANES), lambda i: (i, 0))],
        out_shape=[jax.ShapeDtypeStruct((n, d), F32), jax.ShapeDtypeStruct((n, ROUTER_LANES), F32)],
        compiler_params=_cparams(("parallel",)),
        name="out_proj_ln",
    )(merged, w_o.astype(BF16), x, ln_g.reshape(1, d), ln_b.reshape(1, d), rw_hi, rw_lo)


def _first_max(v, idx, sentinel):
    m = jnp.max(v, axis=0, keepdims=True)
    am = jnp.min(jnp.where(v == m, idx, sentinel), axis=0, keepdims=True)
    return m, am


def _route_kernel(lg_ref, bias_ref, tri_ref, te_ref, w_ref, rk_ref, cnt_ref, carry_ref, *, tm):
    @pl.when(pl.program_id(0) == 0)
    def _():
        carry_ref[...] = jnp.zeros_like(carry_ref)

    e = N_EXPERTS
    gs = e // N_GROUPS
    scores = jax.nn.sigmoid(jnp.transpose(lg_ref[...])[0:e, :])
    biased = scores + bias_ref[...]
    sub = lax.broadcasted_iota(I32, (gs, tm), 0)
    grp_rows = []
    for g in range(N_GROUPS):
        slab = biased[g * gs:(g + 1) * gs, :]
        m1, a1 = _first_max(slab, sub, gs)
        m2 = jnp.max(jnp.where(sub == a1, NEG_INF, slab), axis=0, keepdims=True)
        grp_rows.append(m1 + m2)
    grp = jnp.concatenate(grp_rows, axis=0)
    gidx = lax.broadcasted_iota(I32, (N_GROUPS, tm), 0)
    gsel = jnp.zeros((N_GROUPS, tm), F32)
    for _ in range(TOPK_GROUPS):
        _, am = _first_max(grp, gidx, N_GROUPS)
        hit = gidx == am
        gsel = jnp.where(hit, 1.0, gsel)
        grp = jnp.where(hit, NEG_INF, grp)
    masked = jnp.concatenate(
        [jnp.where(gsel[g:g + 1, :] > 0.0, biased[g * gs:(g + 1) * gs, :], NEG_INF) for g in range(N_GROUPS)],
        axis=0)
    eidx = lax.broadcasted_iota(I32, (e, tm), 0)
    member = jnp.zeros((e, tm), F32)
    tops = []
    ws = []
    for _ in range(TOP_K):
        _, am = _first_max(masked, eidx, e)
        hit = eidx == am
        tops.append(am)
        ws.append(jnp.sum(jnp.where(hit, scores, 0.0), axis=0, keepdims=True))
        member = jnp.where(hit, 1.0, member)
        masked = jnp.where(hit, NEG_INF, masked)
    wsum = ws[0]
    for k in range(1, TOP_K):
        wsum = wsum + ws[k]
    te_ref[...] = jnp.concatenate(tops, axis=0)
    w_ref[...] = jnp.concatenate(ws, axis=0) / wsum * ROUTED_SCALE

    cum = jnp.dot(member.astype(BF16), tri_ref[...], preferred_element_type=F32)
    carry = carry_ref[:, 0:1]
    rank = carry + cum - member
    rks = []
    for k in range(TOP_K):
        rks.append(jnp.sum(jnp.where(eidx == tops[k], rank, 0.0), axis=0, keepdims=True))
    rk_ref[...] = jnp.concatenate(rks, axis=0).astype(I32)
    new_carry = carry + cum[:, tm - 1:tm]
    carry_ref[...] = jnp.broadcast_to(new_carry, carry_ref.shape)
    cnt_ref[...] = jnp.broadcast_to(new_carry, cnt_ref.shape).astype(I32)


def _route(logits, router_bias, tm):
    n = logits.shape[0]
    e = router_bias.shape[0]
    tri = jnp.triu(jnp.ones((tm, tm), F32)).astype(BF16)
    return pl.pallas_call(
        functools.partial(_route_kernel, tm=tm),
        grid=(n // tm,),
        in_specs=[pl.BlockSpec((tm, ROUTER_LANES), lambda i: (i, 0)),
                  pl.BlockSpec((e, 1), lambda i: (0, 0)),
                  pl.BlockSpec((tm, tm), lambda i: (0, 0))],
        out_specs=[pl.BlockSpec((TOP_K, tm), lambda i: (0, i)),
                   pl.BlockSpec((TOP_K, tm), lambda i: (0, i)),
                   pl.BlockSpec((TOP_K, tm), lambda i: (0, i)),
                   pl.BlockSpec((e, 128), lambda i: (0, 0))],
        out_shape=[jax.ShapeDtypeStruct((TOP_K, n), I32), jax.ShapeDtypeStruct((TOP_K, n), F32),
                   jax.ShapeDtypeStruct((TOP_K, n), I32), jax.ShapeDtypeStruct((e, 128), I32)],
        scratch_shapes=[pltpu.VMEM((e, 128), F32)],
        compiler_params=_cparams(("arbitrary",)),
        name="route",
    )(logits, router_bias.reshape(e, 1), tri)


def _dest_kernel(te_ref, rk_ref, ps_ref, d_ref):
    te = te_ref[...]
    e = N_EXPERTS
    tm = te.shape[1]
    eidx = lax.broadcasted_iota(I32, (e, tm), 0)
    ps = ps_ref[...]
    rows = []
    for k in range(TOP_K):
        rows.append(jnp.sum(jnp.where(eidx == te[k:k + 1, :], ps, 0), axis=0, keepdims=True))
    d_ref[...] = jnp.concatenate(rows, axis=0) + rk_ref[...]


def _dest(top_e, rank, pad_starts, tm):
    n = top_e.shape[1]
    return pl.pallas_call(
        _dest_kernel,
        grid=(n // tm,),
        in_specs=[pl.BlockSpec((TOP_K, tm), lambda i: (0, i)),
                  pl.BlockSpec((TOP_K, tm), lambda i: (0, i)),
                  pl.BlockSpec((N_EXPERTS, 1), lambda i: (0, 0))],
        out_specs=pl.BlockSpec((TOP_K, tm), lambda i: (0, i)),
        out_shape=jax.ShapeDtypeStruct((TOP_K, n), I32),
        compiler_params=_cparams(("parallel",)),
        name="dest",
    )(top_e, rank, pad_starts.reshape(N_EXPERTS, 1))


def _invert_kernel(dest_ref, inv_ref):
    def body(j, c):
        inv_ref[dest_ref[j]] = j
        return c

    lax.fori_loop(0, dest_ref.shape[0], body, 0, unroll=32)


def _invert_permutation(dest_flat):
    m = dest_flat.shape[0]
    return pl.pallas_call(
        _invert_kernel,
        in_specs=[pl.BlockSpec(memory_space=pltpu.SMEM)],
        out_specs=pl.BlockSpec(memory_space=pltpu.SMEM),
        out_shape=jax.ShapeDtypeStruct((m,), I32),
        name="invert_perm",
    )(dest_flat)


def _expert_ffn(xb, wgb_ref, wub_ref, wdb_ref):
    hg = jnp.dot(xb, wgb_ref[...], preferred_element_type=F32)
    hu = jnp.dot(xb, wub_ref[...], preferred_element_type=F32)
    hh = (hg * jax.nn.sigmoid(hg)) * hu
    return jnp.dot(hh.astype(BF16), wdb_ref[...], preferred_element_type=F32)


def _expert_kernel(vt_ref, ve_ref, lo_ref, hi_ref, inv_ref, nxt_ref, e0_ref,
                   x_hbm, wg_hbm, wu_hbm, wd_hbm, out_hbm,
                   wgf_ref, wuf_ref, wdf_ref, wgb_ref, wub_ref, wdb_ref, xbuf, obuf, xb_ref, cur_ref,
                   gsem, ssem, wsem, *, n_tok, n_tiles):
    v = pl.program_id(0)
    rows = xbuf.shape[1]
    tile = vt_ref[v]
    lo = lo_ref[v]
    hi = hi_ref[v]
    nonempty = hi > lo
    first = jnp.logical_and(nonempty, lo == 0)
    slot = lax.rem(tile, GATHER_SLOTS)
    oslot = tile % 2
    ahead = GATHER_SLOTS - 1

    def gather_row(t, r):
        src_tile = jnp.minimum(t, n_tiles - 1)
        tok = inv_ref[src_tile * rows + r] & (n_tok - 1)
        s = lax.rem(t, GATHER_SLOTS)
        return pltpu.make_async_copy(x_hbm.at[pl.ds(tok, 1)], xbuf.at[s, pl.ds(r, 1)], gsem.at[s])

    def tile_writeback(t):
        s = t % 2
        r0 = pl.multiple_of(t * rows, rows)
        return pltpu.make_async_copy(obuf.at[s], out_hbm.at[pl.ds(r0, rows)], ssem.at[s])

    def wait_gathered_tile(s):
        pltpu.make_async_copy(x_hbm.at[pl.ds(0, rows)], xbuf.at[s], gsem.at[s]).wait()

    def weight_copies(e):
        return (pltpu.make_async_copy(wg_hbm.at[e], wgf_ref, wsem.at[0]),
                pltpu.make_async_copy(wu_hbm.at[e], wuf_ref, wsem.at[1]),
                pltpu.make_async_copy(wd_hbm.at[e], wdf_ref, wsem.at[2]))

    @pl.when(v == 0)
    def _():
        cur_ref[0] = -1
        for c in weight_copies(e0_ref[0]):
            c.start()

    @pl.when(jnp.logical_and(nonempty, cur_ref[0] != ve_ref[v]))
    def _():
        e = ve_ref[v]
        for c in weight_copies(e):
            c.wait()
        wgb_ref[...] = wgf_ref[...].astype(BF16)
        wub_ref[...] = wuf_ref[...].astype(BF16)
        wdb_ref[...] = wdf_ref[...].astype(BF16)
        cur_ref[0] = e
        ne = nxt_ref[e]

        @pl.when(ne >= 0)
        def _():
            for c in weight_copies(ne):
                c.start()

    @pl.when(jnp.logical_and(first, tile == 0))
    def _():
        def issue(i, c):
            gather_row(i // rows, i % rows).start(priority=ROW_COPY_PRIORITY)
            return c

        lax.fori_loop(0, ahead * rows, issue, 0)

    @pl.when(jnp.logical_and(first, tile >= 2))
    def _():
        tile_writeback(tile - 2).wait()

    @pl.when(first)
    def _():
        wait_gathered_tile(slot)

    @pl.when(first)
    def _():
        xb_ref[...] = xbuf[slot].astype(BF16)
        for r in range(rows):
            gather_row(tile + ahead, r).start(priority=ROW_COPY_PRIORITY)
        obuf[oslot] = _expert_ffn(xb_ref[...], wgb_ref, wub_ref, wdb_ref)

    @pl.when(jnp.logical_and(nonempty, lo > 0))
    def _():
        y = _expert_ffn(xbuf[slot].astype(BF16), wgb_ref, wub_ref, wdb_ref)
        row = lax.broadcasted_iota(I32, (rows, 1), 0)
        mine = jnp.logical_and(row >= lo, row < hi)
        obuf[oslot] = jnp.where(mine, y, obuf[oslot])

    @pl.when(jnp.logical_and(nonempty, hi == rows))
    def _():
        tile_writeback(tile).start()

    @pl.when(v == pl.num_programs(0) - 1)
    def _():
        last = n_tiles - 1
        tile_writeback(last - 1).wait()
        tile_writeback(last).wait()
        for t in range(last + 1, last + 1 + ahead):
            wait_gathered_tile(t % GATHER_SLOTS)


def _experts(x1, inv, visit_tile, visit_expert, visit_lo, visit_hi, next_expert, first_expert,
             w_gate, w_up, w_down):
    n_tok, d = x1.shape
    m = inv.shape[0]
    de = w_gate.shape[2]
    n_tiles = m // EXPERT_TILE
    any_spec = pl.BlockSpec(memory_space=pl.ANY)
    grid_spec = pltpu.PrefetchScalarGridSpec(
        num_scalar_prefetch=7,
        grid=(visit_tile.shape[0],),
        in_specs=[any_spec, any_spec, any_spec, any_spec],
        out_specs=any_spec,
        scratch_shapes=[pltpu.VMEM((d, de), F32), pltpu.VMEM((d, de), F32), pltpu.VMEM((de, d), F32),
                        pltpu.VMEM((d, de), BF16), pltpu.VMEM((d, de), BF16), pltpu.VMEM((de, d), BF16),
                        pltpu.VMEM((GATHER_SLOTS, EXPERT_TILE, d), F32), pltpu.VMEM((2, EXPERT_TILE, d), F32),
                        pltpu.VMEM((EXPERT_TILE, d), BF16), pltpu.SMEM((1,), I32),
                        pltpu.SemaphoreType.DMA((GATHER_SLOTS,)), pltpu.SemaphoreType.DMA((2,)),
                        pltpu.SemaphoreType.DMA((3,))],
    )
    return pl.pallas_call(
        functools.partial(_expert_kernel, n_tok=n_tok, n_tiles=n_tiles),
        grid_spec=grid_spec,
        out_shape=jax.ShapeDtypeStruct((m, d), F32),
        compiler_params=_cparams(("arbitrary",)),
        name="experts",
    )(visit_tile, visit_expert, visit_lo, visit_hi, inv, next_expert, first_expert, x1, w_gate, w_up, w_down)


def _combine_kernel(dcur_ref, dnxt_ref, x1_ref, w_ref, ys_hbm, sg_ref, su_ref, sd_ref, g_ref, b_ref, o_ref,
                    *scratch):
    bufs = scratch[:COMBINE_PHASES]
    sem = scratch[COMBINE_PHASES]
    j = pl.program_id(0)
    tq = bufs[0].shape[1]
    ahead = 2

    def row_copy(dref, col, p, k, t):
        return pltpu.make_async_copy(ys_hbm.at[pl.ds(dref[k, col + t], 1)], bufs[p].at[k, pl.ds(t, 1)], sem.at[p])

    def wait_phase(p):
        pltpu.make_async_copy(bufs[p], bufs[p], sem.at[p]).wait()

    @pl.when(j == 0)
    def _():
        for p in range(ahead):
            def issue(t, c, p=p):
                for k in range(TOP_K):
                    row_copy(dcur_ref, p * tq, p, k, t).start(priority=ROW_COPY_PRIORITY)
                return c

            lax.fori_loop(0, tq, issue, 0)

    x1 = x1_ref[...]
    xb = x1.astype(BF16)
    hg = jnp.dot(xb, sg_ref[...], preferred_element_type=F32)
    hu = jnp.dot(xb, su_ref[...], preferred_element_type=F32)
    hh = (hg * jax.nn.sigmoid(hg)) * hu
    shared = jnp.dot(hh.astype(BF16), sd_ref[...], preferred_element_type=F32)
    base = ALPHA * x1 + shared
    w = w_ref[...]

    for p in range(COMBINE_PHASES):
        wait_phase(p)
        q = p + ahead
        dref, qq = (dcur_ref, q) if q < COMBINE_PHASES else (dnxt_ref, q - COMBINE_PHASES)
        for t in range(tq):
            for k in range(TOP_K):
                row_copy(dref, qq * tq, q % COMBINE_PHASES, k, t).start(priority=ROW_COPY_PRIORITY)
        rs = slice(p * tq, (p + 1) * tq)
        routed = bufs[p][0] * w[rs, 0:1]
        for k in range(1, TOP_K):
            routed = routed + bufs[p][k] * w[rs, k:k + 1]
        o_ref[rs, :] = _layer_norm(base[rs, :] + routed, g_ref[...], b_ref[...])

    @pl.when(j == pl.num_programs(0) - 1)
    def _():
        for p in range(ahead):
            wait_phase(p)


def _combine(x1, dest, w_tok, ys, s_gate, s_up, s_down, ln_g, ln_b, tq):
    n, d = x1.shape
    de = s_gate.shape[1]
    tm = COMBINE_PHASES * tq
    steps = n // tm
    full = lambda shape: pl.BlockSpec(shape, lambda i: (0,) * len(shape))
    return pl.pallas_call(
        _combine_kernel,
        grid=(steps,),
        in_specs=[pl.BlockSpec((TOP_K, tm), lambda i: (0, i), memory_space=pltpu.SMEM),
                  pl.BlockSpec((TOP_K, tm), lambda i: (0, jnp.minimum(i + 1, steps - 1)), memory_space=pltpu.SMEM),
                  pl.BlockSpec((tm, d), lambda i: (i, 0)),
                  pl.BlockSpec((tm, TOP_K), lambda i: (i, 0)),
                  pl.BlockSpec(memory_space=pl.ANY),
                  full((d, de)), full((d, de)), full((de, d)), full((1, d)), full((1, d))],
        out_specs=pl.BlockSpec((tm, d), lambda i: (i, 0)),
        out_shape=jax.ShapeDtypeStruct((n, d), F32),
        scratch_shapes=[pltpu.VMEM((TOP_K, tq, d), F32) for _ in range(COMBINE_PHASES)]
        + [pltpu.SemaphoreType.DMA((COMBINE_PHASES,))],
        compiler_params=_cparams(("arbitrary",)),
        name="combine",
    )(dest, dest, x1, w_tok, ys, s_gate.astype(BF16), s_up.astype(BF16), s_down.astype(BF16),
      ln_g.reshape(1, d), ln_b.reshape(1, d))


def _visit_plan(counts, n_rows):
    e = counts.shape[0]
    n_tiles = n_rows // EXPERT_TILE
    ends = jnp.cumsum(counts)
    starts = ends - counts
    pos = jnp.sort(jnp.concatenate([jnp.arange(n_tiles, dtype=I32) * EXPERT_TILE, starts]))
    nxt = jnp.concatenate([pos[1:], jnp.full((1,), n_rows, I32)])
    tile = jnp.minimum(pos // EXPERT_TILE, n_tiles - 1)
    expert = jnp.minimum(jnp.sum((ends[None, :] <= pos[:, None]).astype(I32), axis=1), e - 1)
    ids = jnp.arange(e, dtype=I32)
    later = jnp.logical_and(ids[None, :] > ids[:, None], counts[None, :] > 0)
    next_expert = jnp.min(jnp.where(later, ids[None, :], e), axis=1)
    next_expert = jnp.where(next_expert < e, next_expert, -1).astype(I32)
    first_expert = jnp.min(jnp.where(counts > 0, ids, e - 1)).astype(I32).reshape(1)
    return starts, tile, expert, pos - tile * EXPERT_TILE, nxt - tile * EXPERT_TILE, next_expert, first_expert


def _moe(x1, logits, router_bias, e_w_gate, e_w_up, e_w_down, s_w_gate, s_w_up, s_w_down, ln_g, ln_b):
    n, d = x1.shape
    top_e, w_t, rank, counts = _route(logits, router_bias, tm=512)
    starts, v_tile, v_expert, v_lo, v_hi, next_e, first_e = _visit_plan(counts[:, 0], n * TOP_K)
    dest = _dest(top_e, rank, starts, tm=1024)
    inv = _invert_permutation(dest.reshape(n * TOP_K))
    ys = _experts(x1, inv, v_tile, v_expert, v_lo, v_hi, next_e, first_e, e_w_gate, e_w_up, e_w_down)
    return _combine(x1, dest, w_t.T, ys, s_w_gate, s_w_up, s_w_down, ln_g, ln_b, tq=64)


def _layer(x, w_in, b_gate, m_conv_w, m_conv_b, m_wq, m_wk, m_wv, m_w_if, m_b_if, m_norm_w, m_skip,
           r_conv_w, r_conv_b, r_wa, r_ba, r_wx, r_bx, r_lambda, w_pm, w_pr, w_o, ln1_g, ln1_b,
           router_w, router_bias, e_w_gate, e_w_up, e_w_down, s_w_gate, s_w_up, s_w_down, ln2_g, ln2_b):
    batch, seq, d = x.shape
    n = batch * seq
    xt = x.reshape(n, d)
    proj = _in_proj(xt.astype(BF16), w_in, bm=1024, bn=1024)
    o_xr = 2 * M_WIDTH
    o_xg = o_xr + R_WIDTH
    o_gate = o_xg + R_WIDTH
    xc, q, k, v, g, gt = _mlstm_prep(proj, seq, m_conv_w, m_conv_b, m_wq, m_wk, m_wv, m_w_if, m_b_if, tm=256)
    y_m = _mlstm(q, k, v, g, gt, xc, proj, m_norm_w, m_skip, batch, seq, MLSTM_CHUNK)
    y_r = _rglru(proj, batch, seq, o_xr, o_xg, r_conv_w, r_conv_b, r_wa, r_ba, r_wx, r_bx, r_lambda,
                 tm=512, cw=512)
    merged = _merge(y_m, y_r, w_pm, w_pr, proj, o_gate, b_gate, tm=512, bn=1024)
    x1, logits = _oproj(merged, w_o, xt, ln1_g, ln1_b, router_w, tm=256)
    out = _moe(x1, logits, router_bias, e_w_gate, e_w_up, e_w_down, s_w_gate, s_w_up, s_w_down, ln2_g, ln2_b)
    return out.reshape(batch, seq, d)


def kernel(x, w_in, b_gate, m_conv_w, m_conv_b, m_wq, m_wk, m_wv, m_w_if, m_b_if, m_norm_w, m_skip, r_conv_w, r_conv_b, r_wa, r_ba, r_wx, r_bx, r_lambda, w_pm, w_pr, w_o, ln1_g, ln1_b, router_w, router_bias, e_w_gate, e_w_up, e_w_down, s_w_gate, s_w_up, s_w_down, ln2_g, ln2_b):
    for l in range(DEPTH):
        x = _layer(x, w_in[l], b_gate[l], m_conv_w[l], m_conv_b[l], m_wq[l], m_wk[l], m_wv[l],
                   m_w_if[l], m_b_if[l], m_norm_w[l], m_skip[l], r_conv_w[l], r_conv_b[l],
                   r_wa[l], r_ba[l], r_wx[l], r_bx[l], r_lambda[l], w_pm[l], w_pr[l], w_o[l],
                   ln1_g[l], ln1_b[l], router_w[l], router_bias[l], e_w_gate[l], e_w_up[l],
                   e_w_down[l], s_w_gate[l], s_w_up[l], s_w_down[l], ln2_g[l], ln2_b[l])
    return x
```

```python
import functools
import math

import jax
import jax.numpy as jnp
from jax import lax
from jax.experimental import pallas as pl
from jax.experimental.pallas import tpu as pltpu

F32 = jnp.float32
BF16 = jnp.bfloat16
I32 = jnp.int32
U32 = jnp.uint32

D_MODEL = 2048
M_WIDTH = 2048
M_HEADS = 8
M_HEAD_DIM = 256
M_QKV_BLOCK = 4
CONV_WIDTH = 4
R_WIDTH = 2560
R_BLOCK = 256
LRU_C = 8.0
N_EXPERTS = 64
TOP_K = 8
N_GROUPS = 8
TOPK_GROUPS = 4
D_EXPERT = 512
ROUTED_SCALE = 2.5
DEPTH = 1
ALPHA = (2.0 * DEPTH) ** 0.25
LN_EPS = 1e-5

V7X_VMEM_LIMIT = 56 * 1024 * 1024
HALO = 8
MLSTM_CHUNK = 256
EXPERT_TILE = 256
NEG_INF = float("-inf")
GATE_LANES = 128
ROUTER_LANES = 128
GATHER_SLOTS = 3
COMBINE_PHASES = 4
ROW_COPY_PRIORITY = 1


def _cparams(sem, vmem=V7X_VMEM_LIMIT):
    return pltpu.CompilerParams(dimension_semantics=sem, vmem_limit_bytes=vmem)


def _inproj_kernel(a_ref, w_ref, o_ref, wb_ref):
    @pl.when(pl.program_id(1) == 0)
    def _():
        wb_ref[...] = w_ref[...].astype(BF16)

    o_ref[...] = jnp.dot(a_ref[...], wb_ref[...], preferred_element_type=F32)


def _in_proj(a, w, bm, bn):
    m, k = a.shape
    n = w.shape[1]
    return pl.pallas_call(
        _inproj_kernel,
        grid=(n // bn, m // bm),
        in_specs=[pl.BlockSpec((bm, k), lambda j, i: (i, 0)),
                  pl.BlockSpec((k, bn), lambda j, i: (0, j))],
        out_specs=pl.BlockSpec((bm, bn), lambda j, i: (i, j)),
        out_shape=jax.ShapeDtypeStruct((m, n), F32),
        scratch_shapes=[pltpu.VMEM((k, bn), BF16)],
        compiler_params=_cparams(("parallel", "arbitrary")),
        name="in_proj",
    )(a, w)


def _sigmoid(x):
    return 0.5 * jnp.tanh(0.5 * x) + 0.5


def _log_sigmoid(x):
    return jnp.minimum(x, 0.0) - jnp.log1p(jnp.exp(-jnp.abs(x)))


def _shift_rows(x3, prev_group, j):
    rot = pltpu.roll(x3, j, axis=1)
    prev = jnp.concatenate([pltpu.roll(prev_group, j, axis=1), rot[:-1]], axis=0)
    sub = lax.broadcasted_iota(I32, x3.shape, 1)
    return jnp.where(sub >= j, rot, prev)


def _causal_conv(halo, x, cw_ref, cb_ref):
    tm, c = x.shape
    x3 = x.reshape(tm // HALO, HALO, c)
    h3 = halo.reshape(1, HALO, c)
    last = CONV_WIDTH - 1
    y = cb_ref[...] + x3 * cw_ref[last:last + 1, :]
    for j in range(1, CONV_WIDTH):
        y = y + _shift_rows(x3, h3, j) * cw_ref[last - j:last - j + 1, :]
    return y


def _mprep_kernel(xm_ref, halo_ref, cw_ref, cb_ref, wq_ref, wk_ref, wv_ref, wif_ref, bif_ref,
                  xc_ref, q_ref, k_ref, v_ref, g_ref, gt_ref, *, tm, tiles_per_seq):
    i = pl.program_id(0)
    first = (i % tiles_per_seq) == 0
    halo = jnp.where(first, 0.0, halo_ref[...])
    xm = xm_ref[...]
    y = _causal_conv(halo, xm, cw_ref, cb_ref).reshape(tm, M_WIDTH)
    xc = y * _sigmoid(y)
    xc_ref[...] = xc
    xcb = xc.astype(BF16)
    xmb = xm.astype(BF16)
    nblk = M_WIDTH // M_HEAD_DIM
    for g in range(nblk):
        sl = slice(g * M_HEAD_DIM, (g + 1) * M_HEAD_DIM)
        q_ref[:, sl] = jnp.dot(xcb[:, sl], wq_ref[g], preferred_element_type=F32).astype(BF16)
        k_ref[:, sl] = jnp.dot(xcb[:, sl], wk_ref[g], preferred_element_type=F32).astype(BF16)
        v_ref[:, sl] = jnp.dot(xmb[:, sl], wv_ref[g], preferred_element_type=F32).astype(BF16)
    qb, kb, vb = q_ref[...], k_ref[...], v_ref[...]
    w = M_WIDTH
    ng = g_ref.shape[1]
    g = (jnp.dot(qb, wif_ref[0:w, :], preferred_element_type=F32)
         + jnp.dot(kb, wif_ref[w:2 * w, :], preferred_element_type=F32)
         + jnp.dot(vb, wif_ref[2 * w:3 * w, :], preferred_element_type=F32) + bif_ref[...])
    col = lax.broadcasted_iota(I32, g.shape, 1)
    g = jnp.where(col >= M_HEADS, _log_sigmoid(g), g)
    g_ref[...] = g[:, 0:ng]
    gt_ref[...] = jnp.transpose(g)[0:ng, :]


def _block_diag_kernel(w_ref, o_ref, *, bi):
    w = w_ref[0]
    group = o_ref.shape[1]
    row = lax.broadcasted_iota(I32, (group, group), 0)
    col = lax.broadcasted_iota(I32, (group, group), 1)
    dense = jnp.zeros((group, group), F32)
    for o in range(w.shape[1]):
        dense = jnp.where(col % w.shape[1] == o, w[:, o:o + 1], dense)
    o_ref[0] = jnp.where(row // bi == col // w.shape[1], dense, 0.0).astype(o_ref.dtype)


def _block_diag_dense(ws, group):
    nb, bi, bo = ws[0].shape
    slabs = len(ws) * nb * bi // group
    w3 = jnp.concatenate(ws, axis=0).reshape(slabs, group, bo)
    return pl.pallas_call(
        functools.partial(_block_diag_kernel, bi=bi),
        grid=(slabs,),
        in_specs=[pl.BlockSpec((1, group, bo), lambda i: (i, 0, 0))],
        out_specs=pl.BlockSpec((1, group, group), lambda i: (i, 0, 0)),
        out_shape=jax.ShapeDtypeStruct((slabs, group, group), BF16),
        compiler_params=_cparams(("parallel",)),
        name="block_diag",
    )(w3)


def _mlstm_prep(proj, seq, conv_w, conv_b, wq, wk, wv, w_if, b_if, tm):
    n = proj.shape[0]
    c = M_WIDTH
    nblk = c // M_HEAD_DIM
    tiles_per_seq = seq // tm
    wd = _block_diag_dense([wq, wk, wv], M_HEAD_DIM)
    ng = 2 * M_HEADS
    wif = jnp.pad(w_if, ((0, 0), (0, GATE_LANES - ng))).astype(BF16)
    bif = jnp.pad(b_if, (0, GATE_LANES - ng)).reshape(1, GATE_LANES)
    hb = tm // HALO
    full = lambda shape: pl.BlockSpec(shape, lambda i: (0,) * len(shape))
    return pl.pallas_call(
        functools.partial(_mprep_kernel, tm=tm, tiles_per_seq=tiles_per_seq),
        grid=(n // tm,),
        in_specs=[pl.BlockSpec((tm, c), lambda i: (i, 0)),
                  pl.BlockSpec((HALO, c), lambda i: (jnp.maximum(i * hb - 1, 0), 0)),
                  full((CONV_WIDTH, c)), full((1, c)),
                  pl.BlockSpec((nblk, M_HEAD_DIM, M_HEAD_DIM), lambda i: (0, 0, 0)),
                  pl.BlockSpec((nblk, M_HEAD_DIM, M_HEAD_DIM), lambda i: (1, 0, 0)),
                  pl.BlockSpec((nblk, M_HEAD_DIM, M_HEAD_DIM), lambda i: (2, 0, 0)),
                  full((3 * c, GATE_LANES)), full((1, GATE_LANES))],
        out_specs=[pl.BlockSpec((tm, c), lambda i: (i, 0)),
                   pl.BlockSpec((tm, c), lambda i: (i, 0)),
                   pl.BlockSpec((tm, c), lambda i: (i, 0)),
                   pl.BlockSpec((tm, c), lambda i: (i, 0)),
                   pl.BlockSpec((tm, ng), lambda i: (i, 0)),
                   pl.BlockSpec((ng, tm), lambda i: (0, i))],
        out_shape=[jax.ShapeDtypeStruct((n, c), F32),
                   jax.ShapeDtypeStruct((n, c), BF16),
                   jax.ShapeDtypeStruct((n, c), BF16),
                   jax.ShapeDtypeStruct((n, c), BF16),
                   jax.ShapeDtypeStruct((n, ng), F32),
                   jax.ShapeDtypeStruct((ng, n), F32)],
        compiler_params=_cparams(("parallel",)),
        name="mlstm_prep",
    )(proj, proj, conv_w, conv_b.reshape(1, c), wd, wd, wd, wif, bif)


def _mlstm_kernel(q_ref, k_ref, v_ref, g_ref, gt_ref, xc_ref, z_ref, nw_ref, sk_ref, o_ref,
                  c_ref, n_ref, m_ref, *, chunk):
    L = chunk
    hd = M_HEAD_DIM

    @pl.when(pl.program_id(1) == 0)
    def _():
        c_ref[...] = jnp.zeros_like(c_ref)
        n_ref[...] = jnp.zeros_like(n_ref)
        m_ref[...] = jnp.zeros_like(m_ref)

    rows = lax.broadcasted_iota(I32, (L, L), 0)
    cols = lax.broadcasted_iota(I32, (L, L), 1)
    causal = cols <= rows
    tril = jnp.where(causal, 1.0, 0.0).astype(F32)
    triu = jnp.where(rows <= cols, 1.0, 0.0).astype(F32)
    g = g_ref[...]
    gt = gt_ref[...]
    hi = lax.Precision.HIGHEST
    bcol_all = jnp.dot(tril, g, precision=hi, preferred_element_type=F32)
    brow_all = jnp.dot(gt, triu, precision=hi, preferred_element_type=F32)
    k_scale = hd ** -0.5
    nt = (((1,), (1,)), ((), ()))
    tn = (((0,), (0,)), ((), ()))

    for h in range(M_HEADS):
        sl = slice(h * hd, (h + 1) * hd)
        qh = q_ref[:, sl]
        kh = k_ref[:, sl]
        vh = v_ref[:, sl]
        i_col = g[:, h:h + 1]
        b_col = bcol_all[:, M_HEADS + h:M_HEADS + h + 1]
        i_row = gt[h:h + 1, :]
        b_row = brow_all[M_HEADS + h:M_HEADS + h + 1, :]
        m_prev = m_ref[h, 0:1, 0:1]
        c_prev = c_ref[h]
        n_prev = n_ref[h]

        dmat = jnp.where(causal, b_col - b_row + i_row, NEG_INF)
        a_col = b_col + m_prev
        m_row = jnp.maximum(a_col, jnp.max(dmat, axis=1, keepdims=True))
        s = lax.dot_general(qh, kh, nt, preferred_element_type=F32) * k_scale
        s = s * jnp.exp(dmat - m_row)
        inter = jnp.exp(a_col - m_row)
        num = inter * jnp.dot(qh, c_prev.astype(BF16), preferred_element_type=F32) \
            + jnp.dot(s.astype(BF16), vh, preferred_element_type=F32)
        qn = jnp.sum(qh.astype(F32) * n_prev, axis=1, keepdims=True)
        den = inter * qn + jnp.sum(s, axis=1, keepdims=True)
        hval = num * (1.0 / jnp.maximum(jnp.abs(den), jnp.exp(-m_row)))

        mu = jnp.mean(hval, axis=1, keepdims=True)
        cen = hval - mu
        var = jnp.mean(cen * cen, axis=1, keepdims=True)
        hn = cen * lax.rsqrt(var + LN_EPS) * nw_ref[:, sl]
        zz = z_ref[:, sl]
        o_ref[:, sl] = ((hn + sk_ref[:, sl] * xc_ref[:, sl]) * (zz * _sigmoid(zz))).astype(o_ref.dtype)

        b_last = b_col[L - 1:L, :]
        w_log = b_last - b_col + i_col
        m_new = jnp.maximum(b_last + m_prev, jnp.max(w_log, axis=0, keepdims=True))
        decay = jnp.exp(b_last + m_prev - m_new)
        kw = kh.astype(F32) * (jnp.exp(w_log - m_new) * k_scale)
        c_ref[h] = decay * c_prev + lax.dot_general(kw.astype(BF16), vh, tn, preferred_element_type=F32)
        n_ref[h] = decay * n_prev + jnp.sum(kw, axis=0, keepdims=True)
        m_ref[h] = jnp.broadcast_to(m_new, m_ref.shape[1:])


def _mlstm(q, k, v, g, gt, xc, proj, norm_w, skip, batch, seq, chunk):
    n, c = q.shape
    nc = seq // chunk
    ng = 2 * M_HEADS
    zcol = M_WIDTH // c
    row = lambda b, j: (b * nc + j, 0)
    return pl.pallas_call(
        functools.partial(_mlstm_kernel, chunk=chunk),
        grid=(batch, nc),
        in_specs=[pl.BlockSpec((chunk, c), row), pl.BlockSpec((chunk, c), row), pl.BlockSpec((chunk, c), row),
                  pl.BlockSpec((chunk, ng), row),
                  pl.BlockSpec((ng, chunk), lambda b, j: (0, b * nc + j)),
                  pl.BlockSpec((chunk, c), row),
                  pl.BlockSpec((chunk, c), lambda b, j: (b * nc + j, zcol)),
                  pl.BlockSpec((1, c), lambda b, j: (0, 0)),
                  pl.BlockSpec((1, c), lambda b, j: (0, 0))],
        out_specs=pl.BlockSpec((chunk, c), row),
        out_shape=jax.ShapeDtypeStruct((n, c), BF16),
        scratch_shapes=[pltpu.VMEM((M_HEADS, M_HEAD_DIM, M_HEAD_DIM), F32),
                        pltpu.VMEM((M_HEADS, 1, M_HEAD_DIM), F32),
                        pltpu.VMEM((M_HEADS, 8, 128), F32)],
        compiler_params=_cparams(("parallel", "arbitrary")),
        name="mlstm_chunk",
    )(q, k, v, g, gt, xc, proj, norm_w.reshape(1, c), skip.reshape(1, c))


def _gelu_tanh(x):
    return 0.5 * x * (1.0 + jnp.tanh(math.sqrt(2.0 / math.pi) * (x + 0.044715 * (x * x * x))))


def _rglru_kernel(xr_ref, halo_ref, xg_ref, cw_ref, cb_ref, wa_ref, ba_ref, wx_ref, bx_ref, lam_ref,
                  o_ref, a_ref, b_ref, h_ref, *, tm, cw):
    t = pl.program_id(2)

    @pl.when(t == 0)
    def _():
        h_ref[...] = jnp.zeros_like(h_ref)

    halo = jnp.where(t == 0, 0.0, halo_ref[...])
    xc = _causal_conv(halo, xr_ref[...], cw_ref, cb_ref).reshape(tm, cw)
    xcb = xc.astype(BF16)
    nblk = cw // R_BLOCK
    ra = []
    rx = []
    for g in range(nblk):
        sl = slice(g * R_BLOCK, (g + 1) * R_BLOCK)
        ra.append(jnp.dot(xcb[:, sl], wa_ref[g], preferred_element_type=F32))
        rx.append(jnp.dot(xcb[:, sl], wx_ref[g], preferred_element_type=F32))
    r = _sigmoid(jnp.concatenate(ra, axis=1) + ba_ref[...])
    ig = _sigmoid(jnp.concatenate(rx, axis=1) + bx_ref[...])
    nl = -lam_ref[...]
    softplus = jnp.maximum(nl, 0.0) + jnp.log1p(jnp.exp(-jnp.abs(nl)))
    a = jnp.exp((-LRU_C * softplus) * r)
    v = 1.0 - a * a
    b = jnp.where(v > 0.0, v * lax.rsqrt(v), 0.0) * (ig * xc)

    a = a.reshape(tm // 8, 8, cw)
    b = b.reshape(tm // 8, 8, cw)
    sub = lax.broadcasted_iota(I32, a.shape, 1)
    for d in (1, 2, 4):
        keep = sub >= d
        a_sh = pltpu.roll(a, d, axis=1)
        b_sh = pltpu.roll(b, d, axis=1)
        b = jnp.where(keep, a * b_sh + b, b)
        a = jnp.where(keep, a * a_sh, a)
    a_ref[...] = a.reshape(tm, cw)
    b_ref[...] = b.reshape(tm, cw)

    def body(g, h):
        r0 = pl.multiple_of(g * 8, 8)
        hh = b_ref[pl.ds(r0, 8), :] + a_ref[pl.ds(r0, 8), :] * h
        b_ref[pl.ds(r0, 8), :] = hh
        return hh[7:8, :]

    h_ref[...] = lax.fori_loop(0, tm // 8, body, h_ref[...], unroll=8)
    o_ref[...] = (b_ref[...] * _gelu_tanh(xg_ref[...])).astype(o_ref.dtype)


def _rglru(proj, batch, seq, xr_off, xg_off, conv_w, conv_b, wa, ba, wx, bx, lam, tm, cw):
    n = proj.shape[0]
    ncol = R_WIDTH // cw
    nt = seq // tm
    per = cw // R_BLOCK
    xr_cb = xr_off // cw
    xg_cb = xg_off // cw
    hb = tm // HALO
    colv = lambda shape: pl.BlockSpec(shape, lambda b, j, t: (0, j))
    return pl.pallas_call(
        functools.partial(_rglru_kernel, tm=tm, cw=cw),
        grid=(batch, ncol, nt),
        in_specs=[pl.BlockSpec((tm, cw), lambda b, j, t: (b * nt + t, xr_cb + j)),
                  pl.BlockSpec((HALO, cw), lambda b, j, t: (jnp.maximum((b * nt + t) * hb - 1, 0), xr_cb + j)),
                  pl.BlockSpec((tm, cw), lambda b, j, t: (b * nt + t, xg_cb + j)),
                  colv((CONV_WIDTH, cw)), colv((1, cw)),
                  pl.BlockSpec((per, R_BLOCK, R_BLOCK), lambda b, j, t: (j, 0, 0)), colv((1, cw)),
                  pl.BlockSpec((per, R_BLOCK, R_BLOCK), lambda b, j, t: (j, 0, 0)), colv((1, cw)),
                  colv((1, cw))],
        out_specs=pl.BlockSpec((tm, cw), lambda b, j, t: (b * nt + t, j)),
        out_shape=jax.ShapeDtypeStruct((n, R_WIDTH), BF16),
        scratch_shapes=[pltpu.VMEM((tm, cw), F32), pltpu.VMEM((tm, cw), F32), pltpu.VMEM((1, cw), F32)],
        compiler_params=_cparams(("parallel", "parallel", "arbitrary")),
        name="rglru",
    )(proj, proj, proj, conv_w, conv_b.reshape(1, R_WIDTH), wa.astype(BF16), ba.reshape(1, R_WIDTH),
      wx.astype(BF16), bx.reshape(1, R_WIDTH), lam.reshape(1, R_WIDTH))


def _merge_kernel(ym_ref, yr_ref, wpm_ref, wpr_ref, g0_ref, g1_ref, bg_ref, o_ref):
    g0 = jax.nn.sigmoid(g0_ref[...] + bg_ref[0:1, :])
    g1 = jax.nn.sigmoid(g1_ref[...] + bg_ref[1:2, :])
    pm = jnp.dot(ym_ref[...], wpm_ref[...], preferred_element_type=F32)
    pr = jnp.dot(yr_ref[...], wpr_ref[...], preferred_element_type=F32)
    o_ref[...] = (g0 * pm + g1 * pr).astype(o_ref.dtype)


def _merge(ym, yr, w_pm, w_pr, proj, gate_off, b_gate, tm, bn):
    n = ym.shape[0]
    d = w_pm.shape[1]
    g0_cb = gate_off // bn
    g1_cb = (gate_off + d) // bn
    return pl.pallas_call(
        _merge_kernel,
        grid=(d // bn, n // tm),
        in_specs=[pl.BlockSpec((tm, ym.shape[1]), lambda j, i: (i, 0)),
                  pl.BlockSpec((tm, yr.shape[1]), lambda j, i: (i, 0)),
                  pl.BlockSpec((w_pm.shape[0], bn), lambda j, i: (0, j)),
                  pl.BlockSpec((w_pr.shape[0], bn), lambda j, i: (0, j)),
                  pl.BlockSpec((tm, bn), lambda j, i: (i, g0_cb + j)),
                  pl.BlockSpec((tm, bn), lambda j, i: (i, g1_cb + j)),
                  pl.BlockSpec((2, bn), lambda j, i: (0, j))],
        out_specs=pl.BlockSpec((tm, bn), lambda j, i: (i, j)),
        out_shape=jax.ShapeDtypeStruct((n, d), BF16),
        compiler_params=_cparams(("parallel", "parallel")),
        name="merge",
    )(ym, yr, w_pm.astype(BF16), w_pr.astype(BF16), proj, proj, b_gate)


def _layer_norm(y, g, b):
    mu = jnp.mean(y, axis=1, keepdims=True)
    cen = y - mu
    var = jnp.mean(cen * cen, axis=1, keepdims=True)
    return cen * lax.rsqrt(var + LN_EPS) * g + b


def _split_hi_lo(x):
    hi = lax.bitcast_convert_type(lax.bitcast_convert_type(x, U32) & jnp.uint32(0xFFFF0000), F32)
    return hi.astype(BF16), (x - hi).astype(BF16)


def _oproj_kernel(mg_ref, wo_ref, x_ref, g_ref, b_ref, rwh_ref, rwl_ref, x1_ref, lg_ref):
    y = ALPHA * x_ref[...] + jnp.dot(mg_ref[...], wo_ref[...], preferred_element_type=F32)
    x1 = _layer_norm(y, g_ref[...], b_ref[...])
    x1_ref[...] = x1
    xh, xl = _split_hi_lo(x1)
    wh = rwh_ref[...]
    wl = rwl_ref[...]
    lg_ref[...] = ((jnp.dot(xh, wh, preferred_element_type=F32) + jnp.dot(xl, wl, preferred_element_type=F32))
                   + (jnp.dot(xl, wh, preferred_element_type=F32) + jnp.dot(xh, wl, preferred_element_type=F32)))


def _oproj(merged, w_o, x, ln_g, ln_b, router_w, tm):
    n, d = x.shape
    e = router_w.shape[1]
    rw_hi, rw_lo = _split_hi_lo(jnp.pad(router_w, ((0, 0), (0, ROUTER_LANES - e))))
    full = lambda shape: pl.BlockSpec(shape, lambda i: (0,) * len(shape))
    return pl.pallas_call(
        _oproj_kernel,
        grid=(n // tm,),
        in_specs=[pl.BlockSpec((tm, d), lambda i: (i, 0)), full((d, d)),
                  pl.BlockSpec((tm, d), lambda i: (i, 0)), full((1, d)), full((1, d)),
                  full((d, ROUTER_LANES)), full((d, ROUTER_LANES))],
        out_specs=[pl.BlockSpec((tm, d), lambda i: (i, 0)), pl.BlockSpec((tm, ROUTER_LANES), lambda i: (i, 0))],
        out_shape=[jax.ShapeDtypeStruct((n, d), F32), jax.ShapeDtypeStruct((n, ROUTER_LANES), F32)],
        compiler_params=_cparams(("parallel",)),
        name="out_proj_ln",
    )(merged, w_o.astype(BF16), x, ln_g.reshape(1, d), ln_b.reshape(1, d), rw_hi, rw_lo)


def _first_max(v, idx, sentinel):
    m = jnp.max(v, axis=0, keepdims=True)
    am = jnp.min(jnp.where(v == m, idx, sentinel), axis=0, keepdims=True)
    return m, am


def _route_kernel(lg_ref, bias_ref, tri_ref, te_ref, w_ref, rk_ref, cnt_ref, carry_ref, *, tm):
    @pl.when(pl.program_id(0) == 0)
    def _():
        carry_ref[...] = jnp.zeros_like(carry_ref)

    e = N_EXPERTS
    gs = e // N_GROUPS
    scores = jax.nn.sigmoid(jnp.transpose(lg_ref[...])[0:e, :])
    biased = scores + bias_ref[...]
    sub = lax.broadcasted_iota(I32, (gs, tm), 0)
    grp_rows = []
    for g in range(N_GROUPS):
        slab = biased[g * gs:(g + 1) * gs, :]
        m1, a1 = _first_max(slab, sub, gs)
        m2 = jnp.max(jnp.where(sub == a1, NEG_INF, slab), axis=0, keepdims=True)
        grp_rows.append(m1 + m2)
    grp = jnp.concatenate(grp_rows, axis=0)
    gidx = lax.broadcasted_iota(I32, (N_GROUPS, tm), 0)
    gsel = jnp.zeros((N_GROUPS, tm), F32)
    for _ in range(TOPK_GROUPS):
        _, am = _first_max(grp, gidx, N_GROUPS)
        hit = gidx == am
        gsel = jnp.where(hit, 1.0, gsel)
        grp = jnp.where(hit, NEG_INF, grp)
    masked = jnp.concatenate(
        [jnp.where(gsel[g:g + 1, :] > 0.0, biased[g * gs:(g + 1) * gs, :], NEG_INF) for g in range(N_GROUPS)],
        axis=0)
    eidx = lax.broadcasted_iota(I32, (e, tm), 0)
    member = jnp.zeros((e, tm), F32)
    tops = []
    ws = []
    for _ in range(TOP_K):
        _, am = _first_max(masked, eidx, e)
        hit = eidx == am
        tops.append(am)
        ws.append(jnp.sum(jnp.where(hit, scores, 0.0), axis=0, keepdims=True))
        member = jnp.where(hit, 1.0, member)
        masked = jnp.where(hit, NEG_INF, masked)
    wsum = ws[0]
    for k in range(1, TOP_K):
        wsum = wsum + ws[k]
    te_ref[...] = jnp.concatenate(tops, axis=0)
    w_ref[...] = jnp.concatenate(ws, axis=0) / wsum * ROUTED_SCALE

    cum = jnp.dot(member.astype(BF16), tri_ref[...], preferred_element_type=F32)
    carry = carry_ref[:, 0:1]
    rank = carry + cum - member
    rks = []
    for k in range(TOP_K):
        rks.append(jnp.sum(jnp.where(eidx == tops[k], rank, 0.0), axis=0, keepdims=True))
    rk_ref[...] = jnp.concatenate(rks, axis=0).astype(I32)
    new_carry = carry + cum[:, tm - 1:tm]
    carry_ref[...] = jnp.broadcast_to(new_carry, carry_ref.shape)
    cnt_ref[...] = jnp.broadcast_to(new_carry, cnt_ref.shape).astype(I32)


def _route(logits, router_bias, tm):
    n = logits.shape[0]
    e = router_bias.shape[0]
    tri = jnp.triu(jnp.ones((tm, tm), F32)).astype(BF16)
    return pl.pallas_call(
        functools.partial(_route_kernel, tm=tm),
        grid=(n // tm,),
        in_specs=[pl.BlockSpec((tm, ROUTER_LANES), lambda i: (i, 0)),
                  pl.BlockSpec((e, 1), lambda i: (0, 0)),
                  pl.BlockSpec((tm, tm), lambda i: (0, 0))],
        out_specs=[pl.BlockSpec((TOP_K, tm), lambda i: (0, i)),
                   pl.BlockSpec((TOP_K, tm), lambda i: (0, i)),
                   pl.BlockSpec((TOP_K, tm), lambda i: (0, i)),
                   pl.BlockSpec((e, 128), lambda i: (0, 0))],
        out_shape=[jax.ShapeDtypeStruct((TOP_K, n), I32), jax.ShapeDtypeStruct((TOP_K, n), F32),
                   jax.ShapeDtypeStruct((TOP_K, n), I32), jax.ShapeDtypeStruct((e, 128), I32)],
        scratch_shapes=[pltpu.VMEM((e, 128), F32)],
        compiler_params=_cparams(("arbitrary",)),
        name="route",
    )(logits, router_bias.reshape(e, 1), tri)


def _dest_kernel(te_ref, rk_ref, ps_ref, d_ref):
    te = te_ref[...]
    e = N_EXPERTS
    tm = te.shape[1]
    eidx = lax.broadcasted_iota(I32, (e, tm), 0)
    ps = ps_ref[...]
    rows = []
    for k in range(TOP_K):
        rows.append(jnp.sum(jnp.where(eidx == te[k:k + 1, :], ps, 0), axis=0, keepdims=True))
    d_ref[...] = jnp.concatenate(rows, axis=0) + rk_ref[...]


def _dest(top_e, rank, pad_starts, tm):
    n = top_e.shape[1]
    return pl.pallas_call(
        _dest_kernel,
        grid=(n // tm,),
        in_specs=[pl.BlockSpec((TOP_K, tm), lambda i: (0, i)),
                  pl.BlockSpec((TOP_K, tm), lambda i: (0, i)),
                  pl.BlockSpec((N_EXPERTS, 1), lambda i: (0, 0))],
        out_specs=pl.BlockSpec((TOP_K, tm), lambda i: (0, i)),
        out_shape=jax.ShapeDtypeStruct((TOP_K, n), I32),
        compiler_params=_cparams(("parallel",)),
        name="dest",
    )(top_e, rank, pad_starts.reshape(N_EXPERTS, 1))


def _invert_kernel(dest_ref, inv_ref):
    def body(j, c):
        inv_ref[dest_ref[j]] = j
        return c

    lax.fori_loop(0, dest_ref.shape[0], body, 0, unroll=32)


def _invert_permutation(dest_flat):
    m = dest_flat.shape[0]
    return pl.pallas_call(
        _invert_kernel,
        in_specs=[pl.BlockSpec(memory_space=pltpu.SMEM)],
        out_specs=pl.BlockSpec(memory_space=pltpu.SMEM),
        out_shape=jax.ShapeDtypeStruct((m,), I32),
        name="invert_perm",
    )(dest_flat)


def _expert_ffn(xb, wgb_ref, wub_ref, wdb_ref):
    hg = jnp.dot(xb, wgb_ref[...], preferred_element_type=F32)
    hu = jnp.dot(xb, wub_ref[...], preferred_element_type=F32)
    hh = (hg * jax.nn.sigmoid(hg)) * hu
    return jnp.dot(hh.astype(BF16), wdb_ref[...], preferred_element_type=F32)


def _expert_kernel(vt_ref, ve_ref, lo_ref, hi_ref, inv_ref, nxt_ref, e0_ref,
                   x_hbm, wg_hbm, wu_hbm, wd_hbm, out_hbm,
                   wgf_ref, wuf_ref, wdf_ref, wgb_ref, wub_ref, wdb_ref, xbuf, obuf, xb_ref, cur_ref,
                   gsem, ssem, wsem, *, n_tok, n_tiles):
    v = pl.program_id(0)
    rows = xbuf.shape[1]
    tile = vt_ref[v]
    lo = lo_ref[v]
    hi = hi_ref[v]
    nonempty = hi > lo
    first = jnp.logical_and(nonempty, lo == 0)
    slot = lax.rem(tile, GATHER_SLOTS)
    oslot = tile % 2
    ahead = GATHER_SLOTS - 1

    def gather_row(t, r):
        src_tile = jnp.minimum(t, n_tiles - 1)
        tok = inv_ref[src_tile * rows + r] & (n_tok - 1)
        s = lax.rem(t, GATHER_SLOTS)
        return pltpu.make_async_copy(x_hbm.at[pl.ds(tok, 1)], xbuf.at[s, pl.ds(r, 1)], gsem.at[s])

    def tile_writeback(t):
        s = t % 2
        r0 = pl.multiple_of(t * rows, rows)
        return pltpu.make_async_copy(obuf.at[s], out_hbm.at[pl.ds(r0, rows)], ssem.at[s])

    def wait_gathered_tile(s):
        pltpu.make_async_copy(x_hbm.at[pl.ds(0, rows)], xbuf.at[s], gsem.at[s]).wait()

    def weight_copies(e):
        return (pltpu.make_async_copy(wg_hbm.at[e], wgf_ref, wsem.at[0]),
                pltpu.make_async_copy(wu_hbm.at[e], wuf_ref, wsem.at[1]),
                pltpu.make_async_copy(wd_hbm.at[e], wdf_ref, wsem.at[2]))

    @pl.when(v == 0)
    def _():
        cur_ref[0] = -1
        for c in weight_copies(e0_ref[0]):
            c.start()

    @pl.when(jnp.logical_and(nonempty, cur_ref[0] != ve_ref[v]))
    def _():
        e = ve_ref[v]
        for c in weight_copies(e):
            c.wait()
        wgb_ref[...] = wgf_ref[...].astype(BF16)
        wub_ref[...] = wuf_ref[...].astype(BF16)
        wdb_ref[...] = wdf_ref[...].astype(BF16)
        cur_ref[0] = e
        ne = nxt_ref[e]

        @pl.when(ne >= 0)
        def _():
            for c in weight_copies(ne):
                c.start()

    @pl.when(jnp.logical_and(first, tile == 0))
    def _():
        def issue(i, c):
            gather_row(i // rows, i % rows).start(priority=ROW_COPY_PRIORITY)
            return c

        lax.fori_loop(0, ahead * rows, issue, 0)

    @pl.when(jnp.logical_and(first, tile >= 2))
    def _():
        tile_writeback(tile - 2).wait()

    @pl.when(first)
    def _():
        wait_gathered_tile(slot)

    @pl.when(first)
    def _():
        xb_ref[...] = xbuf[slot].astype(BF16)
        for r in range(rows):
            gather_row(tile + ahead, r).start(priority=ROW_COPY_PRIORITY)
        obuf[oslot] = _expert_ffn(xb_ref[...], wgb_ref, wub_ref, wdb_ref)

    @pl.when(jnp.logical_and(nonempty, lo > 0))
    def _():
        y = _expert_ffn(xbuf[slot].astype(BF16), wgb_ref, wub_ref, wdb_ref)
        row = lax.broadcasted_iota(I32, (rows, 1), 0)
        mine = jnp.logical_and(row >= lo, row < hi)
        obuf[oslot] = jnp.where(mine, y, obuf[oslot])

    @pl.when(jnp.logical_and(nonempty, hi == rows))
    def _():
        tile_writeback(tile).start()

    @pl.when(v == pl.num_programs(0) - 1)
    def _():
        last = n_tiles - 1
        tile_writeback(last - 1).wait()
        tile_writeback(last).wait()
        for t in range(last + 1, last + 1 + ahead):
            wait_gathered_tile(t % GATHER_SLOTS)


def _experts(x1, inv, visit_tile, visit_expert, visit_lo, visit_hi, next_expert, first_expert,
             w_gate, w_up, w_down):
    n_tok, d = x1.shape
    m = inv.shape[0]
    de = w_gate.shape[2]
    n_tiles = m // EXPERT_TILE
    any_spec = pl.BlockSpec(memory_space=pl.ANY)
    grid_spec = pltpu.PrefetchScalarGridSpec(
        num_scalar_prefetch=7,
        grid=(visit_tile.shape[0],),
        in_specs=[any_spec, any_spec, any_spec, any_spec],
        out_specs=any_spec,
        scratch_shapes=[pltpu.VMEM((d, de), F32), pltpu.VMEM((d, de), F32), pltpu.VMEM((de, d), F32),
                        pltpu.VMEM((d, de), BF16), pltpu.VMEM((d, de), BF16), pltpu.VMEM((de, d), BF16),
                        pltpu.VMEM((GATHER_SLOTS, EXPERT_TILE, d), F32), pltpu.VMEM((2, EXPERT_TILE, d), F32),
                        pltpu.VMEM((EXPERT_TILE, d), BF16), pltpu.SMEM((1,), I32),
                        pltpu.SemaphoreType.DMA((GATHER_SLOTS,)), pltpu.SemaphoreType.DMA((2,)),
                        pltpu.SemaphoreType.DMA((3,))],
    )
    return pl.pallas_call(
        functools.partial(_expert_kernel, n_tok=n_tok, n_tiles=n_tiles),
        grid_spec=grid_spec,
        out_shape=jax.ShapeDtypeStruct((m, d), F32),
        compiler_params=_cparams(("arbitrary",)),
        name="experts",
    )(visit_tile, visit_expert, visit_lo, visit_hi, inv, next_expert, first_expert, x1, w_gate, w_up, w_down)


def _combine_kernel(dcur_ref, dnxt_ref, x1_ref, w_ref, ys_hbm, sg_ref, su_ref, sd_ref, g_ref, b_ref, o_ref,
                    *scratch):
    bufs = scratch[:COMBINE_PHASES]
    sem = scratch[COMBINE_PHASES]
    j = pl.program_id(0)
    tq = bufs[0].shape[1]
    ahead = 2

    def row_copy(dref, col, p, k, t):
        return pltpu.make_async_copy(ys_hbm.at[pl.ds(dref[k, col + t], 1)], bufs[p].at[k, pl.ds(t, 1)], sem.at[p])

    def wait_phase(p):
        pltpu.make_async_copy(bufs[p], bufs[p], sem.at[p]).wait()

    @pl.when(j == 0)
    def _():
        for p in range(ahead):
            def issue(t, c, p=p):
                for k in range(TOP_K):
                    row_copy(dcur_ref, p * tq, p, k, t).start(priority=ROW_COPY_PRIORITY)
                return c

            lax.fori_loop(0, tq, issue, 0)

    x1 = x1_ref[...]
    xb = x1.astype(BF16)
    hg = jnp.dot(xb, sg_ref[...], preferred_element_type=F32)
    hu = jnp.dot(xb, su_ref[...], preferred_element_type=F32)
    hh = (hg * jax.nn.sigmoid(hg)) * hu
    shared = jnp.dot(hh.astype(BF16), sd_ref[...], preferred_element_type=F32)
    base = ALPHA * x1 + shared
    w = w_ref[...]

    for p in range(COMBINE_PHASES):
        wait_phase(p)
        q = p + ahead
        dref, qq = (dcur_ref, q) if q < COMBINE_PHASES else (dnxt_ref, q - COMBINE_PHASES)
        for t in range(tq):
            for k in range(TOP_K):
                row_copy(dref, qq * tq, q % COMBINE_PHASES, k, t).start(priority=ROW_COPY_PRIORITY)
        rs = slice(p * tq, (p + 1) * tq)
        routed = bufs[p][0] * w[rs, 0:1]
        for k in range(1, TOP_K):
            routed = routed + bufs[p][k] * w[rs, k:k + 1]
        o_ref[rs, :] = _layer_norm(base[rs, :] + routed, g_ref[...], b_ref[...])

    @pl.when(j == pl.num_programs(0) - 1)
    def _():
        for p in range(ahead):
            wait_phase(p)


def _combine(x1, dest, w_tok, ys, s_gate, s_up, s_down, ln_g, ln_b, tq):
    n, d = x1.shape
    de = s_gate.shape[1]
    tm = COMBINE_PHASES * tq
    steps = n // tm
    full = lambda shape: pl.BlockSpec(shape, lambda i: (0,) * len(shape))
    return pl.pallas_call(
        _combine_kernel,
        grid=(steps,),
        in_specs=[pl.BlockSpec((TOP_K, tm), lambda i: (0, i), memory_space=pltpu.SMEM),
                  pl.BlockSpec((TOP_K, tm), lambda i: (0, jnp.minimum(i + 1, steps - 1)), memory_space=pltpu.SMEM),
                  pl.BlockSpec((tm, d), lambda i: (i, 0)),
                  pl.BlockSpec((tm, TOP_K), lambda i: (i, 0)),
                  pl.BlockSpec(memory_space=pl.ANY),
                  full((d, de)), full((d, de)), full((de, d)), full((1, d)), full((1, d))],
        out_specs=pl.BlockSpec((tm, d), lambda i: (i, 0)),
        out_shape=jax.ShapeDtypeStruct((n, d), F32),
        scratch_shapes=[pltpu.VMEM((TOP_K, tq, d), F32) for _ in range(COMBINE_PHASES)]
        + [pltpu.SemaphoreType.DMA((COMBINE_PHASES,))],
        compiler_params=_cparams(("arbitrary",)),
        name="combine",
    )(dest, dest, x1, w_tok, ys, s_gate.astype(BF16), s_up.astype(BF16), s_down.astype(BF16),
      ln_g.reshape(1, d), ln_b.reshape(1, d))


def _visit_plan(counts, n_rows):
    e = counts.shape[0]
    n_tiles = n_rows // EXPERT_TILE
    ends = jnp.cumsum(counts)
    starts = ends - counts
    pos = jnp.sort(jnp.concatenate([jnp.arange(n_tiles, dtype=I32) * EXPERT_TILE, starts]))
    nxt = jnp.concatenate([pos[1:], jnp.full((1,), n_rows, I32)])
    tile = jnp.minimum(pos // EXPERT_TILE, n_tiles - 1)
    expert = jnp.minimum(jnp.sum((ends[None, :] <= pos[:, None]).astype(I32), axis=1), e - 1)
    ids = jnp.arange(e, dtype=I32)
    later = jnp.logical_and(ids[None, :] > ids[:, None], counts[None, :] > 0)
    next_expert = jnp.min(jnp.where(later, ids[None, :], e), axis=1)
    next_expert = jnp.where(next_expert < e, next_expert, -1).astype(I32)
    first_expert = jnp.min(jnp.where(counts > 0, ids, e - 1)).astype(I32).reshape(1)
    return starts, tile, expert, pos - tile * EXPERT_TILE, nxt - tile * EXPERT_TILE, next_expert, first_expert


def _moe(x1, logits, router_bias, e_w_gate, e_w_up, e_w_down, s_w_gate, s_w_up, s_w_down, ln_g, ln_b):
    n, d = x1.shape
    top_e, w_t, rank, counts = _route(logits, router_bias, tm=512)
    starts, v_tile, v_expert, v_lo, v_hi, next_e, first_e = _visit_plan(counts[:, 0], n * TOP_K)
    dest = _dest(top_e, rank, starts, tm=1024)
    inv = _invert_permutation(dest.reshape(n * TOP_K))
    ys = _experts(x1, inv, v_tile, v_expert, v_lo, v_hi, next_e, first_e, e_w_gate, e_w_up, e_w_down)
    return _combine(x1, dest, w_t.T, ys, s_w_gate, s_w_up, s_w_down, ln_g, ln_b, tq=64)


def _layer(x, w_in, b_gate, m_conv_w, m_conv_b, m_wq, m_wk, m_wv, m_w_if, m_b_if, m_norm_w, m_skip,
           r_conv_w, r_conv_b, r_wa, r_ba, r_wx, r_bx, r_lambda, w_pm, w_pr, w_o, ln1_g, ln1_b,
           router_w, router_bias, e_w_gate, e_w_up, e_w_down, s_w_gate, s_w_up, s_w_down, ln2_g, ln2_b):
    batch, seq, d = x.shape
    n = batch * seq
    xt = x.reshape(n, d)
    proj = _in_proj(xt.astype(BF16), w_in, bm=1024, bn=1024)
    o_xr = 2 * M_WIDTH
    o_xg = o_xr + R_WIDTH
    o_gate = o_xg + R_WIDTH
    xc, q, k, v, g, gt = _mlstm_prep(proj, seq, m_conv_w, m_conv_b, m_wq, m_wk, m_wv, m_w_if, m_b_if, tm=256)
    y_m = _mlstm(q, k, v, g, gt, xc, proj, m_norm_w, m_skip, batch, seq, MLSTM_CHUNK)
    y_r = _rglru(proj, batch, seq, o_xr, o_xg, r_conv_w, r_conv_b, r_wa, r_ba, r_wx, r_bx, r_lambda,
                 tm=512, cw=512)
    merged = _merge(y_m, y_r, w_pm, w_pr, proj, o_gate, b_gate, tm=512, bn=1024)
    x1, logits = _oproj(merged, w_o, xt, ln1_g, ln1_b, router_w, tm=256)
    out = _moe(x1, logits, router_bias, e_w_gate, e_w_up, e_w_down, s_w_gate, s_w_up, s_w_down, ln2_g, ln2_b)
    return out.reshape(batch, seq, d)


def kernel(x, w_in, b_gate, m_conv_w, m_conv_b, m_wq, m_wk, m_wv, m_w_if, m_b_if, m_norm_w, m_skip, r_conv_w, r_conv_b, r_wa, r_ba, r_wx, r_bx, r_lambda, w_pm, w_pr, w_o, ln1_g, ln1_b, router_w, router_bias, e_w_gate, e_w_up, e_w_down, s_w_gate, s_w_up, s_w_down, ln2_g, ln2_b):
    for l in range(DEPTH):
        x = _layer(x, w_in[l], b_gate[l], m_conv_w[l], m_conv_b[l], m_wq[l], m_wk[l], m_wv[l],
                   m_w_if[l], m_b_if[l], m_norm_w[l], m_skip[l], r_conv_w[l], r_conv_b[l],
                   r_wa[l], r_ba[l], r_wx[l], r_bx[l], r_lambda[l], w_pm[l], w_pr[l], w_o[l],
                   ln1_g[l], ln1_b[l], router_w[l], router_bias[l], e_w_gate[l], e_w_up[l],
                   e_w_down[l], s_w_gate[l], s_w_up[l], s_w_down[l], ln2_g[l], ln2_b[l])
    return x
```

```python
import functools
import math

import jax
import jax.numpy as jnp
from jax import lax
from jax.experimental import pallas as pl
from jax.experimental.pallas import tpu as pltpu

F32 = jnp.float32
BF16 = jnp.bfloat16
I32 = jnp.int32
U32 = jnp.uint32

D_MODEL = 2048
M_WIDTH = 2048
M_HEADS = 8
M_HEAD_DIM = 256
M_QKV_BLOCK = 4
CONV_WIDTH = 4
R_WIDTH = 2560
R_BLOCK = 256
LRU_C = 8.0
N_EXPERTS = 64
TOP_K = 8
N_GROUPS = 8
TOPK_GROUPS = 4
D_EXPERT = 512
ROUTED_SCALE = 2.5
DEPTH = 1
ALPHA = (2.0 * DEPTH) ** 0.25
LN_EPS = 1e-5

V7X_VMEM_LIMIT = 56 * 1024 * 1024
HALO = 8
MLSTM_CHUNK = 256
EXPERT_TILE = 256
NEG_INF = float("-inf")
GATE_LANES = 128
ROUTER_LANES = 128
GATHER_SLOTS = 3
COMBINE_PHASES = 4
ROW_COPY_PRIORITY = 1


def _cparams(sem, vmem=V7X_VMEM_LIMIT):
    return pltpu.CompilerParams(dimension_semantics=sem, vmem_limit_bytes=vmem)


def _inproj_kernel(a_ref, w_ref, o_ref, wb_ref):
    @pl.when(pl.program_id(1) == 0)
    def _():
        wb_ref[...] = w_ref[...].astype(BF16)

    o_ref[...] = jnp.dot(a_ref[...], wb_ref[...], preferred_element_type=F32)


def _in_proj(a, w, bm, bn):
    m, k = a.shape
    n = w.shape[1]
    return pl.pallas_call(
        _inproj_kernel,
        grid=(n // bn, m // bm),
        in_specs=[pl.BlockSpec((bm, k), lambda j, i: (i, 0)),
                  pl.BlockSpec((k, bn), lambda j, i: (0, j))],
        out_specs=pl.BlockSpec((bm, bn), lambda j, i: (i, j)),
        out_shape=jax.ShapeDtypeStruct((m, n), F32),
        scratch_shapes=[pltpu.VMEM((k, bn), BF16)],
        compiler_params=_cparams(("parallel", "arbitrary")),
        name="in_proj",
    )(a, w)


def _sigmoid(x):
    return 0.5 * jnp.tanh(0.5 * x) + 0.5


def _log_sigmoid(x):
    return jnp.minimum(x, 0.0) - jnp.log1p(jnp.exp(-jnp.abs(x)))


def _shift_rows(x3, prev_group, j):
    rot = pltpu.roll(x3, j, axis=1)
    prev = jnp.concatenate([pltpu.roll(prev_group, j, axis=1), rot[:-1]], axis=0)
    sub = lax.broadcasted_iota(I32, x3.shape, 1)
    return jnp.where(sub >= j, rot, prev)


def _causal_conv(halo, x, cw_ref, cb_ref):
    tm, c = x.shape
    x3 = x.reshape(tm // HALO, HALO, c)
    h3 = halo.reshape(1, HALO, c)
    last = CONV_WIDTH - 1
    y = cb_ref[...] + x3 * cw_ref[last:last + 1, :]
    for j in range(1, CONV_WIDTH):
        y = y + _shift_rows(x3, h3, j) * cw_ref[last - j:last - j + 1, :]
    return y


def _mprep_kernel(xm_ref, halo_ref, cw_ref, cb_ref, wq_ref, wk_ref, wv_ref, wif_ref, bif_ref,
                  xc_ref, q_ref, k_ref, v_ref, g_ref, gt_ref, *, tm, tiles_per_seq):
    i = pl.program_id(0)
    first = (i % tiles_per_seq) == 0
    halo = jnp.where(first, 0.0, halo_ref[...])
    xm = xm_ref[...]
    y = _causal_conv(halo, xm, cw_ref, cb_ref).reshape(tm, M_WIDTH)
    xc = y * _sigmoid(y)
    xc_ref[...] = xc
    xcb = xc.astype(BF16)
    xmb = xm.astype(BF16)
    nblk = M_WIDTH // M_HEAD_DIM
    for g in range(nblk):
        sl = slice(g * M_HEAD_DIM, (g + 1) * M_HEAD_DIM)
        q_ref[:, sl] = jnp.dot(xcb[:, sl], wq_ref[g], preferred_element_type=F32).astype(BF16)
        k_ref[:, sl] = jnp.dot(xcb[:, sl], wk_ref[g], preferred_element_type=F32).astype(BF16)
        v_ref[:, sl] = jnp.dot(xmb[:, sl], wv_ref[g], preferred_element_type=F32).astype(BF16)
    qb, kb, vb = q_ref[...], k_ref[...], v_ref[...]
    w = M_WIDTH
    ng = g_ref.shape[1]
    g = (jnp.dot(qb, wif_ref[0:w, :], preferred_element_type=F32)
         + jnp.dot(kb, wif_ref[w:2 * w, :], preferred_element_type=F32)
         + jnp.dot(vb, wif_ref[2 * w:3 * w, :], preferred_element_type=F32) + bif_ref[...])
    col = lax.broadcasted_iota(I32, g.shape, 1)
    g = jnp.where(col >= M_HEADS, _log_sigmoid(g), g)
    g_ref[...] = g[:, 0:ng]
    gt_ref[...] = jnp.transpose(g)[0:ng, :]


def _block_diag_kernel(w_ref, o_ref, *, bi):
    w = w_ref[0]
    group = o_ref.shape[1]
    row = lax.broadcasted_iota(I32, (group, group), 0)
    col = lax.broadcasted_iota(I32, (group, group), 1)
    dense = jnp.zeros((group, group), F32)
    for o in range(w.shape[1]):
        dense = jnp.where(col % w.shape[1] == o, w[:, o:o + 1], dense)
    o_ref[0] = jnp.where(row // bi == col // w.shape[1], dense, 0.0).astype(o_ref.dtype)


def _block_diag_dense(ws, group):
    nb, bi, bo = ws[0].shape
    slabs = len(ws) * nb * bi // group
    w3 = jnp.concatenate(ws, axis=0).reshape(slabs, group, bo)
    return pl.pallas_call(
        functools.partial(_block_diag_kernel, bi=bi),
        grid=(slabs,),
        in_specs=[pl.BlockSpec((1, group, bo), lambda i: (i, 0, 0))],
        out_specs=pl.BlockSpec((1, group, group), lambda i: (i, 0, 0)),
        out_shape=jax.ShapeDtypeStruct((slabs, group, group), BF16),
        compiler_params=_cparams(("parallel",)),
        name="block_diag",
    )(w3)


def _mlstm_prep(proj, seq, conv_w, conv_b, wq, wk, wv, w_if, b_if, tm):
    n = proj.shape[0]
    c = M_WIDTH
    nblk = c // M_HEAD_DIM
    tiles_per_seq = seq // tm
    wd = _block_diag_dense([wq, wk, wv], M_HEAD_DIM)
    ng = 2 * M_HEADS
    wif = jnp.pad(w_if, ((0, 0), (0, GATE_LANES - ng))).astype(BF16)
    bif = jnp.pad(b_if, (0, GATE_LANES - ng)).reshape(1, GATE_LANES)
    hb = tm // HALO
    full = lambda shape: pl.BlockSpec(shape, lambda i: (0,) * len(shape))
    return pl.pallas_call(
        functools.partial(_mprep_kernel, tm=tm, tiles_per_seq=tiles_per_seq),
        grid=(n // tm,),
        in_specs=[pl.BlockSpec((tm, c), lambda i: (i, 0)),
                  pl.BlockSpec((HALO, c), lambda i: (jnp.maximum(i * hb - 1, 0), 0)),
                  full((CONV_WIDTH, c)), full((1, c)),
                  pl.BlockSpec((nblk, M_HEAD_DIM, M_HEAD_DIM), lambda i: (0, 0, 0)),
                  pl.BlockSpec((nblk, M_HEAD_DIM, M_HEAD_DIM), lambda i: (1, 0, 0)),
                  pl.BlockSpec((nblk, M_HEAD_DIM, M_HEAD_DIM), lambda i: (2, 0, 0)),
                  full((3 * c, GATE_LANES)), full((1, GATE_LANES))],
        out_specs=[pl.BlockSpec((tm, c), lambda i: (i, 0)),
                   pl.BlockSpec((tm, c), lambda i: (i, 0)),
                   pl.BlockSpec((tm, c), lambda i: (i, 0)),
                   pl.BlockSpec((tm, c), lambda i: (i, 0)),
                   pl.BlockSpec((tm, ng), lambda i: (i, 0)),
                   pl.BlockSpec((ng, tm), lambda i: (0, i))],
        out_shape=[jax.ShapeDtypeStruct((n, c), F32),
                   jax.ShapeDtypeStruct((n, c), BF16),
                   jax.ShapeDtypeStruct((n, c), BF16),
                   jax.ShapeDtypeStruct((n, c), BF16),
                   jax.ShapeDtypeStruct((n, ng), F32),
                   jax.ShapeDtypeStruct((ng, n), F32)],
        compiler_params=_cparams(("parallel",)),
        name="mlstm_prep",
    )(proj, proj, conv_w, conv_b.reshape(1, c), wd, wd, wd, wif, bif)


def _mlstm_kernel(q_ref, k_ref, v_ref, g_ref, gt_ref, xc_ref, z_ref, nw_ref, sk_ref, o_ref,
                  c_ref, n_ref, m_ref, *, chunk):
    L = chunk
    hd = M_HEAD_DIM

    @pl.when(pl.program_id(1) == 0)
    def _():
        c_ref[...] = jnp.zeros_like(c_ref)
        n_ref[...] = jnp.zeros_like(n_ref)
        m_ref[...] = jnp.zeros_like(m_ref)

    rows = lax.broadcasted_iota(I32, (L, L), 0)
    cols = lax.broadcasted_iota(I32, (L, L), 1)
    causal = cols <= rows
    tril = jnp.where(causal, 1.0, 0.0).astype(F32)
    triu = jnp.where(rows <= cols, 1.0, 0.0).astype(F32)
    g = g_ref[...]
    gt = gt_ref[...]
    hi = lax.Precision.HIGHEST
    bcol_all = jnp.dot(tril, g, precision=hi, preferred_element_type=F32)
    brow_all = jnp.dot(gt, triu, precision=hi, preferred_element_type=F32)
    k_scale = hd ** -0.5
    nt = (((1,), (1,)), ((), ()))
    tn = (((0,), (0,)), ((), ()))

    for h in range(M_HEADS):
        sl = slice(h * hd, (h + 1) * hd)
        qh = q_ref[:, sl]
        kh = k_ref[:, sl]
        vh = v_ref[:, sl]
        i_col = g[:, h:h + 1]
        b_col = bcol_all[:, M_HEADS + h:M_HEADS + h + 1]
        i_row = gt[h:h + 1, :]
        b_row = brow_all[M_HEADS + h:M_HEADS + h + 1, :]
        m_prev = m_ref[h, 0:1, 0:1]
        c_prev = c_ref[h]
        n_prev = n_ref[h]

        dmat = jnp.where(causal, b_col - b_row + i_row, NEG_INF)
        a_col = b_col + m_prev
        m_row = jnp.maximum(a_col, jnp.max(dmat, axis=1, keepdims=True))
        s = lax.dot_general(qh, kh, nt, preferred_element_type=F32) * k_scale
        s = s * jnp.exp(dmat - m_row)
        inter = jnp.exp(a_col - m_row)
        num = inter * jnp.dot(qh, c_prev.astype(BF16), preferred_element_type=F32) \
            + jnp.dot(s.astype(BF16), vh, preferred_element_type=F32)
        qn = jnp.sum(qh.astype(F32) * n_prev, axis=1, keepdims=True)
        den = inter * qn + jnp.sum(s, axis=1, keepdims=True)
        hval = num * (1.0 / jnp.maximum(jnp.abs(den), jnp.exp(-m_row)))

        mu = jnp.mean(hval, axis=1, keepdims=True)
        cen = hval - mu
        var = jnp.mean(cen * cen, axis=1, keepdims=True)
        hn = cen * lax.rsqrt(var + LN_EPS) * nw_ref[:, sl]
        zz = z_ref[:, sl]
        o_ref[:, sl] = ((hn + sk_ref[:, sl] * xc_ref[:, sl]) * (zz * _sigmoid(zz))).astype(o_ref.dtype)

        b_last = b_col[L - 1:L, :]
        w_log = b_last - b_col + i_col
        m_new = jnp.maximum(b_last + m_prev, jnp.max(w_log, axis=0, keepdims=True))
        decay = jnp.exp(b_last + m_prev - m_new)
        kw = kh.astype(F32) * (jnp.exp(w_log - m_new) * k_scale)
        c_ref[h] = decay * c_prev + lax.dot_general(kw.astype(BF16), vh, tn, preferred_element_type=F32)
        n_ref[h] = decay * n_prev + jnp.sum(kw, axis=0, keepdims=True)
        m_ref[h] = jnp.broadcast_to(m_new, m_ref.shape[1:])


def _mlstm(q, k, v, g, gt, xc, proj, norm_w, skip, batch, seq, chunk):
    n, c = q.shape
    nc = seq // chunk
    ng = 2 * M_HEADS
    zcol = M_WIDTH // c
    row = lambda b, j: (b * nc + j, 0)
    return pl.pallas_call(
        functools.partial(_mlstm_kernel, chunk=chunk),
        grid=(batch, nc),
        in_specs=[pl.BlockSpec((chunk, c), row), pl.BlockSpec((chunk, c), row), pl.BlockSpec((chunk, c), row),
                  pl.BlockSpec((chunk, ng), row),
                  pl.BlockSpec((ng, chunk), lambda b, j: (0, b * nc + j)),
                  pl.BlockSpec((chunk, c), row),
                  pl.BlockSpec((chunk, c), lambda b, j: (b * nc + j, zcol)),
                  pl.BlockSpec((1, c), lambda b, j: (0, 0)),
                  pl.BlockSpec((1, c), lambda b, j: (0, 0))],
        out_specs=pl.BlockSpec((chunk, c), row),
        out_shape=jax.ShapeDtypeStruct((n, c), BF16),
        scratch_shapes=[pltpu.VMEM((M_HEADS, M_HEAD_DIM, M_HEAD_DIM), F32),
                        pltpu.VMEM((M_HEADS, 1, M_HEAD_DIM), F32),
                        pltpu.VMEM((M_HEADS, 8, 128), F32)],
        compiler_params=_cparams(("parallel", "arbitrary")),
        name="mlstm_chunk",
    )(q, k, v, g, gt, xc, proj, norm_w.reshape(1, c), skip.reshape(1, c))


def _gelu_tanh(x):
    return 0.5 * x * (1.0 + jnp.tanh(math.sqrt(2.0 / math.pi) * (x + 0.044715 * (x * x * x))))


def _rglru_kernel(xr_ref, halo_ref, xg_ref, cw_ref, cb_ref, wa_ref, ba_ref, wx_ref, bx_ref, lam_ref,
                  o_ref, a_ref, b_ref, h_ref, *, tm, cw):
    t = pl.program_id(2)

    @pl.when(t == 0)
    def _():
        h_ref[...] = jnp.zeros_like(h_ref)

    halo = jnp.where(t == 0, 0.0, halo_ref[...])
    xc = _causal_conv(halo, xr_ref[...], cw_ref, cb_ref).reshape(tm, cw)
    xcb = xc.astype(BF16)
    nblk = cw // R_BLOCK
    ra = []
    rx = []
    for g in range(nblk):
        sl = slice(g * R_BLOCK, (g + 1) * R_BLOCK)
        ra.append(jnp.dot(xcb[:, sl], wa_ref[g], preferred_element_type=F32))
        rx.append(jnp.dot(xcb[:, sl], wx_ref[g], preferred_element_type=F32))
    r = _sigmoid(jnp.concatenate(ra, axis=1) + ba_ref[...])
    ig = _sigmoid(jnp.concatenate(rx, axis=1) + bx_ref[...])
    nl = -lam_ref[...]
    softplus = jnp.maximum(nl, 0.0) + jnp.log1p(jnp.exp(-jnp.abs(nl)))
    a = jnp.exp((-LRU_C * softplus) * r)
    v = 1.0 - a * a
    b = jnp.where(v > 0.0, v * lax.rsqrt(v), 0.0) * (ig * xc)

    a = a.reshape(tm // 8, 8, cw)
    b = b.reshape(tm // 8, 8, cw)
    sub = lax.broadcasted_iota(I32, a.shape, 1)
    for d in (1, 2, 4):
        keep = sub >= d
        a_sh = pltpu.roll(a, d, axis=1)
        b_sh = pltpu.roll(b, d, axis=1)
        b = jnp.where(keep, a * b_sh + b, b)
        a = jnp.where(keep, a * a_sh, a)
    a_ref[...] = a.reshape(tm, cw)
    b_ref[...] = b.reshape(tm, cw)

    def body(g, h):
        r0 = pl.multiple_of(g * 8, 8)
        hh = b_ref[pl.ds(r0, 8), :] + a_ref[pl.ds(r0, 8), :] * h
        b_ref[pl.ds(r0, 8), :] = hh
        return hh[7:8, :]

    h_ref[...] = lax.fori_loop(0, tm // 8, body, h_ref[...], unroll=8)
    o_ref[...] = (b_ref[...] * _gelu_tanh(xg_ref[...])).astype(o_ref.dtype)


def _rglru(proj, batch, seq, xr_off, xg_off, conv_w, conv_b, wa, ba, wx, bx, lam, tm, cw):
    n = proj.shape[0]
    ncol = R_WIDTH // cw
    nt = seq // tm
    per = cw // R_BLOCK
    xr_cb = xr_off // cw
    xg_cb = xg_off // cw
    hb = tm // HALO
    colv = lambda shape: pl.BlockSpec(shape, lambda b, j, t: (0, j))
    return pl.pallas_call(
        functools.partial(_rglru_kernel, tm=tm, cw=cw),
        grid=(batch, ncol, nt),
        in_specs=[pl.BlockSpec((tm, cw), lambda b, j, t: (b * nt + t, xr_cb + j)),
                  pl.BlockSpec((HALO, cw), lambda b, j, t: (jnp.maximum((b * nt + t) * hb - 1, 0), xr_cb + j)),
                  pl.BlockSpec((tm, cw), lambda b, j, t: (b * nt + t, xg_cb + j)),
                  colv((CONV_WIDTH, cw)), colv((1, cw)),
                  pl.BlockSpec((per, R_BLOCK, R_BLOCK), lambda b, j, t: (j, 0, 0)), colv((1, cw)),
                  pl.BlockSpec((per, R_BLOCK, R_BLOCK), lambda b, j, t: (j, 0, 0)), colv((1, cw)),
                  colv((1, cw))],
        out_specs=pl.BlockSpec((tm, cw), lambda b, j, t: (b * nt + t, j)),
        out_shape=jax.ShapeDtypeStruct((n, R_WIDTH), BF16),
        scratch_shapes=[pltpu.VMEM((tm, cw), F32), pltpu.VMEM((tm, cw), F32), pltpu.VMEM((1, cw), F32)],
        compiler_params=_cparams(("parallel", "parallel", "arbitrary")),
        name="rglru",
    )(proj, proj, proj, conv_w, conv_b.reshape(1, R_WIDTH), wa.astype(BF16), ba.reshape(1, R_WIDTH),
      wx.astype(BF16), bx.reshape(1, R_WIDTH), lam.reshape(1, R_WIDTH))


def _merge_kernel(ym_ref, yr_ref, wpm_ref, wpr_ref, g0_ref, g1_ref, bg_ref, o_ref):
    g0 = jax.nn.sigmoid(g0_ref[...] + bg_ref[0:1, :])
    g1 = jax.nn.sigmoid(g1_ref[...] + bg_ref[1:2, :])
    pm = jnp.dot(ym_ref[...], wpm_ref[...], preferred_element_type=F32)
    pr = jnp.dot(yr_ref[...], wpr_ref[...], preferred_element_type=F32)
    o_ref[...] = (g0 * pm + g1 * pr).astype(o_ref.dtype)


def _merge(ym, yr, w_pm, w_pr, proj, gate_off, b_gate, tm, bn):
    n = ym.shape[0]
    d = w_pm.shape[1]
    g0_cb = gate_off // bn
    g1_cb = (gate_off + d) // bn
    return pl.pallas_call(
        _merge_kernel,
        grid=(d // bn, n // tm),
        in_specs=[pl.BlockSpec((tm, ym.shape[1]), lambda j, i: (i, 0)),
                  pl.BlockSpec((tm, yr.shape[1]), lambda j, i: (i, 0)),
                  pl.BlockSpec((w_pm.shape[0], bn), lambda j, i: (0, j)),
                  pl.BlockSpec((w_pr.shape[0], bn), lambda j, i: (0, j)),
                  pl.BlockSpec((tm, bn), lambda j, i: (i, g0_cb + j)),
                  pl.BlockSpec((tm, bn), lambda j, i: (i, g1_cb + j)),
                  pl.BlockSpec((2, bn), lambda j, i: (0, j))],
        out_specs=pl.BlockSpec((tm, bn), lambda j, i: (i, j)),
        out_shape=jax.ShapeDtypeStruct((n, d), BF16),
        compiler_params=_cparams(("parallel", "parallel")),
        name="merge",
    )(ym, yr, w_pm.astype(BF16), w_pr.astype(BF16), proj, proj, b_gate)


def _layer_norm(y, g, b):
    mu = jnp.mean(y, axis=1, keepdims=True)
    cen = y - mu
    var = jnp.mean(cen * cen, axis=1, keepdims=True)
    return cen * lax.rsqrt(var + LN_EPS) * g + b


def _split_hi_lo(x):
    hi = lax.bitcast_convert_type(lax.bitcast_convert_type(x, U32) & jnp.uint32(0xFFFF0000), F32)
    return hi.astype(BF16), (x - hi).astype(BF16)


def _oproj_kernel(mg_ref, wo_ref, x_ref, g_ref, b_ref, rwh_ref, rwl_ref, x1_ref, lg_ref):
    y = ALPHA * x_ref[...] + jnp.dot(mg_ref[...], wo_ref[...], preferred_element_type=F32)
    x1 = _layer_norm(y, g_ref[...], b_ref[...])
    x1_ref[...] = x1
    xh, xl = _split_hi_lo(x1)
    wh = rwh_ref[...]
    wl = rwl_ref[...]
    lg_ref[...] = ((jnp.dot(xh, wh, preferred_element_type=F32) + jnp.dot(xl, wl, preferred_element_type=F32))
                   + (jnp.dot(xl, wh, preferred_element_type=F32) + jnp.dot(xh, wl, preferred_element_type=F32)))


def _oproj(merged, w_o, x, ln_g, ln_b, router_w, tm):
    n, d = x.shape
    e = router_w.shape[1]
    rw_hi, rw_lo = _split_hi_lo(jnp.pad(router_w, ((0, 0), (0, ROUTER_LANES - e))))
    full = lambda shape: pl.BlockSpec(shape, lambda i: (0,) * len(shape))
    return pl.pallas_call(
        _oproj_kernel,
        grid=(n // tm,),
        in_specs=[pl.BlockSpec((tm, d), lambda i: (i, 0)), full((d, d)),
                  pl.BlockSpec((tm, d), lambda i: (i, 0)), full((1, d)), full((1, d)),
                  full((d, ROUTER_LANES)), full((d, ROUTER_LANES))],
        out_specs=[pl.BlockSpec((tm, d), lambda i: (i, 0)), pl.BlockSpec((tm, ROUTER_LANES), lambda i: (i, 0))],
        out_shape=[jax.ShapeDtypeStruct((n, d), F32), jax.ShapeDtypeStruct((n, ROUTER_LANES), F32)],
        compiler_params=_cparams(("parallel",)),
        name="out_proj_ln",
    )(merged, w_o.astype(BF16), x, ln_g.reshape(1, d), ln_b.reshape(1, d), rw_hi, rw_lo)


def _first_max(v, idx, sentinel):
    m = jnp.max(v, axis=0, keepdims=True)
    am = jnp.min(jnp.where(v == m, idx, sentinel), axis=0, keepdims=True)
    return m, am


def _route_kernel(lg_ref, bias_ref, tri_ref, te_ref, w_ref, rk_ref, cnt_ref, carry_ref, *, tm):
    @pl.when(pl.program_id(0) == 0)
    def _():
        carry_ref[...] = jnp.zeros_like(carry_ref)

    e = N_EXPERTS
    gs = e // N_GROUPS
    scores = jax.nn.sigmoid(jnp.transpose(lg_ref[...])[0:e, :])
    biased = scores + bias_ref[...]
    sub = lax.broadcasted_iota(I32, (gs, tm), 0)
    grp_rows = []
    for g in range(N_GROUPS):
        slab = biased[g * gs:(g + 1) * gs, :]
        m1, a1 = _first_max(slab, sub, gs)
        m2 = jnp.max(jnp.where(sub == a1, NEG_INF, slab), axis=0, keepdims=True)
        grp_rows.append(m1 + m2)
    grp = jnp.concatenate(grp_rows, axis=0)
    gidx = lax.broadcasted_iota(I32, (N_GROUPS, tm), 0)
    gsel = jnp.zeros((N_GROUPS, tm), F32)
    for _ in range(TOPK_GROUPS):
        _, am = _first_max(grp, gidx, N_GROUPS)
        hit = gidx == am
        gsel = jnp.where(hit, 1.0, gsel)
        grp = jnp.where(hit, NEG_INF, grp)
    masked = jnp.concatenate(
        [jnp.where(gsel[g:g + 1, :] > 0.0, biased[g * gs:(g + 1) * gs, :], NEG_INF) for g in range(N_GROUPS)],
        axis=0)
    eidx = lax.broadcasted_iota(I32, (e, tm), 0)
    member = jnp.zeros((e, tm), F32)
    tops = []
    ws = []
    for _ in range(TOP_K):
        _, am = _first_max(masked, eidx, e)
        hit = eidx == am
        tops.append(am)
        ws.append(jnp.sum(jnp.where(hit, scores, 0.0), axis=0, keepdims=True))
        member = jnp.where(hit, 1.0, member)
        masked = jnp.where(hit, NEG_INF, masked)
    wsum = ws[0]
    for k in range(1, TOP_K):
        wsum = wsum + ws[k]
    te_ref[...] = jnp.concatenate(tops, axis=0)
    w_ref[...] = jnp.concatenate(ws, axis=0) / wsum * ROUTED_SCALE

    cum = jnp.dot(member.astype(BF16), tri_ref[...], preferred_element_type=F32)
    carry = carry_ref[:, 0:1]
    rank = carry + cum - member
    rks = []
    for k in range(TOP_K):
        rks.append(jnp.sum(jnp.where(eidx == tops[k], rank, 0.0), axis=0, keepdims=True))
    rk_ref[...] = jnp.concatenate(rks, axis=0).astype(I32)
    new_carry = carry + cum[:, tm - 1:tm]
    carry_ref[...] = jnp.broadcast_to(new_carry, carry_ref.shape)
    cnt_ref[...] = jnp.broadcast_to(new_carry, cnt_ref.shape).astype(I32)


def _route(logits, router_bias, tm):
    n = logits.shape[0]
    e = router_bias.shape[0]
    tri = jnp.triu(jnp.ones((tm, tm), F32)).astype(BF16)
    return pl.pallas_call(
        functools.partial(_route_kernel, tm=tm),
        grid=(n // tm,),
        in_specs=[pl.BlockSpec((tm, ROUTER_LANES), lambda i: (i, 0)),
                  pl.BlockSpec((e, 1), lambda i: (0, 0)),
                  pl.BlockSpec((tm, tm), lambda i: (0, 0))],
        out_specs=[pl.BlockSpec((TOP_K, tm), lambda i: (0, i)),
                   pl.BlockSpec((TOP_K, tm), lambda i: (0, i)),
                   pl.BlockSpec((TOP_K, tm), lambda i: (0, i)),
                   pl.BlockSpec((e, 128), lambda i: (0, 0))],
        out_shape=[jax.ShapeDtypeStruct((TOP_K, n), I32), jax.ShapeDtypeStruct((TOP_K, n), F32),
                   jax.ShapeDtypeStruct((TOP_K, n), I32), jax.ShapeDtypeStruct((e, 128), I32)],
        scratch_shapes=[pltpu.VMEM((e, 128), F32)],
        compiler_params=_cparams(("arbitrary",)),
        name="route",
    )(logits, router_bias.reshape(e, 1), tri)


def _dest_kernel(te_ref, rk_ref, ps_ref, d_ref):
    te = te_ref[...]
    e = N_EXPERTS
    tm = te.shape[1]
    eidx = lax.broadcasted_iota(I32, (e, tm), 0)
    ps = ps_ref[...]
    rows = []
    for k in range(TOP_K):
        rows.append(jnp.sum(jnp.where(eidx == te[k:k + 1, :], ps, 0), axis=0, keepdims=True))
    d_ref[...] = jnp.concatenate(rows, axis=0) + rk_ref[...]


def _dest(top_e, rank, pad_starts, tm):
    n = top_e.shape[1]
    return pl.pallas_call(
        _dest_kernel,
        grid=(n // tm,),
        in_specs=[pl.BlockSpec((TOP_K, tm), lambda i: (0, i)),
                  pl.BlockSpec((TOP_K, tm), lambda i: (0, i)),
                  pl.BlockSpec((N_EXPERTS, 1), lambda i: (0, 0))],
        out_specs=pl.BlockSpec((TOP_K, tm), lambda i: (0, i)),
        out_shape=jax.ShapeDtypeStruct((TOP_K, n), I32),
        compiler_params=_cparams(("parallel",)),
        name="dest",
    )(top_e, rank, pad_starts.reshape(N_EXPERTS, 1))


def _invert_kernel(dest_ref, inv_ref):
    def body(j, c):
        inv_ref[dest_ref[j]] = j
        return c

    lax.fori_loop(0, dest_ref.shape[0], body, 0, unroll=32)


def _invert_permutation(dest_flat):
    m = dest_flat.shape[0]
    return pl.pallas_call(
        _invert_kernel,
        in_specs=[pl.BlockSpec(memory_space=pltpu.SMEM)],
        out_specs=pl.BlockSpec(memory_space=pltpu.SMEM),
        out_shape=jax.ShapeDtypeStruct((m,), I32),
        name="invert_perm",
    )(dest_flat)


def _expert_ffn(xb, wgb_ref, wub_ref, wdb_ref):
    hg = jnp.dot(xb, wgb_ref[...], preferred_element_type=F32)
    hu = jnp.dot(xb, wub_ref[...], preferred_element_type=F32)
    hh = (hg * jax.nn.sigmoid(hg)) * hu
    return jnp.dot(hh.astype(BF16), wdb_ref[...], preferred_element_type=F32)


def _expert_kernel(vt_ref, ve_ref, lo_ref, hi_ref, inv_ref, nxt_ref, e0_ref,
                   x_hbm, wg_hbm, wu_hbm, wd_hbm, out_hbm,
                   wgf_ref, wuf_ref, wdf_ref, wgb_ref, wub_ref, wdb_ref, xbuf, obuf, xb_ref, cur_ref,
                   gsem, ssem, wsem, *, n_tok, n_tiles):
    v = pl.program_id(0)
    rows = xbuf.shape[1]
    tile = vt_ref[v]
    lo = lo_ref[v]
    hi = hi_ref[v]
    nonempty = hi > lo
    first = jnp.logical_and(nonempty, lo == 0)
    slot = lax.rem(tile, GATHER_SLOTS)
    oslot = tile % 2
    ahead = GATHER_SLOTS - 1

    def gather_row(t, r):
        src_tile = jnp.minimum(t, n_tiles - 1)
        tok = inv_ref[src_tile * rows + r] & (n_tok - 1)
        s = lax.rem(t, GATHER_SLOTS)
        return pltpu.make_async_copy(x_hbm.at[pl.ds(tok, 1)], xbuf.at[s, pl.ds(r, 1)], gsem.at[s])

    def tile_writeback(t):
        s = t % 2
        r0 = pl.multiple_of(t * rows, rows)
        return pltpu.make_async_copy(obuf.at[s], out_hbm.at[pl.ds(r0, rows)], ssem.at[s])

    def wait_gathered_tile(s):
        pltpu.make_async_copy(x_hbm.at[pl.ds(0, rows)], xbuf.at[s], gsem.at[s]).wait()

    def weight_copies(e):
        return (pltpu.make_async_copy(wg_hbm.at[e], wgf_ref, wsem.at[0]),
                pltpu.make_async_copy(wu_hbm.at[e], wuf_ref, wsem.at[1]),
                pltpu.make_async_copy(wd_hbm.at[e], wdf_ref, wsem.at[2]))

    @pl.when(v == 0)
    def _():
        cur_ref[0] = -1
        for c in weight_copies(e0_ref[0]):
            c.start()

    @pl.when(jnp.logical_and(nonempty, cur_ref[0] != ve_ref[v]))
    def _():
        e = ve_ref[v]
        for c in weight_copies(e):
            c.wait()
        wgb_ref[...] = wgf_ref[...].astype(BF16)
        wub_ref[...] = wuf_ref[...].astype(BF16)
        wdb_ref[...] = wdf_ref[...].astype(BF16)
        cur_ref[0] = e
        ne = nxt_ref[e]

        @pl.when(ne >= 0)
        def _():
            for c in weight_copies(ne):
                c.start()

    @pl.when(jnp.logical_and(first, tile == 0))
    def _():
        def issue(i, c):
            gather_row(i // rows, i % rows).start(priority=ROW_COPY_PRIORITY)
            return c

        lax.fori_loop(0, ahead * rows, issue, 0)

    @pl.when(jnp.logical_and(first, tile >= 2))
    def _():
        tile_writeback(tile - 2).wait()

    @pl.when(first)
    def _():
        wait_gathered_tile(slot)

    half = rows // 2

    @pl.when(jnp.logical_and(first, hi > half))
    def _():
        xb_ref[...] = xbuf[slot].astype(BF16)
        for r in range(rows):
            gather_row(tile + ahead, r).start(priority=ROW_COPY_PRIORITY)
        obuf[oslot] = _expert_ffn(xb_ref[...], wgb_ref, wub_ref, wdb_ref)

    @pl.when(jnp.logical_and(first, hi <= half))
    def _():
        xb_ref[0:half, :] = xbuf[slot, 0:half, :].astype(BF16)
        for r in range(rows):
            gather_row(tile + ahead, r).start(priority=ROW_COPY_PRIORITY)
        obuf[oslot, 0:half, :] = _expert_ffn(xb_ref[0:half, :], wgb_ref, wub_ref, wdb_ref)
        obuf[oslot, half:rows, :] = jnp.zeros((rows - half, obuf.shape[2]), F32)

    @pl.when(jnp.logical_and(nonempty, jnp.logical_and(lo > 0, lo < half)))
    def _():
        y = _expert_ffn(xbuf[slot].astype(BF16), wgb_ref, wub_ref, wdb_ref)
        row = lax.broadcasted_iota(I32, (rows, 1), 0)
        mine = jnp.logical_and(row >= lo, row < hi)
        obuf[oslot] = jnp.where(mine, y, obuf[oslot])

    @pl.when(jnp.logical_and(nonempty, lo >= half))
    def _():
        y = _expert_ffn(xbuf[slot, half:rows, :].astype(BF16), wgb_ref, wub_ref, wdb_ref)
        row = half + lax.broadcasted_iota(I32, (rows - half, 1), 0)
        mine = jnp.logical_and(row >= lo, row < hi)
        obuf[oslot, half:rows, :] = jnp.where(mine, y, obuf[oslot, half:rows, :])

    @pl.when(jnp.logical_and(nonempty, hi == rows))
    def _():
        tile_writeback(tile).start()

    @pl.when(v == pl.num_programs(0) - 1)
    def _():
        last = n_tiles - 1
        tile_writeback(last - 1).wait()
        tile_writeback(last).wait()
        for t in range(last + 1, last + 1 + ahead):
            wait_gathered_tile(t % GATHER_SLOTS)


def _experts(x1, inv, visit_tile, visit_expert, visit_lo, visit_hi, next_expert, first_expert,
             w_gate, w_up, w_down):
    n_tok, d = x1.shape
    m = inv.shape[0]
    de = w_gate.shape[2]
    n_tiles = m // EXPERT_TILE
    any_spec = pl.BlockSpec(memory_space=pl.ANY)
    grid_spec = pltpu.PrefetchScalarGridSpec(
        num_scalar_prefetch=7,
        grid=(visit_tile.shape[0],),
        in_specs=[any_spec, any_spec, any_spec, any_spec],
        out_specs=any_spec,
        scratch_shapes=[pltpu.VMEM((d, de), F32), pltpu.VMEM((d, de), F32), pltpu.VMEM((de, d), F32),
                        pltpu.VMEM((d, de), BF16), pltpu.VMEM((d, de), BF16), pltpu.VMEM((de, d), BF16),
                        pltpu.VMEM((GATHER_SLOTS, EXPERT_TILE, d), F32), pltpu.VMEM((2, EXPERT_TILE, d), F32),
                        pltpu.VMEM((EXPERT_TILE, d), BF16), pltpu.SMEM((1,), I32),
                        pltpu.SemaphoreType.DMA((GATHER_SLOTS,)), pltpu.SemaphoreType.DMA((2,)),
                        pltpu.SemaphoreType.DMA((3,))],
    )
    return pl.pallas_call(
        functools.partial(_expert_kernel, n_tok=n_tok, n_tiles=n_tiles),
        grid_spec=grid_spec,
        out_shape=jax.ShapeDtypeStruct((m, d), F32),
        compiler_params=_cparams(("arbitrary",)),
        name="experts",
    )(visit_tile, visit_expert, visit_lo, visit_hi, inv, next_expert, first_expert, x1, w_gate, w_up, w_down)


def _combine_kernel(dcur_ref, dnxt_ref, x1_ref, w_ref, ys_hbm, sg_ref, su_ref, sd_ref, g_ref, b_ref, o_ref,
                    *scratch):
    bufs = scratch[:COMBINE_PHASES]
    sem = scratch[COMBINE_PHASES]
    j = pl.program_id(0)
    tq = bufs[0].shape[1]
    ahead = 2

    def row_copy(dref, col, p, k, t):
        return pltpu.make_async_copy(ys_hbm.at[pl.ds(dref[k, col + t], 1)], bufs[p].at[k, pl.ds(t, 1)], sem.at[p])

    def wait_phase(p):
        pltpu.make_async_copy(bufs[p], bufs[p], sem.at[p]).wait()

    @pl.when(j == 0)
    def _():
        for p in range(ahead):
            def issue(t, c, p=p):
                for k in range(TOP_K):
                    row_copy(dcur_ref, p * tq, p, k, t).start(priority=ROW_COPY_PRIORITY)
                return c

            lax.fori_loop(0, tq, issue, 0)

    x1 = x1_ref[...]
    xb = x1.astype(BF16)
    hg = jnp.dot(xb, sg_ref[...], preferred_element_type=F32)
    hu = jnp.dot(xb, su_ref[...], preferred_element_type=F32)
    hh = (hg * jax.nn.sigmoid(hg)) * hu
    shared = jnp.dot(hh.astype(BF16), sd_ref[...], preferred_element_type=F32)
    base = ALPHA * x1 + shared
    w = w_ref[...]

    for p in range(COMBINE_PHASES):
        wait_phase(p)
        q = p + ahead
        dref, qq = (dcur_ref, q) if q < COMBINE_PHASES else (dnxt_ref, q - COMBINE_PHASES)
        for t in range(tq):
            for k in range(TOP_K):
                row_copy(dref, qq * tq, q % COMBINE_PHASES, k, t).start(priority=ROW_COPY_PRIORITY)
        rs = slice(p * tq, (p + 1) * tq)
        routed = bufs[p][0] * w[rs, 0:1]
        for k in range(1, TOP_K):
            routed = routed + bufs[p][k] * w[rs, k:k + 1]
        o_ref[rs, :] = _layer_norm(base[rs, :] + routed, g_ref[...], b_ref[...])

    @pl.when(j == pl.num_programs(0) - 1)
    def _():
        for p in range(ahead):
            wait_phase(p)


def _combine(x1, dest, w_tok, ys, s_gate, s_up, s_down, ln_g, ln_b, tq):
    n, d = x1.shape
    de = s_gate.shape[1]
    tm = COMBINE_PHASES * tq
    steps = n // tm
    full = lambda shape: pl.BlockSpec(shape, lambda i: (0,) * len(shape))
    return pl.pallas_call(
        _combine_kernel,
        grid=(steps,),
        in_specs=[pl.BlockSpec((TOP_K, tm), lambda i: (0, i), memory_space=pltpu.SMEM),
                  pl.BlockSpec((TOP_K, tm), lambda i: (0, jnp.minimum(i + 1, steps - 1)), memory_space=pltpu.SMEM),
                  pl.BlockSpec((tm, d), lambda i: (i, 0)),
                  pl.BlockSpec((tm, TOP_K), lambda i: (i, 0)),
                  pl.BlockSpec(memory_space=pl.ANY),
                  full((d, de)), full((d, de)), full((de, d)), full((1, d)), full((1, d))],
        out_specs=pl.BlockSpec((tm, d), lambda i: (i, 0)),
        out_shape=jax.ShapeDtypeStruct((n, d), F32),
        scratch_shapes=[pltpu.VMEM((TOP_K, tq, d), F32) for _ in range(COMBINE_PHASES)]
        + [pltpu.SemaphoreType.DMA((COMBINE_PHASES,))],
        compiler_params=_cparams(("arbitrary",)),
        name="combine",
    )(dest, dest, x1, w_tok, ys, s_gate.astype(BF16), s_up.astype(BF16), s_down.astype(BF16),
      ln_g.reshape(1, d), ln_b.reshape(1, d))


def _visit_plan(counts, n_rows):
    e = counts.shape[0]
    n_tiles = n_rows // EXPERT_TILE
    ends = jnp.cumsum(counts)
    starts = ends - counts
    pos = jnp.sort(jnp.concatenate([jnp.arange(n_tiles, dtype=I32) * EXPERT_TILE, starts]))
    nxt = jnp.concatenate([pos[1:], jnp.full((1,), n_rows, I32)])
    tile = jnp.minimum(pos // EXPERT_TILE, n_tiles - 1)
    expert = jnp.minimum(jnp.sum((ends[None, :] <= pos[:, None]).astype(I32), axis=1), e - 1)
    ids = jnp.arange(e, dtype=I32)
    later = jnp.logical_and(ids[None, :] > ids[:, None], counts[None, :] > 0)
    next_expert = jnp.min(jnp.where(later, ids[None, :], e), axis=1)
    next_expert = jnp.where(next_expert < e, next_expert, -1).astype(I32)
    first_expert = jnp.min(jnp.where(counts > 0, ids, e - 1)).astype(I32).reshape(1)
    return starts, tile, expert, pos - tile * EXPERT_TILE, nxt - tile * EXPERT_TILE, next_expert, first_expert


def _moe(x1, logits, router_bias, e_w_gate, e_w_up, e_w_down, s_w_gate, s_w_up, s_w_down, ln_g, ln_b):
    n, d = x1.shape
    top_e, w_t, rank, counts = _route(logits, router_bias, tm=512)
    starts, v_tile, v_expert, v_lo, v_hi, next_e, first_e = _visit_plan(counts[:, 0], n * TOP_K)
    dest = _dest(top_e, rank, starts, tm=1024)
    inv = _invert_permutation(dest.reshape(n * TOP_K))
    ys = _experts(x1, inv, v_tile, v_expert, v_lo, v_hi, next_e, first_e, e_w_gate, e_w_up, e_w_down)
    return _combine(x1, dest, w_t.T, ys, s_w_gate, s_w_up, s_w_down, ln_g, ln_b, tq=64)


def _layer(x, w_in, b_gate, m_conv_w, m_conv_b, m_wq, m_wk, m_wv, m_w_if, m_b_if, m_norm_w, m_skip,
           r_conv_w, r_conv_b, r_wa, r_ba, r_wx, r_bx, r_lambda, w_pm, w_pr, w_o, ln1_g, ln1_b,
           router_w, router_bias, e_w_gate, e_w_up, e_w_down, s_w_gate, s_w_up, s_w_down, ln2_g, ln2_b):
    batch, seq, d = x.shape
    n = batch * seq
    xt = x.reshape(n, d)
    proj = _in_proj(xt.astype(BF16), w_in, bm=1024, bn=1024)
    o_xr = 2 * M_WIDTH
    o_xg = o_xr + R_WIDTH
    o_gate = o_xg + R_WIDTH
    xc, q, k, v, g, gt = _mlstm_prep(proj, seq, m_conv_w, m_conv_b, m_wq, m_wk, m_wv, m_w_if, m_b_if, tm=256)
    y_m = _mlstm(q, k, v, g, gt, xc, proj, m_norm_w, m_skip, batch, seq, MLSTM_CHUNK)
    y_r = _rglru(proj, batch, seq, o_xr, o_xg, r_conv_w, r_conv_b, r_wa, r_ba, r_wx, r_bx, r_lambda,
                 tm=512, cw=512)
    merged = _merge(y_m, y_r, w_pm, w_pr, proj, o_gate, b_gate, tm=512, bn=1024)
    x1, logits = _oproj(merged, w_o, xt, ln1_g, ln1_b, router_w, tm=256)
    out = _moe(x1, logits, router_bias, e_w_gate, e_w_up, e_w_down, s_w_gate, s_w_up, s_w_down, ln2_g, ln2_b)
    return out.reshape(batch, seq, d)


def kernel(x, w_in, b_gate, m_conv_w, m_conv_b, m_wq, m_wk, m_wv, m_w_if, m_b_if, m_norm_w, m_skip, r_conv_w, r_conv_b, r_wa, r_ba, r_wx, r_bx, r_lambda, w_pm, w_pr, w_o, ln1_g, ln1_b, router_w, router_bias, e_w_gate, e_w_up, e_w_down, s_w_gate, s_w_up, s_w_down, ln2_g, ln2_b):
    for l in range(DEPTH):
        x = _layer(x, w_in[l], b_gate[l], m_conv_w[l], m_conv_b[l], m_wq[l], m_wk[l], m_wv[l],
                   m_w_if[l], m_b_if[l], m_norm_w[l], m_skip[l], r_conv_w[l], r_conv_b[l],
                   r_wa[l], r_ba[l], r_wx[l], r_bx[l], r_lambda[l], w_pm[l], w_pr[l], w_o[l],
                   ln1_g[l], ln1_b[l], router_w[l], router_bias[l], e_w_gate[l], e_w_up[l],
                   e_w_down[l], s_w_gate[l], s_w_up[l], s_w_down[l], ln2_g[l], ln2_b[l])
    return x
```

```python
import functools
import math

import jax
import jax.numpy as jnp
from jax import lax
from jax.experimental import pallas as pl
from jax.experimental.pallas import tpu as pltpu

F32 = jnp.float32
BF16 = jnp.bfloat16
I32 = jnp.int32
U32 = jnp.uint32

D_MODEL = 2048
M_WIDTH = 2048
M_HEADS = 8
M_HEAD_DIM = 256
M_QKV_BLOCK = 4
CONV_WIDTH = 4
R_WIDTH = 2560
R_BLOCK = 256
LRU_C = 8.0
N_EXPERTS = 64
TOP_K = 8
N_GROUPS = 8
TOPK_GROUPS = 4
D_EXPERT = 512
ROUTED_SCALE = 2.5
DEPTH = 1
ALPHA = (2.0 * DEPTH) ** 0.25
LN_EPS = 1e-5

V7X_VMEM_LIMIT = 56 * 1024 * 1024
HALO = 8
MLSTM_CHUNK = 256
EXPERT_TILE = 256
NEG_INF = float("-inf")
GATE_LANES = 128
ROUTER_LANES = 128
GATHER_SLOTS = 3
COMBINE_PHASES = 4
ROW_COPY_PRIORITY = 1


def _cparams(sem, vmem=V7X_VMEM_LIMIT):
    return pltpu.CompilerParams(dimension_semantics=sem, vmem_limit_bytes=vmem)


def _inproj_kernel(a_ref, w_ref, o_ref, wb_ref):
    @pl.when(pl.program_id(1) == 0)
    def _():
        wb_ref[...] = w_ref[...].astype(BF16)

    o_ref[...] = jnp.dot(a_ref[...], wb_ref[...], preferred_element_type=F32)


def _in_proj(a, w, bm, bn):
    m, k = a.shape
    n = w.shape[1]
    return pl.pallas_call(
        _inproj_kernel,
        grid=(n // bn, m // bm),
        in_specs=[pl.BlockSpec((bm, k), lambda j, i: (i, 0)),
                  pl.BlockSpec((k, bn), lambda j, i: (0, j))],
        out_specs=pl.BlockSpec((bm, bn), lambda j, i: (i, j)),
        out_shape=jax.ShapeDtypeStruct((m, n), F32),
        scratch_shapes=[pltpu.VMEM((k, bn), BF16)],
        compiler_params=_cparams(("parallel", "arbitrary")),
        name="in_proj",
    )(a, w)


def _sigmoid(x):
    return 0.5 * jnp.tanh(0.5 * x) + 0.5


def _log_sigmoid(x):
    return jnp.minimum(x, 0.0) - jnp.log1p(jnp.exp(-jnp.abs(x)))


def _shift_rows(x3, prev_group, j):
    rot = pltpu.roll(x3, j, axis=1)
    prev = jnp.concatenate([pltpu.roll(prev_group, j, axis=1), rot[:-1]], axis=0)
    sub = lax.broadcasted_iota(I32, x3.shape, 1)
    return jnp.where(sub >= j, rot, prev)


def _causal_conv(halo, x, cw_ref, cb_ref):
    tm, c = x.shape
    x3 = x.reshape(tm // HALO, HALO, c)
    h3 = halo.reshape(1, HALO, c)
    last = CONV_WIDTH - 1
    y = cb_ref[...] + x3 * cw_ref[last:last + 1, :]
    for j in range(1, CONV_WIDTH):
        y = y + _shift_rows(x3, h3, j) * cw_ref[last - j:last - j + 1, :]
    return y


def _mprep_kernel(xm_ref, halo_ref, cw_ref, cb_ref, wq_ref, wk_ref, wv_ref, wif_ref, bif_ref,
                  xc_ref, q_ref, k_ref, v_ref, g_ref, gt_ref, *, tm, tiles_per_seq):
    i = pl.program_id(0)
    first = (i % tiles_per_seq) == 0
    halo = jnp.where(first, 0.0, halo_ref[...])
    xm = xm_ref[...]
    y = _causal_conv(halo, xm, cw_ref, cb_ref).reshape(tm, M_WIDTH)
    xc = y * _sigmoid(y)
    xc_ref[...] = xc
    xcb = xc.astype(BF16)
    xmb = xm.astype(BF16)
    nblk = M_WIDTH // M_HEAD_DIM
    for g in range(nblk):
        sl = slice(g * M_HEAD_DIM, (g + 1) * M_HEAD_DIM)
        q_ref[:, sl] = jnp.dot(xcb[:, sl], wq_ref[g], preferred_element_type=F32).astype(BF16)
        k_ref[:, sl] = jnp.dot(xcb[:, sl], wk_ref[g], preferred_element_type=F32).astype(BF16)
        v_ref[:, sl] = jnp.dot(xmb[:, sl], wv_ref[g], preferred_element_type=F32).astype(BF16)
    qb, kb, vb = q_ref[...], k_ref[...], v_ref[...]
    w = M_WIDTH
    ng = g_ref.shape[1]
    g = (jnp.dot(qb, wif_ref[0:w, :], preferred_element_type=F32)
         + jnp.dot(kb, wif_ref[w:2 * w, :], preferred_element_type=F32)
         + jnp.dot(vb, wif_ref[2 * w:3 * w, :], preferred_element_type=F32) + bif_ref[...])
    col = lax.broadcasted_iota(I32, g.shape, 1)
    g = jnp.where(col >= M_HEADS, _log_sigmoid(g), g)
    g_ref[...] = g[:, 0:ng]
    gt_ref[...] = jnp.transpose(g)[0:ng, :]


def _block_diag_kernel(w_ref, o_ref, *, bi):
    w = w_ref[0]
    group = o_ref.shape[1]
    row = lax.broadcasted_iota(I32, (group, group), 0)
    col = lax.broadcasted_iota(I32, (group, group), 1)
    dense = jnp.zeros((group, group), F32)
    for o in range(w.shape[1]):
        dense = jnp.where(col % w.shape[1] == o, w[:, o:o + 1], dense)
    o_ref[0] = jnp.where(row // bi == col // w.shape[1], dense, 0.0).astype(o_ref.dtype)


def _block_diag_dense(ws, group):
    nb, bi, bo = ws[0].shape
    slabs = len(ws) * nb * bi // group
    w3 = jnp.concatenate(ws, axis=0).reshape(slabs, group, bo)
    return pl.pallas_call(
        functools.partial(_block_diag_kernel, bi=bi),
        grid=(slabs,),
        in_specs=[pl.BlockSpec((1, group, bo), lambda i: (i, 0, 0))],
        out_specs=pl.BlockSpec((1, group, group), lambda i: (i, 0, 0)),
        out_shape=jax.ShapeDtypeStruct((slabs, group, group), BF16),
        compiler_params=_cparams(("parallel",)),
        name="block_diag",
    )(w3)


def _mlstm_prep(proj, seq, conv_w, conv_b, wq, wk, wv, w_if, b_if, tm):
    n = proj.shape[0]
    c = M_WIDTH
    nblk = c // M_HEAD_DIM
    tiles_per_seq = seq // tm
    wd = _block_diag_dense([wq, wk, wv], M_HEAD_DIM)
    ng = 2 * M_HEADS
    wif = jnp.pad(w_if, ((0, 0), (0, GATE_LANES - ng))).astype(BF16)
    bif = jnp.pad(b_if, (0, GATE_LANES - ng)).reshape(1, GATE_LANES)
    hb = tm // HALO
    full = lambda shape: pl.BlockSpec(shape, lambda i: (0,) * len(shape))
    return pl.pallas_call(
        functools.partial(_mprep_kernel, tm=tm, tiles_per_seq=tiles_per_seq),
        grid=(n // tm,),
        in_specs=[pl.BlockSpec((tm, c), lambda i: (i, 0)),
                  pl.BlockSpec((HALO, c), lambda i: (jnp.maximum(i * hb - 1, 0), 0)),
                  full((CONV_WIDTH, c)), full((1, c)),
                  pl.BlockSpec((nblk, M_HEAD_DIM, M_HEAD_DIM), lambda i: (0, 0, 0)),
                  pl.BlockSpec((nblk, M_HEAD_DIM, M_HEAD_DIM), lambda i: (1, 0, 0)),
                  pl.BlockSpec((nblk, M_HEAD_DIM, M_HEAD_DIM), lambda i: (2, 0, 0)),
                  full((3 * c, GATE_LANES)), full((1, GATE_LANES))],
        out_specs=[pl.BlockSpec((tm, c), lambda i: (i, 0)),
                   pl.BlockSpec((tm, c), lambda i: (i, 0)),
                   pl.BlockSpec((tm, c), lambda i: (i, 0)),
                   pl.BlockSpec((tm, c), lambda i: (i, 0)),
                   pl.BlockSpec((tm, ng), lambda i: (i, 0)),
                   pl.BlockSpec((ng, tm), lambda i: (0, i))],
        out_shape=[jax.ShapeDtypeStruct((n, c), F32),
                   jax.ShapeDtypeStruct((n, c), BF16),
                   jax.ShapeDtypeStruct((n, c), BF16),
                   jax.ShapeDtypeStruct((n, c), BF16),
                   jax.ShapeDtypeStruct((n, ng), F32),
                   jax.ShapeDtypeStruct((ng, n), F32)],
        compiler_params=_cparams(("parallel",)),
        name="mlstm_prep",
    )(proj, proj, conv_w, conv_b.reshape(1, c), wd, wd, wd, wif, bif)


def _mlstm_kernel(q_ref, k_ref, v_ref, g_ref, gt_ref, xc_ref, z_ref, nw_ref, sk_ref, o_ref,
                  c_ref, n_ref, m_ref, *, chunk):
    L = chunk
    hd = M_HEAD_DIM

    @pl.when(pl.program_id(1) == 0)
    def _():
        c_ref[...] = jnp.zeros_like(c_ref)
        n_ref[...] = jnp.zeros_like(n_ref)
        m_ref[...] = jnp.zeros_like(m_ref)

    rows = lax.broadcasted_iota(I32, (L, L), 0)
    cols = lax.broadcasted_iota(I32, (L, L), 1)
    causal = cols <= rows
    tril = jnp.where(causal, 1.0, 0.0).astype(F32)
    triu = jnp.where(rows <= cols, 1.0, 0.0).astype(F32)
    g = g_ref[...]
    gt = gt_ref[...]
    hi = lax.Precision.HIGHEST
    bcol_all = jnp.dot(tril, g, precision=hi, preferred_element_type=F32)
    brow_all = jnp.dot(gt, triu, precision=hi, preferred_element_type=F32)
    k_scale = hd ** -0.5
    nt = (((1,), (1,)), ((), ()))
    tn = (((0,), (0,)), ((), ()))

    for h in range(M_HEADS):
        sl = slice(h * hd, (h + 1) * hd)
        qh = q_ref[:, sl]
        kh = k_ref[:, sl]
        vh = v_ref[:, sl]
        i_col = g[:, h:h + 1]
        b_col = bcol_all[:, M_HEADS + h:M_HEADS + h + 1]
        i_row = gt[h:h + 1, :]
        b_row = brow_all[M_HEADS + h:M_HEADS + h + 1, :]
        m_prev = m_ref[h, 0:1, 0:1]
        c_prev = c_ref[h]
        n_prev = n_ref[h]

        dmat = jnp.where(causal, b_col - b_row + i_row, NEG_INF)
        a_col = b_col + m_prev
        m_row = jnp.maximum(a_col, jnp.max(dmat, axis=1, keepdims=True))
        s = lax.dot_general(qh, kh, nt, preferred_element_type=F32) * k_scale
        s = s * jnp.exp(dmat - m_row)
        inter = jnp.exp(a_col - m_row)
        num = inter * jnp.dot(qh, c_prev.astype(BF16), preferred_element_type=F32) \
            + jnp.dot(s.astype(BF16), vh, preferred_element_type=F32)
        qn = jnp.sum(qh.astype(F32) * n_prev, axis=1, keepdims=True)
        den = inter * qn + jnp.sum(s, axis=1, keepdims=True)
        hval = num * (1.0 / jnp.maximum(jnp.abs(den), jnp.exp(-m_row)))

        mu = jnp.mean(hval, axis=1, keepdims=True)
        cen = hval - mu
        var = jnp.mean(cen * cen, axis=1, keepdims=True)
        hn = cen * lax.rsqrt(var + LN_EPS) * nw_ref[:, sl]
        zz = z_ref[:, sl]
        o_ref[:, sl] = ((hn + sk_ref[:, sl] * xc_ref[:, sl]) * (zz * _sigmoid(zz))).astype(o_ref.dtype)

        b_last = b_col[L - 1:L, :]
        w_log = b_last - b_col + i_col
        m_new = jnp.maximum(b_last + m_prev, jnp.max(w_log, axis=0, keepdims=True))
        decay = jnp.exp(b_last + m_prev - m_new)
        kw = kh.astype(F32) * (jnp.exp(w_log - m_new) * k_scale)
        c_ref[h] = decay * c_prev + lax.dot_general(kw.astype(BF16), vh, tn, preferred_element_type=F32)
        n_ref[h] = decay * n_prev + jnp.sum(kw, axis=0, keepdims=True)
        m_ref[h] = jnp.broadcast_to(m_new, m_ref.shape[1:])


def _mlstm(q, k, v, g, gt, xc, proj, norm_w, skip, batch, seq, chunk):
    n, c = q.shape
    nc = seq // chunk
    ng = 2 * M_HEADS
    zcol = M_WIDTH // c
    row = lambda b, j: (b * nc + j, 0)
    return pl.pallas_call(
        functools.partial(_mlstm_kernel, chunk=chunk),
        grid=(batch, nc),
        in_specs=[pl.BlockSpec((chunk, c), row), pl.BlockSpec((chunk, c), row), pl.BlockSpec((chunk, c), row),
                  pl.BlockSpec((chunk, ng), row),
                  pl.BlockSpec((ng, chunk), lambda b, j: (0, b * nc + j)),
                  pl.BlockSpec((chunk, c), row),
                  pl.BlockSpec((chunk, c), lambda b, j: (b * nc + j, zcol)),
                  pl.BlockSpec((1, c), lambda b, j: (0, 0)),
                  pl.BlockSpec((1, c), lambda b, j: (0, 0))],
        out_specs=pl.BlockSpec((chunk, c), row),
        out_shape=jax.ShapeDtypeStruct((n, c), BF16),
        scratch_shapes=[pltpu.VMEM((M_HEADS, M_HEAD_DIM, M_HEAD_DIM), F32),
                        pltpu.VMEM((M_HEADS, 1, M_HEAD_DIM), F32),
                        pltpu.VMEM((M_HEADS, 8, 128), F32)],
        compiler_params=_cparams(("parallel", "arbitrary")),
        name="mlstm_chunk",
    )(q, k, v, g, gt, xc, proj, norm_w.reshape(1, c), skip.reshape(1, c))


def _gelu_tanh(x):
    return 0.5 * x * (1.0 + jnp.tanh(math.sqrt(2.0 / math.pi) * (x + 0.044715 * (x * x * x))))


def _rglru_kernel(xr_ref, halo_ref, xg_ref, cw_ref, cb_ref, wa_ref, ba_ref, wx_ref, bx_ref, lam_ref,
                  o_ref, a_ref, b_ref, h_ref, *, tm, cw):
    t = pl.program_id(2)

    @pl.when(t == 0)
    def _():
        h_ref[...] = jnp.zeros_like(h_ref)

    halo = jnp.where(t == 0, 0.0, halo_ref[...])
    xc = _causal_conv(halo, xr_ref[...], cw_ref, cb_ref).reshape(tm, cw)
    xcb = xc.astype(BF16)
    nblk = cw // R_BLOCK
    ra = []
    rx = []
    for g in range(nblk):
        sl = slice(g * R_BLOCK, (g + 1) * R_BLOCK)
        ra.append(jnp.dot(xcb[:, sl], wa_ref[g], preferred_element_type=F32))
        rx.append(jnp.dot(xcb[:, sl], wx_ref[g], preferred_element_type=F32))
    r = _sigmoid(jnp.concatenate(ra, axis=1) + ba_ref[...])
    ig = _sigmoid(jnp.concatenate(rx, axis=1) + bx_ref[...])
    nl = -lam_ref[...]
    softplus = jnp.maximum(nl, 0.0) + jnp.log1p(jnp.exp(-jnp.abs(nl)))
    a = jnp.exp((-LRU_C * softplus) * r)
    v = 1.0 - a * a
    b = jnp.where(v > 0.0, v * lax.rsqrt(v), 0.0) * (ig * xc)

    a = a.reshape(tm // 8, 8, cw)
    b = b.reshape(tm // 8, 8, cw)
    sub = lax.broadcasted_iota(I32, a.shape, 1)
    for d in (1, 2, 4):
        keep = sub >= d
        a_sh = pltpu.roll(a, d, axis=1)
        b_sh = pltpu.roll(b, d, axis=1)
        b = jnp.where(keep, a * b_sh + b, b)
        a = jnp.where(keep, a * a_sh, a)
    a_ref[...] = a.reshape(tm, cw)
    b_ref[...] = b.reshape(tm, cw)

    def body(g, h):
        r0 = pl.multiple_of(g * 8, 8)
        hh = b_ref[pl.ds(r0, 8), :] + a_ref[pl.ds(r0, 8), :] * h
        b_ref[pl.ds(r0, 8), :] = hh
        return hh[7:8, :]

    h_ref[...] = lax.fori_loop(0, tm // 8, body, h_ref[...], unroll=8)
    o_ref[...] = (b_ref[...] * _gelu_tanh(xg_ref[...])).astype(o_ref.dtype)


def _rglru(proj, batch, seq, xr_off, xg_off, conv_w, conv_b, wa, ba, wx, bx, lam, tm, cw):
    n = proj.shape[0]
    ncol = R_WIDTH // cw
    nt = seq // tm
    per = cw // R_BLOCK
    xr_cb = xr_off // cw
    xg_cb = xg_off // cw
    hb = tm // HALO
    colv = lambda shape: pl.BlockSpec(shape, lambda b, j, t: (0, j))
    return pl.pallas_call(
        functools.partial(_rglru_kernel, tm=tm, cw=cw),
        grid=(batch, ncol, nt),
        in_specs=[pl.BlockSpec((tm, cw), lambda b, j, t: (b * nt + t, xr_cb + j)),
                  pl.BlockSpec((HALO, cw), lambda b, j, t: (jnp.maximum((b * nt + t) * hb - 1, 0), xr_cb + j)),
                  pl.BlockSpec((tm, cw), lambda b, j, t: (b * nt + t, xg_cb + j)),
                  colv((CONV_WIDTH, cw)), colv((1, cw)),
                  pl.BlockSpec((per, R_BLOCK, R_BLOCK), lambda b, j, t: (j, 0, 0)), colv((1, cw)),
                  pl.BlockSpec((per, R_BLOCK, R_BLOCK), lambda b, j, t: (j, 0, 0)), colv((1, cw)),
                  colv((1, cw))],
        out_specs=pl.BlockSpec((tm, cw), lambda b, j, t: (b * nt + t, j)),
        out_shape=jax.ShapeDtypeStruct((n, R_WIDTH), BF16),
        scratch_shapes=[pltpu.VMEM((tm, cw), F32), pltpu.VMEM((tm, cw), F32), pltpu.VMEM((1, cw), F32)],
        compiler_params=_cparams(("parallel", "parallel", "arbitrary")),
        name="rglru",
    )(proj, proj, proj, conv_w, conv_b.reshape(1, R_WIDTH), wa.astype(BF16), ba.reshape(1, R_WIDTH),
      wx.astype(BF16), bx.reshape(1, R_WIDTH), lam.reshape(1, R_WIDTH))


def _merge_kernel(ym_ref, yr_ref, wpm_ref, wpr_ref, g0_ref, g1_ref, bg_ref, o_ref):
    g0 = jax.nn.sigmoid(g0_ref[...] + bg_ref[0:1, :])
    g1 = jax.nn.sigmoid(g1_ref[...] + bg_ref[1:2, :])
    pm = jnp.dot(ym_ref[...], wpm_ref[...], preferred_element_type=F32)
    pr = jnp.dot(yr_ref[...], wpr_ref[...], preferred_element_type=F32)
    o_ref[...] = (g0 * pm + g1 * pr).astype(o_ref.dtype)


def _merge(ym, yr, w_pm, w_pr, proj, gate_off, b_gate, tm, bn):
    n = ym.shape[0]
    d = w_pm.shape[1]
    g0_cb = gate_off // bn
    g1_cb = (gate_off + d) // bn
    return pl.pallas_call(
        _merge_kernel,
        grid=(d // bn, n // tm),
        in_specs=[pl.BlockSpec((tm, ym.shape[1]), lambda j, i: (i, 0)),
                  pl.BlockSpec((tm, yr.shape[1]), lambda j, i: (i, 0)),
                  pl.BlockSpec((w_pm.shape[0], bn), lambda j, i: (0, j)),
                  pl.BlockSpec((w_pr.shape[0], bn), lambda j, i: (0, j)),
                  pl.BlockSpec((tm, bn), lambda j, i: (i, g0_cb + j)),
                  pl.BlockSpec((tm, bn), lambda j, i: (i, g1_cb + j)),
                  pl.BlockSpec((2, bn), lambda j, i: (0, j))],
        out_specs=pl.BlockSpec((tm, bn), lambda j, i: (i, j)),
        out_shape=jax.ShapeDtypeStruct((n, d), BF16),
        compiler_params=_cparams(("parallel", "parallel")),
        name="merge",
    )(ym, yr, w_pm.astype(BF16), w_pr.astype(BF16), proj, proj, b_gate)


def _layer_norm(y, g, b):
    mu = jnp.mean(y, axis=1, keepdims=True)
    cen = y - mu
    var = jnp.mean(cen * cen, axis=1, keepdims=True)
    return cen * lax.rsqrt(var + LN_EPS) * g + b


def _split_hi_lo(x):
    hi = lax.bitcast_convert_type(lax.bitcast_convert_type(x, U32) & jnp.uint32(0xFFFF0000), F32)
    return hi.astype(BF16), (x - hi).astype(BF16)


def _oproj_kernel(mg_ref, wo_ref, x_ref, g_ref, b_ref, rwh_ref, rwl_ref, x1_ref, lg_ref):
    y = ALPHA * x_ref[...] + jnp.dot(mg_ref[...], wo_ref[...], preferred_element_type=F32)
    x1 = _layer_norm(y, g_ref[...], b_ref[...])
    x1_ref[...] = x1
    xh, xl = _split_hi_lo(x1)
    wh = rwh_ref[...]
    wl = rwl_ref[...]
    lg_ref[...] = ((jnp.dot(xh, wh, preferred_element_type=F32) + jnp.dot(xl, wl, preferred_element_type=F32))
                   + (jnp.dot(xl, wh, preferred_element_type=F32) + jnp.dot(xh, wl, preferred_element_type=F32)))


def _oproj(merged, w_o, x, ln_g, ln_b, router_w, tm):
    n, d = x.shape
    e = router_w.shape[1]
    rw_hi, rw_lo = _split_hi_lo(jnp.pad(router_w, ((0, 0), (0, ROUTER_LANES - e))))
    full = lambda shape: pl.BlockSpec(shape, lambda i: (0,) * len(shape))
    return pl.pallas_call(
        _oproj_kernel,
        grid=(n // tm,),
        in_specs=[pl.BlockSpec((tm, d), lambda i: (i, 0)), full((d, d)),
                  pl.BlockSpec((tm, d), lambda i: (i, 0)), full((1, d)), full((1, d)),
                  full((d, ROUTER_LANES)), full((d, ROUTER_LANES))],
        out_specs=[pl.BlockSpec((tm, d), lambda i: (i, 0)), pl.BlockSpec((tm, ROUTER_LANES), lambda i: (i, 0))],
        out_shape=[jax.ShapeDtypeStruct((n, d), F32), jax.ShapeDtypeStruct((n, ROUTER_LANES), F32)],
        compiler_params=_cparams(("parallel",)),
        name="out_proj_ln",
    )(merged, w_o.astype(BF16), x, ln_g.reshape(1, d), ln_b.reshape(1, d), rw_hi, rw_lo)


def _first_max(v, idx, sentinel):
    m = jnp.max(v, axis=0, keepdims=True)
    am = jnp.min(jnp.where(v == m, idx, sentinel), axis=0, keepdims=True)
    return m, am


def _route_kernel(lg_ref, bias_ref, tri_ref, te_ref, w_ref, rk_ref, cnt_ref, carry_ref, *, tm):
    @pl.when(pl.program_id(0) == 0)
    def _():
        carry_ref[...] = jnp.zeros_like(carry_ref)

    e = N_EXPERTS
    gs = e // N_GROUPS
    scores = jax.nn.sigmoid(jnp.transpose(lg_ref[...])[0:e, :])
    biased = scores + bias_ref[...]
    sub = lax.broadcasted_iota(I32, (gs, tm), 0)
    grp_rows = []
    for g in range(N_GROUPS):
        slab = biased[g * gs:(g + 1) * gs, :]
        m1, a1 = _first_max(slab, sub, gs)
        m2 = jnp.max(jnp.where(sub == a1, NEG_INF, slab), axis=0, keepdims=True)
        grp_rows.append(m1 + m2)
    grp = jnp.concatenate(grp_rows, axis=0)
    gidx = lax.broadcasted_iota(I32, (N_GROUPS, tm), 0)
    gsel = jnp.zeros((N_GROUPS, tm), F32)
    for _ in range(TOPK_GROUPS):
        _, am = _first_max(grp, gidx, N_GROUPS)
        hit = gidx == am
        gsel = jnp.where(hit, 1.0, gsel)
        grp = jnp.where(hit, NEG_INF, grp)
    masked = jnp.concatenate(
        [jnp.where(gsel[g:g + 1, :] > 0.0, biased[g * gs:(g + 1) * gs, :], NEG_INF) for g in range(N_GROUPS)],
        axis=0)
    eidx = lax.broadcasted_iota(I32, (e, tm), 0)
    member = jnp.zeros((e, tm), F32)
    tops = []
    ws = []
    for _ in range(TOP_K):
        _, am = _first_max(masked, eidx, e)
        hit = eidx == am
        tops.append(am)
        ws.append(jnp.sum(jnp.where(hit, scores, 0.0), axis=0, keepdims=True))
        member = jnp.where(hit, 1.0, member)
        masked = jnp.where(hit, NEG_INF, masked)
    wsum = ws[0]
    for k in range(1, TOP_K):
        wsum = wsum + ws[k]
    te_ref[...] = jnp.concatenate(tops, axis=0)
    w_ref[...] = jnp.concatenate(ws, axis=0) / wsum * ROUTED_SCALE

    cum = jnp.dot(member.astype(BF16), tri_ref[...], preferred_element_type=F32)
    carry = carry_ref[:, 0:1]
    rank = carry + cum - member
    rks = []
    for k in range(TOP_K):
        rks.append(jnp.sum(jnp.where(eidx == tops[k], rank, 0.0), axis=0, keepdims=True))
    rk_ref[...] = jnp.concatenate(rks, axis=0).astype(I32)
    new_carry = carry + cum[:, tm - 1:tm]
    carry_ref[...] = jnp.broadcast_to(new_carry, carry_ref.shape)
    cnt_ref[...] = jnp.broadcast_to(new_carry, cnt_ref.shape).astype(I32)


def _route(logits, router_bias, tm):
    n = logits.shape[0]
    e = router_bias.shape[0]
    tri = jnp.triu(jnp.ones((tm, tm), F32)).astype(BF16)
    return pl.pallas_call(
        functools.partial(_route_kernel, tm=tm),
        grid=(n // tm,),
        in_specs=[pl.BlockSpec((tm, ROUTER_LANES), lambda i: (i, 0)),
                  pl.BlockSpec((e, 1), lambda i: (0, 0)),
                  pl.BlockSpec((tm, tm), lambda i: (0, 0))],
        out_specs=[pl.BlockSpec((TOP_K, tm), lambda i: (0, i)),
                   pl.BlockSpec((TOP_K, tm), lambda i: (0, i)),
                   pl.BlockSpec((TOP_K, tm), lambda i: (0, i)),
                   pl.BlockSpec((e, 128), lambda i: (0, 0))],
        out_shape=[jax.ShapeDtypeStruct((TOP_K, n), I32), jax.ShapeDtypeStruct((TOP_K, n), F32),
                   jax.ShapeDtypeStruct((TOP_K, n), I32), jax.ShapeDtypeStruct((e, 128), I32)],
        scratch_shapes=[pltpu.VMEM((e, 128), F32)],
        compiler_params=_cparams(("arbitrary",)),
        name="route",
    )(logits, router_bias.reshape(e, 1), tri)


def _dest_kernel(te_ref, rk_ref, ps_ref, d_ref):
    te = te_ref[...]
    e = N_EXPERTS
    tm = te.shape[1]
    eidx = lax.broadcasted_iota(I32, (e, tm), 0)
    ps = ps_ref[...]
    rows = []
    for k in range(TOP_K):
        rows.append(jnp.sum(jnp.where(eidx == te[k:k + 1, :], ps, 0), axis=0, keepdims=True))
    d_ref[...] = jnp.concatenate(rows, axis=0) + rk_ref[...]


def _dest(top_e, rank, pad_starts, tm):
    n = top_e.shape[1]
    return pl.pallas_call(
        _dest_kernel,
        grid=(n // tm,),
        in_specs=[pl.BlockSpec((TOP_K, tm), lambda i: (0, i)),
                  pl.BlockSpec((TOP_K, tm), lambda i: (0, i)),
                  pl.BlockSpec((N_EXPERTS, 1), lambda i: (0, 0))],
        out_specs=pl.BlockSpec((TOP_K, tm), lambda i: (0, i)),
        out_shape=jax.ShapeDtypeStruct((TOP_K, n), I32),
        compiler_params=_cparams(("parallel",)),
        name="dest",
    )(top_e, rank, pad_starts.reshape(N_EXPERTS, 1))


def _invert_kernel(dest_ref, inv_ref):
    def body(j, c):
        inv_ref[dest_ref[j]] = j
        return c

    lax.fori_loop(0, dest_ref.shape[0], body, 0, unroll=32)


def _invert_permutation(dest_flat):
    m = dest_flat.shape[0]
    return pl.pallas_call(
        _invert_kernel,
        in_specs=[pl.BlockSpec(memory_space=pltpu.SMEM)],
        out_specs=pl.BlockSpec(memory_space=pltpu.SMEM),
        out_shape=jax.ShapeDtypeStruct((m,), I32),
        name="invert_perm",
    )(dest_flat)


def _expert_ffn(xb, wgb_ref, wub_ref, wdb_ref):
    hg = jnp.dot(xb, wgb_ref[...], preferred_element_type=F32)
    hu = jnp.dot(xb, wub_ref[...], preferred_element_type=F32)
    hh = (hg * jax.nn.sigmoid(hg)) * hu
    return jnp.dot(hh.astype(BF16), wdb_ref[...], preferred_element_type=F32)


def _expert_kernel(vt_ref, ve_ref, lo_ref, hi_ref, inv_ref, nxt_ref, e0_ref,
                   x_hbm, wg_hbm, wu_hbm, wd_hbm, out_hbm,
                   wgf_ref, wuf_ref, wdf_ref, wgb_ref, wub_ref, wdb_ref, xbuf, obuf, xb_ref, cur_ref,
                   gsem, ssem, wsem, *, n_tok, n_tiles):
    v = pl.program_id(0)
    rows = xbuf.shape[1]
    tile = vt_ref[v]
    lo = lo_ref[v]
    hi = hi_ref[v]
    nonempty = hi > lo
    first = jnp.logical_and(nonempty, lo == 0)
    slot = lax.rem(tile, GATHER_SLOTS)
    oslot = tile % 2
    ahead = GATHER_SLOTS - 1

    def gather_row(t, r):
        src_tile = jnp.minimum(t, n_tiles - 1)
        tok = inv_ref[src_tile * rows + r] & (n_tok - 1)
        s = lax.rem(t, GATHER_SLOTS)
        return pltpu.make_async_copy(x_hbm.at[pl.ds(tok, 1)], xbuf.at[s, pl.ds(r, 1)], gsem.at[s])

    def tile_writeback(t):
        s = t % 2
        r0 = pl.multiple_of(t * rows, rows)
        return pltpu.make_async_copy(obuf.at[s], out_hbm.at[pl.ds(r0, rows)], ssem.at[s])

    def wait_gathered_tile(s):
        pltpu.make_async_copy(x_hbm.at[pl.ds(0, rows)], xbuf.at[s], gsem.at[s]).wait()

    def weight_copies(e):
        return (pltpu.make_async_copy(wg_hbm.at[e], wgf_ref, wsem.at[0]),
                pltpu.make_async_copy(wu_hbm.at[e], wuf_ref, wsem.at[1]),
                pltpu.make_async_copy(wd_hbm.at[e], wdf_ref, wsem.at[2]))

    @pl.when(v == 0)
    def _():
        cur_ref[0] = -1
        for c in weight_copies(e0_ref[0]):
            c.start()

    @pl.when(jnp.logical_and(nonempty, cur_ref[0] != ve_ref[v]))
    def _():
        e = ve_ref[v]
        for c in weight_copies(e):
            c.wait()
        wgb_ref[...] = wgf_ref[...].astype(BF16)
        wub_ref[...] = wuf_ref[...].astype(BF16)
        wdb_ref[...] = wdf_ref[...].astype(BF16)
        cur_ref[0] = e
        ne = nxt_ref[e]

        @pl.when(ne >= 0)
        def _():
            for c in weight_copies(ne):
                c.start()

    @pl.when(jnp.logical_and(first, tile == 0))
    def _():
        def issue(i, c):
            gather_row(i // rows, i % rows).start(priority=ROW_COPY_PRIORITY)
            return c

        lax.fori_loop(0, ahead * rows, issue, 0)

    @pl.when(jnp.logical_and(first, tile >= 2))
    def _():
        tile_writeback(tile - 2).wait()

    @pl.when(first)
    def _():
        wait_gathered_tile(slot)

    half = rows // 2

    @pl.when(jnp.logical_and(first, hi > half))
    def _():
        xb_ref[...] = xbuf[slot].astype(BF16)
        for r in range(rows):
            gather_row(tile + ahead, r).start(priority=ROW_COPY_PRIORITY)
        obuf[oslot] = _expert_ffn(xb_ref[...], wgb_ref, wub_ref, wdb_ref)

    @pl.when(jnp.logical_and(first, hi <= half))
    def _():
        xb_ref[0:half, :] = xbuf[slot, 0:half, :].astype(BF16)
        for r in range(rows):
            gather_row(tile + ahead, r).start(priority=ROW_COPY_PRIORITY)
        obuf[oslot, 0:half, :] = _expert_ffn(xb_ref[0:half, :], wgb_ref, wub_ref, wdb_ref)
        obuf[oslot, half:rows, :] = jnp.zeros((rows - half, obuf.shape[2]), F32)

    @pl.when(jnp.logical_and(nonempty, jnp.logical_and(lo > 0, lo < half)))
    def _():
        y = _expert_ffn(xbuf[slot].astype(BF16), wgb_ref, wub_ref, wdb_ref)
        row = lax.broadcasted_iota(I32, (rows, 1), 0)
        mine = jnp.logical_and(row >= lo, row < hi)
        obuf[oslot] = jnp.where(mine, y, obuf[oslot])

    @pl.when(jnp.logical_and(nonempty, lo >= half))
    def _():
        y = _expert_ffn(xbuf[slot, half:rows, :].astype(BF16), wgb_ref, wub_ref, wdb_ref)
        row = half + lax.broadcasted_iota(I32, (rows - half, 1), 0)
        mine = jnp.logical_and(row >= lo, row < hi)
        obuf[oslot, half:rows, :] = jnp.where(mine, y, obuf[oslot, half:rows, :])

    @pl.when(jnp.logical_and(nonempty, hi == rows))
    def _():
        tile_writeback(tile).start()

    @pl.when(v == pl.num_programs(0) - 1)
    def _():
        last = n_tiles - 1
        tile_writeback(last - 1).wait()
        tile_writeback(last).wait()
        for t in range(last + 1, last + 1 + ahead):
            wait_gathered_tile(t % GATHER_SLOTS)


def _experts(x1, inv, visit_tile, visit_expert, visit_lo, visit_hi, next_expert, first_expert,
             w_gate, w_up, w_down):
    n_tok, d = x1.shape
    m = inv.shape[0]
    de = w_gate.shape[2]
    n_tiles = m // EXPERT_TILE
    any_spec = pl.BlockSpec(memory_space=pl.ANY)
    grid_spec = pltpu.PrefetchScalarGridSpec(
        num_scalar_prefetch=7,
        grid=(visit_tile.shape[0],),
        in_specs=[any_spec, any_spec, any_spec, any_spec],
        out_specs=any_spec,
        scratch_shapes=[pltpu.VMEM((d, de), F32), pltpu.VMEM((d, de), F32), pltpu.VMEM((de, d), F32),
                        pltpu.VMEM((d, de), BF16), pltpu.VMEM((d, de), BF16), pltpu.VMEM((de, d), BF16),
                        pltpu.VMEM((GATHER_SLOTS, EXPERT_TILE, d), F32), pltpu.VMEM((2, EXPERT_TILE, d), F32),
                        pltpu.VMEM((EXPERT_TILE, d), BF16), pltpu.SMEM((1,), I32),
                        pltpu.SemaphoreType.DMA((GATHER_SLOTS,)), pltpu.SemaphoreType.DMA((2,)),
                        pltpu.SemaphoreType.DMA((3,))],
    )
    return pl.pallas_call(
        functools.partial(_expert_kernel, n_tok=n_tok, n_tiles=n_tiles),
        grid_spec=grid_spec,
        out_shape=jax.ShapeDtypeStruct((m, d), F32),
        compiler_params=_cparams(("arbitrary",)),
        name="experts",
    )(visit_tile, visit_expert, visit_lo, visit_hi, inv, next_expert, first_expert, x1, w_gate, w_up, w_down)


def _combine_kernel(dcur_ref, dnxt_ref, x1_ref, w_ref, ys_hbm, sg_ref, su_ref, sd_ref, g_ref, b_ref, o_ref,
                    *scratch):
    bufs = scratch[:COMBINE_PHASES]
    sem = scratch[COMBINE_PHASES]
    j = pl.program_id(0)
    tq = bufs[0].shape[1]
    ahead = 2

    def row_copy(dref, col, p, k, t):
        return pltpu.make_async_copy(ys_hbm.at[pl.ds(dref[k, col + t], 1)], bufs[p].at[k, pl.ds(t, 1)], sem.at[p])

    def wait_phase(p):
        pltpu.make_async_copy(bufs[p], bufs[p], sem.at[p]).wait()

    @pl.when(j == 0)
    def _():
        for p in range(ahead):
            def issue(t, c, p=p):
                for k in range(TOP_K):
                    row_copy(dcur_ref, p * tq, p, k, t).start(priority=ROW_COPY_PRIORITY)
                return c

            lax.fori_loop(0, tq, issue, 0)

    x1 = x1_ref[...]
    xb = x1.astype(BF16)
    hg = jnp.dot(xb, sg_ref[...], preferred_element_type=F32)
    hu = jnp.dot(xb, su_ref[...], preferred_element_type=F32)
    hh = (hg * jax.nn.sigmoid(hg)) * hu
    shared = jnp.dot(hh.astype(BF16), sd_ref[...], preferred_element_type=F32)
    base = ALPHA * x1 + shared
    w = w_ref[...]

    for p in range(COMBINE_PHASES):
        wait_phase(p)
        q = p + ahead
        dref, qq = (dcur_ref, q) if q < COMBINE_PHASES else (dnxt_ref, q - COMBINE_PHASES)
        for t in range(tq):
            for k in range(TOP_K):
                row_copy(dref, qq * tq, q % COMBINE_PHASES, k, t).start(priority=ROW_COPY_PRIORITY)
        rs = slice(p * tq, (p + 1) * tq)
        routed = bufs[p][0] * w[rs, 0:1]
        for k in range(1, TOP_K):
            routed = routed + bufs[p][k] * w[rs, k:k + 1]
        o_ref[rs, :] = _layer_norm(base[rs, :] + routed, g_ref[...], b_ref[...])

    @pl.when(j == pl.num_programs(0) - 1)
    def _():
        for p in range(ahead):
            wait_phase(p)


def _combine(x1, dest, w_tok, ys, s_gate, s_up, s_down, ln_g, ln_b, tq):
    n, d = x1.shape
    de = s_gate.shape[1]
    tm = COMBINE_PHASES * tq
    steps = n // tm
    full = lambda shape: pl.BlockSpec(shape, lambda i: (0,) * len(shape))
    return pl.pallas_call(
        _combine_kernel,
        grid=(steps,),
        in_specs=[pl.BlockSpec((TOP_K, tm), lambda i: (0, i), memory_space=pltpu.SMEM),
                  pl.BlockSpec((TOP_K, tm), lambda i: (0, jnp.minimum(i + 1, steps - 1)), memory_space=pltpu.SMEM),
                  pl.BlockSpec((tm, d), lambda i: (i, 0)),
                  pl.BlockSpec((tm, TOP_K), lambda i: (i, 0)),
                  pl.BlockSpec(memory_space=pl.ANY),
                  full((d, de)), full((d, de)), full((de, d)), full((1, d)), full((1, d))],
        out_specs=pl.BlockSpec((tm, d), lambda i: (i, 0)),
        out_shape=jax.ShapeDtypeStruct((n, d), F32),
        scratch_shapes=[pltpu.VMEM((TOP_K, tq, d), F32) for _ in range(COMBINE_PHASES)]
        + [pltpu.SemaphoreType.DMA((COMBINE_PHASES,))],
        compiler_params=_cparams(("arbitrary",)),
        name="combine",
    )(dest, dest, x1, w_tok, ys, s_gate.astype(BF16), s_up.astype(BF16), s_down.astype(BF16),
      ln_g.reshape(1, d), ln_b.reshape(1, d))


def _visit_plan(counts, n_rows):
    e = counts.shape[0]
    n_tiles = n_rows // EXPERT_TILE
    ends = jnp.cumsum(counts)
    starts = ends - counts
    pos = jnp.sort(jnp.concatenate([jnp.arange(n_tiles, dtype=I32) * EXPERT_TILE, starts]))
    nxt = jnp.concatenate([pos[1:], jnp.full((1,), n_rows, I32)])
    tile = jnp.minimum(pos // EXPERT_TILE, n_tiles - 1)
    expert = jnp.minimum(jnp.sum((ends[None, :] <= pos[:, None]).astype(I32), axis=1), e - 1)
    ids = jnp.arange(e, dtype=I32)
    later = jnp.logical_and(ids[None, :] > ids[:, None], counts[None, :] > 0)
    next_expert = jnp.min(jnp.where(later, ids[None, :], e), axis=1)
    next_expert = jnp.where(next_expert < e, next_expert, -1).astype(I32)
    first_expert = jnp.min(jnp.where(counts > 0, ids, e - 1)).astype(I32).reshape(1)
    return starts, tile, expert, pos - tile * EXPERT_TILE, nxt - tile * EXPERT_TILE, next_expert, first_expert


def _moe(x1, logits, router_bias, e_w_gate, e_w_up, e_w_down, s_w_gate, s_w_up, s_w_down, ln_g, ln_b):
    n, d = x1.shape
    top_e, w_t, rank, counts = _route(logits, router_bias, tm=512)
    starts, v_tile, v_expert, v_lo, v_hi, next_e, first_e = _visit_plan(counts[:, 0], n * TOP_K)
    dest = _dest(top_e, rank, starts, tm=1024)
    inv = _invert_permutation(dest.reshape(n * TOP_K))
    ys = _experts(x1, inv, v_tile, v_expert, v_lo, v_hi, next_e, first_e, e_w_gate, e_w_up, e_w_down)
    return _combine(x1, dest, w_t.T, ys, s_w_gate, s_w_up, s_w_down, ln_g, ln_b, tq=64)


def _layer(x, w_in, b_gate, m_conv_w, m_conv_b, m_wq, m_wk, m_wv, m_w_if, m_b_if, m_norm_w, m_skip,
           r_conv_w, r_conv_b, r_wa, r_ba, r_wx, r_bx, r_lambda, w_pm, w_pr, w_o, ln1_g, ln1_b,
           router_w, router_bias, e_w_gate, e_w_up, e_w_down, s_w_gate, s_w_up, s_w_down, ln2_g, ln2_b):
    batch, seq, d = x.shape
    n = batch * seq
    xt = x.reshape(n, d)
    proj = _in_proj(xt.astype(BF16), w_in, bm=1024, bn=1024)
    o_xr = 2 * M_WIDTH
    o_xg = o_xr + R_WIDTH
    o_gate = o_xg + R_WIDTH
    xc, q, k, v, g, gt = _mlstm_prep(proj, seq, m_conv_w, m_conv_b, m_wq, m_wk, m_wv, m_w_if, m_b_if, tm=256)
    y_m = _mlstm(q, k, v, g, gt, xc, proj, m_norm_w, m_skip, batch, seq, MLSTM_CHUNK)
    y_r = _rglru(proj, batch, seq, o_xr, o_xg, r_conv_w, r_conv_b, r_wa, r_ba, r_wx, r_bx, r_lambda,
                 tm=2048, cw=512)
    merged = _merge(y_m, y_r, w_pm, w_pr, proj, o_gate, b_gate, tm=512, bn=1024)
    x1, logits = _oproj(merged, w_o, xt, ln1_g, ln1_b, router_w, tm=256)
    out = _moe(x1, logits, router_bias, e_w_gate, e_w_up, e_w_down, s_w_gate, s_w_up, s_w_down, ln2_g, ln2_b)
    return out.reshape(batch, seq, d)


def kernel(x, w_in, b_gate, m_conv_w, m_conv_b, m_wq, m_wk, m_wv, m_w_if, m_b_if, m_norm_w, m_skip, r_conv_w, r_conv_b, r_wa, r_ba, r_wx, r_bx, r_lambda, w_pm, w_pr, w_o, ln1_g, ln1_b, router_w, router_bias, e_w_gate, e_w_up, e_w_down, s_w_gate, s_w_up, s_w_down, ln2_g, ln2_b):
    for l in range(DEPTH):
        x = _layer(x, w_in[l], b_gate[l], m_conv_w[l], m_conv_b[l], m_wq[l], m_wk[l], m_wv[l],
                   m_w_if[l], m_b_if[l], m_norm_w[l], m_skip[l], r_conv_w[l], r_conv_b[l],
                   r_wa[l], r_ba[l], r_wx[l], r_bx[l], r_lambda[l], w_pm[l], w_pr[l], w_o[l],
                   ln1_g[l], ln1_b[l], router_w[l], router_bias[l], e_w_gate[l], e_w_up[l],
                   e_w_down[l], s_w_gate[l], s_w_up[l], s_w_down[l], ln2_g[l], ln2_b[l])
    return x
```

```python
import functools
import math

import jax
import jax.numpy as jnp
from jax import lax
from jax.experimental import pallas as pl
from jax.experimental.pallas import tpu as pltpu

F32 = jnp.float32
BF16 = jnp.bfloat16
I32 = jnp.int32
U32 = jnp.uint32

M_WIDTH = 2048
M_HEADS = 8
M_HEAD_DIM = 256
CONV_WIDTH = 4
R_WIDTH = 2560
R_BLOCK = 256
LRU_C = 8.0
N_EXPERTS = 64
TOP_K = 8
N_GROUPS = 8
TOPK_GROUPS = 4
ROUTED_SCALE = 2.5
DEPTH = 1
ALPHA = (2.0 * DEPTH) ** 0.25
LN_EPS = 1e-5

V7X_VMEM_LIMIT = 56 * 1024 * 1024
HALO = 8
MLSTM_CHUNK = 256
EXPERT_TILE = 256
NEG_INF = float("-inf")
GATE_LANES = 128
ROUTER_LANES = 128
GATHER_SLOTS = 3
COMBINE_PHASES = 4
ROW_COPY_PRIORITY = 1


TILES = dict(
    in_proj_rows=1024, in_proj_cols=1024,
    mlstm_prep_rows=256,
    rglru_rows=2048, rglru_cols=512,
    merge_rows=512, merge_cols=1024,
    out_proj_rows=256,
    route_tokens=512, dest_tokens=1024,
    combine_subtile_tokens=64,
)


def _cparams(sem, vmem=V7X_VMEM_LIMIT):
    return pltpu.CompilerParams(dimension_semantics=sem, vmem_limit_bytes=vmem)


def _inproj_kernel(a_ref, w_ref, o_ref, wb_ref):
    @pl.when(pl.program_id(1) == 0)
    def _():
        wb_ref[...] = w_ref[...].astype(BF16)

    o_ref[...] = jnp.dot(a_ref[...], wb_ref[...], preferred_element_type=F32)


def _in_proj(a, w, bm, bn):
    m, k = a.shape
    n = w.shape[1]
    return pl.pallas_call(
        _inproj_kernel,
        grid=(n // bn, m // bm),
        in_specs=[pl.BlockSpec((bm, k), lambda j, i: (i, 0)),
                  pl.BlockSpec((k, bn), lambda j, i: (0, j))],
        out_specs=pl.BlockSpec((bm, bn), lambda j, i: (i, j)),
        out_shape=jax.ShapeDtypeStruct((m, n), F32),
        scratch_shapes=[pltpu.VMEM((k, bn), BF16)],
        compiler_params=_cparams(("parallel", "arbitrary")),
        name="in_proj",
    )(a, w)


def _sigmoid(x):
    return 0.5 * jnp.tanh(0.5 * x) + 0.5


def _log_sigmoid(x):
    return jnp.minimum(x, 0.0) - jnp.log1p(jnp.exp(-jnp.abs(x)))


def _shift_rows(x3, prev_group, j):
    rot = pltpu.roll(x3, j, axis=1)
    prev = jnp.concatenate([pltpu.roll(prev_group, j, axis=1), rot[:-1]], axis=0)
    sub = lax.broadcasted_iota(I32, x3.shape, 1)
    return jnp.where(sub >= j, rot, prev)


def _causal_conv(halo, x, cw_ref, cb_ref):
    tm, c = x.shape
    x3 = x.reshape(tm // HALO, HALO, c)
    h3 = halo.reshape(1, HALO, c)
    last = CONV_WIDTH - 1
    y = cb_ref[...] + x3 * cw_ref[last:last + 1, :]
    for j in range(1, CONV_WIDTH):
        y = y + _shift_rows(x3, h3, j) * cw_ref[last - j:last - j + 1, :]
    return y


def _mprep_kernel(xm_ref, halo_ref, cw_ref, cb_ref, wq_ref, wk_ref, wv_ref, wif_ref, bif_ref,
                  xc_ref, q_ref, k_ref, v_ref, g_ref, gt_ref, *, tm, tiles_per_seq):
    i = pl.program_id(0)
    first = (i % tiles_per_seq) == 0
    halo = jnp.where(first, 0.0, halo_ref[...])
    xm = xm_ref[...]
    y = _causal_conv(halo, xm, cw_ref, cb_ref).reshape(tm, M_WIDTH)
    xc = y * _sigmoid(y)
    xc_ref[...] = xc
    xcb = xc.astype(BF16)
    xmb = xm.astype(BF16)
    nblk = M_WIDTH // M_HEAD_DIM
    for g in range(nblk):
        sl = slice(g * M_HEAD_DIM, (g + 1) * M_HEAD_DIM)
        q_ref[:, sl] = jnp.dot(xcb[:, sl], wq_ref[g], preferred_element_type=F32).astype(BF16)
        k_ref[:, sl] = jnp.dot(xcb[:, sl], wk_ref[g], preferred_element_type=F32).astype(BF16)
        v_ref[:, sl] = jnp.dot(xmb[:, sl], wv_ref[g], preferred_element_type=F32).astype(BF16)
    qb, kb, vb = q_ref[...], k_ref[...], v_ref[...]
    w = M_WIDTH
    ng = g_ref.shape[1]
    g = (jnp.dot(qb, wif_ref[0:w, :], preferred_element_type=F32)
         + jnp.dot(kb, wif_ref[w:2 * w, :], preferred_element_type=F32)
         + jnp.dot(vb, wif_ref[2 * w:3 * w, :], preferred_element_type=F32) + bif_ref[...])
    col = lax.broadcasted_iota(I32, g.shape, 1)
    g = jnp.where(col >= M_HEADS, _log_sigmoid(g), g)
    g_ref[...] = g[:, 0:ng]
    gt_ref[...] = jnp.transpose(g)[0:ng, :]


def _block_diag_kernel(w_ref, o_ref, *, bi):
    w = w_ref[0]
    group = o_ref.shape[1]
    row = lax.broadcasted_iota(I32, (group, group), 0)
    col = lax.broadcasted_iota(I32, (group, group), 1)
    dense = jnp.zeros((group, group), F32)
    for o in range(w.shape[1]):
        dense = jnp.where(col % w.shape[1] == o, w[:, o:o + 1], dense)
    o_ref[0] = jnp.where(row // bi == col // w.shape[1], dense, 0.0).astype(o_ref.dtype)


def _block_diag_dense(ws, group):
    nb, bi, bo = ws[0].shape
    slabs = len(ws) * nb * bi // group
    w3 = jnp.concatenate(ws, axis=0).reshape(slabs, group, bo)
    return pl.pallas_call(
        functools.partial(_block_diag_kernel, bi=bi),
        grid=(slabs,),
        in_specs=[pl.BlockSpec((1, group, bo), lambda i: (i, 0, 0))],
        out_specs=pl.BlockSpec((1, group, group), lambda i: (i, 0, 0)),
        out_shape=jax.ShapeDtypeStruct((slabs, group, group), BF16),
        compiler_params=_cparams(("parallel",)),
        name="block_diag",
    )(w3)


def _mlstm_prep(proj, seq, conv_w, conv_b, wq, wk, wv, w_if, b_if, tm):
    n = proj.shape[0]
    c = M_WIDTH
    nblk = c // M_HEAD_DIM
    tiles_per_seq = seq // tm
    wd = _block_diag_dense([wq, wk, wv], M_HEAD_DIM)
    ng = 2 * M_HEADS
    wif = jnp.pad(w_if, ((0, 0), (0, GATE_LANES - ng))).astype(BF16)
    bif = jnp.pad(b_if, (0, GATE_LANES - ng)).reshape(1, GATE_LANES)
    hb = tm // HALO
    full = lambda shape: pl.BlockSpec(shape, lambda i: (0,) * len(shape))
    return pl.pallas_call(
        functools.partial(_mprep_kernel, tm=tm, tiles_per_seq=tiles_per_seq),
        grid=(n // tm,),
        in_specs=[pl.BlockSpec((tm, c), lambda i: (i, 0)),
                  pl.BlockSpec((HALO, c), lambda i: (jnp.maximum(i * hb - 1, 0), 0)),
                  full((CONV_WIDTH, c)), full((1, c)),
                  pl.BlockSpec((nblk, M_HEAD_DIM, M_HEAD_DIM), lambda i: (0, 0, 0)),
                  pl.BlockSpec((nblk, M_HEAD_DIM, M_HEAD_DIM), lambda i: (1, 0, 0)),
                  pl.BlockSpec((nblk, M_HEAD_DIM, M_HEAD_DIM), lambda i: (2, 0, 0)),
                  full((3 * c, GATE_LANES)), full((1, GATE_LANES))],
        out_specs=[pl.BlockSpec((tm, c), lambda i: (i, 0)),
                   pl.BlockSpec((tm, c), lambda i: (i, 0)),
                   pl.BlockSpec((tm, c), lambda i: (i, 0)),
                   pl.BlockSpec((tm, c), lambda i: (i, 0)),
                   pl.BlockSpec((tm, ng), lambda i: (i, 0)),
                   pl.BlockSpec((ng, tm), lambda i: (0, i))],
        out_shape=[jax.ShapeDtypeStruct((n, c), F32),
                   jax.ShapeDtypeStruct((n, c), BF16),
                   jax.ShapeDtypeStruct((n, c), BF16),
                   jax.ShapeDtypeStruct((n, c), BF16),
                   jax.ShapeDtypeStruct((n, ng), F32),
                   jax.ShapeDtypeStruct((ng, n), F32)],
        compiler_params=_cparams(("parallel",)),
        name="mlstm_prep",
    )(proj, proj, conv_w, conv_b.reshape(1, c), wd, wd, wd, wif, bif)


def _mlstm_kernel(q_ref, k_ref, v_ref, g_ref, gt_ref, xc_ref, z_ref, nw_ref, sk_ref, o_ref,
                  c_ref, n_ref, m_ref, *, chunk):
    L = chunk
    hd = M_HEAD_DIM

    @pl.when(pl.program_id(1) == 0)
    def _():
        c_ref[...] = jnp.zeros_like(c_ref)
        n_ref[...] = jnp.zeros_like(n_ref)
        m_ref[...] = jnp.zeros_like(m_ref)

    rows = lax.broadcasted_iota(I32, (L, L), 0)
    cols = lax.broadcasted_iota(I32, (L, L), 1)
    causal = cols <= rows
    tril = jnp.where(causal, 1.0, 0.0).astype(F32)
    triu = jnp.where(rows <= cols, 1.0, 0.0).astype(F32)
    g = g_ref[...]
    gt = gt_ref[...]
    hi = lax.Precision.HIGHEST
    bcol_all = jnp.dot(tril, g, precision=hi, preferred_element_type=F32)
    brow_all = jnp.dot(gt, triu, precision=hi, preferred_element_type=F32)
    k_scale = hd ** -0.5
    nt = (((1,), (1,)), ((), ()))
    tn = (((0,), (0,)), ((), ()))

    for h in range(M_HEADS):
        sl = slice(h * hd, (h + 1) * hd)
        qh = q_ref[:, sl]
        kh = k_ref[:, sl]
        vh = v_ref[:, sl]
        i_col = g[:, h:h + 1]
        b_col = bcol_all[:, M_HEADS + h:M_HEADS + h + 1]
        i_row = gt[h:h + 1, :]
        b_row = brow_all[M_HEADS + h:M_HEADS + h + 1, :]
        m_prev = m_ref[h, 0:1, 0:1]
        c_prev = c_ref[h]
        n_prev = n_ref[h]

        dmat = jnp.where(causal, b_col - b_row + i_row, NEG_INF)
        a_col = b_col + m_prev
        m_row = jnp.maximum(a_col, jnp.max(dmat, axis=1, keepdims=True))
        s = lax.dot_general(qh, kh, nt, preferred_element_type=F32) * k_scale
        s = s * jnp.exp(dmat - m_row)
        inter = jnp.exp(a_col - m_row)
        num = inter * jnp.dot(qh, c_prev.astype(BF16), preferred_element_type=F32) \
            + jnp.dot(s.astype(BF16), vh, preferred_element_type=F32)
        qn = jnp.sum(qh.astype(F32) * n_prev, axis=1, keepdims=True)
        den = inter * qn + jnp.sum(s, axis=1, keepdims=True)
        hval = num * (1.0 / jnp.maximum(jnp.abs(den), jnp.exp(-m_row)))

        mu = jnp.mean(hval, axis=1, keepdims=True)
        cen = hval - mu
        var = jnp.mean(cen * cen, axis=1, keepdims=True)
        hn = cen * lax.rsqrt(var + LN_EPS) * nw_ref[:, sl]
        zz = z_ref[:, sl]
        o_ref[:, sl] = ((hn + sk_ref[:, sl] * xc_ref[:, sl]) * (zz * _sigmoid(zz))).astype(o_ref.dtype)

        b_last = b_col[L - 1:L, :]
        w_log = b_last - b_col + i_col
        m_new = jnp.maximum(b_last + m_prev, jnp.max(w_log, axis=0, keepdims=True))
        decay = jnp.exp(b_last + m_prev - m_new)
        kw = kh.astype(F32) * (jnp.exp(w_log - m_new) * k_scale)
        c_ref[h] = decay * c_prev + lax.dot_general(kw.astype(BF16), vh, tn, preferred_element_type=F32)
        n_ref[h] = decay * n_prev + jnp.sum(kw, axis=0, keepdims=True)
        m_ref[h] = jnp.broadcast_to(m_new, m_ref.shape[1:])


def _mlstm(q, k, v, g, gt, xc, proj, norm_w, skip, batch, seq, chunk):
    n, c = q.shape
    nc = seq // chunk
    ng = 2 * M_HEADS
    zcol = M_WIDTH // c
    row = lambda b, j: (b * nc + j, 0)
    return pl.pallas_call(
        functools.partial(_mlstm_kernel, chunk=chunk),
        grid=(batch, nc),
        in_specs=[pl.BlockSpec((chunk, c), row), pl.BlockSpec((chunk, c), row), pl.BlockSpec((chunk, c), row),
                  pl.BlockSpec((chunk, ng), row),
                  pl.BlockSpec((ng, chunk), lambda b, j: (0, b * nc + j)),
                  pl.BlockSpec((chunk, c), row),
                  pl.BlockSpec((chunk, c), lambda b, j: (b * nc + j, zcol)),
                  pl.BlockSpec((1, c), lambda b, j: (0, 0)),
                  pl.BlockSpec((1, c), lambda b, j: (0, 0))],
        out_specs=pl.BlockSpec((chunk, c), row),
        out_shape=jax.ShapeDtypeStruct((n, c), BF16),
        scratch_shapes=[pltpu.VMEM((M_HEADS, M_HEAD_DIM, M_HEAD_DIM), F32),
                        pltpu.VMEM((M_HEADS, 1, M_HEAD_DIM), F32),
                        pltpu.VMEM((M_HEADS, 8, 128), F32)],
        compiler_params=_cparams(("parallel", "arbitrary")),
        name="mlstm_chunk",
    )(q, k, v, g, gt, xc, proj, norm_w.reshape(1, c), skip.reshape(1, c))


def _gelu_tanh(x):
    return 0.5 * x * (1.0 + jnp.tanh(math.sqrt(2.0 / math.pi) * (x + 0.044715 * (x * x * x))))


def _rglru_kernel(xr_ref, halo_ref, xg_ref, cw_ref, cb_ref, wa_ref, ba_ref, wx_ref, bx_ref, lam_ref,
                  o_ref, a_ref, b_ref, h_ref, *, tm, cw):
    t = pl.program_id(2)

    @pl.when(t == 0)
    def _():
        h_ref[...] = jnp.zeros_like(h_ref)

    halo = jnp.where(t == 0, 0.0, halo_ref[...])
    xc = _causal_conv(halo, xr_ref[...], cw_ref, cb_ref).reshape(tm, cw)
    xcb = xc.astype(BF16)
    nblk = cw // R_BLOCK
    ra = []
    rx = []
    for g in range(nblk):
        sl = slice(g * R_BLOCK, (g + 1) * R_BLOCK)
        ra.append(jnp.dot(xcb[:, sl], wa_ref[g], preferred_element_type=F32))
        rx.append(jnp.dot(xcb[:, sl], wx_ref[g], preferred_element_type=F32))
    t_r = jnp.tanh(0.5 * (jnp.concatenate(ra, axis=1) + ba_ref[...]))
    ig = _sigmoid(jnp.concatenate(rx, axis=1) + bx_ref[...])
    nl = -lam_ref[...]
    softplus = jnp.maximum(nl, 0.0) + jnp.log1p(jnp.exp(-jnp.abs(nl)))
    c2 = (-0.5 * LRU_C * math.log2(math.e)) * softplus
    a = jnp.exp2(c2 * t_r + c2)
    v = 1.0 - a * a
    b = jnp.where(v > 0.0, v * lax.rsqrt(v), 0.0) * (ig * xc)

    a = a.reshape(tm // 8, 8, cw)
    b = b.reshape(tm // 8, 8, cw)
    sub = lax.broadcasted_iota(I32, a.shape, 1)
    for d in (1, 2, 4):
        keep = sub >= d
        a_sh = pltpu.roll(a, d, axis=1)
        b_sh = pltpu.roll(b, d, axis=1)
        b = jnp.where(keep, a * b_sh + b, b)
        a = jnp.where(keep, a * a_sh, a)
    a_ref[...] = a.reshape(tm, cw)
    b_ref[...] = b.reshape(tm, cw)

    def body(g, h):
        r0 = pl.multiple_of(g * 8, 8)
        hh = b_ref[pl.ds(r0, 8), :] + a_ref[pl.ds(r0, 8), :] * h
        b_ref[pl.ds(r0, 8), :] = hh
        return hh[7:8, :]

    h_ref[...] = lax.fori_loop(0, tm // 8, body, h_ref[...], unroll=8)
    o_ref[...] = (b_ref[...] * _gelu_tanh(xg_ref[...])).astype(o_ref.dtype)


def _rglru(proj, batch, seq, xr_off, xg_off, conv_w, conv_b, wa, ba, wx, bx, lam, tm, cw):
    n = proj.shape[0]
    ncol = R_WIDTH // cw
    nt = seq // tm
    per = cw // R_BLOCK
    xr_cb = xr_off // cw
    xg_cb = xg_off // cw
    hb = tm // HALO
    colv = lambda shape: pl.BlockSpec(shape, lambda b, j, t: (0, j))
    return pl.pallas_call(
        functools.partial(_rglru_kernel, tm=tm, cw=cw),
        grid=(batch, ncol, nt),
        in_specs=[pl.BlockSpec((tm, cw), lambda b, j, t: (b * nt + t, xr_cb + j)),
                  pl.BlockSpec((HALO, cw), lambda b, j, t: (jnp.maximum((b * nt + t) * hb - 1, 0), xr_cb + j)),
                  pl.BlockSpec((tm, cw), lambda b, j, t: (b * nt + t, xg_cb + j)),
                  colv((CONV_WIDTH, cw)), colv((1, cw)),
                  pl.BlockSpec((per, R_BLOCK, R_BLOCK), lambda b, j, t: (j, 0, 0)), colv((1, cw)),
                  pl.BlockSpec((per, R_BLOCK, R_BLOCK), lambda b, j, t: (j, 0, 0)), colv((1, cw)),
                  colv((1, cw))],
        out_specs=pl.BlockSpec((tm, cw), lambda b, j, t: (b * nt + t, j)),
        out_shape=jax.ShapeDtypeStruct((n, R_WIDTH), BF16),
        scratch_shapes=[pltpu.VMEM((tm, cw), F32), pltpu.VMEM((tm, cw), F32), pltpu.VMEM((1, cw), F32)],
        compiler_params=_cparams(("parallel", "parallel", "arbitrary")),
        name="rglru",
    )(proj, proj, proj, conv_w, conv_b.reshape(1, R_WIDTH), wa.astype(BF16), ba.reshape(1, R_WIDTH),
      wx.astype(BF16), bx.reshape(1, R_WIDTH), lam.reshape(1, R_WIDTH))


def _merge_kernel(ym_ref, yr_ref, wpm_ref, wpr_ref, g0_ref, g1_ref, bg_ref, o_ref):
    g0 = jax.nn.sigmoid(g0_ref[...] + bg_ref[0:1, :])
    g1 = jax.nn.sigmoid(g1_ref[...] + bg_ref[1:2, :])
    pm = jnp.dot(ym_ref[...], wpm_ref[...], preferred_element_type=F32)
    pr = jnp.dot(yr_ref[...], wpr_ref[...], preferred_element_type=F32)
    o_ref[...] = (g0 * pm + g1 * pr).astype(o_ref.dtype)


def _merge(ym, yr, w_pm, w_pr, proj, gate_off, b_gate, tm, bn):
    n = ym.shape[0]
    d = w_pm.shape[1]
    g0_cb = gate_off // bn
    g1_cb = (gate_off + d) // bn
    return pl.pallas_call(
        _merge_kernel,
        grid=(d // bn, n // tm),
        in_specs=[pl.BlockSpec((tm, ym.shape[1]), lambda j, i: (i, 0)),
                  pl.BlockSpec((tm, yr.shape[1]), lambda j, i: (i, 0)),
                  pl.BlockSpec((w_pm.shape[0], bn), lambda j, i: (0, j)),
                  pl.BlockSpec((w_pr.shape[0], bn), lambda j, i: (0, j)),
                  pl.BlockSpec((tm, bn), lambda j, i: (i, g0_cb + j)),
                  pl.BlockSpec((tm, bn), lambda j, i: (i, g1_cb + j)),
                  pl.BlockSpec((2, bn), lambda j, i: (0, j))],
        out_specs=pl.BlockSpec((tm, bn), lambda j, i: (i, j)),
        out_shape=jax.ShapeDtypeStruct((n, d), BF16),
        compiler_params=_cparams(("parallel", "parallel")),
        name="merge",
    )(ym, yr, w_pm.astype(BF16), w_pr.astype(BF16), proj, proj, b_gate)


def _layer_norm(y, g, b):
    mu = jnp.mean(y, axis=1, keepdims=True)
    cen = y - mu
    var = jnp.mean(cen * cen, axis=1, keepdims=True)
    return cen * lax.rsqrt(var + LN_EPS) * g + b


def _split_hi_lo(x):
    hi = lax.bitcast_convert_type(lax.bitcast_convert_type(x, U32) & jnp.uint32(0xFFFF0000), F32)
    return hi.astype(BF16), (x - hi).astype(BF16)


def _oproj_kernel(mg_ref, wo_ref, x_ref, g_ref, b_ref, rwh_ref, rwl_ref, x1_ref, lg_ref):
    y = ALPHA * x_ref[...] + jnp.dot(mg_ref[...], wo_ref[...], preferred_element_type=F32)
    x1 = _layer_norm(y, g_ref[...], b_ref[...])
    x1_ref[...] = x1
    xh, xl = _split_hi_lo(x1)
    wh = rwh_ref[...]
    wl = rwl_ref[...]
    lg_ref[...] = ((jnp.dot(xh, wh, preferred_element_type=F32) + jnp.dot(xl, wl, preferred_element_type=F32))
                   + (jnp.dot(xl, wh, preferred_element_type=F32) + jnp.dot(xh, wl, preferred_element_type=F32)))


def _oproj(merged, w_o, x, ln_g, ln_b, router_w, tm):
    n, d = x.shape
    e = router_w.shape[1]
    rw_hi, rw_lo = _split_hi_lo(jnp.pad(router_w, ((0, 0), (0, ROUTER_LANES - e))))
    full = lambda shape: pl.BlockSpec(shape, lambda i: (0,) * len(shape))
    return pl.pallas_call(
        _oproj_kernel,
        grid=(n // tm,),
        in_specs=[pl.BlockSpec((tm, d), lambda i: (i, 0)), full((d, d)),
                  pl.BlockSpec((tm, d), lambda i: (i, 0)), full((1, d)), full((1, d)),
                  full((d, ROUTER_LANES)), full((d, ROUTER_LANES))],
        out_specs=[pl.BlockSpec((tm, d), lambda i: (i, 0)), pl.BlockSpec((tm, ROUTER_LANES), lambda i: (i, 0))],
        out_shape=[jax.ShapeDtypeStruct((n, d), F32), jax.ShapeDtypeStruct((n, ROUTER_LANES), F32)],
        compiler_params=_cparams(("parallel",)),
        name="out_proj_ln",
    )(merged, w_o.astype(BF16), x, ln_g.reshape(1, d), ln_b.reshape(1, d), rw_hi, rw_lo)


def _first_max(v, idx, sentinel):
    m = jnp.max(v, axis=0, keepdims=True)
    am = jnp.min(jnp.where(v == m, idx, sentinel), axis=0, keepdims=True)
    return m, am


def _route_kernel(lg_ref, bias_ref, tri_ref, te_ref, w_ref, rk_ref, cnt_ref, carry_ref, *, tm):
    @pl.when(pl.program_id(0) == 0)
    def _():
        carry_ref[...] = jnp.zeros_like(carry_ref)

    e = N_EXPERTS
    gs = e // N_GROUPS
    scores = jax.nn.sigmoid(jnp.transpose(lg_ref[...])[0:e, :])
    biased = scores + bias_ref[...]
    sub = lax.broadcasted_iota(I32, (gs, tm), 0)
    grp_rows = []
    for g in range(N_GROUPS):
        slab = biased[g * gs:(g + 1) * gs, :]
        m1, a1 = _first_max(slab, sub, gs)
        m2 = jnp.max(jnp.where(sub == a1, NEG_INF, slab), axis=0, keepdims=True)
        grp_rows.append(m1 + m2)
    grp = jnp.concatenate(grp_rows, axis=0)
    gidx = lax.broadcasted_iota(I32, (N_GROUPS, tm), 0)
    gsel = jnp.zeros((N_GROUPS, tm), F32)
    for _ in range(TOPK_GROUPS):
        _, am = _first_max(grp, gidx, N_GROUPS)
        hit = gidx == am
        gsel = jnp.where(hit, 1.0, gsel)
        grp = jnp.where(hit, NEG_INF, grp)
    masked = jnp.concatenate(
        [jnp.where(gsel[g:g + 1, :] > 0.0, biased[g * gs:(g + 1) * gs, :], NEG_INF) for g in range(N_GROUPS)],
        axis=0)
    eidx = lax.broadcasted_iota(I32, (e, tm), 0)
    member = jnp.zeros((e, tm), F32)
    tops = []
    ws = []
    for _ in range(TOP_K):
        _, am = _first_max(masked, eidx, e)
        hit = eidx == am
        tops.append(am)
        ws.append(jnp.sum(jnp.where(hit, scores, 0.0), axis=0, keepdims=True))
        member = jnp.where(hit, 1.0, member)
        masked = jnp.where(hit, NEG_INF, masked)
    wsum = ws[0]
    for k in range(1, TOP_K):
        wsum = wsum + ws[k]
    te_ref[...] = jnp.concatenate(tops, axis=0)
    w_ref[...] = jnp.concatenate(ws, axis=0) / wsum * ROUTED_SCALE

    cum = jnp.dot(member.astype(BF16), tri_ref[...], preferred_element_type=F32)
    carry = carry_ref[:, 0:1]
    rank = carry + cum - member
    rks = []
    for k in range(TOP_K):
        rks.append(jnp.sum(jnp.where(eidx == tops[k], rank, 0.0), axis=0, keepdims=True))
    rk_ref[...] = jnp.concatenate(rks, axis=0).astype(I32)
    new_carry = carry + cum[:, tm - 1:tm]
    carry_ref[...] = jnp.broadcast_to(new_carry, carry_ref.shape)
    cnt_ref[...] = jnp.broadcast_to(new_carry, cnt_ref.shape).astype(I32)


def _route(logits, router_bias, tm):
    n = logits.shape[0]
    e = router_bias.shape[0]
    tri = jnp.triu(jnp.ones((tm, tm), F32)).astype(BF16)
    return pl.pallas_call(
        functools.partial(_route_kernel, tm=tm),
        grid=(n // tm,),
        in_specs=[pl.BlockSpec((tm, ROUTER_LANES), lambda i: (i, 0)),
                  pl.BlockSpec((e, 1), lambda i: (0, 0)),
                  pl.BlockSpec((tm, tm), lambda i: (0, 0))],
        out_specs=[pl.BlockSpec((TOP_K, tm), lambda i: (0, i)),
                   pl.BlockSpec((TOP_K, tm), lambda i: (0, i)),
                   pl.BlockSpec((TOP_K, tm), lambda i: (0, i)),
                   pl.BlockSpec((e, 128), lambda i: (0, 0))],
        out_shape=[jax.ShapeDtypeStruct((TOP_K, n), I32), jax.ShapeDtypeStruct((TOP_K, n), F32),
                   jax.ShapeDtypeStruct((TOP_K, n), I32), jax.ShapeDtypeStruct((e, 128), I32)],
        scratch_shapes=[pltpu.VMEM((e, 128), F32)],
        compiler_params=_cparams(("arbitrary",)),
        name="route",
    )(logits, router_bias.reshape(e, 1), tri)


def _dest_kernel(te_ref, rk_ref, ps_ref, d_ref):
    te = te_ref[...]
    e = N_EXPERTS
    tm = te.shape[1]
    eidx = lax.broadcasted_iota(I32, (e, tm), 0)
    ps = ps_ref[...]
    rows = []
    for k in range(TOP_K):
        rows.append(jnp.sum(jnp.where(eidx == te[k:k + 1, :], ps, 0), axis=0, keepdims=True))
    d_ref[...] = jnp.concatenate(rows, axis=0) + rk_ref[...]


def _dest(top_e, rank, starts, tm):
    n = top_e.shape[1]
    return pl.pallas_call(
        _dest_kernel,
        grid=(n // tm,),
        in_specs=[pl.BlockSpec((TOP_K, tm), lambda i: (0, i)),
                  pl.BlockSpec((TOP_K, tm), lambda i: (0, i)),
                  pl.BlockSpec((N_EXPERTS, 1), lambda i: (0, 0))],
        out_specs=pl.BlockSpec((TOP_K, tm), lambda i: (0, i)),
        out_shape=jax.ShapeDtypeStruct((TOP_K, n), I32),
        compiler_params=_cparams(("parallel",)),
        name="dest",
    )(top_e, rank, starts.reshape(N_EXPERTS, 1))


def _invert_kernel(dest_ref, inv_ref):
    def body(j, c):
        inv_ref[dest_ref[j]] = j
        return c

    lax.fori_loop(0, dest_ref.shape[0], body, 0, unroll=32)


def _invert_permutation(dest_flat):
    m = dest_flat.shape[0]
    return pl.pallas_call(
        _invert_kernel,
        in_specs=[pl.BlockSpec(memory_space=pltpu.SMEM)],
        out_specs=pl.BlockSpec(memory_space=pltpu.SMEM),
        out_shape=jax.ShapeDtypeStruct((m,), I32),
        name="invert_perm",
    )(dest_flat)


def _expert_ffn(xb, wgb_ref, wub_ref, wdb_ref):
    hg = jnp.dot(xb, wgb_ref[...], preferred_element_type=F32)
    hu = jnp.dot(xb, wub_ref[...], preferred_element_type=F32)
    hh = (hg * jax.nn.sigmoid(hg)) * hu
    return jnp.dot(hh.astype(BF16), wdb_ref[...], preferred_element_type=F32)


def _expert_kernel(vt_ref, ve_ref, lo_ref, hi_ref, inv_ref, nxt_ref, e0_ref,
                   x_hbm, wg_hbm, wu_hbm, wd_hbm, out_hbm,
                   wgf_ref, wuf_ref, wdf_ref, wgb_ref, wub_ref, wdb_ref, xbuf, obuf, xb_ref, cur_ref,
                   gsem, ssem, wsem, *, n_tok, n_tiles):
    v = pl.program_id(0)
    rows = xbuf.shape[1]
    tile = vt_ref[v]
    lo = lo_ref[v]
    hi = hi_ref[v]
    nonempty = hi > lo
    first = jnp.logical_and(nonempty, lo == 0)
    slot = lax.rem(tile, GATHER_SLOTS)
    oslot = tile % 2
    ahead = GATHER_SLOTS - 1

    def gather_row(t, r):
        src_tile = jnp.minimum(t, n_tiles - 1)
        tok = inv_ref[src_tile * rows + r] & (n_tok - 1)
        s = lax.rem(t, GATHER_SLOTS)
        return pltpu.make_async_copy(x_hbm.at[pl.ds(tok, 1)], xbuf.at[s, pl.ds(r, 1)], gsem.at[s])

    def tile_writeback(t):
        s = t % 2
        r0 = pl.multiple_of(t * rows, rows)
        return pltpu.make_async_copy(obuf.at[s], out_hbm.at[pl.ds(r0, rows)], ssem.at[s])

    def wait_gathered_tile(s):
        pltpu.make_async_copy(x_hbm.at[pl.ds(0, rows)], xbuf.at[s], gsem.at[s]).wait()

    def weight_copies(e):
        return (pltpu.make_async_copy(wg_hbm.at[e], wgf_ref, wsem.at[0]),
                pltpu.make_async_copy(wu_hbm.at[e], wuf_ref, wsem.at[1]),
                pltpu.make_async_copy(wd_hbm.at[e], wdf_ref, wsem.at[2]))

    @pl.when(v == 0)
    def _():
        cur_ref[0] = -1
        for c in weight_copies(e0_ref[0]):
            c.start()

    @pl.when(jnp.logical_and(nonempty, cur_ref[0] != ve_ref[v]))
    def _():
        e = ve_ref[v]
        for c in weight_copies(e):
            c.wait()
        wgb_ref[...] = wgf_ref[...].astype(BF16)
        wub_ref[...] = wuf_ref[...].astype(BF16)
        wdb_ref[...] = wdf_ref[...].astype(BF16)
        cur_ref[0] = e
        ne = nxt_ref[e]

        @pl.when(ne >= 0)
        def _():
            for c in weight_copies(ne):
                c.start()

    @pl.when(jnp.logical_and(first, tile == 0))
    def _():
        def issue(i, c):
            gather_row(i // rows, i % rows).start(priority=ROW_COPY_PRIORITY)
            return c

        lax.fori_loop(0, ahead * rows, issue, 0)

    @pl.when(jnp.logical_and(first, tile >= 2))
    def _():
        tile_writeback(tile - 2).wait()

    @pl.when(first)
    def _():
        wait_gathered_tile(slot)

    half = rows // 2

    @pl.when(jnp.logical_and(first, hi > half))
    def _():
        xb_ref[...] = xbuf[slot].astype(BF16)
        for r in range(rows):
            gather_row(tile + ahead, r).start(priority=ROW_COPY_PRIORITY)
        obuf[oslot] = _expert_ffn(xb_ref[...], wgb_ref, wub_ref, wdb_ref)

    @pl.when(jnp.logical_and(first, hi <= half))
    def _():
        xb_ref[0:half, :] = xbuf[slot, 0:half, :].astype(BF16)
        for r in range(rows):
            gather_row(tile + ahead, r).start(priority=ROW_COPY_PRIORITY)
        obuf[oslot, 0:half, :] = _expert_ffn(xb_ref[0:half, :], wgb_ref, wub_ref, wdb_ref)
        obuf[oslot, half:rows, :] = jnp.zeros((rows - half, obuf.shape[2]), F32)

    @pl.when(jnp.logical_and(nonempty, jnp.logical_and(lo > 0, lo < half)))
    def _():
        y = _expert_ffn(xbuf[slot].astype(BF16), wgb_ref, wub_ref, wdb_ref)
        row = lax.broadcasted_iota(I32, (rows, 1), 0)
        mine = jnp.logical_and(row >= lo, row < hi)
        obuf[oslot] = jnp.where(mine, y, obuf[oslot])

    @pl.when(jnp.logical_and(nonempty, lo >= half))
    def _():
        y = _expert_ffn(xbuf[slot, half:rows, :].astype(BF16), wgb_ref, wub_ref, wdb_ref)
        row = half + lax.broadcasted_iota(I32, (rows - half, 1), 0)
        mine = jnp.logical_and(row >= lo, row < hi)
        obuf[oslot, half:rows, :] = jnp.where(mine, y, obuf[oslot, half:rows, :])

    @pl.when(jnp.logical_and(nonempty, hi == rows))
    def _():
        tile_writeback(tile).start()

    @pl.when(v == pl.num_programs(0) - 1)
    def _():
        last = n_tiles - 1
        tile_writeback(last - 1).wait()
        tile_writeback(last).wait()
        for t in range(last + 1, last + 1 + ahead):
            wait_gathered_tile(t % GATHER_SLOTS)


def _experts(x1, inv, visit_tile, visit_expert, visit_lo, visit_hi, next_expert, first_expert,
             w_gate, w_up, w_down):
    n_tok, d = x1.shape
    m = inv.shape[0]
    de = w_gate.shape[2]
    n_tiles = m // EXPERT_TILE
    any_spec = pl.BlockSpec(memory_space=pl.ANY)
    grid_spec = pltpu.PrefetchScalarGridSpec(
        num_scalar_prefetch=7,
        grid=(visit_tile.shape[0],),
        in_specs=[any_spec, any_spec, any_spec, any_spec],
        out_specs=any_spec,
        scratch_shapes=[pltpu.VMEM((d, de), F32), pltpu.VMEM((d, de), F32), pltpu.VMEM((de, d), F32),
                        pltpu.VMEM((d, de), BF16), pltpu.VMEM((d, de), BF16), pltpu.VMEM((de, d), BF16),
                        pltpu.VMEM((GATHER_SLOTS, EXPERT_TILE, d), F32), pltpu.VMEM((2, EXPERT_TILE, d), F32),
                        pltpu.VMEM((EXPERT_TILE, d), BF16), pltpu.SMEM((1,), I32),
                        pltpu.SemaphoreType.DMA((GATHER_SLOTS,)), pltpu.SemaphoreType.DMA((2,)),
                        pltpu.SemaphoreType.DMA((3,))],
    )
    return pl.pallas_call(
        functools.partial(_expert_kernel, n_tok=n_tok, n_tiles=n_tiles),
        grid_spec=grid_spec,
        out_shape=jax.ShapeDtypeStruct((m, d), F32),
        compiler_params=_cparams(("arbitrary",)),
        name="experts",
    )(visit_tile, visit_expert, visit_lo, visit_hi, inv, next_expert, first_expert, x1, w_gate, w_up, w_down)


def _combine_kernel(dcur_ref, dnxt_ref, x1_ref, w_ref, ys_hbm, sg_ref, su_ref, sd_ref, g_ref, b_ref, o_ref,
                    *scratch):
    bufs = scratch[:COMBINE_PHASES]
    sem = scratch[COMBINE_PHASES]
    j = pl.program_id(0)
    tq = bufs[0].shape[1]
    ahead = 2

    def row_copy(dref, col, p, k, t):
        return pltpu.make_async_copy(ys_hbm.at[pl.ds(dref[k, col + t], 1)], bufs[p].at[k, pl.ds(t, 1)], sem.at[p])

    def wait_phase(p):
        pltpu.make_async_copy(bufs[p], bufs[p], sem.at[p]).wait()

    @pl.when(j == 0)
    def _():
        for p in range(ahead):
            def issue(t, c, p=p):
                for k in range(TOP_K):
                    row_copy(dcur_ref, p * tq, p, k, t).start(priority=ROW_COPY_PRIORITY)
                return c

            lax.fori_loop(0, tq, issue, 0)

    def request(q):
        dref, qq = (dcur_ref, q) if q < COMBINE_PHASES else (dnxt_ref, q - COMBINE_PHASES)
        for t in range(tq):
            for k in range(TOP_K):
                row_copy(dref, qq * tq, q % COMBINE_PHASES, k, t).start(priority=k % 2)

    request(ahead)
    x1 = x1_ref[...]
    xb = x1.astype(BF16)
    hg = jnp.dot(xb, sg_ref[...], preferred_element_type=F32)
    hu = jnp.dot(xb, su_ref[...], preferred_element_type=F32)
    hh = (hg * jax.nn.sigmoid(hg)) * hu
    shared = jnp.dot(hh.astype(BF16), sd_ref[...], preferred_element_type=F32)
    base = ALPHA * x1 + shared
    w = w_ref[...]

    for p in range(COMBINE_PHASES):
        wait_phase(p)
        if p + 1 < COMBINE_PHASES:
            request(p + 1 + ahead)
        rs = slice(p * tq, (p + 1) * tq)
        routed = bufs[p][0] * w[rs, 0:1]
        for k in range(1, TOP_K):
            routed = routed + bufs[p][k] * w[rs, k:k + 1]
        o_ref[rs, :] = _layer_norm(base[rs, :] + routed, g_ref[...], b_ref[...])

    @pl.when(j == pl.num_programs(0) - 1)
    def _():
        for p in range(ahead):
            wait_phase(p)


def _combine(x1, dest, w_tok, ys, s_gate, s_up, s_down, ln_g, ln_b, tq):
    n, d = x1.shape
    de = s_gate.shape[1]
    tm = COMBINE_PHASES * tq
    steps = n // tm
    full = lambda shape: pl.BlockSpec(shape, lambda i: (0,) * len(shape))
    return pl.pallas_call(
        _combine_kernel,
        grid=(steps,),
        in_specs=[pl.BlockSpec((TOP_K, tm), lambda i: (0, i), memory_space=pltpu.SMEM),
                  pl.BlockSpec((TOP_K, tm), lambda i: (0, jnp.minimum(i + 1, steps - 1)), memory_space=pltpu.SMEM),
                  pl.BlockSpec((tm, d), lambda i: (i, 0)),
                  pl.BlockSpec((tm, TOP_K), lambda i: (i, 0)),
                  pl.BlockSpec(memory_space=pl.ANY),
                  full((d, de)), full((d, de)), full((de, d)), full((1, d)), full((1, d))],
        out_specs=pl.BlockSpec((tm, d), lambda i: (i, 0)),
        out_shape=jax.ShapeDtypeStruct((n, d), F32),
        scratch_shapes=[pltpu.VMEM((TOP_K, tq, d), F32) for _ in range(COMBINE_PHASES)]
        + [pltpu.SemaphoreType.DMA((COMBINE_PHASES,))],
        compiler_params=_cparams(("arbitrary",)),
        name="combine",
    )(dest, dest, x1, w_tok, ys, s_gate.astype(BF16), s_up.astype(BF16), s_down.astype(BF16),
      ln_g.reshape(1, d), ln_b.reshape(1, d))


def _visit_plan(counts, n_rows):
    e = counts.shape[0]
    n_tiles = n_rows // EXPERT_TILE
    ends = jnp.cumsum(counts)
    starts = ends - counts
    pos = jnp.sort(jnp.concatenate([jnp.arange(n_tiles, dtype=I32) * EXPERT_TILE, starts]))
    nxt = jnp.concatenate([pos[1:], jnp.full((1,), n_rows, I32)])
    tile = jnp.minimum(pos // EXPERT_TILE, n_tiles - 1)
    expert = jnp.minimum(jnp.sum((ends[None, :] <= pos[:, None]).astype(I32), axis=1), e - 1)
    ids = jnp.arange(e, dtype=I32)
    later = jnp.logical_and(ids[None, :] > ids[:, None], counts[None, :] > 0)
    next_expert = jnp.min(jnp.where(later, ids[None, :], e), axis=1)
    next_expert = jnp.where(next_expert < e, next_expert, -1).astype(I32)
    first_expert = jnp.min(jnp.where(counts > 0, ids, e - 1)).astype(I32).reshape(1)
    return starts, tile, expert, pos - tile * EXPERT_TILE, nxt - tile * EXPERT_TILE, next_expert, first_expert


def _moe(x1, logits, router_bias, e_w_gate, e_w_up, e_w_down, s_w_gate, s_w_up, s_w_down, ln_g, ln_b):
    n, d = x1.shape
    top_e, w_t, rank, counts = _route(logits, router_bias, tm=TILES["route_tokens"])
    starts, v_tile, v_expert, v_lo, v_hi, next_e, first_e = _visit_plan(counts[:, 0], n * TOP_K)
    dest = _dest(top_e, rank, starts, tm=TILES["dest_tokens"])
    inv = _invert_permutation(dest.reshape(n * TOP_K))
    ys = _experts(x1, inv, v_tile, v_expert, v_lo, v_hi, next_e, first_e, e_w_gate, e_w_up, e_w_down)
    return _combine(x1, dest, w_t.T, ys, s_w_gate, s_w_up, s_w_down, ln_g, ln_b,
                    tq=TILES["combine_subtile_tokens"])


def _layer(x, w_in, b_gate, m_conv_w, m_conv_b, m_wq, m_wk, m_wv, m_w_if, m_b_if, m_norm_w, m_skip,
           r_conv_w, r_conv_b, r_wa, r_ba, r_wx, r_bx, r_lambda, w_pm, w_pr, w_o, ln1_g, ln1_b,
           router_w, router_bias, e_w_gate, e_w_up, e_w_down, s_w_gate, s_w_up, s_w_down, ln2_g, ln2_b):
    batch, seq, d = x.shape
    n = batch * seq
    xt = x.reshape(n, d)
    proj = _in_proj(xt.astype(BF16), w_in, bm=TILES["in_proj_rows"], bn=TILES["in_proj_cols"])
    o_xr = 2 * M_WIDTH
    o_xg = o_xr + R_WIDTH
    o_gate = o_xg + R_WIDTH
    xc, q, k, v, g, gt = _mlstm_prep(proj, seq, m_conv_w, m_conv_b, m_wq, m_wk, m_wv, m_w_if, m_b_if,
                                     tm=TILES["mlstm_prep_rows"])
    y_m = _mlstm(q, k, v, g, gt, xc, proj, m_norm_w, m_skip, batch, seq, MLSTM_CHUNK)
    y_r = _rglru(proj, batch, seq, o_xr, o_xg, r_conv_w, r_conv_b, r_wa, r_ba, r_wx, r_bx, r_lambda,
                 tm=TILES["rglru_rows"], cw=TILES["rglru_cols"])
    merged = _merge(y_m, y_r, w_pm, w_pr, proj, o_gate, b_gate, tm=TILES["merge_rows"], bn=TILES["merge_cols"])
    x1, logits = _oproj(merged, w_o, xt, ln1_g, ln1_b, router_w, tm=TILES["out_proj_rows"])
    out = _moe(x1, logits, router_bias, e_w_gate, e_w_up, e_w_down, s_w_gate, s_w_up, s_w_down, ln2_g, ln2_b)
    return out.reshape(batch, seq, d)


def kernel(x, w_in, b_gate, m_conv_w, m_conv_b, m_wq, m_wk, m_wv, m_w_if, m_b_if, m_norm_w, m_skip, r_conv_w, r_conv_b, r_wa, r_ba, r_wx, r_bx, r_lambda, w_pm, w_pr, w_o, ln1_g, ln1_b, router_w, router_bias, e_w_gate, e_w_up, e_w_down, s_w_gate, s_w_up, s_w_down, ln2_g, ln2_b):
    for l in range(DEPTH):
        x = _layer(x, w_in[l], b_gate[l], m_conv_w[l], m_conv_b[l], m_wq[l], m_wk[l], m_wv[l],
                   m_w_if[l], m_b_if[l], m_norm_w[l], m_skip[l], r_conv_w[l], r_conv_b[l],
                   r_wa[l], r_ba[l], r_wx[l], r_bx[l], r_lambda[l], w_pm[l], w_pr[l], w_o[l],
                   ln1_g[l], ln1_b[l], router_w[l], router_bias[l], e_w_gate[l], e_w_up[l],
                   e_w_down[l], s_w_gate[l], s_w_up[l], s_w_down[l], ln2_g[l], ln2_b[l])
    return x
```

```python
import functools
import math

import jax
import jax.numpy as jnp
from jax import lax
from jax.experimental import pallas as pl
from jax.experimental.pallas import tpu as pltpu

F32 = jnp.float32
BF16 = jnp.bfloat16
I32 = jnp.int32
U32 = jnp.uint32

M_WIDTH = 2048
M_HEADS = 8
M_HEAD_DIM = 256
CONV_WIDTH = 4
R_WIDTH = 2560
R_BLOCK = 256
LRU_C = 8.0
N_EXPERTS = 64
TOP_K = 8
N_GROUPS = 8
TOPK_GROUPS = 4
ROUTED_SCALE = 2.5
DEPTH = 1
ALPHA = (2.0 * DEPTH) ** 0.25
LN_EPS = 1e-5

V7X_VMEM_LIMIT = 56 * 1024 * 1024
HALO = 8
MLSTM_CHUNK = 256
EXPERT_TILE = 256
NEG_INF = float("-inf")
GATE_LANES = 128
ROUTER_LANES = 128
GATHER_SLOTS = 3
COMBINE_PHASES = 4
ROW_COPY_PRIORITY = 1


TILES = dict(
    in_proj_rows=1024, in_proj_cols=1024,
    mlstm_prep_rows=256,
    rglru_rows=2048, rglru_cols=512,
    merge_rows=512, merge_cols=1024,
    out_proj_rows=256,
    route_tokens=512, dest_tokens=1024,
    combine_subtile_tokens=64,
)


def _cparams(sem, vmem=V7X_VMEM_LIMIT):
    return pltpu.CompilerParams(dimension_semantics=sem, vmem_limit_bytes=vmem)


def _inproj_kernel(a_ref, w_ref, o_ref, wb_ref):
    @pl.when(pl.program_id(1) == 0)
    def _():
        wb_ref[...] = w_ref[...].astype(BF16)

    o_ref[...] = jnp.dot(a_ref[...], wb_ref[...], preferred_element_type=F32)


def _in_proj(a, w, bm, bn):
    m, k = a.shape
    n = w.shape[1]
    return pl.pallas_call(
        _inproj_kernel,
        grid=(n // bn, m // bm),
        in_specs=[pl.BlockSpec((bm, k), lambda j, i: (i, 0)),
                  pl.BlockSpec((k, bn), lambda j, i: (0, j))],
        out_specs=pl.BlockSpec((bm, bn), lambda j, i: (i, j)),
        out_shape=jax.ShapeDtypeStruct((m, n), F32),
        scratch_shapes=[pltpu.VMEM((k, bn), BF16)],
        compiler_params=_cparams(("parallel", "arbitrary")),
        name="in_proj",
    )(a, w)


def _sigmoid(x):
    return 0.5 * jnp.tanh(0.5 * x) + 0.5


def _log_sigmoid(x):
    return jnp.minimum(x, 0.0) - jnp.log1p(jnp.exp(-jnp.abs(x)))


def _shift_rows(x3, prev_group, j):
    rot = pltpu.roll(x3, j, axis=1)
    prev = jnp.concatenate([pltpu.roll(prev_group, j, axis=1), rot[:-1]], axis=0)
    sub = lax.broadcasted_iota(I32, x3.shape, 1)
    return jnp.where(sub >= j, rot, prev)


def _causal_conv(halo, x, cw_ref, cb_ref):
    tm, c = x.shape
    x3 = x.reshape(tm // HALO, HALO, c)
    h3 = halo.reshape(1, HALO, c)
    last = CONV_WIDTH - 1
    y = cb_ref[...] + x3 * cw_ref[last:last + 1, :]
    for j in range(1, CONV_WIDTH):
        y = y + _shift_rows(x3, h3, j) * cw_ref[last - j:last - j + 1, :]
    return y


def _mprep_kernel(xm_ref, halo_ref, cw_ref, cb_ref, wq_ref, wk_ref, wv_ref, wif_ref, bif_ref,
                  xc_ref, q_ref, k_ref, v_ref, g_ref, gt_ref, *, tm, tiles_per_seq):
    i = pl.program_id(0)
    first = (i % tiles_per_seq) == 0
    halo = jnp.where(first, 0.0, halo_ref[...])
    xm = xm_ref[...]
    y = _causal_conv(halo, xm, cw_ref, cb_ref).reshape(tm, M_WIDTH)
    xc = y * _sigmoid(y)
    xc_ref[...] = xc
    xcb = xc.astype(BF16)
    xmb = xm.astype(BF16)
    nblk = M_WIDTH // M_HEAD_DIM
    for g in range(nblk):
        sl = slice(g * M_HEAD_DIM, (g + 1) * M_HEAD_DIM)
        q_ref[:, sl] = jnp.dot(xcb[:, sl], wq_ref[g], preferred_element_type=F32).astype(BF16)
        k_ref[:, sl] = jnp.dot(xcb[:, sl], wk_ref[g], preferred_element_type=F32).astype(BF16)
        v_ref[:, sl] = jnp.dot(xmb[:, sl], wv_ref[g], preferred_element_type=F32).astype(BF16)
    qb, kb, vb = q_ref[...], k_ref[...], v_ref[...]
    w = M_WIDTH
    ng = g_ref.shape[1]
    g = (jnp.dot(qb, wif_ref[0:w, :], preferred_element_type=F32)
         + jnp.dot(kb, wif_ref[w:2 * w, :], preferred_element_type=F32)
         + jnp.dot(vb, wif_ref[2 * w:3 * w, :], preferred_element_type=F32) + bif_ref[...])
    col = lax.broadcasted_iota(I32, g.shape, 1)
    g = jnp.where(col >= M_HEADS, _log_sigmoid(g), g)
    g_ref[...] = g[:, 0:ng]
    gt_ref[...] = jnp.transpose(g)[0:ng, :]


def _block_diag_kernel(w_ref, o_ref, *, bi):
    w = w_ref[0]
    group = o_ref.shape[1]
    row = lax.broadcasted_iota(I32, (group, group), 0)
    col = lax.broadcasted_iota(I32, (group, group), 1)
    dense = jnp.zeros((group, group), F32)
    for o in range(w.shape[1]):
        dense = jnp.where(col % w.shape[1] == o, w[:, o:o + 1], dense)
    o_ref[0] = jnp.where(row // bi == col // w.shape[1], dense, 0.0).astype(o_ref.dtype)


def _block_diag_dense(ws, group):
    nb, bi, bo = ws[0].shape
    slabs = len(ws) * nb * bi // group
    w3 = jnp.concatenate(ws, axis=0).reshape(slabs, group, bo)
    return pl.pallas_call(
        functools.partial(_block_diag_kernel, bi=bi),
        grid=(slabs,),
        in_specs=[pl.BlockSpec((1, group, bo), lambda i: (i, 0, 0))],
        out_specs=pl.BlockSpec((1, group, group), lambda i: (i, 0, 0)),
        out_shape=jax.ShapeDtypeStruct((slabs, group, group), BF16),
        compiler_params=_cparams(("parallel",)),
        name="block_diag",
    )(w3)


def _mlstm_prep(proj, seq, conv_w, conv_b, wq, wk, wv, w_if, b_if, tm):
    n = proj.shape[0]
    c = M_WIDTH
    nblk = c // M_HEAD_DIM
    tiles_per_seq = seq // tm
    wd = _block_diag_dense([wq, wk, wv], M_HEAD_DIM)
    ng = 2 * M_HEADS
    wif = jnp.pad(w_if, ((0, 0), (0, GATE_LANES - ng))).astype(BF16)
    bif = jnp.pad(b_if, (0, GATE_LANES - ng)).reshape(1, GATE_LANES)
    hb = tm // HALO
    full = lambda shape: pl.BlockSpec(shape, lambda i: (0,) * len(shape))
    return pl.pallas_call(
        functools.partial(_mprep_kernel, tm=tm, tiles_per_seq=tiles_per_seq),
        grid=(n // tm,),
        in_specs=[pl.BlockSpec((tm, c), lambda i: (i, 0)),
                  pl.BlockSpec((HALO, c), lambda i: (jnp.maximum(i * hb - 1, 0), 0)),
                  full((CONV_WIDTH, c)), full((1, c)),
                  pl.BlockSpec((nblk, M_HEAD_DIM, M_HEAD_DIM), lambda i: (0, 0, 0)),
                  pl.BlockSpec((nblk, M_HEAD_DIM, M_HEAD_DIM), lambda i: (1, 0, 0)),
                  pl.BlockSpec((nblk, M_HEAD_DIM, M_HEAD_DIM), lambda i: (2, 0, 0)),
                  full((3 * c, GATE_LANES)), full((1, GATE_LANES))],
        out_specs=[pl.BlockSpec((tm, c), lambda i: (i, 0)),
                   pl.BlockSpec((tm, c), lambda i: (i, 0)),
                   pl.BlockSpec((tm, c), lambda i: (i, 0)),
                   pl.BlockSpec((tm, c), lambda i: (i, 0)),
                   pl.BlockSpec((tm, ng), lambda i: (i, 0)),
                   pl.BlockSpec((ng, tm), lambda i: (0, i))],
        out_shape=[jax.ShapeDtypeStruct((n, c), F32),
                   jax.ShapeDtypeStruct((n, c), BF16),
                   jax.ShapeDtypeStruct((n, c), BF16),
                   jax.ShapeDtypeStruct((n, c), BF16),
                   jax.ShapeDtypeStruct((n, ng), F32),
                   jax.ShapeDtypeStruct((ng, n), F32)],
        compiler_params=_cparams(("parallel",)),
        name="mlstm_prep",
    )(proj, proj, conv_w, conv_b.reshape(1, c), wd, wd, wd, wif, bif)


def _mlstm_kernel(q_ref, k_ref, v_ref, g_ref, gt_ref, xc_ref, z_ref, nw_ref, sk_ref, o_ref,
                  c_ref, n_ref, m_ref, *, chunk):
    L = chunk
    hd = M_HEAD_DIM

    @pl.when(pl.program_id(1) == 0)
    def _():
        c_ref[...] = jnp.zeros_like(c_ref)
        n_ref[...] = jnp.zeros_like(n_ref)
        m_ref[...] = jnp.zeros_like(m_ref)

    rows = lax.broadcasted_iota(I32, (L, L), 0)
    cols = lax.broadcasted_iota(I32, (L, L), 1)
    causal = cols <= rows
    tril = jnp.where(causal, 1.0, 0.0).astype(F32)
    triu = jnp.where(rows <= cols, 1.0, 0.0).astype(F32)
    g = g_ref[...]
    gt = gt_ref[...]
    hi = lax.Precision.HIGHEST
    bcol_all = jnp.dot(tril, g, precision=hi, preferred_element_type=F32)
    brow_all = jnp.dot(gt, triu, precision=hi, preferred_element_type=F32)
    k_scale = hd ** -0.5
    nt = (((1,), (1,)), ((), ()))
    tn = (((0,), (0,)), ((), ()))

    for h in range(M_HEADS):
        sl = slice(h * hd, (h + 1) * hd)
        qh = q_ref[:, sl]
        kh = k_ref[:, sl]
        vh = v_ref[:, sl]
        i_col = g[:, h:h + 1]
        b_col = bcol_all[:, M_HEADS + h:M_HEADS + h + 1]
        i_row = gt[h:h + 1, :]
        b_row = brow_all[M_HEADS + h:M_HEADS + h + 1, :]
        m_prev = m_ref[h, 0:1, 0:1]
        c_prev = c_ref[h]
        n_prev = n_ref[h]

        dmat = jnp.where(causal, b_col - b_row + i_row, NEG_INF)
        a_col = b_col + m_prev
        m_row = jnp.maximum(a_col, jnp.max(dmat, axis=1, keepdims=True))
        s = lax.dot_general(qh, kh, nt, preferred_element_type=F32) * k_scale
        s = s * jnp.exp(dmat - m_row)
        inter = jnp.exp(a_col - m_row)
        num = inter * jnp.dot(qh, c_prev.astype(BF16), preferred_element_type=F32) \
            + jnp.dot(s.astype(BF16), vh, preferred_element_type=F32)
        qn = jnp.sum(qh.astype(F32) * n_prev, axis=1, keepdims=True)
        den = inter * qn + jnp.sum(s, axis=1, keepdims=True)
        hval = num * (1.0 / jnp.maximum(jnp.abs(den), jnp.exp(-m_row)))

        mu = jnp.mean(hval, axis=1, keepdims=True)
        cen = hval - mu
        var = jnp.mean(cen * cen, axis=1, keepdims=True)
        hn = cen * lax.rsqrt(var + LN_EPS) * nw_ref[:, sl]
        zz = z_ref[:, sl]
        o_ref[:, sl] = ((hn + sk_ref[:, sl] * xc_ref[:, sl]) * (zz * _sigmoid(zz))).astype(o_ref.dtype)

        b_last = b_col[L - 1:L, :]
        w_log = b_last - b_col + i_col
        m_new = jnp.maximum(b_last + m_prev, jnp.max(w_log, axis=0, keepdims=True))
        decay = jnp.exp(b_last + m_prev - m_new)
        kw = kh.astype(F32) * (jnp.exp(w_log - m_new) * k_scale)
        c_ref[h] = decay * c_prev + lax.dot_general(kw.astype(BF16), vh, tn, preferred_element_type=F32)
        n_ref[h] = decay * n_prev + jnp.sum(kw, axis=0, keepdims=True)
        m_ref[h] = jnp.broadcast_to(m_new, m_ref.shape[1:])


def _mlstm(q, k, v, g, gt, xc, proj, norm_w, skip, batch, seq, chunk):
    n, c = q.shape
    nc = seq // chunk
    ng = 2 * M_HEADS
    zcol = M_WIDTH // c
    row = lambda b, j: (b * nc + j, 0)
    return pl.pallas_call(
        functools.partial(_mlstm_kernel, chunk=chunk),
        grid=(batch, nc),
        in_specs=[pl.BlockSpec((chunk, c), row), pl.BlockSpec((chunk, c), row), pl.BlockSpec((chunk, c), row),
                  pl.BlockSpec((chunk, ng), row),
                  pl.BlockSpec((ng, chunk), lambda b, j: (0, b * nc + j)),
                  pl.BlockSpec((chunk, c), row),
                  pl.BlockSpec((chunk, c), lambda b, j: (b * nc + j, zcol)),
                  pl.BlockSpec((1, c), lambda b, j: (0, 0)),
                  pl.BlockSpec((1, c), lambda b, j: (0, 0))],
        out_specs=pl.BlockSpec((chunk, c), row),
        out_shape=jax.ShapeDtypeStruct((n, c), BF16),
        scratch_shapes=[pltpu.VMEM((M_HEADS, M_HEAD_DIM, M_HEAD_DIM), F32),
                        pltpu.VMEM((M_HEADS, 1, M_HEAD_DIM), F32),
                        pltpu.VMEM((M_HEADS, 8, 128), F32)],
        compiler_params=_cparams(("parallel", "arbitrary")),
        name="mlstm_chunk",
    )(q, k, v, g, gt, xc, proj, norm_w.reshape(1, c), skip.reshape(1, c))


def _gelu_tanh(x):
    return 0.5 * x * (1.0 + jnp.tanh(math.sqrt(2.0 / math.pi) * (x + 0.044715 * (x * x * x))))


def _rglru_kernel(xr_ref, halo_ref, xg_ref, cw_ref, cb_ref, wa_ref, ba_ref, wx_ref, bx_ref, lam_ref,
                  o_ref, a_ref, b_ref, h_ref, *, tm, cw):
    t = pl.program_id(2)

    @pl.when(t == 0)
    def _():
        h_ref[...] = jnp.zeros_like(h_ref)

    halo = jnp.where(t == 0, 0.0, halo_ref[...])
    xc = _causal_conv(halo, xr_ref[...], cw_ref, cb_ref).reshape(tm, cw)
    xcb = xc.astype(BF16)
    nblk = cw // R_BLOCK
    ra = []
    rx = []
    for g in range(nblk):
        sl = slice(g * R_BLOCK, (g + 1) * R_BLOCK)
        ra.append(jnp.dot(xcb[:, sl], wa_ref[g], preferred_element_type=F32))
        rx.append(jnp.dot(xcb[:, sl], wx_ref[g], preferred_element_type=F32))
    t_r = jnp.tanh(0.5 * (jnp.concatenate(ra, axis=1) + ba_ref[...]))
    ig = _sigmoid(jnp.concatenate(rx, axis=1) + bx_ref[...])
    nl = -lam_ref[...]
    softplus = jnp.maximum(nl, 0.0) + jnp.log1p(jnp.exp(-jnp.abs(nl)))
    c2 = (-0.5 * LRU_C * math.log2(math.e)) * softplus
    a = jnp.exp2(c2 * t_r + c2)
    v = 1.0 - a * a
    b = jnp.where(v > 0.0, v * lax.rsqrt(v), 0.0) * (ig * xc)

    a = a.reshape(tm // 8, 8, cw)
    b = b.reshape(tm // 8, 8, cw)
    sub = lax.broadcasted_iota(I32, a.shape, 1)
    for d in (1, 2, 4):
        keep = sub >= d
        a_sh = pltpu.roll(a, d, axis=1)
        b_sh = pltpu.roll(b, d, axis=1)
        b = jnp.where(keep, a * b_sh + b, b)
        a = jnp.where(keep, a * a_sh, a)
    a_ref[...] = a.reshape(tm, cw)
    b_ref[...] = b.reshape(tm, cw)

    def body(g, h):
        r0 = pl.multiple_of(g * 8, 8)
        hh = b_ref[pl.ds(r0, 8), :] + a_ref[pl.ds(r0, 8), :] * h
        b_ref[pl.ds(r0, 8), :] = hh
        return hh[7:8, :]

    h_ref[...] = lax.fori_loop(0, tm // 8, body, h_ref[...], unroll=8)
    o_ref[...] = (b_ref[...] * _gelu_tanh(xg_ref[...])).astype(o_ref.dtype)


def _rglru(proj, batch, seq, xr_off, xg_off, conv_w, conv_b, wa, ba, wx, bx, lam, tm, cw):
    n = proj.shape[0]
    ncol = R_WIDTH // cw
    nt = seq // tm
    per = cw // R_BLOCK
    xr_cb = xr_off // cw
    xg_cb = xg_off // cw
    hb = tm // HALO
    colv = lambda shape: pl.BlockSpec(shape, lambda b, j, t: (0, j))
    return pl.pallas_call(
        functools.partial(_rglru_kernel, tm=tm, cw=cw),
        grid=(batch, ncol, nt),
        in_specs=[pl.BlockSpec((tm, cw), lambda b, j, t: (b * nt + t, xr_cb + j)),
                  pl.BlockSpec((HALO, cw), lambda b, j, t: (jnp.maximum((b * nt + t) * hb - 1, 0), xr_cb + j)),
                  pl.BlockSpec((tm, cw), lambda b, j, t: (b * nt + t, xg_cb + j)),
                  colv((CONV_WIDTH, cw)), colv((1, cw)),
                  pl.BlockSpec((per, R_BLOCK, R_BLOCK), lambda b, j, t: (j, 0, 0)), colv((1, cw)),
                  pl.BlockSpec((per, R_BLOCK, R_BLOCK), lambda b, j, t: (j, 0, 0)), colv((1, cw)),
                  colv((1, cw))],
        out_specs=pl.BlockSpec((tm, cw), lambda b, j, t: (b * nt + t, j)),
        out_shape=jax.ShapeDtypeStruct((n, R_WIDTH), BF16),
        scratch_shapes=[pltpu.VMEM((tm, cw), F32), pltpu.VMEM((tm, cw), F32), pltpu.VMEM((1, cw), F32)],
        compiler_params=_cparams(("parallel", "parallel", "arbitrary")),
        name="rglru",
    )(proj, proj, proj, conv_w, conv_b.reshape(1, R_WIDTH), wa.astype(BF16), ba.reshape(1, R_WIDTH),
      wx.astype(BF16), bx.reshape(1, R_WIDTH), lam.reshape(1, R_WIDTH))


def _merge_kernel(ym_ref, yr_ref, wpm_ref, wpr_ref, g0_ref, g1_ref, bg_ref, o_ref):
    g0 = jax.nn.sigmoid(g0_ref[...] + bg_ref[0:1, :])
    g1 = jax.nn.sigmoid(g1_ref[...] + bg_ref[1:2, :])
    pm = jnp.dot(ym_ref[...], wpm_ref[...], preferred_element_type=F32)
    pr = jnp.dot(yr_ref[...], wpr_ref[...], preferred_element_type=F32)
    o_ref[...] = (g0 * pm + g1 * pr).astype(o_ref.dtype)


def _merge(ym, yr, w_pm, w_pr, proj, gate_off, b_gate, tm, bn):
    n = ym.shape[0]
    d = w_pm.shape[1]
    g0_cb = gate_off // bn
    g1_cb = (gate_off + d) // bn
    return pl.pallas_call(
        _merge_kernel,
        grid=(d // bn, n // tm),
        in_specs=[pl.BlockSpec((tm, ym.shape[1]), lambda j, i: (i, 0)),
                  pl.BlockSpec((tm, yr.shape[1]), lambda j, i: (i, 0)),
                  pl.BlockSpec((w_pm.shape[0], bn), lambda j, i: (0, j)),
                  pl.BlockSpec((w_pr.shape[0], bn), lambda j, i: (0, j)),
                  pl.BlockSpec((tm, bn), lambda j, i: (i, g0_cb + j)),
                  pl.BlockSpec((tm, bn), lambda j, i: (i, g1_cb + j)),
                  pl.BlockSpec((2, bn), lambda j, i: (0, j))],
        out_specs=pl.BlockSpec((tm, bn), lambda j, i: (i, j)),
        out_shape=jax.ShapeDtypeStruct((n, d), BF16),
        compiler_params=_cparams(("parallel", "parallel")),
        name="merge",
    )(ym, yr, w_pm.astype(BF16), w_pr.astype(BF16), proj, proj, b_gate)


def _layer_norm(y, g, b):
    mu = jnp.mean(y, axis=1, keepdims=True)
    cen = y - mu
    var = jnp.mean(cen * cen, axis=1, keepdims=True)
    return cen * lax.rsqrt(var + LN_EPS) * g + b


def _split_hi_lo(x):
    hi = lax.bitcast_convert_type(lax.bitcast_convert_type(x, U32) & jnp.uint32(0xFFFF0000), F32)
    return hi.astype(BF16), (x - hi).astype(BF16)


def _oproj_kernel(mg_ref, wo_ref, x_ref, g_ref, b_ref, rwh_ref, rwl_ref, x1_ref, lg_ref):
    y = ALPHA * x_ref[...] + jnp.dot(mg_ref[...], wo_ref[...], preferred_element_type=F32)
    x1 = _layer_norm(y, g_ref[...], b_ref[...])
    x1_ref[...] = x1
    xh, xl = _split_hi_lo(x1)
    wh = rwh_ref[...]
    wl = rwl_ref[...]
    lg_ref[...] = ((jnp.dot(xh, wh, preferred_element_type=F32) + jnp.dot(xl, wl, preferred_element_type=F32))
                   + (jnp.dot(xl, wh, preferred_element_type=F32) + jnp.dot(xh, wl, preferred_element_type=F32)))


def _oproj(merged, w_o, x, ln_g, ln_b, router_w, tm):
    n, d = x.shape
    e = router_w.shape[1]
    rw_hi, rw_lo = _split_hi_lo(jnp.pad(router_w, ((0, 0), (0, ROUTER_LANES - e))))
    full = lambda shape: pl.BlockSpec(shape, lambda i: (0,) * len(shape))
    return pl.pallas_call(
        _oproj_kernel,
        grid=(n // tm,),
        in_specs=[pl.BlockSpec((tm, d), lambda i: (i, 0)), full((d, d)),
                  pl.BlockSpec((tm, d), lambda i: (i, 0)), full((1, d)), full((1, d)),
                  full((d, ROUTER_LANES)), full((d, ROUTER_LANES))],
        out_specs=[pl.BlockSpec((tm, d), lambda i: (i, 0)), pl.BlockSpec((tm, ROUTER_LANES), lambda i: (i, 0))],
        out_shape=[jax.ShapeDtypeStruct((n, d), F32), jax.ShapeDtypeStruct((n, ROUTER_LANES), F32)],
        compiler_params=_cparams(("parallel",)),
        name="out_proj_ln",
    )(merged, w_o.astype(BF16), x, ln_g.reshape(1, d), ln_b.reshape(1, d), rw_hi, rw_lo)


def _first_max(v, idx, sentinel):
    m = jnp.max(v, axis=0, keepdims=True)
    am = jnp.min(jnp.where(v == m, idx, sentinel), axis=0, keepdims=True)
    return m, am


def _route_kernel(lg_ref, bias_ref, tri_ref, te_ref, w_ref, rk_ref, cnt_ref, carry_ref, *, tm):
    @pl.when(pl.program_id(0) == 0)
    def _():
        carry_ref[...] = jnp.zeros_like(carry_ref)

    e = N_EXPERTS
    gs = e // N_GROUPS
    scores = jax.nn.sigmoid(jnp.transpose(lg_ref[...])[0:e, :])
    biased = scores + bias_ref[...]
    sub = lax.broadcasted_iota(I32, (gs, tm), 0)
    grp_rows = []
    for g in range(N_GROUPS):
        slab = biased[g * gs:(g + 1) * gs, :]
        m1, a1 = _first_max(slab, sub, gs)
        m2 = jnp.max(jnp.where(sub == a1, NEG_INF, slab), axis=0, keepdims=True)
        grp_rows.append(m1 + m2)
    grp = jnp.concatenate(grp_rows, axis=0)
    gidx = lax.broadcasted_iota(I32, (N_GROUPS, tm), 0)
    gsel = jnp.zeros((N_GROUPS, tm), F32)
    for _ in range(TOPK_GROUPS):
        _, am = _first_max(grp, gidx, N_GROUPS)
        hit = gidx == am
        gsel = jnp.where(hit, 1.0, gsel)
        grp = jnp.where(hit, NEG_INF, grp)
    masked = jnp.concatenate(
        [jnp.where(gsel[g:g + 1, :] > 0.0, biased[g * gs:(g + 1) * gs, :], NEG_INF) for g in range(N_GROUPS)],
        axis=0)
    eidx = lax.broadcasted_iota(I32, (e, tm), 0)
    member = jnp.zeros((e, tm), F32)
    tops = []
    ws = []
    for _ in range(TOP_K):
        _, am = _first_max(masked, eidx, e)
        hit = eidx == am
        tops.append(am)
        ws.append(jnp.sum(jnp.where(hit, scores, 0.0), axis=0, keepdims=True))
        member = jnp.where(hit, 1.0, member)
        masked = jnp.where(hit, NEG_INF, masked)
    wsum = ws[0]
    for k in range(1, TOP_K):
        wsum = wsum + ws[k]
    te_ref[...] = jnp.concatenate(tops, axis=0)
    w_ref[...] = jnp.concatenate(ws, axis=0) / wsum * ROUTED_SCALE

    cum = jnp.dot(member.astype(BF16), tri_ref[...], preferred_element_type=F32)
    carry = carry_ref[:, 0:1]
    rank = carry + cum - member
    rks = []
    for k in range(TOP_K):
        rks.append(jnp.sum(jnp.where(eidx == tops[k], rank, 0.0), axis=0, keepdims=True))
    rk_ref[...] = jnp.concatenate(rks, axis=0).astype(I32)
    new_carry = carry + cum[:, tm - 1:tm]
    carry_ref[...] = jnp.broadcast_to(new_carry, carry_ref.shape)
    cnt_ref[...] = jnp.broadcast_to(new_carry, cnt_ref.shape).astype(I32)


def _route(logits, router_bias, tm):
    n = logits.shape[0]
    e = router_bias.shape[0]
    tri = jnp.triu(jnp.ones((tm, tm), F32)).astype(BF16)
    return pl.pallas_call(
        functools.partial(_route_kernel, tm=tm),
        grid=(n // tm,),
        in_specs=[pl.BlockSpec((tm, ROUTER_LANES), lambda i: (i, 0)),
                  pl.BlockSpec((e, 1), lambda i: (0, 0)),
                  pl.BlockSpec((tm, tm), lambda i: (0, 0))],
        out_specs=[pl.BlockSpec((TOP_K, tm), lambda i: (0, i)),
                   pl.BlockSpec((TOP_K, tm), lambda i: (0, i)),
                   pl.BlockSpec((TOP_K, tm), lambda i: (0, i)),
                   pl.BlockSpec((e, 128), lambda i: (0, 0))],
        out_shape=[jax.ShapeDtypeStruct((TOP_K, n), I32), jax.ShapeDtypeStruct((TOP_K, n), F32),
                   jax.ShapeDtypeStruct((TOP_K, n), I32), jax.ShapeDtypeStruct((e, 128), I32)],
        scratch_shapes=[pltpu.VMEM((e, 128), F32)],
        compiler_params=_cparams(("arbitrary",)),
        name="route",
    )(logits, router_bias.reshape(e, 1), tri)


def _dest_kernel(te_ref, rk_ref, ps_ref, d_ref):
    te = te_ref[...]
    e = N_EXPERTS
    tm = te.shape[1]
    eidx = lax.broadcasted_iota(I32, (e, tm), 0)
    ps = ps_ref[...]
    rows = []
    for k in range(TOP_K):
        rows.append(jnp.sum(jnp.where(eidx == te[k:k + 1, :], ps, 0), axis=0, keepdims=True))
    d_ref[...] = jnp.concatenate(rows, axis=0) + rk_ref[...]


def _dest(top_e, rank, starts, tm):
    n = top_e.shape[1]
    return pl.pallas_call(
        _dest_kernel,
        grid=(n // tm,),
        in_specs=[pl.BlockSpec((TOP_K, tm), lambda i: (0, i)),
                  pl.BlockSpec((TOP_K, tm), lambda i: (0, i)),
                  pl.BlockSpec((N_EXPERTS, 1), lambda i: (0, 0))],
        out_specs=pl.BlockSpec((TOP_K, tm), lambda i: (0, i)),
        out_shape=jax.ShapeDtypeStruct((TOP_K, n), I32),
        compiler_params=_cparams(("parallel",)),
        name="dest",
    )(top_e, rank, starts.reshape(N_EXPERTS, 1))


def _invert_kernel(dest_ref, inv_ref):
    def body(j, c):
        inv_ref[dest_ref[j]] = j
        return c

    lax.fori_loop(0, dest_ref.shape[0], body, 0, unroll=32)


def _invert_permutation(dest_flat):
    m = dest_flat.shape[0]
    return pl.pallas_call(
        _invert_kernel,
        in_specs=[pl.BlockSpec(memory_space=pltpu.SMEM)],
        out_specs=pl.BlockSpec(memory_space=pltpu.SMEM),
        out_shape=jax.ShapeDtypeStruct((m,), I32),
        name="invert_perm",
    )(dest_flat)


def _expert_ffn(xb, wgb_ref, wub_ref, wdb_ref):
    hg = jnp.dot(xb, wgb_ref[...], preferred_element_type=F32)
    hu = jnp.dot(xb, wub_ref[...], preferred_element_type=F32)
    hh = (hg * jax.nn.sigmoid(hg)) * hu
    return jnp.dot(hh.astype(BF16), wdb_ref[...], preferred_element_type=F32)


def _expert_kernel(vt_ref, ve_ref, lo_ref, hi_ref, inv_ref, nxt_ref, e0_ref, par_ref,
                   x_hbm, wg_hbm, wu_hbm, wd_hbm, out_hbm,
                   wgf_ref, wuf_ref, wdf_ref, wgb_ref, wub_ref, wdb_ref, xbuf, obuf, xb_ref, cur_ref,
                   gsem, ssem, wsem, *, n_tok, n_tiles):
    v = pl.program_id(0)
    rows = xbuf.shape[1]
    tile = vt_ref[v]
    lo = lo_ref[v]
    hi = hi_ref[v]
    nonempty = hi > lo
    first = jnp.logical_and(nonempty, lo == 0)
    slot = lax.rem(tile, GATHER_SLOTS)
    oslot = tile % 2
    ahead = GATHER_SLOTS - 1

    def gather_row(t, r):
        src_tile = jnp.minimum(t, n_tiles - 1)
        tok = inv_ref[src_tile * rows + r] & (n_tok - 1)
        s = lax.rem(t, GATHER_SLOTS)
        return pltpu.make_async_copy(x_hbm.at[pl.ds(tok, 1)], xbuf.at[s, pl.ds(r, 1)], gsem.at[s])

    def tile_writeback(t):
        s = t % 2
        r0 = pl.multiple_of(t * rows, rows)
        return pltpu.make_async_copy(obuf.at[s], out_hbm.at[pl.ds(r0, rows)], ssem.at[s])

    def wait_gathered_tile(s):
        pltpu.make_async_copy(x_hbm.at[pl.ds(0, rows)], xbuf.at[s], gsem.at[s]).wait()

    def weight_copies(e):
        s = par_ref[e]
        return (pltpu.make_async_copy(wg_hbm.at[e], wgf_ref.at[s], wsem.at[0, s]),
                pltpu.make_async_copy(wu_hbm.at[e], wuf_ref.at[s], wsem.at[1, s]),
                pltpu.make_async_copy(wd_hbm.at[e], wdf_ref.at[s], wsem.at[2, s]))

    @pl.when(v == 0)
    def _():
        cur_ref[0] = -1
        for c in weight_copies(e0_ref[0]):
            c.start()

    @pl.when(jnp.logical_and(nonempty, cur_ref[0] != ve_ref[v]))
    def _():
        e = ve_ref[v]
        for c in weight_copies(e):
            c.wait()
        ne = nxt_ref[e]

        @pl.when(ne >= 0)
        def _():
            for c in weight_copies(ne):
                c.start()

        s = par_ref[e]
        wgb_ref[...] = wgf_ref[s].astype(BF16)
        wub_ref[...] = wuf_ref[s].astype(BF16)
        wdb_ref[...] = wdf_ref[s].astype(BF16)
        cur_ref[0] = e

    @pl.when(jnp.logical_and(first, tile == 0))
    def _():
        def issue(i, c):
            gather_row(i // rows, i % rows).start(priority=ROW_COPY_PRIORITY)
            return c

        lax.fori_loop(0, ahead * rows, issue, 0)

    @pl.when(jnp.logical_and(first, tile >= 2))
    def _():
        tile_writeback(tile - 2).wait()

    @pl.when(first)
    def _():
        wait_gathered_tile(slot)

    half = rows // 2

    @pl.when(jnp.logical_and(first, hi > half))
    def _():
        xb_ref[...] = xbuf[slot].astype(BF16)
        for r in range(rows):
            gather_row(tile + ahead, r).start(priority=ROW_COPY_PRIORITY)
        obuf[oslot] = _expert_ffn(xb_ref[...], wgb_ref, wub_ref, wdb_ref)

    @pl.when(jnp.logical_and(first, hi <= half))
    def _():
        xb_ref[0:half, :] = xbuf[slot, 0:half, :].astype(BF16)
        for r in range(rows):
            gather_row(tile + ahead, r).start(priority=ROW_COPY_PRIORITY)
        obuf[oslot, 0:half, :] = _expert_ffn(xb_ref[0:half, :], wgb_ref, wub_ref, wdb_ref)
        obuf[oslot, half:rows, :] = jnp.zeros((rows - half, obuf.shape[2]), F32)

    @pl.when(jnp.logical_and(nonempty, jnp.logical_and(lo > 0, lo < half)))
    def _():
        y = _expert_ffn(xbuf[slot].astype(BF16), wgb_ref, wub_ref, wdb_ref)
        row = lax.broadcasted_iota(I32, (rows, 1), 0)
        mine = jnp.logical_and(row >= lo, row < hi)
        obuf[oslot] = jnp.where(mine, y, obuf[oslot])

    @pl.when(jnp.logical_and(nonempty, lo >= half))
    def _():
        y = _expert_ffn(xbuf[slot, half:rows, :].astype(BF16), wgb_ref, wub_ref, wdb_ref)
        row = half + lax.broadcasted_iota(I32, (rows - half, 1), 0)
        mine = jnp.logical_and(row >= lo, row < hi)
        obuf[oslot, half:rows, :] = jnp.where(mine, y, obuf[oslot, half:rows, :])

    @pl.when(jnp.logical_and(nonempty, hi == rows))
    def _():
        tile_writeback(tile).start()

    @pl.when(v == pl.num_programs(0) - 1)
    def _():
        last = n_tiles - 1
        tile_writeback(last - 1).wait()
        tile_writeback(last).wait()
        for t in range(last + 1, last + 1 + ahead):
            wait_gathered_tile(t % GATHER_SLOTS)


def _experts(x1, inv, visit_tile, visit_expert, visit_lo, visit_hi, next_expert, first_expert, parity,
             w_gate, w_up, w_down):
    n_tok, d = x1.shape
    m = inv.shape[0]
    de = w_gate.shape[2]
    n_tiles = m // EXPERT_TILE
    any_spec = pl.BlockSpec(memory_space=pl.ANY)
    grid_spec = pltpu.PrefetchScalarGridSpec(
        num_scalar_prefetch=8,
        grid=(visit_tile.shape[0],),
        in_specs=[any_spec, any_spec, any_spec, any_spec],
        out_specs=any_spec,
        scratch_shapes=[pltpu.VMEM((2, d, de), F32), pltpu.VMEM((2, d, de), F32), pltpu.VMEM((2, de, d), F32),
                        pltpu.VMEM((d, de), BF16), pltpu.VMEM((d, de), BF16), pltpu.VMEM((de, d), BF16),
                        pltpu.VMEM((GATHER_SLOTS, EXPERT_TILE, d), F32), pltpu.VMEM((2, EXPERT_TILE, d), F32),
                        pltpu.VMEM((EXPERT_TILE, d), BF16), pltpu.SMEM((1,), I32),
                        pltpu.SemaphoreType.DMA((GATHER_SLOTS,)), pltpu.SemaphoreType.DMA((2,)),
                        pltpu.SemaphoreType.DMA((3, 2))],
    )
    return pl.pallas_call(
        functools.partial(_expert_kernel, n_tok=n_tok, n_tiles=n_tiles),
        grid_spec=grid_spec,
        out_shape=jax.ShapeDtypeStruct((m, d), F32),
        compiler_params=_cparams(("arbitrary",)),
        name="experts",
    )(visit_tile, visit_expert, visit_lo, visit_hi, inv, next_expert, first_expert, parity,
      x1, w_gate, w_up, w_down)


def _combine_kernel(dcur_ref, dnxt_ref, x1_ref, w_ref, ys_hbm, sg_ref, su_ref, sd_ref, g_ref, b_ref, o_ref,
                    *scratch):
    bufs = scratch[:COMBINE_PHASES]
    sem = scratch[COMBINE_PHASES]
    j = pl.program_id(0)
    tq = bufs[0].shape[1]
    ahead = 2

    def row_copy(dref, col, p, k, t):
        return pltpu.make_async_copy(ys_hbm.at[pl.ds(dref[k, col + t], 1)], bufs[p].at[k, pl.ds(t, 1)], sem.at[p])

    def wait_phase(p):
        pltpu.make_async_copy(bufs[p], bufs[p], sem.at[p]).wait()

    @pl.when(j == 0)
    def _():
        for p in range(ahead):
            def issue(t, c, p=p):
                for k in range(TOP_K):
                    row_copy(dcur_ref, p * tq, p, k, t).start(priority=ROW_COPY_PRIORITY)
                return c

            lax.fori_loop(0, tq, issue, 0)

    def request(q):
        dref, qq = (dcur_ref, q) if q < COMBINE_PHASES else (dnxt_ref, q - COMBINE_PHASES)
        for t in range(tq):
            for k in range(TOP_K):
                row_copy(dref, qq * tq, q % COMBINE_PHASES, k, t).start(priority=k % 2)

    request(ahead)
    x1 = x1_ref[...]
    xb = x1.astype(BF16)
    hg = jnp.dot(xb, sg_ref[...], preferred_element_type=F32)
    hu = jnp.dot(xb, su_ref[...], preferred_element_type=F32)
    hh = (hg * jax.nn.sigmoid(hg)) * hu
    shared = jnp.dot(hh.astype(BF16), sd_ref[...], preferred_element_type=F32)
    base = ALPHA * x1 + shared
    w = w_ref[...]

    for p in range(COMBINE_PHASES):
        wait_phase(p)
        if p + 1 < COMBINE_PHASES:
            request(p + 1 + ahead)
        rs = slice(p * tq, (p + 1) * tq)
        routed = bufs[p][0] * w[rs, 0:1]
        for k in range(1, TOP_K):
            routed = routed + bufs[p][k] * w[rs, k:k + 1]
        o_ref[rs, :] = _layer_norm(base[rs, :] + routed, g_ref[...], b_ref[...])

    @pl.when(j == pl.num_programs(0) - 1)
    def _():
        for p in range(ahead):
            wait_phase(p)


def _combine(x1, dest, w_tok, ys, s_gate, s_up, s_down, ln_g, ln_b, tq):
    n, d = x1.shape
    de = s_gate.shape[1]
    tm = COMBINE_PHASES * tq
    steps = n // tm
    full = lambda shape: pl.BlockSpec(shape, lambda i: (0,) * len(shape))
    return pl.pallas_call(
        _combine_kernel,
        grid=(steps,),
        in_specs=[pl.BlockSpec((TOP_K, tm), lambda i: (0, i), memory_space=pltpu.SMEM),
                  pl.BlockSpec((TOP_K, tm), lambda i: (0, jnp.minimum(i + 1, steps - 1)), memory_space=pltpu.SMEM),
                  pl.BlockSpec((tm, d), lambda i: (i, 0)),
                  pl.BlockSpec((tm, TOP_K), lambda i: (i, 0)),
                  pl.BlockSpec(memory_space=pl.ANY),
                  full((d, de)), full((d, de)), full((de, d)), full((1, d)), full((1, d))],
        out_specs=pl.BlockSpec((tm, d), lambda i: (i, 0)),
        out_shape=jax.ShapeDtypeStruct((n, d), F32),
        scratch_shapes=[pltpu.VMEM((TOP_K, tq, d), F32) for _ in range(COMBINE_PHASES)]
        + [pltpu.SemaphoreType.DMA((COMBINE_PHASES,))],
        compiler_params=_cparams(("arbitrary",)),
        name="combine",
    )(dest, dest, x1, w_tok, ys, s_gate.astype(BF16), s_up.astype(BF16), s_down.astype(BF16),
      ln_g.reshape(1, d), ln_b.reshape(1, d))


def _visit_plan(counts, n_rows):
    e = counts.shape[0]
    n_tiles = n_rows // EXPERT_TILE
    ends = jnp.cumsum(counts)
    starts = ends - counts
    pos = jnp.sort(jnp.concatenate([jnp.arange(n_tiles, dtype=I32) * EXPERT_TILE, starts]))
    nxt = jnp.concatenate([pos[1:], jnp.full((1,), n_rows, I32)])
    tile = jnp.minimum(pos // EXPERT_TILE, n_tiles - 1)
    expert = jnp.minimum(jnp.sum((ends[None, :] <= pos[:, None]).astype(I32), axis=1), e - 1)
    ids = jnp.arange(e, dtype=I32)
    later = jnp.logical_and(ids[None, :] > ids[:, None], counts[None, :] > 0)
    next_expert = jnp.min(jnp.where(later, ids[None, :], e), axis=1)
    next_expert = jnp.where(next_expert < e, next_expert, -1).astype(I32)
    first_expert = jnp.min(jnp.where(counts > 0, ids, e - 1)).astype(I32).reshape(1)
    parity = ((jnp.cumsum((counts > 0).astype(I32)) - 1) % 2).astype(I32)
    return (starts, tile, expert, pos - tile * EXPERT_TILE, nxt - tile * EXPERT_TILE, next_expert, first_expert,
            parity)


def _moe(x1, logits, router_bias, e_w_gate, e_w_up, e_w_down, s_w_gate, s_w_up, s_w_down, ln_g, ln_b):
    n, d = x1.shape
    top_e, w_t, rank, counts = _route(logits, router_bias, tm=TILES["route_tokens"])
    starts, v_tile, v_expert, v_lo, v_hi, next_e, first_e, parity = _visit_plan(counts[:, 0], n * TOP_K)
    dest = _dest(top_e, rank, starts, tm=TILES["dest_tokens"])
    inv = _invert_permutation(dest.reshape(n * TOP_K))
    ys = _experts(x1, inv, v_tile, v_expert, v_lo, v_hi, next_e, first_e, parity, e_w_gate, e_w_up, e_w_down)
    return _combine(x1, dest, w_t.T, ys, s_w_gate, s_w_up, s_w_down, ln_g, ln_b,
                    tq=TILES["combine_subtile_tokens"])


def _layer(x, w_in, b_gate, m_conv_w, m_conv_b, m_wq, m_wk, m_wv, m_w_if, m_b_if, m_norm_w, m_skip,
           r_conv_w, r_conv_b, r_wa, r_ba, r_wx, r_bx, r_lambda, w_pm, w_pr, w_o, ln1_g, ln1_b,
           router_w, router_bias, e_w_gate, e_w_up, e_w_down, s_w_gate, s_w_up, s_w_down, ln2_g, ln2_b):
    batch, seq, d = x.shape
    n = batch * seq
    xt = x.reshape(n, d)
    proj = _in_proj(xt.astype(BF16), w_in, bm=TILES["in_proj_rows"], bn=TILES["in_proj_cols"])
    o_xr = 2 * M_WIDTH
    o_xg = o_xr + R_WIDTH
    o_gate = o_xg + R_WIDTH
    xc, q, k, v, g, gt = _mlstm_prep(proj, seq, m_conv_w, m_conv_b, m_wq, m_wk, m_wv, m_w_if, m_b_if,
                                     tm=TILES["mlstm_prep_rows"])
    y_m = _mlstm(q, k, v, g, gt, xc, proj, m_norm_w, m_skip, batch, seq, MLSTM_CHUNK)
    y_r = _rglru(proj, batch, seq, o_xr, o_xg, r_conv_w, r_conv_b, r_wa, r_ba, r_wx, r_bx, r_lambda,
                 tm=TILES["rglru_rows"], cw=TILES["rglru_cols"])
    merged = _merge(y_m, y_r, w_pm, w_pr, proj, o_gate, b_gate, tm=TILES["merge_rows"], bn=TILES["merge_cols"])
    x1, logits = _oproj(merged, w_o, xt, ln1_g, ln1_b, router_w, tm=TILES["out_proj_rows"])
    out = _moe(x1, logits, router_bias, e_w_gate, e_w_up, e_w_down, s_w_gate, s_w_up, s_w_down, ln2_g, ln2_b)
    return out.reshape(batch, seq, d)


def kernel(x, w_in, b_gate, m_conv_w, m_conv_b, m_wq, m_wk, m_wv, m_w_if, m_b_if, m_norm_w, m_skip, r_conv_w, r_conv_b, r_wa, r_ba, r_wx, r_bx, r_lambda, w_pm, w_pr, w_o, ln1_g, ln1_b, router_w, router_bias, e_w_gate, e_w_up, e_w_down, s_w_gate, s_w_up, s_w_down, ln2_g, ln2_b):
    for l in range(DEPTH):
        x = _layer(x, w_in[l], b_gate[l], m_conv_w[l], m_conv_b[l], m_wq[l], m_wk[l], m_wv[l],
                   m_w_if[l], m_b_if[l], m_norm_w[l], m_skip[l], r_conv_w[l], r_conv_b[l],
                   r_wa[l], r_ba[l], r_wx[l], r_bx[l], r_lambda[l], w_pm[l], w_pr[l], w_o[l],
                   ln1_g[l], ln1_b[l], router_w[l], router_bias[l], e_w_gate[l], e_w_up[l],
                   e_w_down[l], s_w_gate[l], s_w_up[l], s_w_down[l], ln2_g[l], ln2_b[l])
    return x
```

```python
import functools
import math

import jax
import jax.numpy as jnp
from jax import lax
from jax.experimental import pallas as pl
from jax.experimental.pallas import tpu as pltpu

F32 = jnp.float32
BF16 = jnp.bfloat16
I32 = jnp.int32
U32 = jnp.uint32

M_WIDTH = 2048
M_HEADS = 8
M_HEAD_DIM = 256
CONV_WIDTH = 4
R_WIDTH = 2560
R_BLOCK = 256
LRU_C = 8.0
N_EXPERTS = 64
TOP_K = 8
N_GROUPS = 8
TOPK_GROUPS = 4
ROUTED_SCALE = 2.5
DEPTH = 1
ALPHA = (2.0 * DEPTH) ** 0.25
LN_EPS = 1e-5

V7X_VMEM_LIMIT = 56 * 1024 * 1024
HALO = 8
MLSTM_CHUNK = 256
EXPERT_TILE = 256
NEG_INF = float("-inf")
GATE_LANES = 128
ROUTER_LANES = 128
GATHER_SLOTS = 3
COMBINE_PHASES = 4
ROW_COPY_PRIORITY = 1


TILES = dict(
    in_proj_rows=1024, in_proj_cols=1024,
    mlstm_prep_rows=512,
    rglru_rows=2048, rglru_cols=512,
    merge_rows=512, merge_cols=1024,
    out_proj_rows=256,
    route_tokens=512, dest_tokens=1024,
    combine_subtile_tokens=64,
)


def _cparams(sem, vmem=V7X_VMEM_LIMIT):
    return pltpu.CompilerParams(dimension_semantics=sem, vmem_limit_bytes=vmem)


def _inproj_kernel(a_ref, w_ref, o_ref, wb_ref):
    @pl.when(pl.program_id(1) == 0)
    def _():
        wb_ref[...] = w_ref[...].astype(BF16)

    o_ref[...] = jnp.dot(a_ref[...], wb_ref[...], preferred_element_type=F32)


def _in_proj(a, w, bm, bn):
    m, k = a.shape
    n = w.shape[1]
    return pl.pallas_call(
        _inproj_kernel,
        grid=(n // bn, m // bm),
        in_specs=[pl.BlockSpec((bm, k), lambda j, i: (i, 0)),
                  pl.BlockSpec((k, bn), lambda j, i: (0, j))],
        out_specs=pl.BlockSpec((bm, bn), lambda j, i: (i, j)),
        out_shape=jax.ShapeDtypeStruct((m, n), F32),
        scratch_shapes=[pltpu.VMEM((k, bn), BF16)],
        compiler_params=_cparams(("parallel", "arbitrary")),
        name="in_proj",
    )(a, w)


def _sigmoid(x):
    return 0.5 * jnp.tanh(0.5 * x) + 0.5


def _log_sigmoid(x):
    return jnp.minimum(x, 0.0) - jnp.log1p(jnp.exp(-jnp.abs(x)))


def _shift_rows(x3, prev_group, j):
    rot = pltpu.roll(x3, j, axis=1)
    prev = jnp.concatenate([pltpu.roll(prev_group, j, axis=1), rot[:-1]], axis=0)
    sub = lax.broadcasted_iota(I32, x3.shape, 1)
    return jnp.where(sub >= j, rot, prev)


def _causal_conv(halo, x, cw_ref, cb_ref):
    tm, c = x.shape
    x3 = x.reshape(tm // HALO, HALO, c)
    h3 = halo.reshape(1, HALO, c)
    last = CONV_WIDTH - 1
    y = cb_ref[...] + x3 * cw_ref[last:last + 1, :]
    for j in range(1, CONV_WIDTH):
        y = y + _shift_rows(x3, h3, j) * cw_ref[last - j:last - j + 1, :]
    return y


def _mprep_kernel(xm_ref, halo_ref, cw_ref, cb_ref, wq_ref, wk_ref, wv_ref, wif_ref, bif_ref,
                  xc_ref, q_ref, k_ref, v_ref, g_ref, gt_ref, *, tm, tiles_per_seq):
    i = pl.program_id(0)
    first = (i % tiles_per_seq) == 0
    halo = jnp.where(first, 0.0, halo_ref[...])
    xm = xm_ref[...]
    y = _causal_conv(halo, xm, cw_ref, cb_ref).reshape(tm, M_WIDTH)
    xc = y * _sigmoid(y)
    xc_ref[...] = xc
    xcb = xc.astype(BF16)
    xmb = xm.astype(BF16)
    nblk = M_WIDTH // M_HEAD_DIM
    for g in range(nblk):
        sl = slice(g * M_HEAD_DIM, (g + 1) * M_HEAD_DIM)
        q_ref[:, sl] = jnp.dot(xcb[:, sl], wq_ref[g], preferred_element_type=F32).astype(BF16)
        k_ref[:, sl] = jnp.dot(xcb[:, sl], wk_ref[g], preferred_element_type=F32).astype(BF16)
        v_ref[:, sl] = jnp.dot(xmb[:, sl], wv_ref[g], preferred_element_type=F32).astype(BF16)
    qb, kb, vb = q_ref[...], k_ref[...], v_ref[...]
    w = M_WIDTH
    ng = g_ref.shape[1]
    g = (jnp.dot(qb, wif_ref[0:w, :], preferred_element_type=F32)
         + jnp.dot(kb, wif_ref[w:2 * w, :], preferred_element_type=F32)
         + jnp.dot(vb, wif_ref[2 * w:3 * w, :], preferred_element_type=F32) + bif_ref[...])
    col = lax.broadcasted_iota(I32, g.shape, 1)
    g = jnp.where(col >= M_HEADS, _log_sigmoid(g), g)
    g_ref[...] = g[:, 0:ng]
    gt_ref[...] = jnp.transpose(g)[0:ng, :]


def _block_diag_kernel(w_ref, o_ref, *, bi):
    w = w_ref[0]
    group = o_ref.shape[1]
    row = lax.broadcasted_iota(I32, (group, group), 0)
    col = lax.broadcasted_iota(I32, (group, group), 1)
    dense = jnp.zeros((group, group), F32)
    for o in range(w.shape[1]):
        dense = jnp.where(col % w.shape[1] == o, w[:, o:o + 1], dense)
    o_ref[0] = jnp.where(row // bi == col // w.shape[1], dense, 0.0).astype(o_ref.dtype)


def _block_diag_dense(ws, group):
    nb, bi, bo = ws[0].shape
    slabs = len(ws) * nb * bi // group
    w3 = jnp.concatenate(ws, axis=0).reshape(slabs, group, bo)
    return pl.pallas_call(
        functools.partial(_block_diag_kernel, bi=bi),
        grid=(slabs,),
        in_specs=[pl.BlockSpec((1, group, bo), lambda i: (i, 0, 0))],
        out_specs=pl.BlockSpec((1, group, group), lambda i: (i, 0, 0)),
        out_shape=jax.ShapeDtypeStruct((slabs, group, group), BF16),
        compiler_params=_cparams(("parallel",)),
        name="block_diag",
    )(w3)


def _mlstm_prep(proj, seq, conv_w, conv_b, wq, wk, wv, w_if, b_if, tm):
    n = proj.shape[0]
    c = M_WIDTH
    nblk = c // M_HEAD_DIM
    tiles_per_seq = seq // tm
    wd = _block_diag_dense([wq, wk, wv], M_HEAD_DIM)
    ng = 2 * M_HEADS
    wif = jnp.pad(w_if, ((0, 0), (0, GATE_LANES - ng))).astype(BF16)
    bif = jnp.pad(b_if, (0, GATE_LANES - ng)).reshape(1, GATE_LANES)
    hb = tm // HALO
    full = lambda shape: pl.BlockSpec(shape, lambda i: (0,) * len(shape))
    return pl.pallas_call(
        functools.partial(_mprep_kernel, tm=tm, tiles_per_seq=tiles_per_seq),
        grid=(n // tm,),
        in_specs=[pl.BlockSpec((tm, c), lambda i: (i, 0)),
                  pl.BlockSpec((HALO, c), lambda i: (jnp.maximum(i * hb - 1, 0), 0)),
                  full((CONV_WIDTH, c)), full((1, c)),
                  pl.BlockSpec((nblk, M_HEAD_DIM, M_HEAD_DIM), lambda i: (0, 0, 0)),
                  pl.BlockSpec((nblk, M_HEAD_DIM, M_HEAD_DIM), lambda i: (1, 0, 0)),
                  pl.BlockSpec((nblk, M_HEAD_DIM, M_HEAD_DIM), lambda i: (2, 0, 0)),
                  full((3 * c, GATE_LANES)), full((1, GATE_LANES))],
        out_specs=[pl.BlockSpec((tm, c), lambda i: (i, 0)),
                   pl.BlockSpec((tm, c), lambda i: (i, 0)),
                   pl.BlockSpec((tm, c), lambda i: (i, 0)),
                   pl.BlockSpec((tm, c), lambda i: (i, 0)),
                   pl.BlockSpec((tm, ng), lambda i: (i, 0)),
                   pl.BlockSpec((ng, tm), lambda i: (0, i))],
        out_shape=[jax.ShapeDtypeStruct((n, c), F32),
                   jax.ShapeDtypeStruct((n, c), BF16),
                   jax.ShapeDtypeStruct((n, c), BF16),
                   jax.ShapeDtypeStruct((n, c), BF16),
                   jax.ShapeDtypeStruct((n, ng), F32),
                   jax.ShapeDtypeStruct((ng, n), F32)],
        compiler_params=_cparams(("parallel",)),
        name="mlstm_prep",
    )(proj, proj, conv_w, conv_b.reshape(1, c), wd, wd, wd, wif, bif)


def _mlstm_kernel(q_ref, k_ref, v_ref, g_ref, gt_ref, xc_ref, z_ref, nw_ref, sk_ref, o_ref,
                  c_ref, n_ref, m_ref, *, chunk):
    L = chunk
    hd = M_HEAD_DIM

    @pl.when(pl.program_id(1) == 0)
    def _():
        c_ref[...] = jnp.zeros_like(c_ref)
        n_ref[...] = jnp.zeros_like(n_ref)
        m_ref[...] = jnp.zeros_like(m_ref)

    rows = lax.broadcasted_iota(I32, (L, L), 0)
    cols = lax.broadcasted_iota(I32, (L, L), 1)
    causal = cols <= rows
    tril = jnp.where(causal, 1.0, 0.0).astype(F32)
    triu = jnp.where(rows <= cols, 1.0, 0.0).astype(F32)
    g = g_ref[...]
    gt = gt_ref[...]
    hi = lax.Precision.HIGHEST
    bcol_all = jnp.dot(tril, g, precision=hi, preferred_element_type=F32)
    brow_all = jnp.dot(gt, triu, precision=hi, preferred_element_type=F32)
    k_scale = hd ** -0.5
    nt = (((1,), (1,)), ((), ()))
    tn = (((0,), (0,)), ((), ()))

    for h in range(M_HEADS):
        sl = slice(h * hd, (h + 1) * hd)
        qh = q_ref[:, sl]
        kh = k_ref[:, sl]
        vh = v_ref[:, sl]
        i_col = g[:, h:h + 1]
        b_col = bcol_all[:, M_HEADS + h:M_HEADS + h + 1]
        i_row = gt[h:h + 1, :]
        b_row = brow_all[M_HEADS + h:M_HEADS + h + 1, :]
        m_prev = m_ref[h, 0:1, 0:1]
        c_prev = c_ref[h]
        n_prev = n_ref[h]

        dmat = jnp.where(causal, b_col - b_row + i_row, NEG_INF)
        a_col = b_col + m_prev
        m_row = jnp.maximum(a_col, jnp.max(dmat, axis=1, keepdims=True))
        s = lax.dot_general(qh, kh, nt, preferred_element_type=F32) * k_scale
        s = s * jnp.exp(dmat - m_row)
        inter = jnp.exp(a_col - m_row)
        num = inter * jnp.dot(qh, c_prev.astype(BF16), preferred_element_type=F32) \
            + jnp.dot(s.astype(BF16), vh, preferred_element_type=F32)
        qn = jnp.sum(qh.astype(F32) * n_prev, axis=1, keepdims=True)
        den = inter * qn + jnp.sum(s, axis=1, keepdims=True)
        hval = num * (1.0 / jnp.maximum(jnp.abs(den), jnp.exp(-m_row)))

        mu = jnp.mean(hval, axis=1, keepdims=True)
        cen = hval - mu
        var = jnp.mean(cen * cen, axis=1, keepdims=True)
        hn = cen * lax.rsqrt(var + LN_EPS) * nw_ref[:, sl]
        zz = z_ref[:, sl]
        o_ref[:, sl] = ((hn + sk_ref[:, sl] * xc_ref[:, sl]) * (zz * _sigmoid(zz))).astype(o_ref.dtype)

        b_last = b_col[L - 1:L, :]
        w_log = b_last - b_col + i_col
        m_new = jnp.maximum(b_last + m_prev, jnp.max(w_log, axis=0, keepdims=True))
        decay = jnp.exp(b_last + m_prev - m_new)
        kw = kh.astype(F32) * (jnp.exp(w_log - m_new) * k_scale)
        c_ref[h] = decay * c_prev + lax.dot_general(kw.astype(BF16), vh, tn, preferred_element_type=F32)
        n_ref[h] = decay * n_prev + jnp.sum(kw, axis=0, keepdims=True)
        m_ref[h] = jnp.broadcast_to(m_new, m_ref.shape[1:])


def _mlstm(q, k, v, g, gt, xc, proj, norm_w, skip, batch, seq, chunk):
    n, c = q.shape
    nc = seq // chunk
    ng = 2 * M_HEADS
    zcol = M_WIDTH // c
    row = lambda b, j: (b * nc + j, 0)
    return pl.pallas_call(
        functools.partial(_mlstm_kernel, chunk=chunk),
        grid=(batch, nc),
        in_specs=[pl.BlockSpec((chunk, c), row), pl.BlockSpec((chunk, c), row), pl.BlockSpec((chunk, c), row),
                  pl.BlockSpec((chunk, ng), row),
                  pl.BlockSpec((ng, chunk), lambda b, j: (0, b * nc + j)),
                  pl.BlockSpec((chunk, c), row),
                  pl.BlockSpec((chunk, c), lambda b, j: (b * nc + j, zcol)),
                  pl.BlockSpec((1, c), lambda b, j: (0, 0)),
                  pl.BlockSpec((1, c), lambda b, j: (0, 0))],
        out_specs=pl.BlockSpec((chunk, c), row),
        out_shape=jax.ShapeDtypeStruct((n, c), BF16),
        scratch_shapes=[pltpu.VMEM((M_HEADS, M_HEAD_DIM, M_HEAD_DIM), F32),
                        pltpu.VMEM((M_HEADS, 1, M_HEAD_DIM), F32),
                        pltpu.VMEM((M_HEADS, 8, 128), F32)],
        compiler_params=_cparams(("parallel", "arbitrary")),
        name="mlstm_chunk",
    )(q, k, v, g, gt, xc, proj, norm_w.reshape(1, c), skip.reshape(1, c))


def _gelu_tanh(x):
    return 0.5 * x * (1.0 + jnp.tanh(math.sqrt(2.0 / math.pi) * (x + 0.044715 * (x * x * x))))


def _rglru_kernel(xr_ref, halo_ref, xg_ref, cw_ref, cb_ref, wa_ref, ba_ref, wx_ref, bx_ref, lam_ref,
                  o_ref, a_ref, b_ref, h_ref, *, tm, cw):
    t = pl.program_id(2)

    @pl.when(t == 0)
    def _():
        h_ref[...] = jnp.zeros_like(h_ref)

    halo = jnp.where(t == 0, 0.0, halo_ref[...])
    xc = _causal_conv(halo, xr_ref[...], cw_ref, cb_ref).reshape(tm, cw)
    xcb = xc.astype(BF16)
    nblk = cw // R_BLOCK
    ra = []
    rx = []
    for g in range(nblk):
        sl = slice(g * R_BLOCK, (g + 1) * R_BLOCK)
        ra.append(jnp.dot(xcb[:, sl], wa_ref[g], preferred_element_type=F32))
        rx.append(jnp.dot(xcb[:, sl], wx_ref[g], preferred_element_type=F32))
    t_r = jnp.tanh(0.5 * (jnp.concatenate(ra, axis=1) + ba_ref[...]))
    ig = _sigmoid(jnp.concatenate(rx, axis=1) + bx_ref[...])
    nl = -lam_ref[...]
    softplus = jnp.maximum(nl, 0.0) + jnp.log1p(jnp.exp(-jnp.abs(nl)))
    c2 = (-0.5 * LRU_C * math.log2(math.e)) * softplus
    a = jnp.exp2(c2 * t_r + c2)
    v = 1.0 - a * a
    b = jnp.where(v > 0.0, v * lax.rsqrt(v), 0.0) * (ig * xc)

    a = a.reshape(tm // 8, 8, cw)
    b = b.reshape(tm // 8, 8, cw)
    sub = lax.broadcasted_iota(I32, a.shape, 1)
    for d in (1, 2, 4):
        keep = sub >= d
        a_sh = pltpu.roll(a, d, axis=1)
        b_sh = pltpu.roll(b, d, axis=1)
        b = jnp.where(keep, a * b_sh + b, b)
        a = jnp.where(keep, a * a_sh, a)
    a_ref[...] = a.reshape(tm, cw)
    b_ref[...] = b.reshape(tm, cw)

    def body(g, h):
        r0 = pl.multiple_of(g * 8, 8)
        hh = b_ref[pl.ds(r0, 8), :] + a_ref[pl.ds(r0, 8), :] * h
        b_ref[pl.ds(r0, 8), :] = hh
        return hh[7:8, :]

    h_ref[...] = lax.fori_loop(0, tm // 8, body, h_ref[...], unroll=8)
    o_ref[...] = (b_ref[...] * _gelu_tanh(xg_ref[...])).astype(o_ref.dtype)


def _rglru(proj, batch, seq, xr_off, xg_off, conv_w, conv_b, wa, ba, wx, bx, lam, tm, cw):
    n = proj.shape[0]
    ncol = R_WIDTH // cw
    nt = seq // tm
    per = cw // R_BLOCK
    xr_cb = xr_off // cw
    xg_cb = xg_off // cw
    hb = tm // HALO
    colv = lambda shape: pl.BlockSpec(shape, lambda b, j, t: (0, j))
    return pl.pallas_call(
        functools.partial(_rglru_kernel, tm=tm, cw=cw),
        grid=(batch, ncol, nt),
        in_specs=[pl.BlockSpec((tm, cw), lambda b, j, t: (b * nt + t, xr_cb + j)),
                  pl.BlockSpec((HALO, cw), lambda b, j, t: (jnp.maximum((b * nt + t) * hb - 1, 0), xr_cb + j)),
                  pl.BlockSpec((tm, cw), lambda b, j, t: (b * nt + t, xg_cb + j)),
                  colv((CONV_WIDTH, cw)), colv((1, cw)),
                  pl.BlockSpec((per, R_BLOCK, R_BLOCK), lambda b, j, t: (j, 0, 0)), colv((1, cw)),
                  pl.BlockSpec((per, R_BLOCK, R_BLOCK), lambda b, j, t: (j, 0, 0)), colv((1, cw)),
                  colv((1, cw))],
        out_specs=pl.BlockSpec((tm, cw), lambda b, j, t: (b * nt + t, j)),
        out_shape=jax.ShapeDtypeStruct((n, R_WIDTH), BF16),
        scratch_shapes=[pltpu.VMEM((tm, cw), F32), pltpu.VMEM((tm, cw), F32), pltpu.VMEM((1, cw), F32)],
        compiler_params=_cparams(("parallel", "parallel", "arbitrary")),
        name="rglru",
    )(proj, proj, proj, conv_w, conv_b.reshape(1, R_WIDTH), wa.astype(BF16), ba.reshape(1, R_WIDTH),
      wx.astype(BF16), bx.reshape(1, R_WIDTH), lam.reshape(1, R_WIDTH))


def _merge_kernel(ym_ref, yr_ref, wpm_ref, wpr_ref, g0_ref, g1_ref, bg_ref, o_ref):
    g0 = jax.nn.sigmoid(g0_ref[...] + bg_ref[0:1, :])
    g1 = jax.nn.sigmoid(g1_ref[...] + bg_ref[1:2, :])
    pm = jnp.dot(ym_ref[...], wpm_ref[...], preferred_element_type=F32)
    pr = jnp.dot(yr_ref[...], wpr_ref[...], preferred_element_type=F32)
    o_ref[...] = (g0 * pm + g1 * pr).astype(o_ref.dtype)


def _merge(ym, yr, w_pm, w_pr, proj, gate_off, b_gate, tm, bn):
    n = ym.shape[0]
    d = w_pm.shape[1]
    g0_cb = gate_off // bn
    g1_cb = (gate_off + d) // bn
    return pl.pallas_call(
        _merge_kernel,
        grid=(d // bn, n // tm),
        in_specs=[pl.BlockSpec((tm, ym.shape[1]), lambda j, i: (i, 0)),
                  pl.BlockSpec((tm, yr.shape[1]), lambda j, i: (i, 0)),
                  pl.BlockSpec((w_pm.shape[0], bn), lambda j, i: (0, j)),
                  pl.BlockSpec((w_pr.shape[0], bn), lambda j, i: (0, j)),
                  pl.BlockSpec((tm, bn), lambda j, i: (i, g0_cb + j)),
                  pl.BlockSpec((tm, bn), lambda j, i: (i, g1_cb + j)),
                  pl.BlockSpec((2, bn), lambda j, i: (0, j))],
        out_specs=pl.BlockSpec((tm, bn), lambda j, i: (i, j)),
        out_shape=jax.ShapeDtypeStruct((n, d), BF16),
        compiler_params=_cparams(("parallel", "parallel")),
        name="merge",
    )(ym, yr, w_pm.astype(BF16), w_pr.astype(BF16), proj, proj, b_gate)


def _layer_norm(y, g, b):
    mu = jnp.mean(y, axis=1, keepdims=True)
    cen = y - mu
    var = jnp.mean(cen * cen, axis=1, keepdims=True)
    return cen * lax.rsqrt(var + LN_EPS) * g + b


def _split_hi_lo(x):
    hi = lax.bitcast_convert_type(lax.bitcast_convert_type(x, U32) & jnp.uint32(0xFFFF0000), F32)
    return hi.astype(BF16), (x - hi).astype(BF16)


def _oproj_kernel(mg_ref, wo_ref, x_ref, g_ref, b_ref, rwh_ref, rwl_ref, x1_ref, lg_ref):
    y = ALPHA * x_ref[...] + jnp.dot(mg_ref[...], wo_ref[...], preferred_element_type=F32)
    x1 = _layer_norm(y, g_ref[...], b_ref[...])
    x1_ref[...] = x1
    xh, xl = _split_hi_lo(x1)
    wh = rwh_ref[...]
    wl = rwl_ref[...]
    lg_ref[...] = ((jnp.dot(xh, wh, preferred_element_type=F32) + jnp.dot(xl, wl, preferred_element_type=F32))
                   + (jnp.dot(xl, wh, preferred_element_type=F32) + jnp.dot(xh, wl, preferred_element_type=F32)))


def _oproj(merged, w_o, x, ln_g, ln_b, router_w, tm):
    n, d = x.shape
    e = router_w.shape[1]
    rw_hi, rw_lo = _split_hi_lo(jnp.pad(router_w, ((0, 0), (0, ROUTER_LANES - e))))
    full = lambda shape: pl.BlockSpec(shape, lambda i: (0,) * len(shape))
    return pl.pallas_call(
        _oproj_kernel,
        grid=(n // tm,),
        in_specs=[pl.BlockSpec((tm, d), lambda i: (i, 0)), full((d, d)),
                  pl.BlockSpec((tm, d), lambda i: (i, 0)), full((1, d)), full((1, d)),
                  full((d, ROUTER_LANES)), full((d, ROUTER_LANES))],
        out_specs=[pl.BlockSpec((tm, d), lambda i: (i, 0)), pl.BlockSpec((tm, ROUTER_LANES), lambda i: (i, 0))],
        out_shape=[jax.ShapeDtypeStruct((n, d), F32), jax.ShapeDtypeStruct((n, ROUTER_LANES), F32)],
        compiler_params=_cparams(("parallel",)),
        name="out_proj_ln",
    )(merged, w_o.astype(BF16), x, ln_g.reshape(1, d), ln_b.reshape(1, d), rw_hi, rw_lo)


def _first_max(v, idx, sentinel):
    m = jnp.max(v, axis=0, keepdims=True)
    am = jnp.min(jnp.where(v == m, idx, sentinel), axis=0, keepdims=True)
    return m, am


def _route_kernel(lg_ref, bias_ref, tri_ref, te_ref, w_ref, rk_ref, cnt_ref, carry_ref, *, tm):
    @pl.when(pl.program_id(0) == 0)
    def _():
        carry_ref[...] = jnp.zeros_like(carry_ref)

    e = N_EXPERTS
    gs = e // N_GROUPS
    scores = jax.nn.sigmoid(jnp.transpose(lg_ref[...])[0:e, :])
    biased = scores + bias_ref[...]
    sub = lax.broadcasted_iota(I32, (gs, tm), 0)
    grp_rows = []
    for g in range(N_GROUPS):
        slab = biased[g * gs:(g + 1) * gs, :]
        m1, a1 = _first_max(slab, sub, gs)
        m2 = jnp.max(jnp.where(sub == a1, NEG_INF, slab), axis=0, keepdims=True)
        grp_rows.append(m1 + m2)
    grp = jnp.concatenate(grp_rows, axis=0)
    gidx = lax.broadcasted_iota(I32, (N_GROUPS, tm), 0)
    gsel = jnp.zeros((N_GROUPS, tm), F32)
    for _ in range(TOPK_GROUPS):
        _, am = _first_max(grp, gidx, N_GROUPS)
        hit = gidx == am
        gsel = jnp.where(hit, 1.0, gsel)
        grp = jnp.where(hit, NEG_INF, grp)
    masked = jnp.concatenate(
        [jnp.where(gsel[g:g + 1, :] > 0.0, biased[g * gs:(g + 1) * gs, :], NEG_INF) for g in range(N_GROUPS)],
        axis=0)
    eidx = lax.broadcasted_iota(I32, (e, tm), 0)
    member = jnp.zeros((e, tm), F32)
    tops = []
    ws = []
    for _ in range(TOP_K):
        _, am = _first_max(masked, eidx, e)
        hit = eidx == am
        tops.append(am)
        ws.append(jnp.sum(jnp.where(hit, scores, 0.0), axis=0, keepdims=True))
        member = jnp.where(hit, 1.0, member)
        masked = jnp.where(hit, NEG_INF, masked)
    wsum = ws[0]
    for k in range(1, TOP_K):
        wsum = wsum + ws[k]
    te_ref[...] = jnp.concatenate(tops, axis=0)
    w_ref[...] = jnp.concatenate(ws, axis=0) / wsum * ROUTED_SCALE

    cum = jnp.dot(member.astype(BF16), tri_ref[...], preferred_element_type=F32)
    carry = carry_ref[:, 0:1]
    rank = carry + cum - member
    rks = []
    for k in range(TOP_K):
        rks.append(jnp.sum(jnp.where(eidx == tops[k], rank, 0.0), axis=0, keepdims=True))
    rk_ref[...] = jnp.concatenate(rks, axis=0).astype(I32)
    new_carry = carry + cum[:, tm - 1:tm]
    carry_ref[...] = jnp.broadcast_to(new_carry, carry_ref.shape)
    cnt_ref[...] = jnp.broadcast_to(new_carry, cnt_ref.shape).astype(I32)


def _route(logits, router_bias, tm):
    n = logits.shape[0]
    e = router_bias.shape[0]
    tri = jnp.triu(jnp.ones((tm, tm), F32)).astype(BF16)
    return pl.pallas_call(
        functools.partial(_route_kernel, tm=tm),
        grid=(n // tm,),
        in_specs=[pl.BlockSpec((tm, ROUTER_LANES), lambda i: (i, 0)),
                  pl.BlockSpec((e, 1), lambda i: (0, 0)),
                  pl.BlockSpec((tm, tm), lambda i: (0, 0))],
        out_specs=[pl.BlockSpec((TOP_K, tm), lambda i: (0, i)),
                   pl.BlockSpec((TOP_K, tm), lambda i: (0, i)),
                   pl.BlockSpec((TOP_K, tm), lambda i: (0, i)),
                   pl.BlockSpec((e, 128), lambda i: (0, 0))],
        out_shape=[jax.ShapeDtypeStruct((TOP_K, n), I32), jax.ShapeDtypeStruct((TOP_K, n), F32),
                   jax.ShapeDtypeStruct((TOP_K, n), I32), jax.ShapeDtypeStruct((e, 128), I32)],
        scratch_shapes=[pltpu.VMEM((e, 128), F32)],
        compiler_params=_cparams(("arbitrary",)),
        name="route",
    )(logits, router_bias.reshape(e, 1), tri)


def _dest_kernel(te_ref, rk_ref, ps_ref, d_ref):
    te = te_ref[...]
    e = N_EXPERTS
    tm = te.shape[1]
    eidx = lax.broadcasted_iota(I32, (e, tm), 0)
    ps = ps_ref[...]
    rows = []
    for k in range(TOP_K):
        rows.append(jnp.sum(jnp.where(eidx == te[k:k + 1, :], ps, 0), axis=0, keepdims=True))
    d_ref[...] = jnp.concatenate(rows, axis=0) + rk_ref[...]


def _dest(top_e, rank, starts, tm):
    n = top_e.shape[1]
    return pl.pallas_call(
        _dest_kernel,
        grid=(n // tm,),
        in_specs=[pl.BlockSpec((TOP_K, tm), lambda i: (0, i)),
                  pl.BlockSpec((TOP_K, tm), lambda i: (0, i)),
                  pl.BlockSpec((N_EXPERTS, 1), lambda i: (0, 0))],
        out_specs=pl.BlockSpec((TOP_K, tm), lambda i: (0, i)),
        out_shape=jax.ShapeDtypeStruct((TOP_K, n), I32),
        compiler_params=_cparams(("parallel",)),
        name="dest",
    )(top_e, rank, starts.reshape(N_EXPERTS, 1))


def _invert_kernel(dest_ref, inv_ref):
    def body(j, c):
        inv_ref[dest_ref[j]] = j
        return c

    lax.fori_loop(0, dest_ref.shape[0], body, 0, unroll=32)


def _invert_permutation(dest_flat):
    m = dest_flat.shape[0]
    return pl.pallas_call(
        _invert_kernel,
        in_specs=[pl.BlockSpec(memory_space=pltpu.SMEM)],
        out_specs=pl.BlockSpec(memory_space=pltpu.SMEM),
        out_shape=jax.ShapeDtypeStruct((m,), I32),
        name="invert_perm",
    )(dest_flat)


def _expert_ffn(xb, wgb_ref, wub_ref, wdb_ref):
    hg = jnp.dot(xb, wgb_ref[...], preferred_element_type=F32)
    hu = jnp.dot(xb, wub_ref[...], preferred_element_type=F32)
    hh = (hg * jax.nn.sigmoid(hg)) * hu
    return jnp.dot(hh.astype(BF16), wdb_ref[...], preferred_element_type=F32)


def _expert_kernel(vt_ref, ve_ref, lo_ref, hi_ref, inv_ref, nxt_ref, e0_ref, par_ref,
                   x_hbm, wg_hbm, wu_hbm, wd_hbm, out_hbm,
                   wgf_ref, wuf_ref, wdf_ref, wgb_ref, wub_ref, wdb_ref, xbuf, obuf, xb_ref, cur_ref,
                   gsem, ssem, wsem, *, n_tok, n_tiles):
    v = pl.program_id(0)
    rows = xbuf.shape[1]
    tile = vt_ref[v]
    lo = lo_ref[v]
    hi = hi_ref[v]
    nonempty = hi > lo
    first = jnp.logical_and(nonempty, lo == 0)
    slot = lax.rem(tile, GATHER_SLOTS)
    oslot = tile % 2
    ahead = GATHER_SLOTS - 1

    def gather_row(t, r):
        src_tile = jnp.minimum(t, n_tiles - 1)
        tok = inv_ref[src_tile * rows + r] & (n_tok - 1)
        s = lax.rem(t, GATHER_SLOTS)
        return pltpu.make_async_copy(x_hbm.at[pl.ds(tok, 1)], xbuf.at[s, pl.ds(r, 1)], gsem.at[s])

    def tile_writeback(t):
        s = t % 2
        r0 = pl.multiple_of(t * rows, rows)
        return pltpu.make_async_copy(obuf.at[s], out_hbm.at[pl.ds(r0, rows)], ssem.at[s])

    def wait_gathered_tile(s):
        pltpu.make_async_copy(x_hbm.at[pl.ds(0, rows)], xbuf.at[s], gsem.at[s]).wait()

    def weight_copies(e):
        s = par_ref[e]
        return (pltpu.make_async_copy(wg_hbm.at[e], wgf_ref.at[s], wsem.at[0, s]),
                pltpu.make_async_copy(wu_hbm.at[e], wuf_ref.at[s], wsem.at[1, s]),
                pltpu.make_async_copy(wd_hbm.at[e], wdf_ref.at[s], wsem.at[2, s]))

    @pl.when(v == 0)
    def _():
        cur_ref[0] = -1
        for c in weight_copies(e0_ref[0]):
            c.start()

    @pl.when(jnp.logical_and(nonempty, cur_ref[0] != ve_ref[v]))
    def _():
        e = ve_ref[v]
        for c in weight_copies(e):
            c.wait()
        ne = nxt_ref[e]

        @pl.when(ne >= 0)
        def _():
            for c in weight_copies(ne):
                c.start()

        s = par_ref[e]
        wgb_ref[...] = wgf_ref[s].astype(BF16)
        wub_ref[...] = wuf_ref[s].astype(BF16)
        wdb_ref[...] = wdf_ref[s].astype(BF16)
        cur_ref[0] = e

    @pl.when(jnp.logical_and(first, tile == 0))
    def _():
        def issue(i, c):
            gather_row(i // rows, i % rows).start(priority=ROW_COPY_PRIORITY)
            return c

        lax.fori_loop(0, ahead * rows, issue, 0)

    @pl.when(jnp.logical_and(first, tile >= 2))
    def _():
        tile_writeback(tile - 2).wait()

    @pl.when(first)
    def _():
        wait_gathered_tile(slot)

    half = rows // 2

    @pl.when(jnp.logical_and(first, hi > half))
    def _():
        xb_ref[...] = xbuf[slot].astype(BF16)
        for r in range(rows):
            gather_row(tile + ahead, r).start(priority=ROW_COPY_PRIORITY)
        obuf[oslot] = _expert_ffn(xb_ref[...], wgb_ref, wub_ref, wdb_ref)

    @pl.when(jnp.logical_and(first, hi <= half))
    def _():
        xb_ref[0:half, :] = xbuf[slot, 0:half, :].astype(BF16)
        for r in range(rows):
            gather_row(tile + ahead, r).start(priority=ROW_COPY_PRIORITY)
        obuf[oslot, 0:half, :] = _expert_ffn(xb_ref[0:half, :], wgb_ref, wub_ref, wdb_ref)
        obuf[oslot, half:rows, :] = jnp.zeros((rows - half, obuf.shape[2]), F32)

    @pl.when(jnp.logical_and(nonempty, jnp.logical_and(lo > 0, lo < half)))
    def _():
        y = _expert_ffn(xbuf[slot].astype(BF16), wgb_ref, wub_ref, wdb_ref)
        row = lax.broadcasted_iota(I32, (rows, 1), 0)
        mine = jnp.logical_and(row >= lo, row < hi)
        obuf[oslot] = jnp.where(mine, y, obuf[oslot])

    @pl.when(jnp.logical_and(nonempty, lo >= half))
    def _():
        y = _expert_ffn(xbuf[slot, half:rows, :].astype(BF16), wgb_ref, wub_ref, wdb_ref)
        row = half + lax.broadcasted_iota(I32, (rows - half, 1), 0)
        mine = jnp.logical_and(row >= lo, row < hi)
        obuf[oslot, half:rows, :] = jnp.where(mine, y, obuf[oslot, half:rows, :])

    @pl.when(jnp.logical_and(nonempty, hi == rows))
    def _():
        tile_writeback(tile).start()

    @pl.when(v == pl.num_programs(0) - 1)
    def _():
        last = n_tiles - 1
        tile_writeback(last - 1).wait()
        tile_writeback(last).wait()
        for t in range(last + 1, last + 1 + ahead):
            wait_gathered_tile(t % GATHER_SLOTS)


def _experts(x1, inv, visit_tile, visit_expert, visit_lo, visit_hi, next_expert, first_expert, parity,
             w_gate, w_up, w_down):
    n_tok, d = x1.shape
    m = inv.shape[0]
    de = w_gate.shape[2]
    n_tiles = m // EXPERT_TILE
    any_spec = pl.BlockSpec(memory_space=pl.ANY)
    grid_spec = pltpu.PrefetchScalarGridSpec(
        num_scalar_prefetch=8,
        grid=(visit_tile.shape[0],),
        in_specs=[any_spec, any_spec, any_spec, any_spec],
        out_specs=any_spec,
        scratch_shapes=[pltpu.VMEM((2, d, de), F32), pltpu.VMEM((2, d, de), F32), pltpu.VMEM((2, de, d), F32),
                        pltpu.VMEM((d, de), BF16), pltpu.VMEM((d, de), BF16), pltpu.VMEM((de, d), BF16),
                        pltpu.VMEM((GATHER_SLOTS, EXPERT_TILE, d), F32), pltpu.VMEM((2, EXPERT_TILE, d), F32),
                        pltpu.VMEM((EXPERT_TILE, d), BF16), pltpu.SMEM((1,), I32),
                        pltpu.SemaphoreType.DMA((GATHER_SLOTS,)), pltpu.SemaphoreType.DMA((2,)),
                        pltpu.SemaphoreType.DMA((3, 2))],
    )
    return pl.pallas_call(
        functools.partial(_expert_kernel, n_tok=n_tok, n_tiles=n_tiles),
        grid_spec=grid_spec,
        out_shape=jax.ShapeDtypeStruct((m, d), F32),
        compiler_params=_cparams(("arbitrary",)),
        name="experts",
    )(visit_tile, visit_expert, visit_lo, visit_hi, inv, next_expert, first_expert, parity,
      x1, w_gate, w_up, w_down)


def _combine_kernel(dcur_ref, dnxt_ref, x1_ref, w_ref, ys_hbm, sg_ref, su_ref, sd_ref, g_ref, b_ref, o_ref,
                    *scratch):
    bufs = scratch[:COMBINE_PHASES]
    sem = scratch[COMBINE_PHASES]
    j = pl.program_id(0)
    tq = bufs[0].shape[1]
    ahead = 2

    def row_copy(dref, col, p, k, t):
        return pltpu.make_async_copy(ys_hbm.at[pl.ds(dref[k, col + t], 1)], bufs[p].at[k, pl.ds(t, 1)], sem.at[p])

    def wait_phase(p):
        pltpu.make_async_copy(bufs[p], bufs[p], sem.at[p]).wait()

    @pl.when(j == 0)
    def _():
        for p in range(ahead):
            def issue(t, c, p=p):
                for k in range(TOP_K):
                    row_copy(dcur_ref, p * tq, p, k, t).start(priority=ROW_COPY_PRIORITY)
                return c

            lax.fori_loop(0, tq, issue, 0)

    def request(q):
        dref, qq = (dcur_ref, q) if q < COMBINE_PHASES else (dnxt_ref, q - COMBINE_PHASES)
        for t in range(tq):
            for k in range(TOP_K):
                row_copy(dref, qq * tq, q % COMBINE_PHASES, k, t).start(priority=k % 2)

    request(ahead)
    x1 = x1_ref[...]
    xb = x1.astype(BF16)
    hg = jnp.dot(xb, sg_ref[...], preferred_element_type=F32)
    hu = jnp.dot(xb, su_ref[...], preferred_element_type=F32)
    hh = (hg * jax.nn.sigmoid(hg)) * hu
    shared = jnp.dot(hh.astype(BF16), sd_ref[...], preferred_element_type=F32)
    base = ALPHA * x1 + shared
    w = w_ref[...]

    for p in range(COMBINE_PHASES):
        wait_phase(p)
        if p + 1 < COMBINE_PHASES:
            request(p + 1 + ahead)
        rs = slice(p * tq, (p + 1) * tq)
        routed = bufs[p][0] * w[rs, 0:1]
        for k in range(1, TOP_K):
            routed = routed + bufs[p][k] * w[rs, k:k + 1]
        o_ref[rs, :] = _layer_norm(base[rs, :] + routed, g_ref[...], b_ref[...])

    @pl.when(j == pl.num_programs(0) - 1)
    def _():
        for p in range(ahead):
            wait_phase(p)


def _combine(x1, dest, w_tok, ys, s_gate, s_up, s_down, ln_g, ln_b, tq):
    n, d = x1.shape
    de = s_gate.shape[1]
    tm = COMBINE_PHASES * tq
    steps = n // tm
    full = lambda shape: pl.BlockSpec(shape, lambda i: (0,) * len(shape))
    return pl.pallas_call(
        _combine_kernel,
        grid=(steps,),
        in_specs=[pl.BlockSpec((TOP_K, tm), lambda i: (0, i), memory_space=pltpu.SMEM),
                  pl.BlockSpec((TOP_K, tm), lambda i: (0, jnp.minimum(i + 1, steps - 1)), memory_space=pltpu.SMEM),
                  pl.BlockSpec((tm, d), lambda i: (i, 0)),
                  pl.BlockSpec((tm, TOP_K), lambda i: (i, 0)),
                  pl.BlockSpec(memory_space=pl.ANY),
                  full((d, de)), full((d, de)), full((de, d)), full((1, d)), full((1, d))],
        out_specs=pl.BlockSpec((tm, d), lambda i: (i, 0)),
        out_shape=jax.ShapeDtypeStruct((n, d), F32),
        scratch_shapes=[pltpu.VMEM((TOP_K, tq, d), F32) for _ in range(COMBINE_PHASES)]
        + [pltpu.SemaphoreType.DMA((COMBINE_PHASES,))],
        compiler_params=_cparams(("arbitrary",)),
        name="combine",
    )(dest, dest, x1, w_tok, ys, s_gate.astype(BF16), s_up.astype(BF16), s_down.astype(BF16),
      ln_g.reshape(1, d), ln_b.reshape(1, d))


def _visit_plan(counts, n_rows):
    e = counts.shape[0]
    n_tiles = n_rows // EXPERT_TILE
    ends = jnp.cumsum(counts)
    starts = ends - counts
    pos = jnp.sort(jnp.concatenate([jnp.arange(n_tiles, dtype=I32) * EXPERT_TILE, starts]))
    nxt = jnp.concatenate([pos[1:], jnp.full((1,), n_rows, I32)])
    tile = jnp.minimum(pos // EXPERT_TILE, n_tiles - 1)
    expert = jnp.minimum(jnp.sum((ends[None, :] <= pos[:, None]).astype(I32), axis=1), e - 1)
    ids = jnp.arange(e, dtype=I32)
    later = jnp.logical_and(ids[None, :] > ids[:, None], counts[None, :] > 0)
    next_expert = jnp.min(jnp.where(later, ids[None, :], e), axis=1)
    next_expert = jnp.where(next_expert < e, next_expert, -1).astype(I32)
    first_expert = jnp.min(jnp.where(counts > 0, ids, e - 1)).astype(I32).reshape(1)
    parity = ((jnp.cumsum((counts > 0).astype(I32)) - 1) % 2).astype(I32)
    return (starts, tile, expert, pos - tile * EXPERT_TILE, nxt - tile * EXPERT_TILE, next_expert, first_expert,
            parity)


def _moe(x1, logits, router_bias, e_w_gate, e_w_up, e_w_down, s_w_gate, s_w_up, s_w_down, ln_g, ln_b):
    n, d = x1.shape
    top_e, w_t, rank, counts = _route(logits, router_bias, tm=TILES["route_tokens"])
    starts, v_tile, v_expert, v_lo, v_hi, next_e, first_e, parity = _visit_plan(counts[:, 0], n * TOP_K)
    dest = _dest(top_e, rank, starts, tm=TILES["dest_tokens"])
    inv = _invert_permutation(dest.reshape(n * TOP_K))
    ys = _experts(x1, inv, v_tile, v_expert, v_lo, v_hi, next_e, first_e, parity, e_w_gate, e_w_up, e_w_down)
    return _combine(x1, dest, w_t.T, ys, s_w_gate, s_w_up, s_w_down, ln_g, ln_b,
                    tq=TILES["combine_subtile_tokens"])


def _layer(x, w_in, b_gate, m_conv_w, m_conv_b, m_wq, m_wk, m_wv, m_w_if, m_b_if, m_norm_w, m_skip,
           r_conv_w, r_conv_b, r_wa, r_ba, r_wx, r_bx, r_lambda, w_pm, w_pr, w_o, ln1_g, ln1_b,
           router_w, router_bias, e_w_gate, e_w_up, e_w_down, s_w_gate, s_w_up, s_w_down, ln2_g, ln2_b):
    batch, seq, d = x.shape
    n = batch * seq
    xt = x.reshape(n, d)
    proj = _in_proj(xt.astype(BF16), w_in, bm=TILES["in_proj_rows"], bn=TILES["in_proj_cols"])
    o_xr = 2 * M_WIDTH
    o_xg = o_xr + R_WIDTH
    o_gate = o_xg + R_WIDTH
    xc, q, k, v, g, gt = _mlstm_prep(proj, seq, m_conv_w, m_conv_b, m_wq, m_wk, m_wv, m_w_if, m_b_if,
                                     tm=TILES["mlstm_prep_rows"])
    y_m = _mlstm(q, k, v, g, gt, xc, proj, m_norm_w, m_skip, batch, seq, MLSTM_CHUNK)
    y_r = _rglru(proj, batch, seq, o_xr, o_xg, r_conv_w, r_conv_b, r_wa, r_ba, r_wx, r_bx, r_lambda,
                 tm=TILES["rglru_rows"], cw=TILES["rglru_cols"])
    merged = _merge(y_m, y_r, w_pm, w_pr, proj, o_gate, b_gate, tm=TILES["merge_rows"], bn=TILES["merge_cols"])
    x1, logits = _oproj(merged, w_o, xt, ln1_g, ln1_b, router_w, tm=TILES["out_proj_rows"])
    out = _moe(x1, logits, router_bias, e_w_gate, e_w_up, e_w_down, s_w_gate, s_w_up, s_w_down, ln2_g, ln2_b)
    return out.reshape(batch, seq, d)


def kernel(x, w_in, b_gate, m_conv_w, m_conv_b, m_wq, m_wk, m_wv, m_w_if, m_b_if, m_norm_w, m_skip, r_conv_w, r_conv_b, r_wa, r_ba, r_wx, r_bx, r_lambda, w_pm, w_pr, w_o, ln1_g, ln1_b, router_w, router_bias, e_w_gate, e_w_up, e_w_down, s_w_gate, s_w_up, s_w_down, ln2_g, ln2_b):
    for l in range(DEPTH):
        x = _layer(x, w_in[l], b_gate[l], m_conv_w[l], m_conv_b[l], m_wq[l], m_wk[l], m_wv[l],
                   m_w_if[l], m_b_if[l], m_norm_w[l], m_skip[l], r_conv_w[l], r_conv_b[l],
                   r_wa[l], r_ba[l], r_wx[l], r_bx[l], r_lambda[l], w_pm[l], w_pr[l], w_o[l],
                   ln1_g[l], ln1_b[l], router_w[l], router_bias[l], e_w_gate[l], e_w_up[l],
                   e_w_down[l], s_w_gate[l], s_w_up[l], s_w_down[l], ln2_g[l], ln2_b[l])
    return x
```

```python
import functools
import math

import jax
import jax.numpy as jnp
from jax import lax
from jax.experimental import pallas as pl
from jax.experimental.pallas import tpu as pltpu

F32 = jnp.float32
BF16 = jnp.bfloat16
I32 = jnp.int32
U32 = jnp.uint32

M_WIDTH = 2048
M_HEADS = 8
M_HEAD_DIM = 256
CONV_WIDTH = 4
R_WIDTH = 2560
R_BLOCK = 256
LRU_C = 8.0
N_EXPERTS = 64
TOP_K = 8
N_GROUPS = 8
TOPK_GROUPS = 4
ROUTED_SCALE = 2.5
DEPTH = 1
ALPHA = (2.0 * DEPTH) ** 0.25
LN_EPS = 1e-5

V7X_VMEM_LIMIT = 56 * 1024 * 1024
HALO = 8
MLSTM_CHUNK = 256
EXPERT_TILE = 256
NEG_INF = float("-inf")
GATE_LANES = 128
ROUTER_LANES = 128
GATHER_SLOTS = 4
COMBINE_PHASES = 4
ROW_COPY_PRIORITY = 1


TILES = dict(
    in_proj_rows=1024, in_proj_cols=1024,
    mlstm_prep_rows=512, mlstm_heads_per_step=8,
    rglru_rows=2048, rglru_cols=512,
    merge_rows=512, merge_cols=1024,
    out_proj_rows=256,
    route_tokens=512, dest_tokens=1024,
    combine_subtile_tokens=64,
)


def _cparams(sem, vmem=V7X_VMEM_LIMIT):
    return pltpu.CompilerParams(dimension_semantics=sem, vmem_limit_bytes=vmem)


def _inproj_kernel(a_ref, w_ref, o_ref, wb_ref):
    @pl.when(pl.program_id(1) == 0)
    def _():
        wb_ref[...] = w_ref[...].astype(BF16)

    o_ref[...] = jnp.dot(a_ref[...], wb_ref[...], preferred_element_type=F32)


def _in_proj(a, w, bm, bn):
    m, k = a.shape
    n = w.shape[1]
    return pl.pallas_call(
        _inproj_kernel,
        grid=(n // bn, m // bm),
        in_specs=[pl.BlockSpec((bm, k), lambda j, i: (i, 0)),
                  pl.BlockSpec((k, bn), lambda j, i: (0, j))],
        out_specs=pl.BlockSpec((bm, bn), lambda j, i: (i, j)),
        out_shape=jax.ShapeDtypeStruct((m, n), F32),
        scratch_shapes=[pltpu.VMEM((k, bn), BF16)],
        compiler_params=_cparams(("parallel", "arbitrary")),
        name="in_proj",
    )(a, w)


def _sigmoid(x):
    return 0.5 * jnp.tanh(0.5 * x) + 0.5


def _log_sigmoid(x):
    return jnp.minimum(x, 0.0) - jnp.log1p(jnp.exp(-jnp.abs(x)))


def _shift_rows(x3, prev_group, j):
    rot = pltpu.roll(x3, j, axis=1)
    prev = jnp.concatenate([pltpu.roll(prev_group, j, axis=1), rot[:-1]], axis=0)
    sub = lax.broadcasted_iota(I32, x3.shape, 1)
    return jnp.where(sub >= j, rot, prev)


def _causal_conv(halo, x, cw_ref, cb_ref):
    tm, c = x.shape
    x3 = x.reshape(tm // HALO, HALO, c)
    h3 = halo.reshape(1, HALO, c)
    last = CONV_WIDTH - 1
    y = cb_ref[...] + x3 * cw_ref[last:last + 1, :]
    for j in range(1, CONV_WIDTH):
        y = y + _shift_rows(x3, h3, j) * cw_ref[last - j:last - j + 1, :]
    return y


def _mprep_kernel(xm_ref, halo_ref, cw_ref, cb_ref, wq_ref, wk_ref, wv_ref, wif_ref, bif_ref,
                  xc_ref, q_ref, k_ref, v_ref, g_ref, gt_ref, *, tm, tiles_per_seq):
    i = pl.program_id(0)
    first = (i % tiles_per_seq) == 0
    halo = jnp.where(first, 0.0, halo_ref[...])
    xm = xm_ref[...]
    y = _causal_conv(halo, xm, cw_ref, cb_ref).reshape(tm, M_WIDTH)
    xc = y * _sigmoid(y)
    xc_ref[...] = xc
    xcb = xc.astype(BF16)
    xmb = xm.astype(BF16)
    nblk = M_WIDTH // M_HEAD_DIM
    for g in range(nblk):
        sl = slice(g * M_HEAD_DIM, (g + 1) * M_HEAD_DIM)
        q_ref[:, sl] = jnp.dot(xcb[:, sl], wq_ref[g], preferred_element_type=F32).astype(BF16)
        k_ref[:, sl] = jnp.dot(xcb[:, sl], wk_ref[g], preferred_element_type=F32).astype(BF16)
        v_ref[:, sl] = jnp.dot(xmb[:, sl], wv_ref[g], preferred_element_type=F32).astype(BF16)
    qb, kb, vb = q_ref[...], k_ref[...], v_ref[...]
    w = M_WIDTH
    ng = g_ref.shape[1]
    g = (jnp.dot(qb, wif_ref[0:w, :], preferred_element_type=F32)
         + jnp.dot(kb, wif_ref[w:2 * w, :], preferred_element_type=F32)
         + jnp.dot(vb, wif_ref[2 * w:3 * w, :], preferred_element_type=F32) + bif_ref[...])
    col = lax.broadcasted_iota(I32, g.shape, 1)
    g = jnp.where(col >= M_HEADS, _log_sigmoid(g), g)
    g_ref[...] = g[:, 0:ng]
    gt_ref[...] = jnp.transpose(g)[0:ng, :]


def _block_diag_kernel(w_ref, o_ref, *, bi):
    w = w_ref[0]
    group = o_ref.shape[1]
    row = lax.broadcasted_iota(I32, (group, group), 0)
    col = lax.broadcasted_iota(I32, (group, group), 1)
    dense = jnp.zeros((group, group), F32)
    for o in range(w.shape[1]):
        dense = jnp.where(col % w.shape[1] == o, w[:, o:o + 1], dense)
    o_ref[0] = jnp.where(row // bi == col // w.shape[1], dense, 0.0).astype(o_ref.dtype)


def _block_diag_dense(ws, group):
    nb, bi, bo = ws[0].shape
    slabs = len(ws) * nb * bi // group
    w3 = jnp.concatenate(ws, axis=0).reshape(slabs, group, bo)
    return pl.pallas_call(
        functools.partial(_block_diag_kernel, bi=bi),
        grid=(slabs,),
        in_specs=[pl.BlockSpec((1, group, bo), lambda i: (i, 0, 0))],
        out_specs=pl.BlockSpec((1, group, group), lambda i: (i, 0, 0)),
        out_shape=jax.ShapeDtypeStruct((slabs, group, group), BF16),
        compiler_params=_cparams(("parallel",)),
        name="block_diag",
    )(w3)


def _mlstm_prep(proj, seq, conv_w, conv_b, wq, wk, wv, w_if, b_if, tm):
    n = proj.shape[0]
    c = M_WIDTH
    nblk = c // M_HEAD_DIM
    tiles_per_seq = seq // tm
    wd = _block_diag_dense([wq, wk, wv], M_HEAD_DIM)
    ng = 2 * M_HEADS
    wif = jnp.pad(w_if, ((0, 0), (0, GATE_LANES - ng))).astype(BF16)
    bif = jnp.pad(b_if, (0, GATE_LANES - ng)).reshape(1, GATE_LANES)
    hb = tm // HALO
    full = lambda shape: pl.BlockSpec(shape, lambda i: (0,) * len(shape))
    return pl.pallas_call(
        functools.partial(_mprep_kernel, tm=tm, tiles_per_seq=tiles_per_seq),
        grid=(n // tm,),
        in_specs=[pl.BlockSpec((tm, c), lambda i: (i, 0)),
                  pl.BlockSpec((HALO, c), lambda i: (jnp.maximum(i * hb - 1, 0), 0)),
                  full((CONV_WIDTH, c)), full((1, c)),
                  pl.BlockSpec((nblk, M_HEAD_DIM, M_HEAD_DIM), lambda i: (0, 0, 0)),
                  pl.BlockSpec((nblk, M_HEAD_DIM, M_HEAD_DIM), lambda i: (1, 0, 0)),
                  pl.BlockSpec((nblk, M_HEAD_DIM, M_HEAD_DIM), lambda i: (2, 0, 0)),
                  full((3 * c, GATE_LANES)), full((1, GATE_LANES))],
        out_specs=[pl.BlockSpec((tm, c), lambda i: (i, 0)),
                   pl.BlockSpec((tm, c), lambda i: (i, 0)),
                   pl.BlockSpec((tm, c), lambda i: (i, 0)),
                   pl.BlockSpec((tm, c), lambda i: (i, 0)),
                   pl.BlockSpec((tm, ng), lambda i: (i, 0)),
                   pl.BlockSpec((ng, tm), lambda i: (0, i))],
        out_shape=[jax.ShapeDtypeStruct((n, c), F32),
                   jax.ShapeDtypeStruct((n, c), BF16),
                   jax.ShapeDtypeStruct((n, c), BF16),
                   jax.ShapeDtypeStruct((n, c), BF16),
                   jax.ShapeDtypeStruct((n, ng), F32),
                   jax.ShapeDtypeStruct((ng, n), F32)],
        compiler_params=_cparams(("parallel",)),
        name="mlstm_prep",
    )(proj, proj, conv_w, conv_b.reshape(1, c), wd, wd, wd, wif, bif)


def _mlstm_kernel(q_ref, k_ref, v_ref, g_ref, gt_ref, xc_ref, z_ref, nw_ref, sk_ref, o_ref,
                  c_ref, n_ref, m_ref, *, chunk, heads):
    L = chunk
    hd = M_HEAD_DIM

    @pl.when(pl.program_id(2) == 0)
    def _():
        c_ref[...] = jnp.zeros_like(c_ref)
        n_ref[...] = jnp.zeros_like(n_ref)
        m_ref[...] = jnp.zeros_like(m_ref)

    rows = lax.broadcasted_iota(I32, (L, L), 0)
    cols = lax.broadcasted_iota(I32, (L, L), 1)
    causal = cols <= rows
    tril = jnp.where(causal, 1.0, 0.0).astype(F32)
    triu = jnp.where(rows <= cols, 1.0, 0.0).astype(F32)
    g = g_ref[0]
    gt = gt_ref[0]
    hi = lax.Precision.HIGHEST
    bcol_all = jnp.dot(tril, g, precision=hi, preferred_element_type=F32)
    brow_all = jnp.dot(gt, triu, precision=hi, preferred_element_type=F32)
    k_scale = hd ** -0.5
    nt = (((1,), (1,)), ((), ()))
    tn = (((0,), (0,)), ((), ()))

    for h in range(heads):
        sl = slice(h * hd, (h + 1) * hd)
        qh = q_ref[:, sl]
        kh = k_ref[:, sl]
        vh = v_ref[:, sl]
        i_col = g[:, h:h + 1]
        b_col = bcol_all[:, heads + h:heads + h + 1]
        i_row = gt[h:h + 1, :]
        b_row = brow_all[heads + h:heads + h + 1, :]
        m_prev = m_ref[h, 0:1, 0:1]
        c_prev = c_ref[h]
        n_prev = n_ref[h]

        dmat = jnp.where(causal, b_col - b_row + i_row, NEG_INF)
        a_col = b_col + m_prev
        m_row = jnp.maximum(a_col, jnp.max(dmat, axis=1, keepdims=True))
        s = lax.dot_general(qh, kh, nt, preferred_element_type=F32) * k_scale
        s = s * jnp.exp(dmat - m_row)
        inter = jnp.exp(a_col - m_row)
        num = inter * jnp.dot(qh, c_prev.astype(BF16), preferred_element_type=F32) \
            + jnp.dot(s.astype(BF16), vh, preferred_element_type=F32)
        qn = jnp.sum(qh.astype(F32) * n_prev, axis=1, keepdims=True)
        den = inter * qn + jnp.sum(s, axis=1, keepdims=True)
        hval = num * (1.0 / jnp.maximum(jnp.abs(den), jnp.exp(-m_row)))

        mu = jnp.mean(hval, axis=1, keepdims=True)
        cen = hval - mu
        var = jnp.mean(cen * cen, axis=1, keepdims=True)
        hn = cen * lax.rsqrt(var + LN_EPS) * nw_ref[:, sl]
        zz = z_ref[:, sl]
        o_ref[:, sl] = ((hn + sk_ref[:, sl] * xc_ref[:, sl]) * (zz * _sigmoid(zz))).astype(o_ref.dtype)

        b_last = b_col[L - 1:L, :]
        w_log = b_last - b_col + i_col
        m_new = jnp.maximum(b_last + m_prev, jnp.max(w_log, axis=0, keepdims=True))
        decay = jnp.exp(b_last + m_prev - m_new)
        kw = kh.astype(F32) * (jnp.exp(w_log - m_new) * k_scale)
        c_ref[h] = decay * c_prev + lax.dot_general(kw.astype(BF16), vh, tn, preferred_element_type=F32)
        n_ref[h] = decay * n_prev + jnp.sum(kw, axis=0, keepdims=True)
        m_ref[h] = jnp.broadcast_to(m_new, m_ref.shape[1:])


def _mlstm(q, k, v, g, gt, xc, proj, norm_w, skip, batch, seq, chunk, heads):
    n, c = q.shape
    nc = seq // chunk
    groups = M_HEADS // heads
    cb = heads * M_HEAD_DIM
    zcb = M_WIDTH // cb
    g2 = jnp.stack([jnp.concatenate([g[:, i * heads:(i + 1) * heads],
                                     g[:, M_HEADS + i * heads:M_HEADS + (i + 1) * heads]], axis=1)
                    for i in range(groups)])
    gt2 = jnp.stack([jnp.concatenate([gt[i * heads:(i + 1) * heads],
                                      gt[M_HEADS + i * heads:M_HEADS + (i + 1) * heads]], axis=0)
                     for i in range(groups)])
    row = lambda b, hg, j: (b * nc + j, hg)
    return pl.pallas_call(
        functools.partial(_mlstm_kernel, chunk=chunk, heads=heads),
        grid=(batch, groups, nc),
        in_specs=[pl.BlockSpec((chunk, cb), row), pl.BlockSpec((chunk, cb), row), pl.BlockSpec((chunk, cb), row),
                  pl.BlockSpec((1, chunk, 2 * heads), lambda b, hg, j: (hg, b * nc + j, 0)),
                  pl.BlockSpec((1, 2 * heads, chunk), lambda b, hg, j: (hg, 0, b * nc + j)),
                  pl.BlockSpec((chunk, cb), row),
                  pl.BlockSpec((chunk, cb), lambda b, hg, j: (b * nc + j, zcb + hg)),
                  pl.BlockSpec((1, cb), lambda b, hg, j: (0, hg)),
                  pl.BlockSpec((1, cb), lambda b, hg, j: (0, hg))],
        out_specs=pl.BlockSpec((chunk, cb), row),
        out_shape=jax.ShapeDtypeStruct((n, c), BF16),
        scratch_shapes=[pltpu.VMEM((heads, M_HEAD_DIM, M_HEAD_DIM), F32),
                        pltpu.VMEM((heads, 1, M_HEAD_DIM), F32),
                        pltpu.VMEM((heads, 8, 128), F32)],
        compiler_params=_cparams(("parallel", "parallel", "arbitrary")),
        name="mlstm_chunk",
    )(q, k, v, g2, gt2, xc, proj, norm_w.reshape(1, c), skip.reshape(1, c))


def _gelu_tanh(x):
    return 0.5 * x * (1.0 + jnp.tanh(math.sqrt(2.0 / math.pi) * (x + 0.044715 * (x * x * x))))


def _rglru_kernel(xr_ref, halo_ref, xg_ref, cw_ref, cb_ref, wa_ref, ba_ref, wx_ref, bx_ref, lam_ref,
                  o_ref, a_ref, b_ref, h_ref, *, tm, cw):
    t = pl.program_id(2)

    @pl.when(t == 0)
    def _():
        h_ref[...] = jnp.zeros_like(h_ref)

    halo = jnp.where(t == 0, 0.0, halo_ref[...])
    xc = _causal_conv(halo, xr_ref[...], cw_ref, cb_ref).reshape(tm, cw)
    xcb = xc.astype(BF16)
    nblk = cw // R_BLOCK
    ra = []
    rx = []
    for g in range(nblk):
        sl = slice(g * R_BLOCK, (g + 1) * R_BLOCK)
        ra.append(jnp.dot(xcb[:, sl], wa_ref[g], preferred_element_type=F32))
        rx.append(jnp.dot(xcb[:, sl], wx_ref[g], preferred_element_type=F32))
    t_r = jnp.tanh(0.5 * (jnp.concatenate(ra, axis=1) + ba_ref[...]))
    ig = _sigmoid(jnp.concatenate(rx, axis=1) + bx_ref[...])
    nl = -lam_ref[...]
    softplus = jnp.maximum(nl, 0.0) + jnp.log1p(jnp.exp(-jnp.abs(nl)))
    c2 = (-0.5 * LRU_C * math.log2(math.e)) * softplus
    a = jnp.exp2(c2 * t_r + c2)
    v = 1.0 - a * a
    b = jnp.where(v > 0.0, v * lax.rsqrt(v), 0.0) * (ig * xc)

    a = a.reshape(tm // 8, 8, cw)
    b = b.reshape(tm // 8, 8, cw)
    sub = lax.broadcasted_iota(I32, a.shape, 1)
    for d in (1, 2, 4):
        keep = sub >= d
        a_sh = pltpu.roll(a, d, axis=1)
        b_sh = pltpu.roll(b, d, axis=1)
        b = jnp.where(keep, a * b_sh + b, b)
        a = jnp.where(keep, a * a_sh, a)
    a_ref[...] = a.reshape(tm, cw)
    b_ref[...] = b.reshape(tm, cw)

    def body(g, h):
        r0 = pl.multiple_of(g * 8, 8)
        hh = b_ref[pl.ds(r0, 8), :] + a_ref[pl.ds(r0, 8), :] * h
        b_ref[pl.ds(r0, 8), :] = hh
        return hh[7:8, :]

    h_ref[...] = lax.fori_loop(0, tm // 8, body, h_ref[...], unroll=8)
    o_ref[...] = (b_ref[...] * _gelu_tanh(xg_ref[...])).astype(o_ref.dtype)


def _rglru(proj, batch, seq, xr_off, xg_off, conv_w, conv_b, wa, ba, wx, bx, lam, tm, cw):
    n = proj.shape[0]
    ncol = R_WIDTH // cw
    nt = seq // tm
    per = cw // R_BLOCK
    xr_cb = xr_off // cw
    xg_cb = xg_off // cw
    hb = tm // HALO
    colv = lambda shape: pl.BlockSpec(shape, lambda b, j, t: (0, j))
    return pl.pallas_call(
        functools.partial(_rglru_kernel, tm=tm, cw=cw),
        grid=(batch, ncol, nt),
        in_specs=[pl.BlockSpec((tm, cw), lambda b, j, t: (b * nt + t, xr_cb + j)),
                  pl.BlockSpec((HALO, cw), lambda b, j, t: (jnp.maximum((b * nt + t) * hb - 1, 0), xr_cb + j)),
                  pl.BlockSpec((tm, cw), lambda b, j, t: (b * nt + t, xg_cb + j)),
                  colv((CONV_WIDTH, cw)), colv((1, cw)),
                  pl.BlockSpec((per, R_BLOCK, R_BLOCK), lambda b, j, t: (j, 0, 0)), colv((1, cw)),
                  pl.BlockSpec((per, R_BLOCK, R_BLOCK), lambda b, j, t: (j, 0, 0)), colv((1, cw)),
                  colv((1, cw))],
        out_specs=pl.BlockSpec((tm, cw), lambda b, j, t: (b * nt + t, j)),
        out_shape=jax.ShapeDtypeStruct((n, R_WIDTH), BF16),
        scratch_shapes=[pltpu.VMEM((tm, cw), F32), pltpu.VMEM((tm, cw), F32), pltpu.VMEM((1, cw), F32)],
        compiler_params=_cparams(("parallel", "parallel", "arbitrary")),
        name="rglru",
    )(proj, proj, proj, conv_w, conv_b.reshape(1, R_WIDTH), wa.astype(BF16), ba.reshape(1, R_WIDTH),
      wx.astype(BF16), bx.reshape(1, R_WIDTH), lam.reshape(1, R_WIDTH))


def _merge_kernel(ym_ref, yr_ref, wpm_ref, wpr_ref, g0_ref, g1_ref, bg_ref, o_ref):
    g0 = jax.nn.sigmoid(g0_ref[...] + bg_ref[0:1, :])
    g1 = jax.nn.sigmoid(g1_ref[...] + bg_ref[1:2, :])
    pm = jnp.dot(ym_ref[...], wpm_ref[...], preferred_element_type=F32)
    pr = jnp.dot(yr_ref[...], wpr_ref[...], preferred_element_type=F32)
    o_ref[...] = (g0 * pm + g1 * pr).astype(o_ref.dtype)


def _merge(ym, yr, w_pm, w_pr, proj, gate_off, b_gate, tm, bn):
    n = ym.shape[0]
    d = w_pm.shape[1]
    g0_cb = gate_off // bn
    g1_cb = (gate_off + d) // bn
    return pl.pallas_call(
        _merge_kernel,
        grid=(d // bn, n // tm),
        in_specs=[pl.BlockSpec((tm, ym.shape[1]), lambda j, i: (i, 0)),
                  pl.BlockSpec((tm, yr.shape[1]), lambda j, i: (i, 0)),
                  pl.BlockSpec((w_pm.shape[0], bn), lambda j, i: (0, j)),
                  pl.BlockSpec((w_pr.shape[0], bn), lambda j, i: (0, j)),
                  pl.BlockSpec((tm, bn), lambda j, i: (i, g0_cb + j)),
                  pl.BlockSpec((tm, bn), lambda j, i: (i, g1_cb + j)),
                  pl.BlockSpec((2, bn), lambda j, i: (0, j))],
        out_specs=pl.BlockSpec((tm, bn), lambda j, i: (i, j)),
        out_shape=jax.ShapeDtypeStruct((n, d), BF16),
        compiler_params=_cparams(("parallel", "parallel")),
        name="merge",
    )(ym, yr, w_pm.astype(BF16), w_pr.astype(BF16), proj, proj, b_gate)


def _layer_norm(y, g, b):
    mu = jnp.mean(y, axis=1, keepdims=True)
    cen = y - mu
    var = jnp.mean(cen * cen, axis=1, keepdims=True)
    return cen * lax.rsqrt(var + LN_EPS) * g + b


def _split_hi_lo(x):
    hi = lax.bitcast_convert_type(lax.bitcast_convert_type(x, U32) & jnp.uint32(0xFFFF0000), F32)
    return hi.astype(BF16), (x - hi).astype(BF16)


def _oproj_kernel(mg_ref, wo_ref, x_ref, g_ref, b_ref, rwh_ref, rwl_ref, x1_ref, lg_ref):
    y = ALPHA * x_ref[...] + jnp.dot(mg_ref[...], wo_ref[...], preferred_element_type=F32)
    x1 = _layer_norm(y, g_ref[...], b_ref[...])
    x1_ref[...] = x1
    xh, xl = _split_hi_lo(x1)
    wh = rwh_ref[...]
    wl = rwl_ref[...]
    lg_ref[...] = ((jnp.dot(xh, wh, preferred_element_type=F32) + jnp.dot(xl, wl, preferred_element_type=F32))
                   + (jnp.dot(xl, wh, preferred_element_type=F32) + jnp.dot(xh, wl, preferred_element_type=F32)))


def _oproj(merged, w_o, x, ln_g, ln_b, router_w, tm):
    n, d = x.shape
    e = router_w.shape[1]
    rw_hi, rw_lo = _split_hi_lo(jnp.pad(router_w, ((0, 0), (0, ROUTER_LANES - e))))
    full = lambda shape: pl.BlockSpec(shape, lambda i: (0,) * len(shape))
    return pl.pallas_call(
        _oproj_kernel,
        grid=(n // tm,),
        in_specs=[pl.BlockSpec((tm, d), lambda i: (i, 0)), full((d, d)),
                  pl.BlockSpec((tm, d), lambda i: (i, 0)), full((1, d)), full((1, d)),
                  full((d, ROUTER_LANES)), full((d, ROUTER_LANES))],
        out_specs=[pl.BlockSpec((tm, d), lambda i: (i, 0)), pl.BlockSpec((tm, ROUTER_LANES), lambda i: (i, 0))],
        out_shape=[jax.ShapeDtypeStruct((n, d), F32), jax.ShapeDtypeStruct((n, ROUTER_LANES), F32)],
        compiler_params=_cparams(("parallel",)),
        name="out_proj_ln",
    )(merged, w_o.astype(BF16), x, ln_g.reshape(1, d), ln_b.reshape(1, d), rw_hi, rw_lo)


def _first_max(v, idx, sentinel):
    m = jnp.max(v, axis=0, keepdims=True)
    am = jnp.min(jnp.where(v == m, idx, sentinel), axis=0, keepdims=True)
    return m, am


def _route_kernel(lg_ref, bias_ref, tri_ref, te_ref, w_ref, rk_ref, cnt_ref, carry_ref, *, tm):
    @pl.when(pl.program_id(0) == 0)
    def _():
        carry_ref[...] = jnp.zeros_like(carry_ref)

    e = N_EXPERTS
    gs = e // N_GROUPS
    scores = jax.nn.sigmoid(jnp.transpose(lg_ref[...])[0:e, :])
    biased = scores + bias_ref[...]
    sub = lax.broadcasted_iota(I32, (gs, tm), 0)
    grp_rows = []
    for g in range(N_GROUPS):
        slab = biased[g * gs:(g + 1) * gs, :]
        m1, a1 = _first_max(slab, sub, gs)
        m2 = jnp.max(jnp.where(sub == a1, NEG_INF, slab), axis=0, keepdims=True)
        grp_rows.append(m1 + m2)
    grp = jnp.concatenate(grp_rows, axis=0)
    gidx = lax.broadcasted_iota(I32, (N_GROUPS, tm), 0)
    gsel = jnp.zeros((N_GROUPS, tm), F32)
    for _ in range(TOPK_GROUPS):
        _, am = _first_max(grp, gidx, N_GROUPS)
        hit = gidx == am
        gsel = jnp.where(hit, 1.0, gsel)
        grp = jnp.where(hit, NEG_INF, grp)
    masked = jnp.concatenate(
        [jnp.where(gsel[g:g + 1, :] > 0.0, biased[g * gs:(g + 1) * gs, :], NEG_INF) for g in range(N_GROUPS)],
        axis=0)
    eidx = lax.broadcasted_iota(I32, (e, tm), 0)
    member = jnp.zeros((e, tm), F32)
    tops = []
    ws = []
    for _ in range(TOP_K):
        _, am = _first_max(masked, eidx, e)
        hit = eidx == am
        tops.append(am)
        ws.append(jnp.sum(jnp.where(hit, scores, 0.0), axis=0, keepdims=True))
        member = jnp.where(hit, 1.0, member)
        masked = jnp.where(hit, NEG_INF, masked)
    wsum = ws[0]
    for k in range(1, TOP_K):
        wsum = wsum + ws[k]
    te_ref[...] = jnp.concatenate(tops, axis=0)
    w_ref[...] = jnp.concatenate(ws, axis=0) / wsum * ROUTED_SCALE

    cum = jnp.dot(member.astype(BF16), tri_ref[...], preferred_element_type=F32)
    carry = carry_ref[:, 0:1]
    rank = carry + cum - member
    rks = []
    for k in range(TOP_K):
        rks.append(jnp.sum(jnp.where(eidx == tops[k], rank, 0.0), axis=0, keepdims=True))
    rk_ref[...] = jnp.concatenate(rks, axis=0).astype(I32)
    new_carry = carry + cum[:, tm - 1:tm]
    carry_ref[...] = jnp.broadcast_to(new_carry, carry_ref.shape)
    cnt_ref[...] = jnp.broadcast_to(new_carry, cnt_ref.shape).astype(I32)


def _route(logits, router_bias, tm):
    n = logits.shape[0]
    e = router_bias.shape[0]
    tri = jnp.triu(jnp.ones((tm, tm), F32)).astype(BF16)
    return pl.pallas_call(
        functools.partial(_route_kernel, tm=tm),
        grid=(n // tm,),
        in_specs=[pl.BlockSpec((tm, ROUTER_LANES), lambda i: (i, 0)),
                  pl.BlockSpec((e, 1), lambda i: (0, 0)),
                  pl.BlockSpec((tm, tm), lambda i: (0, 0))],
        out_specs=[pl.BlockSpec((TOP_K, tm), lambda i: (0, i)),
                   pl.BlockSpec((TOP_K, tm), lambda i: (0, i)),
                   pl.BlockSpec((TOP_K, tm), lambda i: (0, i)),
                   pl.BlockSpec((e, 128), lambda i: (0, 0))],
        out_shape=[jax.ShapeDtypeStruct((TOP_K, n), I32), jax.ShapeDtypeStruct((TOP_K, n), F32),
                   jax.ShapeDtypeStruct((TOP_K, n), I32), jax.ShapeDtypeStruct((e, 128), I32)],
        scratch_shapes=[pltpu.VMEM((e, 128), F32)],
        compiler_params=_cparams(("arbitrary",)),
        name="route",
    )(logits, router_bias.reshape(e, 1), tri)


def _dest_kernel(te_ref, rk_ref, ps_ref, d_ref):
    te = te_ref[...]
    e = N_EXPERTS
    tm = te.shape[1]
    eidx = lax.broadcasted_iota(I32, (e, tm), 0)
    ps = ps_ref[...]
    rows = []
    for k in range(TOP_K):
        rows.append(jnp.sum(jnp.where(eidx == te[k:k + 1, :], ps, 0), axis=0, keepdims=True))
    d_ref[...] = jnp.concatenate(rows, axis=0) + rk_ref[...]


def _dest(top_e, rank, starts, tm):
    n = top_e.shape[1]
    return pl.pallas_call(
        _dest_kernel,
        grid=(n // tm,),
        in_specs=[pl.BlockSpec((TOP_K, tm), lambda i: (0, i)),
                  pl.BlockSpec((TOP_K, tm), lambda i: (0, i)),
                  pl.BlockSpec((N_EXPERTS, 1), lambda i: (0, 0))],
        out_specs=pl.BlockSpec((TOP_K, tm), lambda i: (0, i)),
        out_shape=jax.ShapeDtypeStruct((TOP_K, n), I32),
        compiler_params=_cparams(("parallel",)),
        name="dest",
    )(top_e, rank, starts.reshape(N_EXPERTS, 1))


def _invert_kernel(dest_ref, inv_ref):
    def body(j, c):
        inv_ref[dest_ref[j]] = j
        return c

    lax.fori_loop(0, dest_ref.shape[0], body, 0, unroll=32)


def _invert_permutation(dest_flat):
    m = dest_flat.shape[0]
    return pl.pallas_call(
        _invert_kernel,
        in_specs=[pl.BlockSpec(memory_space=pltpu.SMEM)],
        out_specs=pl.BlockSpec(memory_space=pltpu.SMEM),
        out_shape=jax.ShapeDtypeStruct((m,), I32),
        name="invert_perm",
    )(dest_flat)


def _expert_ffn(xb, wgb_ref, wub_ref, wdb_ref):
    hg = jnp.dot(xb, wgb_ref[...], preferred_element_type=F32)
    hu = jnp.dot(xb, wub_ref[...], preferred_element_type=F32)
    hh = (hg * jax.nn.sigmoid(hg)) * hu
    return jnp.dot(hh.astype(BF16), wdb_ref[...], preferred_element_type=F32)


def _expert_kernel(vt_ref, ve_ref, lo_ref, hi_ref, inv_ref, nxt_ref, e0_ref, par_ref,
                   x_hbm, wg_hbm, wu_hbm, wd_hbm, out_hbm,
                   wgf_ref, wuf_ref, wdf_ref, wgb_ref, wub_ref, wdb_ref, xbuf, obuf, xb_ref, cur_ref,
                   gsem, ssem, wsem, *, n_tok, n_tiles):
    v = pl.program_id(0)
    rows = xbuf.shape[1]
    tile = vt_ref[v]
    lo = lo_ref[v]
    hi = hi_ref[v]
    nonempty = hi > lo
    first = jnp.logical_and(nonempty, lo == 0)
    slot = lax.rem(tile, GATHER_SLOTS)
    oslot = tile % 2
    ahead = GATHER_SLOTS - 1

    def gather_row(t, r):
        src_tile = jnp.minimum(t, n_tiles - 1)
        tok = inv_ref[src_tile * rows + r] & (n_tok - 1)
        s = lax.rem(t, GATHER_SLOTS)
        return pltpu.make_async_copy(x_hbm.at[pl.ds(tok, 1)], xbuf.at[s, pl.ds(r, 1)], gsem.at[s])

    def tile_writeback(t):
        s = t % 2
        r0 = pl.multiple_of(t * rows, rows)
        return pltpu.make_async_copy(obuf.at[s], out_hbm.at[pl.ds(r0, rows)], ssem.at[s])

    def wait_gathered_tile(s):
        pltpu.make_async_copy(x_hbm.at[pl.ds(0, rows)], xbuf.at[s], gsem.at[s]).wait()

    def weight_copies(e):
        s = par_ref[e]
        return (pltpu.make_async_copy(wg_hbm.at[e], wgf_ref.at[s], wsem.at[0, s]),
                pltpu.make_async_copy(wu_hbm.at[e], wuf_ref.at[s], wsem.at[1, s]),
                pltpu.make_async_copy(wd_hbm.at[e], wdf_ref.at[s], wsem.at[2, s]))

    @pl.when(v == 0)
    def _():
        cur_ref[0] = -1
        for c in weight_copies(e0_ref[0]):
            c.start()

    @pl.when(jnp.logical_and(nonempty, cur_ref[0] != ve_ref[v]))
    def _():
        e = ve_ref[v]
        for c in weight_copies(e):
            c.wait()
        ne = nxt_ref[e]

        @pl.when(ne >= 0)
        def _():
            for c in weight_copies(ne):
                c.start()

        s = par_ref[e]
        wgb_ref[...] = wgf_ref[s].astype(BF16)
        wub_ref[...] = wuf_ref[s].astype(BF16)
        wdb_ref[...] = wdf_ref[s].astype(BF16)
        cur_ref[0] = e

    @pl.when(jnp.logical_and(first, tile == 0))
    def _():
        def issue(i, c):
            gather_row(i // rows, i % rows).start(priority=ROW_COPY_PRIORITY)
            return c

        lax.fori_loop(0, ahead * rows, issue, 0)

    @pl.when(jnp.logical_and(first, tile >= 2))
    def _():
        tile_writeback(tile - 2).wait()

    @pl.when(first)
    def _():
        wait_gathered_tile(slot)

    half = rows // 2

    @pl.when(jnp.logical_and(first, hi > half))
    def _():
        xb_ref[...] = xbuf[slot].astype(BF16)
        for r in range(rows):
            gather_row(tile + ahead, r).start(priority=ROW_COPY_PRIORITY)
        obuf[oslot] = _expert_ffn(xb_ref[...], wgb_ref, wub_ref, wdb_ref)

    @pl.when(jnp.logical_and(first, hi <= half))
    def _():
        xb_ref[0:half, :] = xbuf[slot, 0:half, :].astype(BF16)
        for r in range(rows):
            gather_row(tile + ahead, r).start(priority=ROW_COPY_PRIORITY)
        obuf[oslot, 0:half, :] = _expert_ffn(xb_ref[0:half, :], wgb_ref, wub_ref, wdb_ref)
        obuf[oslot, half:rows, :] = jnp.zeros((rows - half, obuf.shape[2]), F32)

    @pl.when(jnp.logical_and(nonempty, jnp.logical_and(lo > 0, lo < half)))
    def _():
        y = _expert_ffn(xbuf[slot].astype(BF16), wgb_ref, wub_ref, wdb_ref)
        row = lax.broadcasted_iota(I32, (rows, 1), 0)
        mine = jnp.logical_and(row >= lo, row < hi)
        obuf[oslot] = jnp.where(mine, y, obuf[oslot])

    @pl.when(jnp.logical_and(nonempty, lo >= half))
    def _():
        y = _expert_ffn(xbuf[slot, half:rows, :].astype(BF16), wgb_ref, wub_ref, wdb_ref)
        row = half + lax.broadcasted_iota(I32, (rows - half, 1), 0)
        mine = jnp.logical_and(row >= lo, row < hi)
        obuf[oslot, half:rows, :] = jnp.where(mine, y, obuf[oslot, half:rows, :])

    @pl.when(jnp.logical_and(nonempty, hi == rows))
    def _():
        tile_writeback(tile).start()

    @pl.when(v == pl.num_programs(0) - 1)
    def _():
        last = n_tiles - 1
        tile_writeback(last - 1).wait()
        tile_writeback(last).wait()
        for t in range(last + 1, last + 1 + ahead):
            wait_gathered_tile(t % GATHER_SLOTS)


def _experts(x1, inv, visit_tile, visit_expert, visit_lo, visit_hi, next_expert, first_expert, parity,
             w_gate, w_up, w_down):
    n_tok, d = x1.shape
    m = inv.shape[0]
    de = w_gate.shape[2]
    n_tiles = m // EXPERT_TILE
    any_spec = pl.BlockSpec(memory_space=pl.ANY)
    grid_spec = pltpu.PrefetchScalarGridSpec(
        num_scalar_prefetch=8,
        grid=(visit_tile.shape[0],),
        in_specs=[any_spec, any_spec, any_spec, any_spec],
        out_specs=any_spec,
        scratch_shapes=[pltpu.VMEM((2, d, de), F32), pltpu.VMEM((2, d, de), F32), pltpu.VMEM((2, de, d), F32),
                        pltpu.VMEM((d, de), BF16), pltpu.VMEM((d, de), BF16), pltpu.VMEM((de, d), BF16),
                        pltpu.VMEM((GATHER_SLOTS, EXPERT_TILE, d), F32), pltpu.VMEM((2, EXPERT_TILE, d), F32),
                        pltpu.VMEM((EXPERT_TILE, d), BF16), pltpu.SMEM((1,), I32),
                        pltpu.SemaphoreType.DMA((GATHER_SLOTS,)), pltpu.SemaphoreType.DMA((2,)),
                        pltpu.SemaphoreType.DMA((3, 2))],
    )
    return pl.pallas_call(
        functools.partial(_expert_kernel, n_tok=n_tok, n_tiles=n_tiles),
        grid_spec=grid_spec,
        out_shape=jax.ShapeDtypeStruct((m, d), F32),
        compiler_params=_cparams(("arbitrary",)),
        name="experts",
    )(visit_tile, visit_expert, visit_lo, visit_hi, inv, next_expert, first_expert, parity,
      x1, w_gate, w_up, w_down)


def _combine_kernel(dcur_ref, dnxt_ref, x1_ref, w_ref, ys_hbm, sg_ref, su_ref, sd_ref, g_ref, b_ref, o_ref,
                    *scratch):
    bufs = scratch[:COMBINE_PHASES]
    sem = scratch[COMBINE_PHASES]
    j = pl.program_id(0)
    tq = bufs[0].shape[1]
    ahead = 2

    def row_copy(dref, col, p, k, t):
        return pltpu.make_async_copy(ys_hbm.at[pl.ds(dref[k, col + t], 1)], bufs[p].at[k, pl.ds(t, 1)], sem.at[p])

    def wait_phase(p):
        pltpu.make_async_copy(bufs[p], bufs[p], sem.at[p]).wait()

    @pl.when(j == 0)
    def _():
        for p in range(ahead):
            def issue(t, c, p=p):
                for k in range(TOP_K):
                    row_copy(dcur_ref, p * tq, p, k, t).start(priority=ROW_COPY_PRIORITY)
                return c

            lax.fori_loop(0, tq, issue, 0)

    def request(q):
        dref, qq = (dcur_ref, q) if q < COMBINE_PHASES else (dnxt_ref, q - COMBINE_PHASES)
        for t in range(tq):
            for k in range(TOP_K):
                row_copy(dref, qq * tq, q % COMBINE_PHASES, k, t).start(priority=k % 2)

    request(ahead)
    x1 = x1_ref[...]
    xb = x1.astype(BF16)
    hg = jnp.dot(xb, sg_ref[...], preferred_element_type=F32)
    hu = jnp.dot(xb, su_ref[...], preferred_element_type=F32)
    hh = (hg * jax.nn.sigmoid(hg)) * hu
    shared = jnp.dot(hh.astype(BF16), sd_ref[...], preferred_element_type=F32)
    base = ALPHA * x1 + shared
    w = w_ref[...]

    for p in range(COMBINE_PHASES):
        wait_phase(p)
        if p + 1 < COMBINE_PHASES:
            request(p + 1 + ahead)
        rs = slice(p * tq, (p + 1) * tq)
        routed = bufs[p][0] * w[rs, 0:1]
        for k in range(1, TOP_K):
            routed = routed + bufs[p][k] * w[rs, k:k + 1]
        o_ref[rs, :] = _layer_norm(base[rs, :] + routed, g_ref[...], b_ref[...])

    @pl.when(j == pl.num_programs(0) - 1)
    def _():
        for p in range(ahead):
            wait_phase(p)


def _combine(x1, dest, w_tok, ys, s_gate, s_up, s_down, ln_g, ln_b, tq):
    n, d = x1.shape
    de = s_gate.shape[1]
    tm = COMBINE_PHASES * tq
    steps = n // tm
    full = lambda shape: pl.BlockSpec(shape, lambda i: (0,) * len(shape))
    return pl.pallas_call(
        _combine_kernel,
        grid=(steps,),
        in_specs=[pl.BlockSpec((TOP_K, tm), lambda i: (0, i), memory_space=pltpu.SMEM),
                  pl.BlockSpec((TOP_K, tm), lambda i: (0, jnp.minimum(i + 1, steps - 1)), memory_space=pltpu.SMEM),
                  pl.BlockSpec((tm, d), lambda i: (i, 0)),
                  pl.BlockSpec((tm, TOP_K), lambda i: (i, 0)),
                  pl.BlockSpec(memory_space=pl.ANY),
                  full((d, de)), full((d, de)), full((de, d)), full((1, d)), full((1, d))],
        out_specs=pl.BlockSpec((tm, d), lambda i: (i, 0)),
        out_shape=jax.ShapeDtypeStruct((n, d), F32),
        scratch_shapes=[pltpu.VMEM((TOP_K, tq, d), F32) for _ in range(COMBINE_PHASES)]
        + [pltpu.SemaphoreType.DMA((COMBINE_PHASES,))],
        compiler_params=_cparams(("arbitrary",)),
        name="combine",
    )(dest, dest, x1, w_tok, ys, s_gate.astype(BF16), s_up.astype(BF16), s_down.astype(BF16),
      ln_g.reshape(1, d), ln_b.reshape(1, d))


def _visit_plan(counts, n_rows):
    e = counts.shape[0]
    n_tiles = n_rows // EXPERT_TILE
    ends = jnp.cumsum(counts)
    starts = ends - counts
    pos = jnp.sort(jnp.concatenate([jnp.arange(n_tiles, dtype=I32) * EXPERT_TILE, starts]))
    nxt = jnp.concatenate([pos[1:], jnp.full((1,), n_rows, I32)])
    tile = jnp.minimum(pos // EXPERT_TILE, n_tiles - 1)
    expert = jnp.minimum(jnp.sum((ends[None, :] <= pos[:, None]).astype(I32), axis=1), e - 1)
    ids = jnp.arange(e, dtype=I32)
    later = jnp.logical_and(ids[None, :] > ids[:, None], counts[None, :] > 0)
    next_expert = jnp.min(jnp.where(later, ids[None, :], e), axis=1)
    next_expert = jnp.where(next_expert < e, next_expert, -1).astype(I32)
    first_expert = jnp.min(jnp.where(counts > 0, ids, e - 1)).astype(I32).reshape(1)
    parity = ((jnp.cumsum((counts > 0).astype(I32)) - 1) % 2).astype(I32)
    return (starts, tile, expert, pos - tile * EXPERT_TILE, nxt - tile * EXPERT_TILE, next_expert, first_expert,
            parity)


def _moe(x1, logits, router_bias, e_w_gate, e_w_up, e_w_down, s_w_gate, s_w_up, s_w_down, ln_g, ln_b):
    n, d = x1.shape
    top_e, w_t, rank, counts = _route(logits, router_bias, tm=TILES["route_tokens"])
    starts, v_tile, v_expert, v_lo, v_hi, next_e, first_e, parity = _visit_plan(counts[:, 0], n * TOP_K)
    dest = _dest(top_e, rank, starts, tm=TILES["dest_tokens"])
    inv = _invert_permutation(dest.reshape(n * TOP_K))
    ys = _experts(x1, inv, v_tile, v_expert, v_lo, v_hi, next_e, first_e, parity, e_w_gate, e_w_up, e_w_down)
    return _combine(x1, dest, w_t.T, ys, s_w_gate, s_w_up, s_w_down, ln_g, ln_b,
                    tq=TILES["combine_subtile_tokens"])


def _layer(x, w_in, b_gate, m_conv_w, m_conv_b, m_wq, m_wk, m_wv, m_w_if, m_b_if, m_norm_w, m_skip,
           r_conv_w, r_conv_b, r_wa, r_ba, r_wx, r_bx, r_lambda, w_pm, w_pr, w_o, ln1_g, ln1_b,
           router_w, router_bias, e_w_gate, e_w_up, e_w_down, s_w_gate, s_w_up, s_w_down, ln2_g, ln2_b):
    batch, seq, d = x.shape
    n = batch * seq
    xt = x.reshape(n, d)
    proj = _in_proj(xt.astype(BF16), w_in, bm=TILES["in_proj_rows"], bn=TILES["in_proj_cols"])
    o_xr = 2 * M_WIDTH
    o_xg = o_xr + R_WIDTH
    o_gate = o_xg + R_WIDTH
    xc, q, k, v, g, gt = _mlstm_prep(proj, seq, m_conv_w, m_conv_b, m_wq, m_wk, m_wv, m_w_if, m_b_if,
                                     tm=TILES["mlstm_prep_rows"])
    y_m = _mlstm(q, k, v, g, gt, xc, proj, m_norm_w, m_skip, batch, seq, MLSTM_CHUNK, TILES["mlstm_heads_per_step"])
    y_r = _rglru(proj, batch, seq, o_xr, o_xg, r_conv_w, r_conv_b, r_wa, r_ba, r_wx, r_bx, r_lambda,
                 tm=TILES["rglru_rows"], cw=TILES["rglru_cols"])
    merged = _merge(y_m, y_r, w_pm, w_pr, proj, o_gate, b_gate, tm=TILES["merge_rows"], bn=TILES["merge_cols"])
    x1, logits = _oproj(merged, w_o, xt, ln1_g, ln1_b, router_w, tm=TILES["out_proj_rows"])
    out = _moe(x1, logits, router_bias, e_w_gate, e_w_up, e_w_down, s_w_gate, s_w_up, s_w_down, ln2_g, ln2_b)
    return out.reshape(batch, seq, d)


def kernel(x, w_in, b_gate, m_conv_w, m_conv_b, m_wq, m_wk, m_wv, m_w_if, m_b_if, m_norm_w, m_skip, r_conv_w, r_conv_b, r_wa, r_ba, r_wx, r_bx, r_lambda, w_pm, w_pr, w_o, ln1_g, ln1_b, router_w, router_bias, e_w_gate, e_w_up, e_w_down, s_w_gate, s_w_up, s_w_down, ln2_g, ln2_b):
    for l in range(DEPTH):
        x = _layer(x, w_in[l], b_gate[l], m_conv_w[l], m_conv_b[l], m_wq[l], m_wk[l], m_wv[l],
                   m_w_if[l], m_b_if[l], m_norm_w[l], m_skip[l], r_conv_w[l], r_conv_b[l],
                   r_wa[l], r_ba[l], r_wx[l], r_bx[l], r_lambda[l], w_pm[l], w_pr[l], w_o[l],
                   ln1_g[l], ln1_b[l], router_w[l], router_bias[l], e_w_gate[l], e_w_up[l],
                   e_w_down[l], s_w_gate[l], s_w_up[l], s_w_down[l], ln2_g[l], ln2_b[l])
    return x
```
